```python
import math
import jax, jax.numpy as jnp
from jax import lax
import numpy as np

D_MODEL = 1024
BATCH = 1
SEQ = 16384
DEPTH = 2
DEC_BATCH = 8
DEC_SEQ = 16
PAST_LEN = 4096

CHUNK = 64
BAND_CHUNKS = 8
WIN_A = BAND_CHUNKS * CHUNK
HEAD_DIM = 64
H_A = 4
REL_CLIP = 128
H_B = 4
H_C = 8
C_NOPE = 64
C_ROPE = 32
C_V = 64
Q_RANK = 384
KV_RANK = 256
ROPE_THETA = 10000.0
D_FF = 2816
Q_BLOCK = 128
EPS = 1e-6
N_BRANCH = 3
W_A = H_A * HEAD_DIM
W_B = H_B * HEAD_DIM
W_C = H_C * C_V
D_IN = 3 * W_A + 3 * W_B + Q_RANK + KV_RANK + C_ROPE + N_BRANCH * D_MODEL
MLA_SCALE = (C_NOPE + C_ROPE) ** -0.5

kernel_name = 'hybrid_streaming_encoder_step'


def _rmsnorm(x, g):
    xf = x.astype(jnp.float32)
    y = xf * lax.rsqrt(jnp.mean(xf * xf, axis=-1, keepdims=True) + EPS)
    return (y * g.astype(jnp.float32)).astype(x.dtype)


def _half_ffn(x, g_pre, g_post, w_gate, w_up, w_down):
    h = _rmsnorm(x, g_pre)
    h = (jax.nn.silu(h @ w_gate) * (h @ w_up)) @ w_down
    return x + 0.5 * _rmsnorm(h, g_post)


def _rope(x, pos):
    half = C_ROPE // 2
    freq = ROPE_THETA ** (-jnp.arange(half, dtype=jnp.float32) / half)
    ang = pos.astype(jnp.float32)[:, None] * freq[None, :]
    cos = jnp.cos(ang)[None, :, None, :]
    sin = jnp.sin(ang)[None, :, None, :]
    xf = x.astype(jnp.float32)
    x1, x2 = xf[..., :half], xf[..., half:]
    return jnp.concatenate([x1 * cos - x2 * sin, x2 * cos + x1 * sin], axis=-1).astype(x.dtype)


def _chunk_visible(q_pos, k_pos, band):
    qc = q_pos[..., :, None] // CHUNK
    kc = k_pos[..., None, :] // CHUNK
    vis = (k_pos[..., None, :] >= 0) & (kc <= qc)
    if band is not None:
        vis = vis & (kc >= qc - band)
    return vis


def _mixer_inputs(u, pos, w_in, g_cq, g_ckv, w_uq):
    B, S, _ = u.shape
    sizes = (W_A, W_A, W_A, W_B, W_B, W_B, Q_RANK, KV_RANK, C_ROPE, N_BRANCH * D_MODEL)
    cuts = np.cumsum(sizes)[:-1].tolist()
    proj = jnp.einsum('bsd,de->bse', u, w_in)
    qa, ka, va, qb, kb, vb, cq, ckv, kr, gates = jnp.split(proj, cuts, axis=-1)
    heads_a = lambda t: t.reshape(B, S, H_A, HEAD_DIM)
    heads_b = lambda t: t.reshape(B, S, H_B, HEAD_DIM)
    qc = jnp.einsum('bsr,rhd->bshd', _rmsnorm(cq, g_cq), w_uq)
    q_nope = qc[..., :C_NOPE]
    q_rope = _rope(qc[..., C_NOPE:], pos)
    ckv = _rmsnorm(ckv, g_ckv)
    kr = _rope(kr[:, :, None, :], pos)[:, :, 0, :]
    return (heads_a(qa), heads_a(ka), heads_a(va), heads_b(qb), heads_b(kb), heads_b(vb),
            q_nope, q_rope, ckv, kr, gates)


def _band_attention(q, k, v, q_pos, k_pos, rel_bias):
    s = jnp.einsum('bnqhd,bnkhd->bnhqk', q, k).astype(jnp.float32) * (HEAD_DIM ** -0.5)
    rel = jnp.clip(q_pos[:, :, None] - k_pos[:, None, :], -REL_CLIP, REL_CLIP) + REL_CLIP
    bias = jnp.moveaxis(rel_bias.astype(jnp.float32)[:, rel], 0, 1)
    vis = _chunk_visible(q_pos, k_pos, BAND_CHUNKS)[:, None]
    s = jnp.where(vis[None], s + bias[None], -jnp.inf)
    p = jax.nn.softmax(s, axis=-1).astype(v.dtype)
    return jnp.einsum('bnhqk,bnkhd->bnqhd', p, v)


def _mixer_a_prompt(q, k, v, rel_bias):
    B, S, H, D = q.shape
    n = S // CHUNK
    pad = ((0, 0), (WIN_A, 0), (0, 0), (0, 0))
    kp = jnp.pad(k, pad).reshape(B, n + BAND_CHUNKS, CHUNK, H, D)
    vp = jnp.pad(v, pad).reshape(B, n + BAND_CHUNKS, CHUNK, H, D)
    k_band = jnp.concatenate([kp[:, i:i + n] for i in range(BAND_CHUNKS + 1)], axis=2)
    v_band = jnp.concatenate([vp[:, i:i + n] for i in range(BAND_CHUNKS + 1)], axis=2)
    q_pos = jnp.arange(S).reshape(n, CHUNK)
    k_pos = jnp.arange(n)[:, None] * CHUNK - WIN_A + jnp.arange((BAND_CHUNKS + 1) * CHUNK)[None, :]
    o = _band_attention(q.reshape(B, n, CHUNK, H, D), k_band, v_band, q_pos, k_pos, rel_bias)
    return o.reshape(B, S, H, D)


def _mixer_a_sample(q, k, v, cache_k, cache_v, q_pos, past_len, rel_bias):
    L = cache_k.shape[1]
    k_all = jnp.concatenate([cache_k, k], axis=1)[:, None]
    v_all = jnp.concatenate([cache_v, v], axis=1)[:, None]
    k_pos = jnp.concatenate([past_len - L + jnp.arange(L), q_pos])
    o = _band_attention(q[:, None], k_all, v_all, q_pos[None], k_pos[None], rel_bias)
    return o[:, 0]


def _stick_breaking(q, k, v, q_pos, k_pos):
    z = jnp.einsum('bqhd,bkhd->bhqk', q, k).astype(jnp.float32) * (HEAD_DIM ** -0.5)
    causal = k_pos[None, :] < q_pos[:, None]
    log_keep = jnp.where(causal, jax.nn.log_sigmoid(-z), 0.0)
    tail = lax.cumsum(log_keep, axis=3, reverse=True) - log_keep
    w = jnp.where(causal, jnp.exp(jax.nn.log_sigmoid(z) + tail), 0.0)
    return jnp.einsum('bhqk,bkhd->bqhd', w.astype(v.dtype), v)


def _mixer_b_prompt(q, k, v):
    B, S, H, D = q.shape
    n = S // Q_BLOCK
    q_blocks = jnp.moveaxis(q.reshape(B, n, Q_BLOCK, H, D), 1, 0)
    pos_blocks = jnp.arange(S).reshape(n, Q_BLOCK)
    k_pos = jnp.arange(S)
    o = lax.map(lambda a: _stick_breaking(a[0], k, v, a[1], k_pos), (q_blocks, pos_blocks))
    return jnp.moveaxis(o, 0, 1).reshape(B, S, H, D)


def _mla_decompress(c_kv, w_uk, w_uv):
    k_nope = jnp.einsum('bkr,rhd->bkhd', c_kv, w_uk)
    v = jnp.einsum('bkr,rhd->bkhd', c_kv, w_uv)
    return k_nope, v


def _mla_block(q_nope, q_rope, k_nope, k_rope, v, q_pos, k_pos):
    s = (jnp.einsum('bqhd,bkhd->bhqk', q_nope, k_nope)
         + jnp.einsum('bqhr,bkr->bhqk', q_rope, k_rope)).astype(jnp.float32) * MLA_SCALE
    s = jnp.where(_chunk_visible(q_pos, k_pos, None), s, -jnp.inf)
    p = jax.nn.softmax(s, axis=-1).astype(v.dtype)
    return jnp.einsum('bhqk,bkhd->bqhd', p, v)


def _mla_prompt(q_nope, q_rope, k_nope, k_rope, v):
    B, S, H, _ = q_nope.shape
    n = S // Q_BLOCK
    qn = jnp.moveaxis(q_nope.reshape(B, n, Q_BLOCK, H, C_NOPE), 1, 0)
    qr = jnp.moveaxis(q_rope.reshape(B, n, Q_BLOCK, H, C_ROPE), 1, 0)
    pos_blocks = jnp.arange(S).reshape(n, Q_BLOCK)
    k_pos = jnp.arange(S)
    o = lax.map(lambda a: _mla_block(a[0], a[1], k_nope, k_rope, v, a[2], k_pos), (qn, qr, pos_blocks))
    return jnp.moveaxis(o, 0, 1).reshape(B, S, H, C_V)


def _merge(ya, yb, yc, gates, w_br_a, w_br_b, w_br_c, w_out):
    B, S = ya.shape[:2]
    branches = jnp.stack([ya.reshape(B, S, W_A) @ w_br_a,
                          yb.reshape(B, S, W_B) @ w_br_b,
                          yc.reshape(B, S, W_C) @ w_br_c], axis=2)
    g = jax.nn.sigmoid(gates.reshape(B, S, N_BRANCH, D_MODEL))
    return jnp.sum(g * branches, axis=2) @ w_out


def _token_mixing(x, pos, mix, cache):
    (g_pre, g_post, w_in, g_cq, g_ckv, w_uq, w_uk, w_uv, rel_bias,
     w_br_a, w_br_b, w_br_c, w_out) = mix
    u = _rmsnorm(x, g_pre)
    qa, ka, va, qb, kb, vb, qn, qr, ckv, kr, gates = _mixer_inputs(u, pos, w_in, g_cq, g_ckv, w_uq)
    if cache is None:
        ya = _mixer_a_prompt(qa, ka, va, rel_bias)
        yb = _mixer_b_prompt(qb, kb, vb)
        kn, vc = _mla_decompress(ckv, w_uk, w_uv)
        yc = _mla_prompt(qn, qr, kn, kr, vc)
        win = min(WIN_A, x.shape[1])
        state = (ka[:, -win:], va[:, -win:], kb, vb, ckv, kr)
    else:
        ca_k, ca_v, cb_k, cb_v, cc_kv, cc_kr = cache
        past_len = cb_k.shape[1]
        k_pos = jnp.arange(past_len + x.shape[1])
        ya = _mixer_a_sample(qa, ka, va, ca_k, ca_v, pos, past_len, rel_bias)
        yb = _stick_breaking(qb, jnp.concatenate([cb_k, kb], axis=1),
                             jnp.concatenate([cb_v, vb], axis=1), pos, k_pos)
        kn, vc = _mla_decompress(jnp.concatenate([cc_kv, ckv], axis=1), w_uk, w_uv)
        yc = _mla_block(qn, qr, kn, jnp.concatenate([cc_kr, kr], axis=1), vc, pos, k_pos)
        state = (ka, va, kb, vb, ckv, kr)
    mixed = _merge(ya, yb, yc, gates, w_br_a, w_br_b, w_br_c, w_out)
    return x + _rmsnorm(mixed, g_post), state


def setup_inputs(seed: int = 0) -> dict:
    key = jax.random.key(seed)
    it = iter(jax.random.split(key, 40))

    def nrm(shape, scale):
        return scale * jax.random.normal(next(it), shape, jnp.float32)

    def gain(n):
        return 1.0 + nrm((DEPTH, n), 0.05)

    win_cache = min(WIN_A, PAST_LEN)
    return {
        'x_prompt': nrm((BATCH, SEQ, D_MODEL), 1.0),
        'x_sample': nrm((DEC_BATCH, DEC_SEQ, D_MODEL), 1.0),
        'cache_a_k': nrm((DEPTH, DEC_BATCH, win_cache, H_A, HEAD_DIM), 1.0),
        'cache_a_v': nrm((DEPTH, DEC_BATCH, win_cache, H_A, HEAD_DIM), 1.0),
        'cache_b_k': nrm((DEPTH, DEC_BATCH, PAST_LEN, H_B, HEAD_DIM), 1.0),
        'cache_b_v': nrm((DEPTH, DEC_BATCH, PAST_LEN, H_B, HEAD_DIM), 1.0),
        'cache_c_kv': nrm((DEPTH, DEC_BATCH, PAST_LEN, KV_RANK), 1.0),
        'cache_c_kr': nrm((DEPTH, DEC_BATCH, PAST_LEN, C_ROPE), 1.0),
        'ffn1_norm_pre': gain(D_MODEL),
        'ffn1_norm_post': gain(D_MODEL),
        'ffn1_w_gate': nrm((DEPTH, D_MODEL, D_FF), D_MODEL ** -0.5),
        'ffn1_w_up': nrm((DEPTH, D_MODEL, D_FF), D_MODEL ** -0.5),
        'ffn1_w_down': nrm((DEPTH, D_FF, D_MODEL), D_FF ** -0.5),
        'mix_norm_pre': gain(D_MODEL),
        'mix_norm_post': gain(D_MODEL),
        'w_in': nrm((DEPTH, D_MODEL, D_IN), D_MODEL ** -0.5),
        'cq_norm': gain(Q_RANK),
        'ckv_norm': gain(KV_RANK),
        'w_uq': nrm((DEPTH, Q_RANK, H_C, C_NOPE + C_ROPE), Q_RANK ** -0.5),
        'w_uk': nrm((DEPTH, KV_RANK, H_C, C_NOPE), KV_RANK ** -0.5),
        'w_uv': nrm((DEPTH, KV_RANK, H_C, C_V), KV_RANK ** -0.5),
        'rel_bias_a': nrm((DEPTH, H_A, 2 * REL_CLIP + 1), 0.1),
        'w_br_a': nrm((DEPTH, W_A, D_MODEL), W_A ** -0.5),
        'w_br_b': nrm((DEPTH, W_B, D_MODEL), W_B ** -0.5),
        'w_br_c': nrm((DEPTH, W_C, D_MODEL), W_C ** -0.5),
        'w_out': nrm((DEPTH, D_MODEL, D_MODEL), D_MODEL ** -0.5),
        'ffn2_norm_pre': gain(D_MODEL),
        'ffn2_norm_post': gain(D_MODEL),
        'ffn2_w_gate': nrm((DEPTH, D_MODEL, D_FF), D_MODEL ** -0.5),
        'ffn2_w_up': nrm((DEPTH, D_MODEL, D_FF), D_MODEL ** -0.5),
        'ffn2_w_down': nrm((DEPTH, D_FF, D_MODEL), D_FF ** -0.5),
    }


def reference(x_prompt, x_sample, cache_a_k, cache_a_v, cache_b_k, cache_b_v, cache_c_kv, cache_c_kr,
              ffn1_norm_pre, ffn1_norm_post, ffn1_w_gate, ffn1_w_up, ffn1_w_down,
              mix_norm_pre, mix_norm_post, w_in, cq_norm, ckv_norm, w_uq, w_uk, w_uv, rel_bias_a,
              w_br_a, w_br_b, w_br_c, w_out,
              ffn2_norm_pre, ffn2_norm_post, ffn2_w_gate, ffn2_w_up, ffn2_w_down):
    past_len = cache_b_k.shape[2]
    pos_p = jnp.arange(x_prompt.shape[1])
    pos_s = past_len + jnp.arange(x_sample.shape[1])
    xp, xs = x_prompt, x_sample
    states_p, states_s = [], []
    for l in range(DEPTH):
        ffn1 = (ffn1_norm_pre[l], ffn1_norm_post[l], ffn1_w_gate[l], ffn1_w_up[l], ffn1_w_down[l])
        ffn2 = (ffn2_norm_pre[l], ffn2_norm_post[l], ffn2_w_gate[l], ffn2_w_up[l], ffn2_w_down[l])
        mix = (mix_norm_pre[l], mix_norm_post[l], w_in[l], cq_norm[l], ckv_norm[l], w_uq[l], w_uk[l],
               w_uv[l], rel_bias_a[l], w_br_a[l], w_br_b[l], w_br_c[l], w_out[l])
        cache_l = (cache_a_k[l], cache_a_v[l], cache_b_k[l], cache_b_v[l], cache_c_kv[l], cache_c_kr[l])
        xp = _half_ffn(xp, *ffn1)
        xp, sp = _token_mixing(xp, pos_p, mix, None)
        xp = _half_ffn(xp, *ffn2)
        xs = _half_ffn(xs, *ffn1)
        xs, ss = _token_mixing(xs, pos_s, mix, cache_l)
        xs = _half_ffn(xs, *ffn2)
        states_p.append(sp)
        states_s.append(ss)
    stack = lambda states, i: jnp.stack([s[i] for s in states], axis=0)
    a_k_prompt, a_v_prompt = stack(states_p, 0), stack(states_p, 1)
    b_k_prompt, b_v_prompt = stack(states_p, 2), stack(states_p, 3)
    c_kv_prompt, c_kr_prompt = stack(states_p, 4), stack(states_p, 5)
    a_k_sample, a_v_sample = stack(states_s, 0), stack(states_s, 1)
    b_k_sample, b_v_sample = stack(states_s, 2), stack(states_s, 3)
    c_kv_sample, c_kr_sample = stack(states_s, 4), stack(states_s, 5)
    return (xp, xs, a_k_prompt, a_v_prompt, b_k_prompt, b_v_prompt, c_kv_prompt, c_kr_prompt,
            a_k_sample, a_v_sample, b_k_sample, b_v_sample, c_kv_sample, c_kr_sample)
```

```python
import functools

import numpy as np
import jax
import jax.numpy as jnp
from jax import lax
from jax.experimental import pallas as pl
from jax.experimental.pallas import tpu as pltpu

F32 = jnp.float32
BF16 = jnp.bfloat16

CHUNK = 64
BAND_CHUNKS = 8
WIN_A = BAND_CHUNKS * CHUNK
HEAD_DIM = 64
H_A = 4
H_B = 4
H_C = 8
C_NOPE = 64
C_ROPE = 32
C_V = 64
REL_CLIP = 128
ROPE_THETA = 10000.0
EPS = 1e-6
N_BRANCH = 3
W_AB = H_A * HEAD_DIM
C_PAD = 128
ATTN_SCALE = HEAD_DIM ** -0.5
MLA_SCALE = (C_NOPE + C_ROPE) ** -0.5
NEG = -1e30

LANES = 128
VMEM_LIMIT = 52 * 1024 * 1024

NT_DIMS = (((1,), (1,)), ((), ()))


def _rms(x, g):
    return x * lax.rsqrt(jnp.mean(x * x, axis=-1, keepdims=True) + EPS) * g


def _dot(a, b):
    return jnp.dot(a, b, preferred_element_type=F32)


def _dot_nt(a, b):
    return lax.dot_general(a, b, NT_DIMS, preferred_element_type=F32)


def _const_spec(shape):
    zeros = (0,) * len(shape)
    return pl.BlockSpec(shape, lambda *_: zeros, pipeline_mode=pl.Buffered(1))


def _row_spec(tm, width):
    return pl.BlockSpec((tm, width), lambda i: (i, 0))


def _params(n_grid):
    return pltpu.CompilerParams(dimension_semantics=("arbitrary",) * n_grid,
                                vmem_limit_bytes=VMEM_LIMIT)


def _ffn_kernel(x_ref, gpre_ref, gpost_ref, wg_ref, wu_ref, wd_ref, o_ref, *, f_chunk):
    x = x_ref[...]
    h = _rms(x, gpre_ref[...]).astype(BF16)
    d_ff = wg_ref.shape[1]
    y = jnp.zeros(x.shape, F32)
    for c in range(d_ff // f_chunk):
        sl = slice(c * f_chunk, (c + 1) * f_chunk)
        g = _dot(h, wg_ref[:, sl])
        u = _dot(h, wu_ref[:, sl])
        a = (g * jax.nn.sigmoid(g) * u).astype(BF16)
        y = y + _dot(a, wd_ref[sl, :])
    o_ref[...] = x + 0.5 * _rms(y, gpost_ref[...])


def _ffn(x, g_pre, g_post, w_gate, w_up, w_down, tm):
    n, d = x.shape
    d_ff = w_gate.shape[1]
    f_chunk = d_ff // 2 if (d_ff // 2) % LANES == 0 else d_ff
    return pl.pallas_call(
        functools.partial(_ffn_kernel, f_chunk=f_chunk),
        grid=(n // tm,),
        in_specs=[_row_spec(tm, d), _const_spec((1, d)), _const_spec((1, d)),
                  _const_spec((d, d_ff)), _const_spec((d, d_ff)), _const_spec((d_ff, d))],
        out_specs=_row_spec(tm, d),
        out_shape=jax.ShapeDtypeStruct((n, d), F32),
        compiler_params=_params(1),
        name="ffn",
    )(x, g_pre, g_post, w_gate, w_up, w_down)


def _rope(blk, tc, ts1, ts2):
    return blk * tc + pltpu.roll(blk, 16, 1) * ts1 + pltpu.roll(blk, C_PAD - 16, 1) * ts2


def _mixer_in_kernel(x_ref, gpre_ref, win_ref, gcq_ref, gckv_ref, wuq_ref, wukv_ref,
                     tc_ref, ts1_ref, ts2_ref,
                     qa_ref, kab_ref, vab_ref, qb_ref, kbb_ref, vbb_ref,
                     ka_ref, va_ref, kb_ref, vb_ref, ckv_ref, krb_ref,
                     qc_ref, kc_ref, vc_ref):
    w = W_AB
    q_rank = gcq_ref.shape[1]
    kv_rank = gckv_ref.shape[1]
    u = _rms(x_ref[...], gpre_ref[...]).astype(BF16)
    proj = _dot(u, win_ref[...])
    qa, ka, va = proj[:, 0:w], proj[:, w:2 * w], proj[:, 2 * w:3 * w]
    qb, kb, vb = proj[:, 3 * w:4 * w], proj[:, 4 * w:5 * w], proj[:, 5 * w:6 * w]
    o = 6 * w
    cq = proj[:, o:o + q_rank]
    ckv = proj[:, o + q_rank:o + q_rank + kv_rank]
    krb = proj[:, o + q_rank + kv_rank:o + q_rank + kv_rank + C_PAD]
    qa_ref[...] = (qa * ATTN_SCALE).astype(BF16)
    qb_ref[...] = (qb * ATTN_SCALE).astype(BF16)
    ka_ref[...] = ka
    va_ref[...] = va
    kb_ref[...] = kb
    vb_ref[...] = vb
    kab_ref[...] = ka.astype(BF16)
    vab_ref[...] = va.astype(BF16)
    kbb_ref[...] = kb.astype(BF16)
    vbb_ref[...] = vb.astype(BF16)
    tc, ts1, ts2 = tc_ref[...], ts1_ref[...], ts2_ref[...]
    cqn = _rms(cq, gcq_ref[...]).astype(BF16)
    qall = _dot(cqn, wuq_ref[...])
    ckvn = _rms(ckv, gckv_ref[...])
    ckv_ref[...] = ckvn
    kv = _dot(ckvn.astype(BF16), wukv_ref[...])
    krot = _rope(krb, tc, ts1, ts2)
    krb_ref[...] = krot
    for h in range(H_C):
        sl = slice(h * C_PAD, (h + 1) * C_PAD)
        qc_ref[:, sl] = (_rope(qall[:, sl], tc, ts1, ts2) * MLA_SCALE).astype(BF16)
        kc_ref[:, sl] = (kv[:, sl] + krot).astype(BF16)
    vc_ref[...] = kv[:, H_C * C_PAD:].astype(BF16)


def _mixer_in(x, g_pre, w_main, g_cq, g_ckv, w_uq, w_ukv, tabs, tm):
    n, d = x.shape
    q_rank, kv_rank = g_cq.shape[1], g_ckv.shape[1]
    wc = H_C * C_PAD
    wv = H_C * C_V
    bf = lambda width: jax.ShapeDtypeStruct((n, width), BF16)
    f32 = lambda width: jax.ShapeDtypeStruct((n, width), F32)
    out_shape = ([bf(W_AB)] * 6 + [f32(W_AB)] * 4 + [f32(kv_rank), f32(C_PAD)]
                 + [bf(wc), bf(wc), bf(wv)])
    out_specs = ([_row_spec(tm, W_AB)] * 10 + [_row_spec(tm, kv_rank), _row_spec(tm, C_PAD)]
                 + [_row_spec(tm, wc), _row_spec(tm, wc), _row_spec(tm, wv)])
    return pl.pallas_call(
        _mixer_in_kernel,
        grid=(n // tm,),
        in_specs=[_row_spec(tm, d), _const_spec((1, d)), _const_spec(w_main.shape),
                  _const_spec((1, q_rank)), _const_spec((1, kv_rank)),
                  _const_spec(w_uq.shape), _const_spec(w_ukv.shape),
                  _row_spec(tm, C_PAD), _row_spec(tm, C_PAD), _row_spec(tm, C_PAD)],
        out_specs=out_specs,
        out_shape=out_shape,
        compiler_params=_params(1),
        name="mixer_in",
    )(x, g_pre, w_main, g_cq, g_ckv, w_uq, w_ukv, *tabs)


def _merge_kernel(x_ref, ya_ref, yb_ref, yc_ref, gpre_ref, gpost_ref, wgates_ref,
                  wa_ref, wb_ref, wc_ref, wout_ref, o_ref):
    x = x_ref[...]
    d = x.shape[1]
    u = _rms(x, gpre_ref[...]).astype(BF16)
    m = jnp.zeros(x.shape, F32)
    for i, (y_ref, w_ref) in enumerate(((ya_ref, wa_ref), (yb_ref, wb_ref), (yc_ref, wc_ref))):
        gate = jax.nn.sigmoid(_dot(u, wgates_ref[:, i * d:(i + 1) * d]))
        m = m + gate * _dot(y_ref[...], w_ref[...])
    mixed = _dot(m.astype(BF16), wout_ref[...])
    o_ref[...] = x + _rms(mixed, gpost_ref[...])


def _merge(x, ya, yb, yc, g_pre, g_post, w_gates, w_a, w_b, w_c, w_out, tm):
    n, d = x.shape
    return pl.pallas_call(
        _merge_kernel,
        grid=(n // tm,),
        in_specs=[_row_spec(tm, d), _row_spec(tm, ya.shape[1]), _row_spec(tm, yb.shape[1]),
                  _row_spec(tm, yc.shape[1]), _const_spec((1, d)), _const_spec((1, d)),
                  _const_spec(w_gates.shape), _const_spec(w_a.shape), _const_spec(w_b.shape),
                  _const_spec(w_c.shape), _const_spec(w_out.shape)],
        out_specs=_row_spec(tm, d),
        out_shape=jax.ShapeDtypeStruct((n, d), F32),
        compiler_params=_params(1),
        name="merge",
    )(x, ya, yb, yc, g_pre, g_post, w_gates, w_a, w_b, w_c, w_out)


def _head_masked(q, width):
    lane_head = lax.broadcasted_iota(jnp.int32, q.shape, 1) // width
    zero = jnp.zeros_like(q)
    return [jnp.where(lane_head == h, q, zero) for h in range(q.shape[1] // width)]


def _strict_upper_ones(n):
    j = lax.broadcasted_iota(jnp.int32, (n, n), 0)
    s = lax.broadcasted_iota(jnp.int32, (n, n), 1)
    return (j > s).astype(BF16)


def _sb_tile(qm, kt, vt, carry, acc, ones_ut, causal):
    z = _dot_nt(qm, kt)
    t = jnp.log1p(jnp.exp(-jnp.abs(z)))
    log_keep = -jnp.maximum(z, 0.0) - t
    log_beta = jnp.minimum(z, 0.0) - t
    if causal is not None:
        log_keep = jnp.where(causal, log_keep, 0.0)
    hi = log_keep.astype(BF16)
    lo = (log_keep - hi.astype(F32)).astype(BF16)
    tail = _dot(hi, ones_ut) + _dot(lo, ones_ut)
    w = jnp.exp(log_beta + tail + carry)
    if causal is not None:
        w = jnp.where(causal, w, 0.0)
    acc = acc + _dot(w.astype(BF16), vt)
    carry = carry + jnp.sum(log_keep, axis=1, keepdims=True)
    return carry, acc


def _softmax_tile(s, vt, m, l, acc):
    m_new = jnp.maximum(m, jnp.max(s, axis=1, keepdims=True))
    alpha = jnp.exp(m - m_new)
    p = jnp.exp(s - m_new)
    l = alpha * l + jnp.sum(p, axis=1, keepdims=True)
    acc = alpha * acc + _dot(p.astype(BF16), vt)
    return m_new, l, acc


def _select_heads(parts, width):
    lane_head = lax.broadcasted_iota(jnp.int32, parts[0].shape, 1) // width
    out = parts[0]
    for h in range(1, len(parts)):
        out = jnp.where(lane_head == h, parts[h], out)
    return out


def _band_kernel(q_ref, k0_ref, k1_ref, k2_ref, v0_ref, v1_ref, v2_ref, bias_ref, o_ref):
    i = pl.program_id(0)
    tq = q_ref.shape[0]
    kwin = jnp.concatenate([k0_ref[...], k1_ref[...], k2_ref[...]], axis=0)
    vwin = jnp.concatenate([v0_ref[...], v1_ref[...], v2_ref[...]], axis=0)
    col = lax.broadcasted_iota(jnp.int32, (tq, 3 * tq), 1)
    in_range = col >= (2 - i) * tq
    parts = []
    for h, qm in enumerate(_head_masked(q_ref[...], HEAD_DIM)):
        s = jnp.where(in_range, _dot_nt(qm, kwin) + bias_ref[h], NEG)
        m = jnp.max(s, axis=1, keepdims=True)
        p = jnp.exp(s - m)
        l = jnp.sum(p, axis=1, keepdims=True)
        parts.append(_dot(p.astype(BF16), vwin) / l)
    o_ref[...] = _select_heads(parts, HEAD_DIM).astype(BF16)


def _band_attention(q, k, v, bias, tq):
    n, w = q.shape
    blk = lambda back: pl.BlockSpec((tq, w), lambda i: (jnp.maximum(i - back, 0), 0))
    return pl.pallas_call(
        _band_kernel,
        grid=(n // tq,),
        in_specs=[blk(0), blk(2), blk(1), blk(0), blk(2), blk(1), blk(0), _const_spec(bias.shape)],
        out_specs=blk(0),
        out_shape=jax.ShapeDtypeStruct((n, w), BF16),
        compiler_params=_params(1),
        name="band_attn",
    )(q, k, k, k, v, v, v, bias)


def _sb_kernel(q_ref, k_ref, v_ref, o_ref, *, tile):
    i = pl.program_id(0)
    row = lax.broadcasted_iota(jnp.int32, (tile, tile), 0)
    col = lax.broadcasted_iota(jnp.int32, (tile, tile), 1)
    causal = col < row
    ones_ut = _strict_upper_ones(tile)
    width = q_ref.shape[1]
    parts = []
    for qm in _head_masked(q_ref[...], HEAD_DIM):
        def tile_at(kb, carry, acc, mask, qm=qm):
            start = pl.multiple_of(kb * tile, tile)
            return _sb_tile(qm, k_ref[pl.ds(start, tile), :], v_ref[pl.ds(start, tile), :],
                            carry, acc, ones_ut, mask)
        state = tile_at(i, jnp.zeros((tile, 1), F32), jnp.zeros((tile, width), F32), causal)
        state = lax.fori_loop(0, i, lambda j, st: tile_at(i - 1 - j, st[0], st[1], None), state)
        parts.append(state[1])
    o_ref[...] = _select_heads(parts, HEAD_DIM).astype(BF16)


def _sb_attention(q, k, v, tile):
    n, w = q.shape
    return pl.pallas_call(
        functools.partial(_sb_kernel, tile=tile),
        grid=(n // tile,),
        in_specs=[_row_spec(tile, w), _const_spec((n, w)), _const_spec((n, w))],
        out_specs=_row_spec(tile, w),
        out_shape=jax.ShapeDtypeStruct((n, w), BF16),
        compiler_params=_params(1),
        name="sb_attn",
    )(q, k, v)


def _mla_kernel(q_ref, k_ref, v_ref, o_ref, *, tile):
    i = pl.program_id(1)
    row = lax.broadcasted_iota(jnp.int32, (tile, tile), 0)
    col = lax.broadcasted_iota(jnp.int32, (tile, tile), 1)
    visible = (col // CHUNK) <= (row // CHUNK)
    qms = _head_masked(q_ref[...], C_PAD)
    wv = v_ref.shape[1]

    def step(kb, state, mask):
        start = pl.multiple_of(kb * tile, tile)
        kt = k_ref[pl.ds(start, tile), :]
        vt = v_ref[pl.ds(start, tile), :]
        out = []
        for qm, (m, l, acc) in zip(qms, state):
            s = _dot_nt(qm, kt)
            if mask is not None:
                s = jnp.where(mask, s, NEG)
            out.append(_softmax_tile(s, vt, m, l, acc))
        return tuple(out)

    init = tuple((jnp.full((tile, 1), NEG, F32), jnp.zeros((tile, 1), F32),
                  jnp.zeros((tile, wv), F32)) for _ in qms)
    state = lax.fori_loop(0, i, lambda kb, st: step(kb, st, None), init)
    state = step(i, state, visible)
    o_ref[...] = _select_heads([acc / l for _, l, acc in state], C_V).astype(BF16)


def _mla_attention(q, k, v, tile):
    n = q.shape[0]
    pairs = H_C // 2
    return pl.pallas_call(
        functools.partial(_mla_kernel, tile=tile),
        grid=(pairs, n // tile),
        in_specs=[pl.BlockSpec((tile, 2 * C_PAD), lambda p, i: (i, p)),
                  pl.BlockSpec((n, 2 * C_PAD), lambda p, i: (0, p), pipeline_mode=pl.Buffered(1)),
                  pl.BlockSpec((n, 2 * C_V), lambda p, i: (0, p), pipeline_mode=pl.Buffered(1))],
        out_specs=pl.BlockSpec((tile, 2 * C_V), lambda p, i: (i, p)),
        out_shape=jax.ShapeDtypeStruct((n, H_C * C_V), BF16),
        compiler_params=_params(2),
        name="mla_attn",
    )(q, k, v)


def _sample_attn_kernel(qa_ref, ka_ref, va_ref, cak_ref, cav_ref, biasc_ref, biasn_ref,
                        qb_ref, kb_ref, vb_ref, cbk_ref, cbv_ref,
                        qc_ref, kc_ref, vc_ref, ckv_ref, ckr_ref, wukv_ref,
                        ya_ref, yb_ref, yc_ref, *, tile):
    ds = qa_ref.shape[1]
    past = cbk_ref.shape[1]

    ka, va, cak, cav = ka_ref[0], va_ref[0], cak_ref[0], cav_ref[0]
    parts = []
    for h, qm in enumerate(_head_masked(qa_ref[0], HEAD_DIM)):
        s_c = _dot_nt(qm, cak) + biasc_ref[h]
        s_n = _dot_nt(qm, ka) + biasn_ref[h]
        m = jnp.maximum(jnp.max(s_c, axis=1, keepdims=True), jnp.max(s_n, axis=1, keepdims=True))
        p_c = jnp.exp(s_c - m)
        p_n = jnp.exp(s_n - m)
        l = jnp.sum(p_c, axis=1, keepdims=True) + jnp.sum(p_n, axis=1, keepdims=True)
        parts.append((_dot(p_c.astype(BF16), cav) + _dot(p_n.astype(BF16), va)) / l)
    ya_ref[0] = _select_heads(parts, HEAD_DIM).astype(BF16)

    row = lax.broadcasted_iota(jnp.int32, (ds, ds), 0)
    col = lax.broadcasted_iota(jnp.int32, (ds, ds), 1)
    causal = col < row
    ones_new = _strict_upper_ones(ds)
    ones_ut = _strict_upper_ones(tile)
    kb, vb = kb_ref[0], vb_ref[0]
    n_tiles = past // tile
    parts = []
    for qm in _head_masked(qb_ref[0], HEAD_DIM):
        state = _sb_tile(qm, kb, vb, jnp.zeros((ds, 1), F32), jnp.zeros((ds, W_AB), F32),
                         ones_new, causal)

        def cache_tile(j, st, qm=qm):
            start = pl.multiple_of((n_tiles - 1 - j) * tile, tile)
            return _sb_tile(qm, cbk_ref[0, pl.ds(start, tile), :], cbv_ref[0, pl.ds(start, tile), :],
                            st[0], st[1], ones_ut, None)
        state = lax.fori_loop(0, n_tiles, cache_tile, state)
        parts.append(state[1])
    yb_ref[0] = _select_heads(parts, HEAD_DIM).astype(BF16)

    wk = H_C * C_PAD
    pairs = H_C // 2
    qc = qc_ref[0]
    qms = []
    for p in range(pairs):
        qms.extend(_head_masked(qc[:, p * 2 * C_PAD:(p + 1) * 2 * C_PAD], C_PAD))

    def attend(kall, vall, state):
        out = []
        for h, (m, l, acc) in enumerate(state):
            p = h // 2
            s = _dot_nt(qms[h], kall[:, p * 2 * C_PAD:(p + 1) * 2 * C_PAD])
            out.append(_softmax_tile(s, vall[:, p * 2 * C_V:(p + 1) * 2 * C_V], m, l, acc))
        return tuple(out)

    def cache_step(j, state):
        start = pl.multiple_of(j * tile, tile)
        lat = ckv_ref[0, pl.ds(start, tile), :]
        kv = _dot(lat, wukv_ref[...])
        krb = ckr_ref[0, pl.ds(start, tile), :]
        kall = jnp.concatenate([kv[:, h * C_PAD:(h + 1) * C_PAD] + krb for h in range(H_C)],
                               axis=1).astype(BF16)
        return attend(kall, kv[:, wk:].astype(BF16), state)

    init = tuple((jnp.full((ds, 1), NEG, F32), jnp.zeros((ds, 1), F32),
                  jnp.zeros((ds, 2 * C_V), F32)) for _ in range(H_C))
    state = lax.fori_loop(0, n_tiles, cache_step, init)
    state = attend(kc_ref[0], vc_ref[0], state)
    outs = []
    for p in range(pairs):
        outs.append(_select_heads([state[2 * p + e][2] / state[2 * p + e][1] for e in range(2)], C_V))
    yc_ref[0] = jnp.concatenate(outs, axis=1).astype(BF16)


def _sample_attention(new, caches, bias_c, bias_n, w_ukv, tile):
    qa, ka, va, qb, kb, vb, qc, kc, vc = new
    cak, cav, cbk, cbv, ckv, ckr = caches
    nb, ds, _ = qa.shape
    req = lambda a: pl.BlockSpec((1,) + a.shape[1:], lambda b: (b, 0, 0))
    ins = [qa, ka, va, cak, cav, bias_c, bias_n, qb, kb, vb, cbk, cbv, qc, kc, vc, ckv, ckr, w_ukv]
    specs = [req(a) for a in ins]
    for idx in (5, 6, 17):
        specs[idx] = _const_spec(ins[idx].shape)
    out_shape = [jax.ShapeDtypeStruct((nb, ds, W_AB), BF16), jax.ShapeDtypeStruct((nb, ds, W_AB), BF16),
                 jax.ShapeDtypeStruct((nb, ds, H_C * C_V), BF16)]
    return pl.pallas_call(
        functools.partial(_sample_attn_kernel, tile=tile),
        grid=(nb,),
        in_specs=specs,
        out_specs=[req(s) for s in out_shape],
        out_shape=out_shape,
        compiler_params=_params(1),
        name="sample_attn",
    )(*ins)


def _rope_tables(pos):
    half = C_ROPE // 2
    freq = ROPE_THETA ** (-jnp.arange(half, dtype=F32) / half)
    ang = pos.astype(F32)[:, None] * freq[None, :]
    cos, sin = jnp.cos(ang), jnp.sin(ang)
    n = pos.shape[0]
    z = lambda w: jnp.zeros((n, w), F32)
    tc = jnp.concatenate([jnp.ones((n, C_NOPE), F32), cos, cos, z(C_PAD - C_NOPE - C_ROPE)], axis=1)
    ts1 = jnp.concatenate([z(C_NOPE + half), sin, z(C_PAD - C_NOPE - C_ROPE)], axis=1)
    ts2 = jnp.concatenate([z(C_NOPE), -sin, z(C_PAD - C_NOPE - half)], axis=1)
    return tc, ts1, ts2


def _band_bias(rel_bias, q_pos, k_pos):
    rel = np.clip(q_pos[:, None] - k_pos[None, :], -REL_CLIP, REL_CLIP) + REL_CLIP
    qc, kc = q_pos[:, None] // CHUNK, k_pos[None, :] // CHUNK
    vis = (k_pos[None, :] >= 0) & (kc <= qc) & (kc >= qc - BAND_CHUNKS)
    return jnp.where(jnp.asarray(vis)[None], rel_bias.astype(F32)[:, rel], NEG)


def _layer_weights(l, w_in, w_uq, w_uk, w_uv):
    d = w_in.shape[1]
    q_rank, kv_rank = w_uq.shape[1], w_uk.shape[1]
    o = 6 * W_AB + q_rank + kv_rank
    zc = lambda w: jnp.zeros((d, w), w_in.dtype)
    w_main = jnp.concatenate([w_in[l, :, :o], zc(C_NOPE), w_in[l, :, o:o + C_ROPE],
                              zc(C_PAD - C_NOPE - C_ROPE)], axis=1).astype(BF16)
    w_gates = w_in[l, :, o + C_ROPE:].astype(BF16)
    pad_heads = lambda w: jnp.pad(w, ((0, 0), (0, 0), (0, C_PAD - w.shape[2]))).reshape(w.shape[0], -1)
    w_uq_p = pad_heads(w_uq[l]).astype(BF16)
    w_ukv = jnp.concatenate([pad_heads(w_uk[l]), w_uv[l].reshape(kv_rank, -1)], axis=1).astype(BF16)
    return w_main, w_gates, w_uq_p, w_ukv


def kernel(x_prompt, x_sample, cache_a_k, cache_a_v, cache_b_k, cache_b_v, cache_c_kv, cache_c_kr, ffn1_norm_pre, ffn1_norm_post, ffn1_w_gate, ffn1_w_up, ffn1_w_down, mix_norm_pre, mix_norm_post, w_in, cq_norm, ckv_norm, w_uq, w_uk, w_uv, rel_bias_a, w_br_a, w_br_b, w_br_c, w_out, ffn2_norm_pre, ffn2_norm_post, ffn2_w_gate, ffn2_w_up, ffn2_w_down):
    batch, seq, d = x_prompt.shape
    nb, ds, _ = x_sample.shape
    depth = w_in.shape[0]
    past = cache_b_k.shape[2]
    win_cache = cache_a_k.shape[2]
    assert batch == 1, "prompt group is a single sequence"
    tile = 256
    tm = 512 if seq % 512 == 0 else tile
    assert seq % tile == 0 and past % tile == 0 and tile % CHUNK == 0 and 2 * tile >= WIN_A
    assert past % CHUNK + ds <= CHUNK
    ns = nb * ds

    xp = x_prompt.reshape(seq, d)
    xs = x_sample.reshape(ns, d)
    tabs_p = _rope_tables(jnp.arange(seq))
    tabs_s = _rope_tables(jnp.tile(past + jnp.arange(ds), nb))

    row = lambda g, l: g[l][None, :]
    q_pos_p = np.arange(tile)
    k_pos_p = np.arange(3 * tile) - 2 * tile
    base = 4 * tile
    q_pos_s = past + np.arange(ds)
    k_pos_c = past - win_cache + np.arange(win_cache)

    states_p, states_s = [], []
    for l in range(depth):
        w_main, w_gates, w_uq_p, w_ukv = _layer_weights(l, w_in, w_uq, w_uk, w_uv)
        ffn1 = (row(ffn1_norm_pre, l), row(ffn1_norm_post, l), ffn1_w_gate[l].astype(BF16),
                ffn1_w_up[l].astype(BF16), ffn1_w_down[l].astype(BF16))
        ffn2 = (row(ffn2_norm_pre, l), row(ffn2_norm_post, l), ffn2_w_gate[l].astype(BF16),
                ffn2_w_up[l].astype(BF16), ffn2_w_down[l].astype(BF16))
        mix_in = (row(mix_norm_pre, l), w_main, row(cq_norm, l), row(ckv_norm, l), w_uq_p, w_ukv)
        mrg = (row(mix_norm_pre, l), row(mix_norm_post, l), w_gates, w_br_a[l].astype(BF16),
               w_br_b[l].astype(BF16), w_br_c[l].astype(BF16), w_out[l].astype(BF16))
        bias_p = _band_bias(rel_bias_a[l], base + q_pos_p, base + k_pos_p)
        bias_c = _band_bias(rel_bias_a[l], q_pos_s, k_pos_c)
        bias_n = _band_bias(rel_bias_a[l], q_pos_s, q_pos_s)

        xp = _ffn(xp, *ffn1, tm)
        (qa, kab, vab, qb, kbb, vbb, ka, va, kb, vb, ckv, krb, qc, kc, vc) = _mixer_in(
            xp, *mix_in, tabs_p, tm)
        ya = _band_attention(qa, kab, vab, bias_p, tile)
        yb = _sb_attention(qb, kbb, vbb, tile)
        yc = _mla_attention(qc, kc, vc, tile)
        xp = _merge(xp, ya, yb, yc, *mrg, tm)
        xp = _ffn(xp, *ffn2, tm)
        win = min(WIN_A, seq)
        heads = lambda t, h: t.reshape(1, t.shape[0], h, t.shape[1] // h)
        states_p.append((heads(ka[-win:], H_A), heads(va[-win:], H_A), heads(kb, H_B), heads(vb, H_B),
                         ckv[None], krb[None, :, C_NOPE:C_NOPE + C_ROPE]))

        xs = _ffn(xs, *ffn1, ns)
        (qa, kab, vab, qb, kbb, vbb, ka, va, kb, vb, ckv, krb, qc, kc, vc) = _mixer_in(
            xs, *mix_in, tabs_s, ns)
        per_req = lambda t: t.reshape(nb, ds, t.shape[1])
        flat = lambda c: c.reshape(nb, c.shape[1], -1).astype(BF16)
        ckr_blk = jnp.pad(cache_c_kr[l], ((0, 0), (0, 0), (C_NOPE, C_PAD - C_NOPE - C_ROPE)))
        caches = (flat(cache_a_k[l]), flat(cache_a_v[l]), flat(cache_b_k[l]), flat(cache_b_v[l]),
                  cache_c_kv[l].astype(BF16), ckr_blk)
        new = tuple(per_req(t) for t in (qa, kab, vab, qb, kbb, vbb, qc, kc, vc))
        ya, yb, yc = _sample_attention(new, caches, bias_c, bias_n, w_ukv, tile)
        xs = _merge(xs, ya.reshape(ns, -1), yb.reshape(ns, -1), yc.reshape(ns, -1), *mrg, ns)
        xs = _ffn(xs, *ffn2, ns)
        heads_s = lambda t, h: t.reshape(nb, ds, h, t.shape[1] // h)
        states_s.append((heads_s(ka, H_A), heads_s(va, H_A), heads_s(kb, H_B), heads_s(vb, H_B),
                         ckv.reshape(nb, ds, -1), krb[:, C_NOPE:C_NOPE + C_ROPE].reshape(nb, ds, C_ROPE)))

    stack = lambda states, i: jnp.stack([s[i] for s in states], axis=0)
    return ((xp.reshape(batch, seq, d), xs.reshape(nb, ds, d))
            + tuple(stack(states_p, i) for i in range(6))
            + tuple(stack(states_s, i) for i in range(6)))
```

```python
import functools

import numpy as np
import jax
import jax.numpy as jnp
from jax import lax
from jax.experimental import pallas as pl
from jax.experimental.pallas import tpu as pltpu

F32 = jnp.float32
BF16 = jnp.bfloat16

CHUNK = 64
BAND_CHUNKS = 8
WIN_A = BAND_CHUNKS * CHUNK
HEAD_DIM = 64
H_A = 4
H_B = 4
H_C = 8
C_NOPE = 64
C_ROPE = 32
C_V = 64
REL_CLIP = 128
ROPE_THETA = 10000.0
EPS = 1e-6
N_BRANCH = 3
W_AB = H_A * HEAD_DIM
C_PAD = 128
ATTN_SCALE = HEAD_DIM ** -0.5
MLA_SCALE = (C_NOPE + C_ROPE) ** -0.5
NEG = -1e30
SB_DEAD = -104.0

LANES = 128
VMEM_LIMIT = 52 * 1024 * 1024

NT_DIMS = (((1,), (1,)), ((), ()))


def _rms(x, g):
    return x * lax.rsqrt(jnp.mean(x * x, axis=-1, keepdims=True) + EPS) * g


def _dot(a, b):
    return jnp.dot(a, b, preferred_element_type=F32)


def _dot_nt(a, b):
    return lax.dot_general(a, b, NT_DIMS, preferred_element_type=F32)


def _const_spec(shape):
    zeros = (0,) * len(shape)
    return pl.BlockSpec(shape, lambda *_: zeros, pipeline_mode=pl.Buffered(1))


def _row_spec(tm, width):
    return pl.BlockSpec((tm, width), lambda i: (i, 0))


def _params(n_grid):
    return pltpu.CompilerParams(dimension_semantics=("arbitrary",) * n_grid,
                                vmem_limit_bytes=VMEM_LIMIT)


def _ffn_kernel(x_ref, gpre_ref, gpost_ref, wg_ref, wu_ref, wd_ref, o_ref, *, f_chunk):
    x = x_ref[...]
    h = _rms(x, gpre_ref[...]).astype(BF16)
    d_ff = wg_ref.shape[1]
    y = jnp.zeros(x.shape, F32)
    for c in range(d_ff // f_chunk):
        sl = slice(c * f_chunk, (c + 1) * f_chunk)
        g = _dot(h, wg_ref[:, sl])
        u = _dot(h, wu_ref[:, sl])
        a = (g * jax.nn.sigmoid(g) * u).astype(BF16)
        y = y + _dot(a, wd_ref[sl, :])
    o_ref[...] = x + 0.5 * _rms(y, gpost_ref[...])


def _ffn(x, g_pre, g_post, w_gate, w_up, w_down, tm):
    n, d = x.shape
    d_ff = w_gate.shape[1]
    f_chunk = d_ff // 2 if (d_ff // 2) % LANES == 0 else d_ff
    return pl.pallas_call(
        functools.partial(_ffn_kernel, f_chunk=f_chunk),
        grid=(n // tm,),
        in_specs=[_row_spec(tm, d), _const_spec((1, d)), _const_spec((1, d)),
                  _const_spec((d, d_ff)), _const_spec((d, d_ff)), _const_spec((d_ff, d))],
        out_specs=_row_spec(tm, d),
        out_shape=jax.ShapeDtypeStruct((n, d), F32),
        compiler_params=_params(1),
        name="ffn",
    )(x, g_pre, g_post, w_gate, w_up, w_down)


def _rope(blk, tc, ts1, ts2):
    return blk * tc + pltpu.roll(blk, 16, 1) * ts1 + pltpu.roll(blk, C_PAD - 16, 1) * ts2


def _mixer_in_kernel(x_ref, gpre_ref, win_ref, gcq_ref, gckv_ref, wuq_ref, wukv_ref,
                     tc_ref, ts1_ref, ts2_ref,
                     qa_ref, kab_ref, vab_ref, qb_ref, kbb_ref, vbb_ref,
                     ka_ref, va_ref, kb_ref, vb_ref, ckv_ref, krb_ref,
                     qc_ref, kc_ref, vc_ref):
    w = W_AB
    q_rank = gcq_ref.shape[1]
    kv_rank = gckv_ref.shape[1]
    u = _rms(x_ref[...], gpre_ref[...]).astype(BF16)
    proj = _dot(u, win_ref[...])
    qa, ka, va = proj[:, 0:w], proj[:, w:2 * w], proj[:, 2 * w:3 * w]
    qb, kb, vb = proj[:, 3 * w:4 * w], proj[:, 4 * w:5 * w], proj[:, 5 * w:6 * w]
    o = 6 * w
    cq = proj[:, o:o + q_rank]
    ckv = proj[:, o + q_rank:o + q_rank + kv_rank]
    krb = proj[:, o + q_rank + kv_rank:o + q_rank + kv_rank + C_PAD]
    qa_ref[...] = (qa * ATTN_SCALE).astype(BF16)
    qb_ref[...] = (qb * ATTN_SCALE).astype(BF16)
    ka_ref[...] = ka
    va_ref[...] = va
    kb_ref[...] = kb
    vb_ref[...] = vb
    kab_ref[...] = ka.astype(BF16)
    vab_ref[...] = va.astype(BF16)
    kbb_ref[...] = kb.astype(BF16)
    vbb_ref[...] = vb.astype(BF16)
    tc, ts1, ts2 = tc_ref[...], ts1_ref[...], ts2_ref[...]
    cqn = _rms(cq, gcq_ref[...]).astype(BF16)
    qall = _dot(cqn, wuq_ref[...])
    ckvn = _rms(ckv, gckv_ref[...])
    ckv_ref[...] = ckvn
    kv = _dot(ckvn.astype(BF16), wukv_ref[...])
    krot = _rope(krb, tc, ts1, ts2)
    krb_ref[...] = krot
    for h in range(H_C):
        sl = slice(h * C_PAD, (h + 1) * C_PAD)
        qc_ref[:, sl] = (_rope(qall[:, sl], tc, ts1, ts2) * MLA_SCALE).astype(BF16)
        kc_ref[:, sl] = (kv[:, sl] + krot).astype(BF16)
    vc_ref[...] = kv[:, H_C * C_PAD:].astype(BF16)


def _mixer_in(x, g_pre, w_main, g_cq, g_ckv, w_uq, w_ukv, tabs, tm):
    n, d = x.shape
    q_rank, kv_rank = g_cq.shape[1], g_ckv.shape[1]
    wc = H_C * C_PAD
    wv = H_C * C_V
    bf = lambda width: jax.ShapeDtypeStruct((n, width), BF16)
    f32 = lambda width: jax.ShapeDtypeStruct((n, width), F32)
    out_shape = ([bf(W_AB)] * 6 + [f32(W_AB)] * 4 + [f32(kv_rank), f32(C_PAD)]
                 + [bf(wc), bf(wc), bf(wv)])
    out_specs = ([_row_spec(tm, W_AB)] * 10 + [_row_spec(tm, kv_rank), _row_spec(tm, C_PAD)]
                 + [_row_spec(tm, wc), _row_spec(tm, wc), _row_spec(tm, wv)])
    return pl.pallas_call(
        _mixer_in_kernel,
        grid=(n // tm,),
        in_specs=[_row_spec(tm, d), _const_spec((1, d)), _const_spec(w_main.shape),
                  _const_spec((1, q_rank)), _const_spec((1, kv_rank)),
                  _const_spec(w_uq.shape), _const_spec(w_ukv.shape),
                  _row_spec(tm, C_PAD), _row_spec(tm, C_PAD), _row_spec(tm, C_PAD)],
        out_specs=out_specs,
        out_shape=out_shape,
        compiler_params=_params(1),
        name="mixer_in",
    )(x, g_pre, w_main, g_cq, g_ckv, w_uq, w_ukv, *tabs)


def _merge_kernel(x_ref, ya_ref, yb_ref, yc_ref, gpre_ref, gpost_ref, wgates_ref,
                  wa_ref, wb_ref, wc_ref, wout_ref, o_ref):
    x = x_ref[...]
    d = x.shape[1]
    u = _rms(x, gpre_ref[...]).astype(BF16)
    m = jnp.zeros(x.shape, F32)
    for i, (y_ref, w_ref) in enumerate(((ya_ref, wa_ref), (yb_ref, wb_ref), (yc_ref, wc_ref))):
        gate = jax.nn.sigmoid(_dot(u, wgates_ref[:, i * d:(i + 1) * d]))
        m = m + gate * _dot(y_ref[...], w_ref[...])
    mixed = _dot(m.astype(BF16), wout_ref[...])
    o_ref[...] = x + _rms(mixed, gpost_ref[...])


def _merge(x, ya, yb, yc, g_pre, g_post, w_gates, w_a, w_b, w_c, w_out, tm):
    n, d = x.shape
    return pl.pallas_call(
        _merge_kernel,
        grid=(n // tm,),
        in_specs=[_row_spec(tm, d), _row_spec(tm, ya.shape[1]), _row_spec(tm, yb.shape[1]),
                  _row_spec(tm, yc.shape[1]), _const_spec((1, d)), _const_spec((1, d)),
                  _const_spec(w_gates.shape), _const_spec(w_a.shape), _const_spec(w_b.shape),
                  _const_spec(w_c.shape), _const_spec(w_out.shape)],
        out_specs=_row_spec(tm, d),
        out_shape=jax.ShapeDtypeStruct((n, d), F32),
        compiler_params=_params(1),
        name="merge",
    )(x, ya, yb, yc, g_pre, g_post, w_gates, w_a, w_b, w_c, w_out)


def _head_masked(q, width):
    lane_head = lax.broadcasted_iota(jnp.int32, q.shape, 1) // width
    zero = jnp.zeros_like(q)
    return [jnp.where(lane_head == h, q, zero) for h in range(q.shape[1] // width)]


def _strict_upper_ones(n):
    j = lax.broadcasted_iota(jnp.int32, (n, n), 0)
    s = lax.broadcasted_iota(jnp.int32, (n, n), 1)
    return (j > s).astype(BF16)


def _sb_tile(qm, kt, vt, carry, acc, ones_ut, causal):
    z = _dot_nt(qm, kt)
    t = jnp.log1p(jnp.exp(-jnp.abs(z)))
    log_keep = -jnp.maximum(z, 0.0) - t
    log_beta = jnp.minimum(z, 0.0) - t
    if causal is not None:
        log_keep = jnp.where(causal, log_keep, 0.0)
    hi = log_keep.astype(BF16)
    lo = (log_keep - hi.astype(F32)).astype(BF16)
    tail = _dot(hi, ones_ut) + _dot(lo, ones_ut)
    w = jnp.exp(log_beta + tail + carry)
    if causal is not None:
        w = jnp.where(causal, w, 0.0)
    acc = acc + _dot(w.astype(BF16), vt)
    carry = carry + jnp.sum(log_keep, axis=1, keepdims=True)
    return carry, acc


def _sb_alive(carry):
    return jnp.max(carry) > SB_DEAD


def _softmax_tile(s, vt, m, l, acc):
    m_new = jnp.maximum(m, jnp.max(s, axis=1, keepdims=True))
    alpha = jnp.exp(m - m_new)
    p = jnp.exp(s - m_new)
    l = alpha * l + jnp.sum(p, axis=1, keepdims=True)
    acc = alpha * acc + _dot(p.astype(BF16), vt)
    return m_new, l, acc


def _select_heads(parts, width):
    lane_head = lax.broadcasted_iota(jnp.int32, parts[0].shape, 1) // width
    out = parts[0]
    for h in range(1, len(parts)):
        out = jnp.where(lane_head == h, parts[h], out)
    return out


def _band_kernel(q_ref, k0_ref, k1_ref, k2_ref, v0_ref, v1_ref, v2_ref, bias_ref, o_ref):
    i = pl.program_id(0)
    tq = q_ref.shape[0]
    kwin = jnp.concatenate([k0_ref[...], k1_ref[...], k2_ref[...]], axis=0)
    vwin = jnp.concatenate([v0_ref[...], v1_ref[...], v2_ref[...]], axis=0)
    col = lax.broadcasted_iota(jnp.int32, (tq, 3 * tq), 1)
    in_range = col >= (2 - i) * tq
    parts = []
    for h, qm in enumerate(_head_masked(q_ref[...], HEAD_DIM)):
        s = jnp.where(in_range, _dot_nt(qm, kwin) + bias_ref[h], NEG)
        m = jnp.max(s, axis=1, keepdims=True)
        p = jnp.exp(s - m)
        l = jnp.sum(p, axis=1, keepdims=True)
        parts.append(_dot(p.astype(BF16), vwin) / l)
    o_ref[...] = _select_heads(parts, HEAD_DIM).astype(BF16)


def _band_attention(q, k, v, bias, tq):
    n, w = q.shape
    blk = lambda back: pl.BlockSpec((tq, w), lambda i: (jnp.maximum(i - back, 0), 0))
    return pl.pallas_call(
        _band_kernel,
        grid=(n // tq,),
        in_specs=[blk(0), blk(2), blk(1), blk(0), blk(2), blk(1), blk(0), _const_spec(bias.shape)],
        out_specs=blk(0),
        out_shape=jax.ShapeDtypeStruct((n, w), BF16),
        compiler_params=_params(1),
        name="band_attn",
    )(q, k, k, k, v, v, v, bias)


def _sb_kernel(q_ref, k_ref, v_ref, o_ref, *, tile):
    i = pl.program_id(0)
    row = lax.broadcasted_iota(jnp.int32, (tile, tile), 0)
    col = lax.broadcasted_iota(jnp.int32, (tile, tile), 1)
    causal = col < row
    ones_ut = _strict_upper_ones(tile)
    width = q_ref.shape[1]
    parts = []
    for qm in _head_masked(q_ref[...], HEAD_DIM):
        def tile_at(kb, carry, acc, mask, qm=qm):
            start = pl.multiple_of(kb * tile, tile)
            return _sb_tile(qm, k_ref[pl.ds(start, tile), :], v_ref[pl.ds(start, tile), :],
                            carry, acc, ones_ut, mask)
        state = tile_at(i, jnp.zeros((tile, 1), F32), jnp.zeros((tile, width), F32), causal)
        state = lax.while_loop(
            lambda st: (st[0] < i) & _sb_alive(st[1]),
            lambda st: (st[0] + 1,) + tile_at(i - 1 - st[0], st[1], st[2], None),
            (jnp.int32(0),) + state)
        parts.append(state[2])
    o_ref[...] = _select_heads(parts, HEAD_DIM).astype(BF16)


def _sb_attention(q, k, v, tile):
    n, w = q.shape
    return pl.pallas_call(
        functools.partial(_sb_kernel, tile=tile),
        grid=(n // tile,),
        in_specs=[_row_spec(tile, w), _const_spec((n, w)), _const_spec((n, w))],
        out_specs=_row_spec(tile, w),
        out_shape=jax.ShapeDtypeStruct((n, w), BF16),
        compiler_params=_params(1),
        name="sb_attn",
    )(q, k, v)


def _mla_kernel(qt_ref, k_ref, vt_ref, ot_ref, *, tile):
    i = pl.program_id(1)
    key = lax.broadcasted_iota(jnp.int32, (tile, tile), 0)
    qry = lax.broadcasted_iota(jnp.int32, (tile, tile), 1)
    visible = (key // CHUNK) <= (qry // CHUNK)
    heads = range(2)

    def scores(kb):
        return tuple(_dot(k_ref[kb, :, e * C_PAD:(e + 1) * C_PAD], qt_ref[e * C_PAD:(e + 1) * C_PAD, :])
                     for e in heads)

    def weighted_values(kb, p):
        return tuple(_dot(vt_ref[kb, e * C_V:(e + 1) * C_V, :], p[e]) for e in heads)

    def softmax_update(s, m, l):
        m_new = jnp.maximum(m, jnp.max(s, axis=0, keepdims=True))
        alpha = jnp.exp(m - m_new)
        p = jnp.exp(s - m_new)
        return m_new, alpha * l + jnp.sum(p, axis=0, keepdims=True), alpha, p.astype(BF16)

    def body(kb, carry):
        s, p_prev, a_prev, m, l, acc = carry
        s_next = scores(kb + 1)
        pv = weighted_values(jnp.maximum(kb - 1, 0), p_prev)
        acc = tuple(a_prev[e] * acc[e] + pv[e] for e in heads)
        upd = [softmax_update(s[e], m[e], l[e]) for e in heads]
        return (s_next, tuple(u[3] for u in upd), tuple(u[2] for u in upd),
                tuple(u[0] for u in upd), tuple(u[1] for u in upd), acc)

    init = (scores(0),
            tuple(jnp.zeros((tile, tile), BF16) for _ in heads),
            tuple(jnp.ones((1, tile), F32) for _ in heads),
            tuple(jnp.full((1, tile), NEG, F32) for _ in heads),
            tuple(jnp.zeros((1, tile), F32) for _ in heads),
            tuple(jnp.zeros((C_V, tile), F32) for _ in heads))
    s, p_prev, a_prev, m, l, acc = lax.fori_loop(0, i, body, init)
    pv = weighted_values(jnp.maximum(i - 1, 0), p_prev)
    upd = [softmax_update(jnp.where(visible, s[e], NEG), m[e], l[e]) for e in heads]
    pv_diag = weighted_values(i, tuple(u[3] for u in upd))
    out = [(upd[e][2] * (a_prev[e] * acc[e] + pv[e]) + pv_diag[e]) / upd[e][1] for e in heads]
    ot_ref[...] = jnp.concatenate(out, axis=0).astype(BF16)


def _mla_attention(q, k, v, tile):
    n = q.shape[0]
    pairs = H_C // 2
    n_tiles = n // tile
    qt = q.T
    k3 = k.reshape(n_tiles, tile, H_C * C_PAD)
    vt3 = jnp.swapaxes(v.reshape(n_tiles, tile, H_C * C_V), 1, 2)
    out_t = pl.pallas_call(
        functools.partial(_mla_kernel, tile=tile),
        grid=(pairs, n_tiles),
        in_specs=[pl.BlockSpec((2 * C_PAD, tile), lambda p, i: (p, i)),
                  pl.BlockSpec((n_tiles, tile, 2 * C_PAD), lambda p, i: (0, 0, p),
                               pipeline_mode=pl.Buffered(1)),
                  pl.BlockSpec((n_tiles, 2 * C_V, tile), lambda p, i: (0, p, 0),
                               pipeline_mode=pl.Buffered(1))],
        out_specs=pl.BlockSpec((2 * C_V, tile), lambda p, i: (p, i)),
        out_shape=jax.ShapeDtypeStruct((H_C * C_V, n), BF16),
        compiler_params=_params(2),
        name="mla_attn",
    )(qt, k3, vt3)
    return out_t.T


def _sample_attn_kernel(qa_ref, ka_ref, va_ref, cak_ref, cav_ref, biasc_ref, biasn_ref,
                        qb_ref, kb_ref, vb_ref, cbk_ref, cbv_ref,
                        qc_ref, kc_ref, vc_ref, ckv_ref, ckr_ref, wukv_ref,
                        ya_ref, yb_ref, yc_ref, *, tile):
    ds = qa_ref.shape[1]
    past = cbk_ref.shape[1]

    ka, va, cak, cav = ka_ref[0], va_ref[0], cak_ref[0], cav_ref[0]
    parts = []
    for h, qm in enumerate(_head_masked(qa_ref[0], HEAD_DIM)):
        s_c = _dot_nt(qm, cak) + biasc_ref[h]
        s_n = _dot_nt(qm, ka) + biasn_ref[h]
        m = jnp.maximum(jnp.max(s_c, axis=1, keepdims=True), jnp.max(s_n, axis=1, keepdims=True))
        p_c = jnp.exp(s_c - m)
        p_n = jnp.exp(s_n - m)
        l = jnp.sum(p_c, axis=1, keepdims=True) + jnp.sum(p_n, axis=1, keepdims=True)
        parts.append((_dot(p_c.astype(BF16), cav) + _dot(p_n.astype(BF16), va)) / l)
    ya_ref[0] = _select_heads(parts, HEAD_DIM).astype(BF16)

    row = lax.broadcasted_iota(jnp.int32, (ds, ds), 0)
    col = lax.broadcasted_iota(jnp.int32, (ds, ds), 1)
    causal = col < row
    ones_new = _strict_upper_ones(ds)
    ones_ut = _strict_upper_ones(tile)
    kb, vb = kb_ref[0], vb_ref[0]
    n_tiles = past // tile
    parts = []
    for qm in _head_masked(qb_ref[0], HEAD_DIM):
        state = _sb_tile(qm, kb, vb, jnp.zeros((ds, 1), F32), jnp.zeros((ds, W_AB), F32),
                         ones_new, causal)

        def cache_tile(st, qm=qm):
            start = pl.multiple_of((n_tiles - 1 - st[0]) * tile, tile)
            return (st[0] + 1,) + _sb_tile(qm, cbk_ref[0, pl.ds(start, tile), :],
                                           cbv_ref[0, pl.ds(start, tile), :], st[1], st[2], ones_ut, None)
        state = lax.while_loop(lambda st: (st[0] < n_tiles) & _sb_alive(st[1]), cache_tile,
                               (jnp.int32(0),) + state)
        parts.append(state[2])
    yb_ref[0] = _select_heads(parts, HEAD_DIM).astype(BF16)

    wk = H_C * C_PAD
    pairs = H_C // 2
    qc = qc_ref[0]
    qms = []
    for p in range(pairs):
        qms.extend(_head_masked(qc[:, p * 2 * C_PAD:(p + 1) * 2 * C_PAD], C_PAD))

    def attend(kall, vall, state):
        out = []
        for h, (m, l, acc) in enumerate(state):
            p = h // 2
            s = _dot_nt(qms[h], kall[:, p * 2 * C_PAD:(p + 1) * 2 * C_PAD])
            out.append(_softmax_tile(s, vall[:, p * 2 * C_V:(p + 1) * 2 * C_V], m, l, acc))
        return tuple(out)

    def cache_step(j, state):
        start = pl.multiple_of(j * tile, tile)
        lat = ckv_ref[0, pl.ds(start, tile), :]
        kv = _dot(lat, wukv_ref[...])
        krb = ckr_ref[0, pl.ds(start, tile), :]
        kall = jnp.concatenate([kv[:, h * C_PAD:(h + 1) * C_PAD] + krb for h in range(H_C)],
                               axis=1).astype(BF16)
        return attend(kall, kv[:, wk:].astype(BF16), state)

    init = tuple((jnp.full((ds, 1), NEG, F32), jnp.zeros((ds, 1), F32),
                  jnp.zeros((ds, 2 * C_V), F32)) for _ in range(H_C))
    state = lax.fori_loop(0, n_tiles, cache_step, init)
    state = attend(kc_ref[0], vc_ref[0], state)
    outs = []
    for p in range(pairs):
        outs.append(_select_heads([state[2 * p + e][2] / state[2 * p + e][1] for e in range(2)], C_V))
    yc_ref[0] = jnp.concatenate(outs, axis=1).astype(BF16)


def _sample_attention(new, caches, bias_c, bias_n, w_ukv, tile):
    qa, ka, va, qb, kb, vb, qc, kc, vc = new
    cak, cav, cbk, cbv, ckv, ckr = caches
    nb, ds, _ = qa.shape
    req = lambda a: pl.BlockSpec((1,) + a.shape[1:], lambda b: (b, 0, 0))
    ins = [qa, ka, va, cak, cav, bias_c, bias_n, qb, kb, vb, cbk, cbv, qc, kc, vc, ckv, ckr, w_ukv]
    specs = [req(a) for a in ins]
    for idx in (5, 6, 17):
        specs[idx] = _const_spec(ins[idx].shape)
    out_shape = [jax.ShapeDtypeStruct((nb, ds, W_AB), BF16), jax.ShapeDtypeStruct((nb, ds, W_AB), BF16),
                 jax.ShapeDtypeStruct((nb, ds, H_C * C_V), BF16)]
    return pl.pallas_call(
        functools.partial(_sample_attn_kernel, tile=tile),
        grid=(nb,),
        in_specs=specs,
        out_specs=[req(s) for s in out_shape],
        out_shape=out_shape,
        compiler_params=_params(1),
        name="sample_attn",
    )(*ins)


def _rope_tables(pos):
    half = C_ROPE // 2
    freq = ROPE_THETA ** (-jnp.arange(half, dtype=F32) / half)
    ang = pos.astype(F32)[:, None] * freq[None, :]
    cos, sin = jnp.cos(ang), jnp.sin(ang)
    n = pos.shape[0]
    z = lambda w: jnp.zeros((n, w), F32)
    tc = jnp.concatenate([jnp.ones((n, C_NOPE), F32), cos, cos, z(C_PAD - C_NOPE - C_ROPE)], axis=1)
    ts1 = jnp.concatenate([z(C_NOPE + half), sin, z(C_PAD - C_NOPE - C_ROPE)], axis=1)
    ts2 = jnp.concatenate([z(C_NOPE), -sin, z(C_PAD - C_NOPE - half)], axis=1)
    return tc, ts1, ts2


def _band_bias(rel_bias, q0, nq, k0, nk):
    period = 1 << int(np.ceil(np.log2(nq + nk)))
    d = np.arange(period)
    d = np.where(d >= nk, d - period, d)
    idx = np.clip((q0 - k0) - d, -REL_CLIP, REL_CLIP) + REL_CLIP
    g = rel_bias.astype(F32)[:, idx]
    toe = jnp.tile(g, (1, nq))[:, :nq * (period - 1)].reshape(-1, nq, period - 1)[:, :, :nk]
    q_pos, k_pos = q0 + np.arange(nq), k0 + np.arange(nk)
    qc, kc = q_pos[:, None] // CHUNK, k_pos[None, :] // CHUNK
    vis = (kc <= qc) & (kc >= qc - BAND_CHUNKS)
    return jnp.where(jnp.asarray(vis)[None], toe, NEG)


def _layer_weights(l, w_in, w_uq, w_uk, w_uv):
    d = w_in.shape[1]
    q_rank, kv_rank = w_uq.shape[1], w_uk.shape[1]
    o = 6 * W_AB + q_rank + kv_rank
    zc = lambda w: jnp.zeros((d, w), w_in.dtype)
    w_main = jnp.concatenate([w_in[l, :, :o], zc(C_NOPE), w_in[l, :, o:o + C_ROPE],
                              zc(C_PAD - C_NOPE - C_ROPE)], axis=1).astype(BF16)
    w_gates = w_in[l, :, o + C_ROPE:].astype(BF16)
    pad_heads = lambda w: jnp.pad(w, ((0, 0), (0, 0), (0, C_PAD - w.shape[2]))).reshape(w.shape[0], -1)
    w_uq_p = pad_heads(w_uq[l]).astype(BF16)
    w_ukv = jnp.concatenate([pad_heads(w_uk[l]), w_uv[l].reshape(kv_rank, -1)], axis=1).astype(BF16)
    return w_main, w_gates, w_uq_p, w_ukv


def kernel(x_prompt, x_sample, cache_a_k, cache_a_v, cache_b_k, cache_b_v, cache_c_kv, cache_c_kr, ffn1_norm_pre, ffn1_norm_post, ffn1_w_gate, ffn1_w_up, ffn1_w_down, mix_norm_pre, mix_norm_post, w_in, cq_norm, ckv_norm, w_uq, w_uk, w_uv, rel_bias_a, w_br_a, w_br_b, w_br_c, w_out, ffn2_norm_pre, ffn2_norm_post, ffn2_w_gate, ffn2_w_up, ffn2_w_down):
    batch, seq, d = x_prompt.shape
    nb, ds, _ = x_sample.shape
    depth = w_in.shape[0]
    past = cache_b_k.shape[2]
    win_cache = cache_a_k.shape[2]
    assert batch == 1, "prompt group is a single sequence"
    tile = 256
    tm = 512 if seq % 512 == 0 else tile
    assert seq % tile == 0 and past % tile == 0 and tile % CHUNK == 0 and 2 * tile >= WIN_A
    assert past % CHUNK + ds <= CHUNK
    ns = nb * ds

    xp = x_prompt.reshape(seq, d)
    xs = x_sample.reshape(ns, d)
    tabs_p = _rope_tables(jnp.arange(seq))
    tabs_s = _rope_tables(jnp.tile(past + jnp.arange(ds), nb))

    row = lambda g, l: g[l][None, :]
    base = 2 * tile

    states_p, states_s = [], []
    for l in range(depth):
        w_main, w_gates, w_uq_p, w_ukv = _layer_weights(l, w_in, w_uq, w_uk, w_uv)
        ffn1 = (row(ffn1_norm_pre, l), row(ffn1_norm_post, l), ffn1_w_gate[l].astype(BF16),
                ffn1_w_up[l].astype(BF16), ffn1_w_down[l].astype(BF16))
        ffn2 = (row(ffn2_norm_pre, l), row(ffn2_norm_post, l), ffn2_w_gate[l].astype(BF16),
                ffn2_w_up[l].astype(BF16), ffn2_w_down[l].astype(BF16))
        mix_in = (row(mix_norm_pre, l), w_main, row(cq_norm, l), row(ckv_norm, l), w_uq_p, w_ukv)
        mrg = (row(mix_norm_pre, l), row(mix_norm_post, l), w_gates, w_br_a[l].astype(BF16),
               w_br_b[l].astype(BF16), w_br_c[l].astype(BF16), w_out[l].astype(BF16))
        bias_p = _band_bias(rel_bias_a[l], base, tile, base - 2 * tile, 3 * tile)
        bias_c = _band_bias(rel_bias_a[l], past, ds, past - win_cache, win_cache)
        bias_n = _band_bias(rel_bias_a[l], past, ds, past, ds)

        xp = _ffn(xp, *ffn1, tm)
        (qa, kab, vab, qb, kbb, vbb, ka, va, kb, vb, ckv, krb, qc, kc, vc) = _mixer_in(
            xp, *mix_in, tabs_p, tm)
        ya = _band_attention(qa, kab, vab, bias_p, tile)
        yb = _sb_attention(qb, kbb, vbb, tile)
        yc = _mla_attention(qc, kc, vc, tile)
        xp = _merge(xp, ya, yb, yc, *mrg, tm)
        xp = _ffn(xp, *ffn2, tm)
        win = min(WIN_A, seq)
        heads = lambda t, h: t.reshape(1, t.shape[0], h, t.shape[1] // h)
        states_p.append((heads(ka[-win:], H_A), heads(va[-win:], H_A), heads(kb, H_B), heads(vb, H_B),
                         ckv[None], krb[None, :, C_NOPE:C_NOPE + C_ROPE]))

        xs = _ffn(xs, *ffn1, ns)
        (qa, kab, vab, qb, kbb, vbb, ka, va, kb, vb, ckv, krb, qc, kc, vc) = _mixer_in(
            xs, *mix_in, tabs_s, ns)
        per_req = lambda t: t.reshape(nb, ds, t.shape[1])
        flat = lambda c: c.reshape(nb, c.shape[1], -1).astype(BF16)
        ckr_blk = jnp.pad(cache_c_kr[l], ((0, 0), (0, 0), (C_NOPE, C_PAD - C_NOPE - C_ROPE)))
        caches = (flat(cache_a_k[l]), flat(cache_a_v[l]), flat(cache_b_k[l]), flat(cache_b_v[l]),
                  cache_c_kv[l].astype(BF16), ckr_blk)
        new = tuple(per_req(t) for t in (qa, kab, vab, qb, kbb, vbb, qc, kc, vc))
        ya, yb, yc = _sample_attention(new, caches, bias_c, bias_n, w_ukv, tile)
        xs = _merge(xs, ya.reshape(ns, -1), yb.reshape(ns, -1), yc.reshape(ns, -1), *mrg, ns)
        xs = _ffn(xs, *ffn2, ns)
        heads_s = lambda t, h: t.reshape(nb, ds, h, t.shape[1] // h)
        states_s.append((heads_s(ka, H_A), heads_s(va, H_A), heads_s(kb, H_B), heads_s(vb, H_B),
                         ckv.reshape(nb, ds, -1), krb[:, C_NOPE:C_NOPE + C_ROPE].reshape(nb, ds, C_ROPE)))

    stack = lambda states, i: jnp.stack([s[i] for s in states], axis=0)
    return ((xp.reshape(batch, seq, d), xs.reshape(nb, ds, d))
            + tuple(stack(states_p, i) for i in range(6))
            + tuple(stack(states_s, i) for i in range(6)))
```

```python
import functools

import numpy as np
import jax
import jax.numpy as jnp
from jax import lax
from jax.experimental import pallas as pl
from jax.experimental.pallas import tpu as pltpu

F32 = jnp.float32
BF16 = jnp.bfloat16

CHUNK = 64
BAND_CHUNKS = 8
WIN_A = BAND_CHUNKS * CHUNK
HEAD_DIM = 64
H_A = 4
H_B = 4
H_C = 8
C_NOPE = 64
C_ROPE = 32
C_V = 64
REL_CLIP = 128
ROPE_THETA = 10000.0
EPS = 1e-6
N_BRANCH = 3
W_AB = H_A * HEAD_DIM
C_PAD = 128
ATTN_SCALE = HEAD_DIM ** -0.5
MLA_SCALE = (C_NOPE + C_ROPE) ** -0.5
LOG2E = float(np.log2(np.e))
C_VX = C_V + 16
SAMPLE_LATENT_TILE = 1024
NEG = -1e30
SB_DEAD = -104.0

LANES = 128
VMEM_LIMIT = 52 * 1024 * 1024

NT_DIMS = (((1,), (1,)), ((), ()))


def _rms(x, g):
    return x * lax.rsqrt(jnp.mean(x * x, axis=-1, keepdims=True) + EPS) * g


def _dot(a, b):
    return jnp.dot(a, b, preferred_element_type=F32)


def _dot_nt(a, b):
    return lax.dot_general(a, b, NT_DIMS, preferred_element_type=F32)


def _const_spec(shape):
    zeros = (0,) * len(shape)
    return pl.BlockSpec(shape, lambda *_: zeros, pipeline_mode=pl.Buffered(1))


def _row_spec(tm, width):
    return pl.BlockSpec((tm, width), lambda i: (i, 0))


def _params(n_grid):
    return pltpu.CompilerParams(dimension_semantics=("arbitrary",) * n_grid,
                                vmem_limit_bytes=VMEM_LIMIT)


def _ffn_kernel(x_ref, gpre_ref, gpost_ref, wg_ref, wu_ref, wd_ref, o_ref, *, f_chunk):
    x = x_ref[...]
    h = _rms(x, gpre_ref[...]).astype(BF16)
    d_ff = wg_ref.shape[1]
    y = jnp.zeros(x.shape, F32)
    for c in range(d_ff // f_chunk):
        sl = slice(c * f_chunk, (c + 1) * f_chunk)
        g = _dot(h, wg_ref[:, sl])
        u = _dot(h, wu_ref[:, sl])
        a = (g * jax.nn.sigmoid(g) * u).astype(BF16)
        y = y + _dot(a, wd_ref[sl, :])
    o_ref[...] = x + 0.5 * _rms(y, gpost_ref[...])


def _ffn(x, g_pre, g_post, w_gate, w_up, w_down, tm):
    n, d = x.shape
    d_ff = w_gate.shape[1]
    f_chunk = d_ff // 2 if (d_ff // 2) % LANES == 0 else d_ff
    return pl.pallas_call(
        functools.partial(_ffn_kernel, f_chunk=f_chunk),
        grid=(n // tm,),
        in_specs=[_row_spec(tm, d), _const_spec((1, d)), _const_spec((1, d)),
                  _const_spec((d, d_ff)), _const_spec((d, d_ff)), _const_spec((d_ff, d))],
        out_specs=_row_spec(tm, d),
        out_shape=jax.ShapeDtypeStruct((n, d), F32),
        compiler_params=_params(1),
        name="ffn",
    )(x, g_pre, g_post, w_gate, w_up, w_down)


def _rope(blk, tc, ts1, ts2):
    return blk * tc + pltpu.roll(blk, 16, 1) * ts1 + pltpu.roll(blk, C_PAD - 16, 1) * ts2


def _mixer_in_kernel(x_ref, gpre_ref, win_ref, gcq_ref, gckv_ref, wuq_ref, wukv_ref,
                     tc_ref, ts1_ref, ts2_ref,
                     qa_ref, kab_ref, vab_ref, qb_ref, kbb_ref, vbb_ref,
                     ka_ref, va_ref, kb_ref, vb_ref, ckv_ref, krb_ref,
                     qc_ref, kc_ref, vc_ref):
    w = W_AB
    q_rank = gcq_ref.shape[1]
    kv_rank = gckv_ref.shape[1]
    u = _rms(x_ref[...], gpre_ref[...]).astype(BF16)
    proj = _dot(u, win_ref[...])
    qa, ka, va = proj[:, 0:w], proj[:, w:2 * w], proj[:, 2 * w:3 * w]
    qb, kb, vb = proj[:, 3 * w:4 * w], proj[:, 4 * w:5 * w], proj[:, 5 * w:6 * w]
    o = 6 * w
    cq = proj[:, o:o + q_rank]
    ckv = proj[:, o + q_rank:o + q_rank + kv_rank]
    krb = proj[:, o + q_rank + kv_rank:o + q_rank + kv_rank + C_PAD]
    qa_ref[...] = (qa * ATTN_SCALE).astype(BF16)
    qb_ref[...] = (qb * ATTN_SCALE).astype(BF16)
    ka_ref[...] = ka
    va_ref[...] = va
    kb_ref[...] = kb
    vb_ref[...] = vb
    kab_ref[...] = ka.astype(BF16)
    vab_ref[...] = va.astype(BF16)
    kbb_ref[...] = kb.astype(BF16)
    vbb_ref[...] = vb.astype(BF16)
    tc, ts1, ts2 = tc_ref[...], ts1_ref[...], ts2_ref[...]
    cqn = _rms(cq, gcq_ref[...]).astype(BF16)
    qall = _dot(cqn, wuq_ref[...])
    ckvn = _rms(ckv, gckv_ref[...])
    ckv_ref[...] = ckvn
    kv = _dot(ckvn.astype(BF16), wukv_ref[...])
    krot = _rope(krb, tc, ts1, ts2)
    krb_ref[...] = krot
    for h in range(H_C):
        sl = slice(h * C_PAD, (h + 1) * C_PAD)
        qc_ref[:, sl] = (_rope(qall[:, sl], tc, ts1, ts2) * (MLA_SCALE * LOG2E)).astype(BF16)
        kc_ref[:, sl] = (kv[:, sl] + krot).astype(BF16)
    vc_ref[...] = kv[:, H_C * C_PAD:].astype(BF16)


def _mixer_in(x, g_pre, w_main, g_cq, g_ckv, w_uq, w_ukv, tabs, tm):
    n, d = x.shape
    q_rank, kv_rank = g_cq.shape[1], g_ckv.shape[1]
    wc = H_C * C_PAD
    wv = H_C * C_V
    bf = lambda width: jax.ShapeDtypeStruct((n, width), BF16)
    f32 = lambda width: jax.ShapeDtypeStruct((n, width), F32)
    out_shape = ([bf(W_AB)] * 6 + [f32(W_AB)] * 4 + [f32(kv_rank), f32(C_PAD)]
                 + [bf(wc), bf(wc), bf(wv)])
    out_specs = ([_row_spec(tm, W_AB)] * 10 + [_row_spec(tm, kv_rank), _row_spec(tm, C_PAD)]
                 + [_row_spec(tm, wc), _row_spec(tm, wc), _row_spec(tm, wv)])
    return pl.pallas_call(
        _mixer_in_kernel,
        grid=(n // tm,),
        in_specs=[_row_spec(tm, d), _const_spec((1, d)), _const_spec(w_main.shape),
                  _const_spec((1, q_rank)), _const_spec((1, kv_rank)),
                  _const_spec(w_uq.shape), _const_spec(w_ukv.shape),
                  _row_spec(tm, C_PAD), _row_spec(tm, C_PAD), _row_spec(tm, C_PAD)],
        out_specs=out_specs,
        out_shape=out_shape,
        compiler_params=_params(1),
        name="mixer_in",
    )(x, g_pre, w_main, g_cq, g_ckv, w_uq, w_ukv, *tabs)


def _merge_kernel(x_ref, ya_ref, yb_ref, yc_ref, gpre_ref, gpost_ref, wgates_ref,
                  wa_ref, wb_ref, wc_ref, wout_ref, o_ref):
    x = x_ref[...]
    d = x.shape[1]
    u = _rms(x, gpre_ref[...]).astype(BF16)
    m = jnp.zeros(x.shape, F32)
    for i, (y_ref, w_ref) in enumerate(((ya_ref, wa_ref), (yb_ref, wb_ref), (yc_ref, wc_ref))):
        gate = jax.nn.sigmoid(_dot(u, wgates_ref[:, i * d:(i + 1) * d]))
        m = m + gate * _dot(y_ref[...], w_ref[...])
    mixed = _dot(m.astype(BF16), wout_ref[...])
    o_ref[...] = x + _rms(mixed, gpost_ref[...])


def _merge(x, ya, yb, yc, g_pre, g_post, w_gates, w_a, w_b, w_c, w_out, tm):
    n, d = x.shape
    return pl.pallas_call(
        _merge_kernel,
        grid=(n // tm,),
        in_specs=[_row_spec(tm, d), _row_spec(tm, ya.shape[1]), _row_spec(tm, yb.shape[1]),
                  _row_spec(tm, yc.shape[1]), _const_spec((1, d)), _const_spec((1, d)),
                  _const_spec(w_gates.shape), _const_spec(w_a.shape), _const_spec(w_b.shape),
                  _const_spec(w_c.shape), _const_spec(w_out.shape)],
        out_specs=_row_spec(tm, d),
        out_shape=jax.ShapeDtypeStruct((n, d), F32),
        compiler_params=_params(1),
        name="merge",
    )(x, ya, yb, yc, g_pre, g_post, w_gates, w_a, w_b, w_c, w_out)


def _head_masked(q, width):
    lane_head = lax.broadcasted_iota(jnp.int32, q.shape, 1) // width
    zero = jnp.zeros_like(q)
    return [jnp.where(lane_head == h, q, zero) for h in range(q.shape[1] // width)]


def _strict_upper_ones(n):
    j = lax.broadcasted_iota(jnp.int32, (n, n), 0)
    s = lax.broadcasted_iota(jnp.int32, (n, n), 1)
    return (j > s).astype(BF16)


def _sb_terms(qm, kt, ones_ut, causal):
    z = _dot_nt(qm, kt)
    t = jnp.log1p(jnp.exp(-jnp.abs(z)))
    log_keep = -jnp.maximum(z, 0.0) - t
    log_beta = jnp.minimum(z, 0.0) - t
    if causal is not None:
        log_keep = jnp.where(causal, log_keep, 0.0)
    hi = log_keep.astype(BF16)
    lo = (log_keep - hi.astype(F32)).astype(BF16)
    tail = _dot(hi, ones_ut) + _dot(lo, ones_ut)
    return log_beta + tail, jnp.sum(log_keep, axis=1, keepdims=True)


def _sb_apply(terms, vt, carry, acc, causal):
    logw, total = terms
    w = jnp.exp(logw + carry)
    if causal is not None:
        w = jnp.where(causal, w, 0.0)
    return carry + total, acc + _dot(w.astype(BF16), vt)


def _sb_tile(qm, kt, vt, carry, acc, ones_ut, causal):
    return _sb_apply(_sb_terms(qm, kt, ones_ut, causal), vt, carry, acc, causal)


def _sb_alive(carry):
    return jnp.max(carry) > SB_DEAD


def _softmax_tile(s, vt, m, l, acc):
    m_new = jnp.maximum(m, jnp.max(s, axis=1, keepdims=True))
    alpha = jnp.exp2(m - m_new)
    p = jnp.exp2(s - m_new)
    l = alpha * l + jnp.sum(p, axis=1, keepdims=True)
    acc = alpha * acc + _dot(p.astype(BF16), vt)
    return m_new, l, acc


def _select_heads(parts, width):
    lane_head = lax.broadcasted_iota(jnp.int32, parts[0].shape, 1) // width
    out = parts[0]
    for h in range(1, len(parts)):
        out = jnp.where(lane_head == h, parts[h], out)
    return out


def _band_kernel(q_ref, k0_ref, k1_ref, k2_ref, v0_ref, v1_ref, v2_ref, bias_ref, o_ref):
    i = pl.program_id(0)
    tq = q_ref.shape[0]
    kwin = jnp.concatenate([k0_ref[...], k1_ref[...], k2_ref[...]], axis=0)
    vwin = jnp.concatenate([v0_ref[...], v1_ref[...], v2_ref[...]], axis=0)
    col = lax.broadcasted_iota(jnp.int32, (tq, 3 * tq), 1)
    in_range = col >= (2 - i) * tq
    parts = []
    for h, qm in enumerate(_head_masked(q_ref[...], HEAD_DIM)):
        s = jnp.where(in_range, _dot_nt(qm, kwin) + bias_ref[h], NEG)
        m = jnp.max(s, axis=1, keepdims=True)
        p = jnp.exp(s - m)
        l = jnp.sum(p, axis=1, keepdims=True)
        parts.append(_dot(p.astype(BF16), vwin) / l)
    o_ref[...] = _select_heads(parts, HEAD_DIM).astype(BF16)


def _band_attention(q, k, v, bias, tq):
    n, w = q.shape
    blk = lambda back: pl.BlockSpec((tq, w), lambda i: (jnp.maximum(i - back, 0), 0))
    return pl.pallas_call(
        _band_kernel,
        grid=(n // tq,),
        in_specs=[blk(0), blk(2), blk(1), blk(0), blk(2), blk(1), blk(0), _const_spec(bias.shape)],
        out_specs=blk(0),
        out_shape=jax.ShapeDtypeStruct((n, w), BF16),
        compiler_params=_params(1),
        name="band_attn",
    )(q, k, k, k, v, v, v, bias)


def _sb_kernel(q_ref, k_ref, v_ref, o_ref, *, tile):
    i = pl.program_id(0)
    row = lax.broadcasted_iota(jnp.int32, (tile, tile), 0)
    col = lax.broadcasted_iota(jnp.int32, (tile, tile), 1)
    causal = col < row
    ones_ut = _strict_upper_ones(tile)
    width = q_ref.shape[1]
    qms = _head_masked(q_ref[...], HEAD_DIM)
    rows = lambda ref, kb: ref[pl.ds(pl.multiple_of(kb * tile, tile), tile), :]

    prev = jnp.maximum(i - 1, 0)
    k_diag, k_prev = rows(k_ref, i), rows(k_ref, prev)
    terms_diag = [_sb_terms(qm, k_diag, ones_ut, causal) for qm in qms]
    terms_prev = [_sb_terms(qm, k_prev, ones_ut, None) for qm in qms]
    v_diag, v_prev = rows(v_ref, i), rows(v_ref, prev)
    parts = []
    for qm, t_diag, t_prev in zip(qms, terms_diag, terms_prev):
        carry, acc = _sb_apply(t_diag, v_diag, jnp.zeros((tile, 1), F32), jnp.zeros((tile, width), F32),
                               causal)
        carry, acc = _sb_apply(t_prev, v_prev, jnp.where(i > 0, carry, NEG), acc, None)
        state = lax.while_loop(
            lambda st: (st[0] < i - 1) & _sb_alive(st[1]),
            lambda st, qm=qm: (st[0] + 1,) + _sb_tile(qm, rows(k_ref, i - 2 - st[0]), rows(v_ref, i - 2 - st[0]),
                                                      st[1], st[2], ones_ut, None),
            (jnp.int32(0), carry, acc))
        parts.append(state[2])
    o_ref[...] = _select_heads(parts, HEAD_DIM).astype(BF16)


def _sb_attention(q, k, v, tile):
    n, w = q.shape
    return pl.pallas_call(
        functools.partial(_sb_kernel, tile=tile),
        grid=(n // tile,),
        in_specs=[_row_spec(tile, w), _const_spec((n, w)), _const_spec((n, w))],
        out_specs=_row_spec(tile, w),
        out_shape=jax.ShapeDtypeStruct((n, w), BF16),
        compiler_params=_params(1),
        name="sb_attn",
    )(q, k, v)


def _mla_kernel(qt_ref, k_ref, vt_ref, ot_ref, s_scr, p_scr, a_scr, m_scr, acc_scr, *, tile):
    i = pl.program_id(1)
    heads = range(2)

    def scores_into(slot, kb):
        for e in heads:
            s_scr[slot, e] = _dot(k_ref[kb, :, e * C_PAD:(e + 1) * C_PAD],
                                  qt_ref[e * C_PAD:(e + 1) * C_PAD, :])

    def add_values(slot, kb):
        for e in heads:
            pv = _dot(vt_ref[kb, e * C_VX:(e + 1) * C_VX, :], p_scr[slot, e])
            acc_scr[e] = a_scr[slot, e] * acc_scr[e] + pv

    def softmax_into(slot, mask):
        for e in heads:
            s = s_scr[slot, e]
            if mask is not None:
                s = jnp.where(mask, s, NEG)
            m = m_scr[e]
            m_new = jnp.maximum(m, jnp.max(s, axis=0, keepdims=True))
            m_scr[e] = m_new
            a_scr[slot, e] = jnp.exp2(m - m_new)
            p_scr[slot, e] = jnp.exp2(s - m_new).astype(BF16)

    def step(kb, cur):
        nxt = 1 - cur
        scores_into(nxt, kb + 1)
        add_values(nxt, jnp.maximum(kb - 1, 0))
        softmax_into(cur, None)

    def finish(cur):
        key = lax.broadcasted_iota(jnp.int32, (tile, tile), 0)
        qry = lax.broadcasted_iota(jnp.int32, (tile, tile), 1)
        add_values(1 - cur, jnp.maximum(i - 1, 0))
        softmax_into(cur, (key // CHUNK) <= (qry // CHUNK))
        add_values(cur, i)
        ot_ref[...] = jnp.concatenate([acc_scr[e, :C_V] / acc_scr[e, C_V:C_V + 1] for e in heads],
                                      axis=0).astype(BF16)

    p_scr[1] = jnp.zeros(p_scr.shape[1:], BF16)
    a_scr[1] = jnp.ones(a_scr.shape[1:], F32)
    m_scr[...] = jnp.full(m_scr.shape, NEG, F32)
    acc_scr[...] = jnp.zeros(acc_scr.shape, F32)
    scores_into(0, 0)

    def two_steps(j, _):
        step(2 * j, 0)
        step(2 * j + 1, 1)
        return 0
    lax.fori_loop(0, i // 2, two_steps, 0)

    @pl.when(i % 2 == 1)
    def _():
        step(i - 1, 0)
        finish(1)

    @pl.when(i % 2 == 0)
    def _():
        finish(0)


def _mla_attention(q, k, v, tile):
    n = q.shape[0]
    pairs = H_C // 2
    n_tiles = n // tile
    qt = q.T
    k3 = k.reshape(n_tiles, tile, H_C * C_PAD)
    vx = jnp.concatenate([v.reshape(n, H_C, C_V), jnp.ones((n, H_C, 1), v.dtype),
                          jnp.zeros((n, H_C, C_VX - C_V - 1), v.dtype)], axis=2)
    vt3 = jnp.swapaxes(vx.reshape(n_tiles, tile, H_C * C_VX), 1, 2)
    out_t = pl.pallas_call(
        functools.partial(_mla_kernel, tile=tile),
        grid=(pairs, n_tiles),
        in_specs=[pl.BlockSpec((2 * C_PAD, tile), lambda p, i: (p, i)),
                  pl.BlockSpec((n_tiles, tile, 2 * C_PAD), lambda p, i: (0, 0, p),
                               pipeline_mode=pl.Buffered(1)),
                  pl.BlockSpec((n_tiles, 2 * C_VX, tile), lambda p, i: (0, p, 0),
                               pipeline_mode=pl.Buffered(1))],
        out_specs=pl.BlockSpec((2 * C_V, tile), lambda p, i: (p, i)),
        out_shape=jax.ShapeDtypeStruct((H_C * C_V, n), BF16),
        scratch_shapes=[pltpu.VMEM((2, 2, tile, tile), F32), pltpu.VMEM((2, 2, tile, tile), BF16),
                        pltpu.VMEM((2, 2, 1, tile), F32), pltpu.VMEM((2, 1, tile), F32),
                        pltpu.VMEM((2, C_VX, tile), F32)],
        compiler_params=_params(2),
        name="mla_attn",
    )(qt, k3, vt3)
    return out_t.T


def _sample_attn_kernel(qa_ref, ka_ref, va_ref, cak_ref, cav_ref, biasc_ref, biasn_ref,
                        qb_ref, kb_ref, vb_ref, cbk_ref, cbv_ref,
                        qc_ref, kc_ref, vc_ref, ckv_ref, ckr_ref, wukv_ref,
                        ya_ref, yb_ref, yc_ref, *, tile):
    ds = qa_ref.shape[1]
    past = cbk_ref.shape[1]

    ka, va, cak, cav = ka_ref[0], va_ref[0], cak_ref[0], cav_ref[0]
    parts = []
    for h, qm in enumerate(_head_masked(qa_ref[0], HEAD_DIM)):
        s_c = _dot_nt(qm, cak) + biasc_ref[h]
        s_n = _dot_nt(qm, ka) + biasn_ref[h]
        m = jnp.maximum(jnp.max(s_c, axis=1, keepdims=True), jnp.max(s_n, axis=1, keepdims=True))
        p_c = jnp.exp(s_c - m)
        p_n = jnp.exp(s_n - m)
        l = jnp.sum(p_c, axis=1, keepdims=True) + jnp.sum(p_n, axis=1, keepdims=True)
        parts.append((_dot(p_c.astype(BF16), cav) + _dot(p_n.astype(BF16), va)) / l)
    ya_ref[0] = _select_heads(parts, HEAD_DIM).astype(BF16)

    row = lax.broadcasted_iota(jnp.int32, (ds, ds), 0)
    col = lax.broadcasted_iota(jnp.int32, (ds, ds), 1)
    causal = col < row
    ones_new = _strict_upper_ones(ds)
    ones_ut = _strict_upper_ones(tile)
    kb, vb = kb_ref[0], vb_ref[0]
    n_tiles = past // tile
    parts = []
    for qm in _head_masked(qb_ref[0], HEAD_DIM):
        state = _sb_tile(qm, kb, vb, jnp.zeros((ds, 1), F32), jnp.zeros((ds, W_AB), F32),
                         ones_new, causal)

        def cache_tile(st, qm=qm):
            start = pl.multiple_of((n_tiles - 1 - st[0]) * tile, tile)
            return (st[0] + 1,) + _sb_tile(qm, cbk_ref[0, pl.ds(start, tile), :],
                                           cbv_ref[0, pl.ds(start, tile), :], st[1], st[2], ones_ut, None)
        state = lax.while_loop(lambda st: (st[0] < n_tiles) & _sb_alive(st[1]), cache_tile,
                               (jnp.int32(0),) + state)
        parts.append(state[2])
    yb_ref[0] = _select_heads(parts, HEAD_DIM).astype(BF16)

    wk = H_C * C_PAD
    pairs = H_C // 2
    qc = qc_ref[0]
    qms = []
    for p in range(pairs):
        qms.extend(_head_masked(qc[:, p * 2 * C_PAD:(p + 1) * 2 * C_PAD], C_PAD))

    def attend(kall, vall, state):
        out = []
        for h, (m, l, acc) in enumerate(state):
            p = h // 2
            s = _dot_nt(qms[h], kall[:, p * 2 * C_PAD:(p + 1) * 2 * C_PAD])
            out.append(_softmax_tile(s, vall[:, p * 2 * C_V:(p + 1) * 2 * C_V], m, l, acc))
        return tuple(out)

    tile_c = SAMPLE_LATENT_TILE if past % SAMPLE_LATENT_TILE == 0 else tile

    def cache_step(j, state):
        start = pl.multiple_of(j * tile_c, tile_c)
        lat = ckv_ref[0, pl.ds(start, tile_c), :]
        kv = _dot(lat, wukv_ref[...])
        krb = ckr_ref[0, pl.ds(start, tile_c), :]
        kall = jnp.concatenate([kv[:, h * C_PAD:(h + 1) * C_PAD] + krb for h in range(H_C)],
                               axis=1).astype(BF16)
        return attend(kall, kv[:, wk:].astype(BF16), state)

    init = tuple((jnp.full((ds, 1), NEG, F32), jnp.zeros((ds, 1), F32),
                  jnp.zeros((ds, 2 * C_V), F32)) for _ in range(H_C))
    state = lax.fori_loop(0, past // tile_c, cache_step, init)
    state = attend(kc_ref[0], vc_ref[0], state)
    outs = []
    for p in range(pairs):
        outs.append(_select_heads([state[2 * p + e][2] / state[2 * p + e][1] for e in range(2)], C_V))
    yc_ref[0] = jnp.concatenate(outs, axis=1).astype(BF16)


def _sample_attention(new, caches, bias_c, bias_n, w_ukv, tile):
    qa, ka, va, qb, kb, vb, qc, kc, vc = new
    cak, cav, cbk, cbv, ckv, ckr = caches
    nb, ds, _ = qa.shape
    req = lambda a: pl.BlockSpec((1,) + a.shape[1:], lambda b: (b, 0, 0))
    ins = [qa, ka, va, cak, cav, bias_c, bias_n, qb, kb, vb, cbk, cbv, qc, kc, vc, ckv, ckr, w_ukv]
    specs = [req(a) for a in ins]
    for idx in (5, 6, 17):
        specs[idx] = _const_spec(ins[idx].shape)
    out_shape = [jax.ShapeDtypeStruct((nb, ds, W_AB), BF16), jax.ShapeDtypeStruct((nb, ds, W_AB), BF16),
                 jax.ShapeDtypeStruct((nb, ds, H_C * C_V), BF16)]
    return pl.pallas_call(
        functools.partial(_sample_attn_kernel, tile=tile),
        grid=(nb,),
        in_specs=specs,
        out_specs=[req(s) for s in out_shape],
        out_shape=out_shape,
        compiler_params=_params(1),
        name="sample_attn",
    )(*ins)


def _rope_tables(pos):
    half = C_ROPE // 2
    freq = ROPE_THETA ** (-jnp.arange(half, dtype=F32) / half)
    ang = pos.astype(F32)[:, None] * freq[None, :]
    cos, sin = jnp.cos(ang), jnp.sin(ang)
    n = pos.shape[0]
    z = lambda w: jnp.zeros((n, w), F32)
    tc = jnp.concatenate([jnp.ones((n, C_NOPE), F32), cos, cos, z(C_PAD - C_NOPE - C_ROPE)], axis=1)
    ts1 = jnp.concatenate([z(C_NOPE + half), sin, z(C_PAD - C_NOPE - C_ROPE)], axis=1)
    ts2 = jnp.concatenate([z(C_NOPE), -sin, z(C_PAD - C_NOPE - half)], axis=1)
    return tc, ts1, ts2


def _band_bias(rel_bias, q0, nq, k0, nk):
    period = 1 << int(np.ceil(np.log2(nq + nk)))
    d = np.arange(period)
    d = np.where(d >= nk, d - period, d)
    idx = np.clip((q0 - k0) - d, -REL_CLIP, REL_CLIP) + REL_CLIP
    g = rel_bias.astype(F32)[:, idx]
    toe = jnp.tile(g, (1, nq))[:, :nq * (period - 1)].reshape(-1, nq, period - 1)[:, :, :nk]
    q_pos, k_pos = q0 + np.arange(nq), k0 + np.arange(nk)
    qc, kc = q_pos[:, None] // CHUNK, k_pos[None, :] // CHUNK
    vis = (kc <= qc) & (kc >= qc - BAND_CHUNKS)
    return jnp.where(jnp.asarray(vis)[None], toe, NEG)


def _layer_weights(l, w_in, w_uq, w_uk, w_uv):
    d = w_in.shape[1]
    q_rank, kv_rank = w_uq.shape[1], w_uk.shape[1]
    o = 6 * W_AB + q_rank + kv_rank
    zc = lambda w: jnp.zeros((d, w), w_in.dtype)
    w_main = jnp.concatenate([w_in[l, :, :o], zc(C_NOPE), w_in[l, :, o:o + C_ROPE],
                              zc(C_PAD - C_NOPE - C_ROPE)], axis=1).astype(BF16)
    w_gates = w_in[l, :, o + C_ROPE:].astype(BF16)
    pad_heads = lambda w: jnp.pad(w, ((0, 0), (0, 0), (0, C_PAD - w.shape[2]))).reshape(w.shape[0], -1)
    w_uq_p = pad_heads(w_uq[l]).astype(BF16)
    w_ukv = jnp.concatenate([pad_heads(w_uk[l]), w_uv[l].reshape(kv_rank, -1)], axis=1).astype(BF16)
    return w_main, w_gates, w_uq_p, w_ukv


def kernel(x_prompt, x_sample, cache_a_k, cache_a_v, cache_b_k, cache_b_v, cache_c_kv, cache_c_kr, ffn1_norm_pre, ffn1_norm_post, ffn1_w_gate, ffn1_w_up, ffn1_w_down, mix_norm_pre, mix_norm_post, w_in, cq_norm, ckv_norm, w_uq, w_uk, w_uv, rel_bias_a, w_br_a, w_br_b, w_br_c, w_out, ffn2_norm_pre, ffn2_norm_post, ffn2_w_gate, ffn2_w_up, ffn2_w_down):
    batch, seq, d = x_prompt.shape
    nb, ds, _ = x_sample.shape
    depth = w_in.shape[0]
    past = cache_b_k.shape[2]
    win_cache = cache_a_k.shape[2]
    assert batch == 1, "prompt group is a single sequence"
    tile = 256
    tm = 512 if seq % 512 == 0 else tile
    tile_c = tm
    assert seq % tile == 0 and past % tile == 0 and tile % CHUNK == 0 and 2 * tile >= WIN_A
    assert past % CHUNK + ds <= CHUNK
    ns = nb * ds

    xp = x_prompt.reshape(seq, d)
    xs = x_sample.reshape(ns, d)
    tabs_p = _rope_tables(jnp.arange(seq))
    tabs_s = _rope_tables(jnp.tile(past + jnp.arange(ds), nb))

    row = lambda g, l: g[l][None, :]
    base = 2 * tile

    states_p, states_s = [], []
    for l in range(depth):
        w_main, w_gates, w_uq_p, w_ukv = _layer_weights(l, w_in, w_uq, w_uk, w_uv)
        ffn1 = (row(ffn1_norm_pre, l), row(ffn1_norm_post, l), ffn1_w_gate[l].astype(BF16),
                ffn1_w_up[l].astype(BF16), ffn1_w_down[l].astype(BF16))
        ffn2 = (row(ffn2_norm_pre, l), row(ffn2_norm_post, l), ffn2_w_gate[l].astype(BF16),
                ffn2_w_up[l].astype(BF16), ffn2_w_down[l].astype(BF16))
        mix_in = (row(mix_norm_pre, l), w_main, row(cq_norm, l), row(ckv_norm, l), w_uq_p, w_ukv)
        mrg = (row(mix_norm_pre, l), row(mix_norm_post, l), w_gates, w_br_a[l].astype(BF16),
               w_br_b[l].astype(BF16), w_br_c[l].astype(BF16), w_out[l].astype(BF16))
        bias_p = _band_bias(rel_bias_a[l], base, tile, base - 2 * tile, 3 * tile)
        bias_c = _band_bias(rel_bias_a[l], past, ds, past - win_cache, win_cache)
        bias_n = _band_bias(rel_bias_a[l], past, ds, past, ds)

        xp = _ffn(xp, *ffn1, tm)
        (qa, kab, vab, qb, kbb, vbb, ka, va, kb, vb, ckv, krb, qc, kc, vc) = _mixer_in(
            xp, *mix_in, tabs_p, tm)
        ya = _band_attention(qa, kab, vab, bias_p, tile)
        yb = _sb_attention(qb, kbb, vbb, tile)
        yc = _mla_attention(qc, kc, vc, tile_c)
        xp = _merge(xp, ya, yb, yc, *mrg, tm)
        xp = _ffn(xp, *ffn2, tm)
        win = min(WIN_A, seq)
        heads = lambda t, h: t.reshape(1, t.shape[0], h, t.shape[1] // h)
        states_p.append((heads(ka[-win:], H_A), heads(va[-win:], H_A), heads(kb, H_B), heads(vb, H_B),
                         ckv[None], krb[None, :, C_NOPE:C_NOPE + C_ROPE]))

        xs = _ffn(xs, *ffn1, ns)
        (qa, kab, vab, qb, kbb, vbb, ka, va, kb, vb, ckv, krb, qc, kc, vc) = _mixer_in(
            xs, *mix_in, tabs_s, ns)
        per_req = lambda t: t.reshape(nb, ds, t.shape[1])
        flat = lambda c: c.reshape(nb, c.shape[1], -1).astype(BF16)
        ckr_blk = jnp.pad(cache_c_kr[l], ((0, 0), (0, 0), (C_NOPE, C_PAD - C_NOPE - C_ROPE)))
        caches = (flat(cache_a_k[l]), flat(cache_a_v[l]), flat(cache_b_k[l]), flat(cache_b_v[l]),
                  cache_c_kv[l].astype(BF16), ckr_blk)
        new = tuple(per_req(t) for t in (qa, kab, vab, qb, kbb, vbb, qc, kc, vc))
        ya, yb, yc = _sample_attention(new, caches, bias_c, bias_n, w_ukv, tile)
        xs = _merge(xs, ya.reshape(ns, -1), yb.reshape(ns, -1), yc.reshape(ns, -1), *mrg, ns)
        xs = _ffn(xs, *ffn2, ns)
        heads_s = lambda t, h: t.reshape(nb, ds, h, t.shape[1] // h)
        states_s.append((heads_s(ka, H_A), heads_s(va, H_A), heads_s(kb, H_B), heads_s(vb, H_B),
                         ckv.reshape(nb, ds, -1), krb[:, C_NOPE:C_NOPE + C_ROPE].reshape(nb, ds, C_ROPE)))

    stack = lambda states, i: jnp.stack([s[i] for s in states], axis=0)
    return ((xp.reshape(batch, seq, d), xs.reshape(nb, ds, d))
            + tuple(stack(states_p, i) for i in range(6))
            + tuple(stack(states_s, i) for i in range(6)))
```

```python
import functools

import numpy as np
import jax
import jax.numpy as jnp
from jax import lax
from jax.experimental import pallas as pl
from jax.experimental.pallas import tpu as pltpu

F32 = jnp.float32
BF16 = jnp.bfloat16

CHUNK = 64
BAND_CHUNKS = 8
WIN_A = BAND_CHUNKS * CHUNK
HEAD_DIM = 64
H_A = 4
H_B = 4
H_C = 8
C_NOPE = 64
C_ROPE = 32
C_V = 64
REL_CLIP = 128
ROPE_THETA = 10000.0
EPS = 1e-6
N_BRANCH = 3
W_AB = H_A * HEAD_DIM
C_PAD = 128
ATTN_SCALE = HEAD_DIM ** -0.5
MLA_SCALE = (C_NOPE + C_ROPE) ** -0.5
LOG2E = float(np.log2(np.e))
C_VX = C_V + 16
MLA_SLOTS = 4
SAMPLE_LATENT_TILE = 1024
NEG = -1e30
SB_DEAD = -104.0

LANES = 128
VMEM_LIMIT = 52 * 1024 * 1024

NT_DIMS = (((1,), (1,)), ((), ()))
TN_DIMS = (((0,), (0,)), ((), ()))


def _rms(x, g):
    return x * lax.rsqrt(jnp.mean(x * x, axis=-1, keepdims=True) + EPS) * g


def _dot(a, b):
    return jnp.dot(a, b, preferred_element_type=F32)


def _dot_nt(a, b):
    return lax.dot_general(a, b, NT_DIMS, preferred_element_type=F32)


def _dot_tn(a, b):
    return lax.dot_general(a, b, TN_DIMS, preferred_element_type=F32)


def _const_spec(shape):
    zeros = (0,) * len(shape)
    return pl.BlockSpec(shape, lambda *_: zeros, pipeline_mode=pl.Buffered(1))


def _row_spec(tm, width):
    return pl.BlockSpec((tm, width), lambda i: (i, 0))


def _params(n_grid):
    return pltpu.CompilerParams(dimension_semantics=("arbitrary",) * n_grid,
                                vmem_limit_bytes=VMEM_LIMIT)


def _ffn_kernel(x_ref, gpre_ref, gpost_ref, wg_ref, wu_ref, wd_ref, o_ref, *, f_chunk):
    x = x_ref[...]
    h = _rms(x, gpre_ref[...]).astype(BF16)
    d_ff = wg_ref.shape[1]
    y = jnp.zeros(x.shape, F32)
    for c in range(d_ff // f_chunk):
        sl = slice(c * f_chunk, (c + 1) * f_chunk)
        g = _dot(h, wg_ref[:, sl])
        u = _dot(h, wu_ref[:, sl])
        a = (g * jax.nn.sigmoid(g) * u).astype(BF16)
        y = y + _dot(a, wd_ref[sl, :])
    o_ref[...] = x + 0.5 * _rms(y, gpost_ref[...])


def _ffn(x, g_pre, g_post, w_gate, w_up, w_down, tm):
    n, d = x.shape
    d_ff = w_gate.shape[1]
    f_chunk = d_ff // 2 if (d_ff // 2) % LANES == 0 else d_ff
    return pl.pallas_call(
        functools.partial(_ffn_kernel, f_chunk=f_chunk),
        grid=(n // tm,),
        in_specs=[_row_spec(tm, d), _const_spec((1, d)), _const_spec((1, d)),
                  _const_spec((d, d_ff)), _const_spec((d, d_ff)), _const_spec((d_ff, d))],
        out_specs=_row_spec(tm, d),
        out_shape=jax.ShapeDtypeStruct((n, d), F32),
        compiler_params=_params(1),
        name="ffn",
    )(x, g_pre, g_post, w_gate, w_up, w_down)


def _rope(blk, tc, ts1, ts2):
    return blk * tc + pltpu.roll(blk, 16, 1) * ts1 + pltpu.roll(blk, C_PAD - 16, 1) * ts2


def _mixer_in_kernel(x_ref, gpre_ref, win_ref, gcq_ref, gckv_ref, wuq_ref, wukv_ref,
                     tc_ref, ts1_ref, ts2_ref,
                     qa_ref, kab_ref, vab_ref, qb_ref, kbb_ref, vbb_ref,
                     ka_ref, va_ref, kb_ref, vb_ref, ckv_ref, krb_ref,
                     qc_ref, kc_ref, vc_ref, *, q_transposed):
    w = W_AB
    q_rank = gcq_ref.shape[1]
    kv_rank = gckv_ref.shape[1]
    u = _rms(x_ref[...], gpre_ref[...]).astype(BF16)
    proj = _dot(u, win_ref[...])
    qa, ka, va = proj[:, 0:w], proj[:, w:2 * w], proj[:, 2 * w:3 * w]
    qb, kb, vb = proj[:, 3 * w:4 * w], proj[:, 4 * w:5 * w], proj[:, 5 * w:6 * w]
    o = 6 * w
    cq = proj[:, o:o + q_rank]
    ckv = proj[:, o + q_rank:o + q_rank + kv_rank]
    krb = proj[:, o + q_rank + kv_rank:o + q_rank + kv_rank + C_PAD]
    qa_ref[...] = (qa * ATTN_SCALE).astype(BF16)
    qb_ref[...] = (qb * ATTN_SCALE).astype(BF16)
    ka_ref[...] = ka
    va_ref[...] = va
    kb_ref[...] = kb
    vb_ref[...] = vb
    kab_ref[...] = ka.astype(BF16)
    vab_ref[...] = va.astype(BF16)
    kbb_ref[...] = kb.astype(BF16)
    vbb_ref[...] = vb.astype(BF16)
    tc, ts1, ts2 = tc_ref[...], ts1_ref[...], ts2_ref[...]
    cqn = _rms(cq, gcq_ref[...]).astype(BF16)
    qall = _dot(cqn, wuq_ref[...])
    ckvn = _rms(ckv, gckv_ref[...])
    ckv_ref[...] = ckvn
    kv = _dot(ckvn.astype(BF16), wukv_ref[...])
    krot = _rope(krb, tc, ts1, ts2)
    krb_ref[...] = krot
    for h in range(H_C):
        sl = slice(h * C_PAD, (h + 1) * C_PAD)
        qh = _rope(qall[:, sl], tc, ts1, ts2) * (MLA_SCALE * LOG2E)
        if q_transposed:
            qc_ref[sl, :] = qh.T.astype(BF16)
        else:
            qc_ref[:, sl] = qh.astype(BF16)
        kc_ref[:, sl] = (kv[:, sl] + krot).astype(BF16)
    if not q_transposed:
        vc_ref[...] = kv[:, H_C * C_PAD:].astype(BF16)
    else:
        ones_row = (lax.broadcasted_iota(jnp.int32, (C_VX - C_V, x_ref.shape[0]), 0) == 0).astype(BF16)
        for p in range(H_C // 2):
            o = H_C * C_PAD + p * 2 * C_V
            vt = kv[:, o:o + 2 * C_V].T
            for e in range(2):
                r = (2 * p + e) * C_VX
                vc_ref[0, r:r + C_V, :] = vt[e * C_V:(e + 1) * C_V, :].astype(BF16)
                vc_ref[0, r + C_V:r + C_VX, :] = ones_row


def _mixer_in(x, g_pre, w_main, g_cq, g_ckv, w_uq, w_ukv, tabs, tm, q_transposed=False):
    n, d = x.shape
    q_rank, kv_rank = g_cq.shape[1], g_ckv.shape[1]
    wc = H_C * C_PAD
    wv = H_C * C_V
    bf = lambda width: jax.ShapeDtypeStruct((n, width), BF16)
    f32 = lambda width: jax.ShapeDtypeStruct((n, width), F32)
    q_shape = jax.ShapeDtypeStruct((wc, n), BF16) if q_transposed else bf(wc)
    q_spec = pl.BlockSpec((wc, tm), lambda i: (0, i)) if q_transposed else _row_spec(tm, wc)
    v_shape = jax.ShapeDtypeStruct((n // tm, H_C * C_VX, tm), BF16) if q_transposed else bf(wv)
    v_spec = (pl.BlockSpec((1, H_C * C_VX, tm), lambda i: (i, 0, 0)) if q_transposed
              else _row_spec(tm, wv))
    out_shape = ([bf(W_AB)] * 6 + [f32(W_AB)] * 4 + [f32(kv_rank), f32(C_PAD)]
                 + [q_shape, bf(wc), v_shape])
    out_specs = ([_row_spec(tm, W_AB)] * 10 + [_row_spec(tm, kv_rank), _row_spec(tm, C_PAD)]
                 + [q_spec, _row_spec(tm, wc), v_spec])
    return pl.pallas_call(
        functools.partial(_mixer_in_kernel, q_transposed=q_transposed),
        grid=(n // tm,),
        in_specs=[_row_spec(tm, d), _const_spec((1, d)), _const_spec(w_main.shape),
                  _const_spec((1, q_rank)), _const_spec((1, kv_rank)),
                  _const_spec(w_uq.shape), _const_spec(w_ukv.shape),
                  _row_spec(tm, C_PAD), _row_spec(tm, C_PAD), _row_spec(tm, C_PAD)],
        out_specs=out_specs,
        out_shape=out_shape,
        compiler_params=_params(1),
        name="mixer_in",
    )(x, g_pre, w_main, g_cq, g_ckv, w_uq, w_ukv, *tabs)


def _merge_kernel(x_ref, ya_ref, yb_ref, yc_ref, gpre_ref, gpost_ref, wgates_ref,
                  wa_ref, wb_ref, wc_ref, wout_ref, o_ref, *, yc_transposed):
    x = x_ref[...]
    d = x.shape[1]
    u = _rms(x, gpre_ref[...]).astype(BF16)
    m = jnp.zeros(x.shape, F32)
    for i, (y_ref, w_ref) in enumerate(((ya_ref, wa_ref), (yb_ref, wb_ref), (yc_ref, wc_ref))):
        gate = jax.nn.sigmoid(_dot(u, wgates_ref[:, i * d:(i + 1) * d]))
        branch_dot = _dot_tn if (yc_transposed and y_ref is yc_ref) else _dot
        m = m + gate * branch_dot(y_ref[...], w_ref[...])
    mixed = _dot(m.astype(BF16), wout_ref[...])
    o_ref[...] = x + _rms(mixed, gpost_ref[...])


def _merge(x, ya, yb, yc, g_pre, g_post, w_gates, w_a, w_b, w_c, w_out, tm, yc_transposed=False):
    n, d = x.shape
    yc_spec = (pl.BlockSpec((yc.shape[0], tm), lambda i: (0, i)) if yc_transposed
               else _row_spec(tm, yc.shape[1]))
    return pl.pallas_call(
        functools.partial(_merge_kernel, yc_transposed=yc_transposed),
        grid=(n // tm,),
        in_specs=[_row_spec(tm, d), _row_spec(tm, ya.shape[1]), _row_spec(tm, yb.shape[1]),
                  yc_spec, _const_spec((1, d)), _const_spec((1, d)),
                  _const_spec(w_gates.shape), _const_spec(w_a.shape), _const_spec(w_b.shape),
                  _const_spec(w_c.shape), _const_spec(w_out.shape)],
        out_specs=_row_spec(tm, d),
        out_shape=jax.ShapeDtypeStruct((n, d), F32),
        compiler_params=_params(1),
        name="merge",
    )(x, ya, yb, yc, g_pre, g_post, w_gates, w_a, w_b, w_c, w_out)


def _head_masked(q, width):
    lane_head = lax.broadcasted_iota(jnp.int32, q.shape, 1) // width
    zero = jnp.zeros_like(q)
    return [jnp.where(lane_head == h, q, zero) for h in range(q.shape[1] // width)]


def _strict_upper_ones(n):
    j = lax.broadcasted_iota(jnp.int32, (n, n), 0)
    s = lax.broadcasted_iota(jnp.int32, (n, n), 1)
    return (j > s).astype(BF16)


def _sb_terms(qm, kt, ones_ut, causal):
    z = _dot_nt(qm, kt)
    t = jnp.log1p(jnp.exp(-jnp.abs(z)))
    log_keep = -jnp.maximum(z, 0.0) - t
    log_beta = jnp.minimum(z, 0.0) - t
    if causal is not None:
        log_keep = jnp.where(causal, log_keep, 0.0)
    hi = log_keep.astype(BF16)
    lo = (log_keep - hi.astype(F32)).astype(BF16)
    tail = _dot(hi, ones_ut) + _dot(lo, ones_ut)
    return log_beta + tail, jnp.sum(log_keep, axis=1, keepdims=True)


def _sb_apply(terms, vt, carry, acc, causal):
    logw, total = terms
    w = jnp.exp(logw + carry)
    if causal is not None:
        w = jnp.where(causal, w, 0.0)
    return carry + total, acc + _dot(w.astype(BF16), vt)


def _sb_tile(qm, kt, vt, carry, acc, ones_ut, causal):
    return _sb_apply(_sb_terms(qm, kt, ones_ut, causal), vt, carry, acc, causal)


def _sb_alive(carry):
    return jnp.max(carry) > SB_DEAD


def _softmax_tile(s, vt, m, l, acc):
    m_new = jnp.maximum(m, jnp.max(s, axis=1, keepdims=True))
    alpha = jnp.exp2(m - m_new)
    p = jnp.exp2(s - m_new)
    l = alpha * l + jnp.sum(p, axis=1, keepdims=True)
    acc = alpha * acc + _dot(p.astype(BF16), vt)
    return m_new, l, acc


def _select_heads(parts, width):
    lane_head = lax.broadcasted_iota(jnp.int32, parts[0].shape, 1) // width
    out = parts[0]
    for h in range(1, len(parts)):
        out = jnp.where(lane_head == h, parts[h], out)
    return out


def _band_kernel(q_ref, k0_ref, k1_ref, k2_ref, v0_ref, v1_ref, v2_ref, bias_ref, o_ref):
    i = pl.program_id(0)
    tq = q_ref.shape[0]
    kwin = jnp.concatenate([k0_ref[...], k1_ref[...], k2_ref[...]], axis=0)
    vwin = jnp.concatenate([v0_ref[...], v1_ref[...], v2_ref[...]], axis=0)
    col = lax.broadcasted_iota(jnp.int32, (tq, 3 * tq), 1)
    in_range = col >= (2 - i) * tq
    parts = []
    for h, qm in enumerate(_head_masked(q_ref[...], HEAD_DIM)):
        s = jnp.where(in_range, _dot_nt(qm, kwin) + bias_ref[h], NEG)
        m = jnp.max(s, axis=1, keepdims=True)
        p = jnp.exp(s - m)
        l = jnp.sum(p, axis=1, keepdims=True)
        parts.append(_dot(p.astype(BF16), vwin) / l)
    o_ref[...] = _select_heads(parts, HEAD_DIM).astype(BF16)


def _band_attention(q, k, v, bias, tq):
    n, w = q.shape
    blk = lambda back: pl.BlockSpec((tq, w), lambda i: (jnp.maximum(i - back, 0), 0))
    return pl.pallas_call(
        _band_kernel,
        grid=(n // tq,),
        in_specs=[blk(0), blk(2), blk(1), blk(0), blk(2), blk(1), blk(0), _const_spec(bias.shape)],
        out_specs=blk(0),
        out_shape=jax.ShapeDtypeStruct((n, w), BF16),
        compiler_params=_params(1),
        name="band_attn",
    )(q, k, k, k, v, v, v, bias)


def _sb_kernel(q_ref, k_ref, v_ref, o_ref, *, tile):
    i = pl.program_id(0)
    row = lax.broadcasted_iota(jnp.int32, (tile, tile), 0)
    col = lax.broadcasted_iota(jnp.int32, (tile, tile), 1)
    causal = col < row
    ones_ut = _strict_upper_ones(tile)
    width = q_ref.shape[1]
    qms = _head_masked(q_ref[...], HEAD_DIM)
    rows = lambda ref, kb: ref[pl.ds(pl.multiple_of(kb * tile, tile), tile), :]

    prev = jnp.maximum(i - 1, 0)
    k_diag, k_prev = rows(k_ref, i), rows(k_ref, prev)
    terms_diag = [_sb_terms(qm, k_diag, ones_ut, causal) for qm in qms]
    terms_prev = [_sb_terms(qm, k_prev, ones_ut, None) for qm in qms]
    v_diag, v_prev = rows(v_ref, i), rows(v_ref, prev)
    parts = []
    for qm, t_diag, t_prev in zip(qms, terms_diag, terms_prev):
        carry, acc = _sb_apply(t_diag, v_diag, jnp.zeros((tile, 1), F32), jnp.zeros((tile, width), F32),
                               causal)
        carry, acc = _sb_apply(t_prev, v_prev, jnp.where(i > 0, carry, NEG), acc, None)
        state = lax.while_loop(
            lambda st: (st[0] < i - 1) & _sb_alive(st[1]),
            lambda st, qm=qm: (st[0] + 1,) + _sb_tile(qm, rows(k_ref, i - 2 - st[0]), rows(v_ref, i - 2 - st[0]),
                                                      st[1], st[2], ones_ut, None),
            (jnp.int32(0), carry, acc))
        parts.append(state[2])
    o_ref[...] = _select_heads(parts, HEAD_DIM).astype(BF16)


def _sb_attention(q, k, v, tile):
    n, w = q.shape
    return pl.pallas_call(
        functools.partial(_sb_kernel, tile=tile),
        grid=(n // tile,),
        in_specs=[_row_spec(tile, w), _const_spec((n, w)), _const_spec((n, w))],
        out_specs=_row_spec(tile, w),
        out_shape=jax.ShapeDtypeStruct((n, w), BF16),
        compiler_params=_params(1),
        name="sb_attn",
    )(q, k, v)


def _mla_kernel(qt_ref, k_ref, vt_ref, ot_ref, s_scr, p_scr, a_scr, m_scr, acc_scr, *, tile):
    i = pl.program_id(1)
    heads = range(2)

    def scores_into(slot, kb):
        for e in heads:
            s_scr[slot, e] = _dot(k_ref[kb, :, e * C_PAD:(e + 1) * C_PAD],
                                  qt_ref[e * C_PAD:(e + 1) * C_PAD, :])

    def add_values(slot, kb):
        for e in heads:
            pv = _dot(vt_ref[kb, e * C_VX:(e + 1) * C_VX, :], p_scr[slot, e])
            acc_scr[e] = a_scr[slot, e] * acc_scr[e] + pv

    def softmax_into(slot, mask):
        for e in heads:
            s = s_scr[slot, e]
            if mask is not None:
                s = jnp.where(mask, s, NEG)
            m = m_scr[e]
            m_new = jnp.maximum(m, jnp.max(s, axis=0, keepdims=True))
            m_scr[e] = m_new
            a_scr[slot, e] = jnp.exp2(m - m_new)
            p_scr[slot, e] = jnp.exp2(s - m_new).astype(BF16)

    def stage(t, j, issue_scores, mask):
        ahead = (j + 2) % MLA_SLOTS
        if issue_scores:
            scores_into(ahead, jnp.minimum(t + 2, i))
        add_values(ahead, jnp.maximum(t - 2, 0))
        softmax_into(j, mask)

    for slot in (2, 3):
        p_scr[slot] = jnp.zeros(p_scr.shape[1:], BF16)
        a_scr[slot] = jnp.ones(a_scr.shape[1:], F32)
    m_scr[...] = jnp.full(m_scr.shape, NEG, F32)
    acc_scr[...] = jnp.zeros(acc_scr.shape, F32)
    scores_into(0, 0)
    scores_into(1, jnp.minimum(1, i))

    def four_stages(g, _):
        for j in range(MLA_SLOTS):
            stage(MLA_SLOTS * g + j, j, True, None)
        return 0
    groups = i // MLA_SLOTS
    lax.fori_loop(0, groups, four_stages, 0)

    key = lax.broadcasted_iota(jnp.int32, (tile, tile), 0)
    qry = lax.broadcasted_iota(jnp.int32, (tile, tile), 1)
    visible = (key // CHUNK) <= (qry // CHUNK)
    for rest in range(MLA_SLOTS):
        @pl.when(i % MLA_SLOTS == rest)
        def _(rest=rest):
            base = MLA_SLOTS * groups
            for j in range(rest):
                stage(base + j, j, j + 2 <= rest, None)
            stage(i, rest, False, visible)
            add_values((rest - 1) % MLA_SLOTS, jnp.maximum(i - 1, 0))
            add_values(rest, i)
            ot_ref[...] = jnp.concatenate([acc_scr[e, :C_V] / acc_scr[e, C_V:C_V + 1] for e in heads],
                                          axis=0).astype(BF16)


def _mla_attention(qt, k, vt3, tile):
    n = k.shape[0]
    pairs = H_C // 2
    n_tiles = n // tile
    k3 = k.reshape(n_tiles, tile, H_C * C_PAD)
    assert vt3.shape == (n_tiles, H_C * C_VX, tile)
    return pl.pallas_call(
        functools.partial(_mla_kernel, tile=tile),
        grid=(pairs, n_tiles),
        in_specs=[pl.BlockSpec((2 * C_PAD, tile), lambda p, i: (p, i)),
                  pl.BlockSpec((n_tiles, tile, 2 * C_PAD), lambda p, i: (0, 0, p),
                               pipeline_mode=pl.Buffered(1)),
                  pl.BlockSpec((n_tiles, 2 * C_VX, tile), lambda p, i: (0, p, 0),
                               pipeline_mode=pl.Buffered(1))],
        out_specs=pl.BlockSpec((2 * C_V, tile), lambda p, i: (p, i)),
        out_shape=jax.ShapeDtypeStruct((H_C * C_V, n), BF16),
        scratch_shapes=[pltpu.VMEM((MLA_SLOTS, 2, tile, tile), F32), pltpu.VMEM((MLA_SLOTS, 2, tile, tile), BF16),
                        pltpu.VMEM((MLA_SLOTS, 2, 1, tile), F32), pltpu.VMEM((2, 1, tile), F32),
                        pltpu.VMEM((2, C_VX, tile), F32)],
        compiler_params=_params(2),
        name="mla_attn",
    )(qt, k3, vt3)


def _sample_attn_kernel(qa_ref, ka_ref, va_ref, cak_ref, cav_ref, biasc_ref, biasn_ref,
                        qb_ref, kb_ref, vb_ref, cbk_ref, cbv_ref,
                        qc_ref, kc_ref, vc_ref, ckv_ref, ckr_ref, wukv_ref,
                        ya_ref, yb_ref, yc_ref, *, tile):
    ds = qa_ref.shape[1]
    past = cbk_ref.shape[1]

    ka, va, cak, cav = ka_ref[0], va_ref[0], cak_ref[0], cav_ref[0]
    parts = []
    for h, qm in enumerate(_head_masked(qa_ref[0], HEAD_DIM)):
        s_c = _dot_nt(qm, cak) + biasc_ref[h]
        s_n = _dot_nt(qm, ka) + biasn_ref[h]
        m = jnp.maximum(jnp.max(s_c, axis=1, keepdims=True), jnp.max(s_n, axis=1, keepdims=True))
        p_c = jnp.exp(s_c - m)
        p_n = jnp.exp(s_n - m)
        l = jnp.sum(p_c, axis=1, keepdims=True) + jnp.sum(p_n, axis=1, keepdims=True)
        parts.append((_dot(p_c.astype(BF16), cav) + _dot(p_n.astype(BF16), va)) / l)
    ya_ref[0] = _select_heads(parts, HEAD_DIM).astype(BF16)

    row = lax.broadcasted_iota(jnp.int32, (ds, ds), 0)
    col = lax.broadcasted_iota(jnp.int32, (ds, ds), 1)
    causal = col < row
    ones_new = _strict_upper_ones(ds)
    ones_ut = _strict_upper_ones(tile)
    kb, vb = kb_ref[0], vb_ref[0]
    n_tiles = past // tile
    parts = []
    for qm in _head_masked(qb_ref[0], HEAD_DIM):
        state = _sb_tile(qm, kb, vb, jnp.zeros((ds, 1), F32), jnp.zeros((ds, W_AB), F32),
                         ones_new, causal)

        def cache_tile(st, qm=qm):
            start = pl.multiple_of((n_tiles - 1 - st[0]) * tile, tile)
            return (st[0] + 1,) + _sb_tile(qm, cbk_ref[0, pl.ds(start, tile), :],
                                           cbv_ref[0, pl.ds(start, tile), :], st[1], st[2], ones_ut, None)
        state = lax.while_loop(lambda st: (st[0] < n_tiles) & _sb_alive(st[1]), cache_tile,
                               (jnp.int32(0),) + state)
        parts.append(state[2])
    yb_ref[0] = _select_heads(parts, HEAD_DIM).astype(BF16)

    wk = H_C * C_PAD
    pairs = H_C // 2
    qc = qc_ref[0]
    qms = []
    for p in range(pairs):
        qms.extend(_head_masked(qc[:, p * 2 * C_PAD:(p + 1) * 2 * C_PAD], C_PAD))

    def attend(kall, vall, state):
        out = []
        for h, (m, l, acc) in enumerate(state):
            p = h // 2
            s = _dot_nt(qms[h], kall[:, p * 2 * C_PAD:(p + 1) * 2 * C_PAD])
            out.append(_softmax_tile(s, vall[:, p * 2 * C_V:(p + 1) * 2 * C_V], m, l, acc))
        return tuple(out)

    tile_c = SAMPLE_LATENT_TILE if past % SAMPLE_LATENT_TILE == 0 else tile

    def cache_step(j, state):
        start = pl.multiple_of(j * tile_c, tile_c)
        lat = ckv_ref[0, pl.ds(start, tile_c), :]
        kv = _dot(lat, wukv_ref[...])
        krb = ckr_ref[0, pl.ds(start, tile_c), :]
        kall = jnp.concatenate([kv[:, h * C_PAD:(h + 1) * C_PAD] + krb for h in range(H_C)],
                               axis=1).astype(BF16)
        return attend(kall, kv[:, wk:].astype(BF16), state)

    init = tuple((jnp.full((ds, 1), NEG, F32), jnp.zeros((ds, 1), F32),
                  jnp.zeros((ds, 2 * C_V), F32)) for _ in range(H_C))
    state = lax.fori_loop(0, past // tile_c, cache_step, init)
    state = attend(kc_ref[0], vc_ref[0], state)
    outs = []
    for p in range(pairs):
        outs.append(_select_heads([state[2 * p + e][2] / state[2 * p + e][1] for e in range(2)], C_V))
    yc_ref[0] = jnp.concatenate(outs, axis=1).astype(BF16)


def _sample_attention(new, caches, bias_c, bias_n, w_ukv, tile):
    qa, ka, va, qb, kb, vb, qc, kc, vc = new
    cak, cav, cbk, cbv, ckv, ckr = caches
    nb, ds, _ = qa.shape
    req = lambda a: pl.BlockSpec((1,) + a.shape[1:], lambda b: (b, 0, 0))
    ins = [qa, ka, va, cak, cav, bias_c, bias_n, qb, kb, vb, cbk, cbv, qc, kc, vc, ckv, ckr, w_ukv]
    specs = [req(a) for a in ins]
    for idx in (5, 6, 17):
        specs[idx] = _const_spec(ins[idx].shape)
    out_shape = [jax.ShapeDtypeStruct((nb, ds, W_AB), BF16), jax.ShapeDtypeStruct((nb, ds, W_AB), BF16),
                 jax.ShapeDtypeStruct((nb, ds, H_C * C_V), BF16)]
    return pl.pallas_call(
        functools.partial(_sample_attn_kernel, tile=tile),
        grid=(nb,),
        in_specs=specs,
        out_specs=[req(s) for s in out_shape],
        out_shape=out_shape,
        compiler_params=_params(1),
        name="sample_attn",
    )(*ins)


def _rope_tables(pos):
    half = C_ROPE // 2
    freq = ROPE_THETA ** (-jnp.arange(half, dtype=F32) / half)
    ang = pos.astype(F32)[:, None] * freq[None, :]
    cos, sin = jnp.cos(ang), jnp.sin(ang)
    n = pos.shape[0]
    z = lambda w: jnp.zeros((n, w), F32)
    tc = jnp.concatenate([jnp.ones((n, C_NOPE), F32), cos, cos, z(C_PAD - C_NOPE - C_ROPE)], axis=1)
    ts1 = jnp.concatenate([z(C_NOPE + half), sin, z(C_PAD - C_NOPE - C_ROPE)], axis=1)
    ts2 = jnp.concatenate([z(C_NOPE), -sin, z(C_PAD - C_NOPE - half)], axis=1)
    return tc, ts1, ts2


def _band_bias(rel_bias, q0, nq, k0, nk):
    period = 1 << int(np.ceil(np.log2(nq + nk)))
    d = np.arange(period)
    d = np.where(d >= nk, d - period, d)
    idx = np.clip((q0 - k0) - d, -REL_CLIP, REL_CLIP) + REL_CLIP
    g = rel_bias.astype(F32)[:, idx]
    toe = jnp.tile(g, (1, nq))[:, :nq * (period - 1)].reshape(-1, nq, period - 1)[:, :, :nk]
    q_pos, k_pos = q0 + np.arange(nq), k0 + np.arange(nk)
    qc, kc = q_pos[:, None] // CHUNK, k_pos[None, :] // CHUNK
    vis = (kc <= qc) & (kc >= qc - BAND_CHUNKS)
    return jnp.where(jnp.asarray(vis)[None], toe, NEG)


def _cast_kernel(w_ref, o_ref):
    o_ref[...] = w_ref[0].astype(BF16)


def _layer_bf16(w, l):
    _, rows, cols = w.shape
    rb = rows // 4 if rows % 64 == 0 else rows
    return pl.pallas_call(
        _cast_kernel,
        grid=(rows // rb,),
        in_specs=[pl.BlockSpec((1, rb, cols), lambda i: (l, i, 0))],
        out_specs=pl.BlockSpec((rb, cols), lambda i: (i, 0)),
        out_shape=jax.ShapeDtypeStruct((rows, cols), BF16),
        compiler_params=_params(1),
        name="cast_bf16",
    )(w)


def _layer_weights(l, w_in, w_uq, w_uk, w_uv):
    d = w_in.shape[1]
    q_rank, kv_rank = w_uq.shape[1], w_uk.shape[1]
    o = 6 * W_AB + q_rank + kv_rank
    zc = lambda w: jnp.zeros((d, w), w_in.dtype)
    w_main = jnp.concatenate([w_in[l, :, :o], zc(C_NOPE), w_in[l, :, o:o + C_ROPE],
                              zc(C_PAD - C_NOPE - C_ROPE)], axis=1).astype(BF16)
    w_gates = w_in[l, :, o + C_ROPE:].astype(BF16)
    pad_heads = lambda w: jnp.pad(w, ((0, 0), (0, 0), (0, C_PAD - w.shape[2]))).reshape(w.shape[0], -1)
    w_uq_p = pad_heads(w_uq[l]).astype(BF16)
    w_ukv = jnp.concatenate([pad_heads(w_uk[l]), w_uv[l].reshape(kv_rank, -1)], axis=1).astype(BF16)
    return w_main, w_gates, w_uq_p, w_ukv


def kernel(x_prompt, x_sample, cache_a_k, cache_a_v, cache_b_k, cache_b_v, cache_c_kv, cache_c_kr, ffn1_norm_pre, ffn1_norm_post, ffn1_w_gate, ffn1_w_up, ffn1_w_down, mix_norm_pre, mix_norm_post, w_in, cq_norm, ckv_norm, w_uq, w_uk, w_uv, rel_bias_a, w_br_a, w_br_b, w_br_c, w_out, ffn2_norm_pre, ffn2_norm_post, ffn2_w_gate, ffn2_w_up, ffn2_w_down):
    batch, seq, d = x_prompt.shape
    nb, ds, _ = x_sample.shape
    depth = w_in.shape[0]
    past = cache_b_k.shape[2]
    win_cache = cache_a_k.shape[2]
    assert batch == 1, "prompt group is a single sequence"
    tile = 256
    tm = 512 if seq % 512 == 0 else tile
    tile_c = tm
    assert seq % tile == 0 and past % tile == 0 and tile % CHUNK == 0 and 2 * tile >= WIN_A
    assert past % CHUNK + ds <= CHUNK
    ns = nb * ds

    xp = x_prompt.reshape(seq, d)
    xs = x_sample.reshape(ns, d)
    tabs_p = _rope_tables(jnp.arange(seq))
    tabs_s = _rope_tables(jnp.tile(past + jnp.arange(ds), nb))

    row = lambda g, l: g[l][None, :]
    base = 2 * tile

    states_p, states_s = [], []
    for l in range(depth):
        w_main, w_gates, w_uq_p, w_ukv = _layer_weights(l, w_in, w_uq, w_uk, w_uv)
        ffn1 = (row(ffn1_norm_pre, l), row(ffn1_norm_post, l), _layer_bf16(ffn1_w_gate, l),
                _layer_bf16(ffn1_w_up, l), _layer_bf16(ffn1_w_down, l))
        ffn2 = (row(ffn2_norm_pre, l), row(ffn2_norm_post, l), _layer_bf16(ffn2_w_gate, l),
                _layer_bf16(ffn2_w_up, l), _layer_bf16(ffn2_w_down, l))
        mix_in = (row(mix_norm_pre, l), w_main, row(cq_norm, l), row(ckv_norm, l), w_uq_p, w_ukv)
        mrg = (row(mix_norm_pre, l), row(mix_norm_post, l), w_gates, _layer_bf16(w_br_a, l),
               _layer_bf16(w_br_b, l), _layer_bf16(w_br_c, l), _layer_bf16(w_out, l))
        bias_p = _band_bias(rel_bias_a[l], base, tile, base - 2 * tile, 3 * tile)
        bias_c = _band_bias(rel_bias_a[l], past, ds, past - win_cache, win_cache)
        bias_n = _band_bias(rel_bias_a[l], past, ds, past, ds)

        xp = _ffn(xp, *ffn1, tm)
        (qa, kab, vab, qb, kbb, vbb, ka, va, kb, vb, ckv, krb, qc, kc, vc) = _mixer_in(
            xp, *mix_in, tabs_p, tm, q_transposed=True)
        ya = _band_attention(qa, kab, vab, bias_p, tile)
        yb = _sb_attention(qb, kbb, vbb, tile)
        yc = _mla_attention(qc, kc, vc, tile_c)
        xp = _merge(xp, ya, yb, yc, *mrg, tm, yc_transposed=True)
        xp = _ffn(xp, *ffn2, tm)
        win = min(WIN_A, seq)
        heads = lambda t, h: t.reshape(1, t.shape[0], h, t.shape[1] // h)
        states_p.append((heads(ka[-win:], H_A), heads(va[-win:], H_A), heads(kb, H_B), heads(vb, H_B),
                         ckv[None], krb[None, :, C_NOPE:C_NOPE + C_ROPE]))

        xs = _ffn(xs, *ffn1, ns)
        (qa, kab, vab, qb, kbb, vbb, ka, va, kb, vb, ckv, krb, qc, kc, vc) = _mixer_in(
            xs, *mix_in, tabs_s, ns)
        per_req = lambda t: t.reshape(nb, ds, t.shape[1])
        flat = lambda c: c.reshape(nb, c.shape[1], -1).astype(BF16)
        ckr_blk = jnp.pad(cache_c_kr[l], ((0, 0), (0, 0), (C_NOPE, C_PAD - C_NOPE - C_ROPE)))
        caches = (flat(cache_a_k[l]), flat(cache_a_v[l]), flat(cache_b_k[l]), flat(cache_b_v[l]),
                  cache_c_kv[l].astype(BF16), ckr_blk)
        new = tuple(per_req(t) for t in (qa, kab, vab, qb, kbb, vbb, qc, kc, vc))
        ya, yb, yc = _sample_attention(new, caches, bias_c, bias_n, w_ukv, tile)
        xs = _merge(xs, ya.reshape(ns, -1), yb.reshape(ns, -1), yc.reshape(ns, -1), *mrg, ns)
        xs = _ffn(xs, *ffn2, ns)
        heads_s = lambda t, h: t.reshape(nb, ds, h, t.shape[1] // h)
        states_s.append((heads_s(ka, H_A), heads_s(va, H_A), heads_s(kb, H_B), heads_s(vb, H_B),
                         ckv.reshape(nb, ds, -1), krb[:, C_NOPE:C_NOPE + C_ROPE].reshape(nb, ds, C_ROPE)))

    stack = lambda states, i: jnp.stack([s[i] for s in states], axis=0)
    return ((xp.reshape(batch, seq, d), xs.reshape(nb, ds, d))
            + tuple(stack(states_p, i) for i in range(6))
            + tuple(stack(states_s, i) for i in range(6)))
```

```python
import functools

import numpy as np
import jax
import jax.numpy as jnp
from jax import lax
from jax.experimental import pallas as pl
from jax.experimental.pallas import tpu as pltpu

F32 = jnp.float32
BF16 = jnp.bfloat16

CHUNK = 64
BAND_CHUNKS = 8
WIN_A = BAND_CHUNKS * CHUNK
HEAD_DIM = 64
H_A = 4
H_B = 4
H_C = 8
C_NOPE = 64
C_ROPE = 32
C_V = 64
REL_CLIP = 128
ROPE_THETA = 10000.0
EPS = 1e-6
N_BRANCH = 3
W_AB = H_A * HEAD_DIM
C_PAD = 128
ATTN_SCALE = HEAD_DIM ** -0.5
MLA_SCALE = (C_NOPE + C_ROPE) ** -0.5
LOG2E = float(np.log2(np.e))
C_VX = C_V + 16
MLA_SLOTS = 4
SAMPLE_LATENT_TILE = 1024
NEG = -1e30
SB_DEAD = -104.0

LANES = 128
VMEM_LIMIT = 52 * 1024 * 1024

NT_DIMS = (((1,), (1,)), ((), ()))
TN_DIMS = (((0,), (0,)), ((), ()))


def _rms(x, g):
    return x * lax.rsqrt(jnp.mean(x * x, axis=-1, keepdims=True) + EPS) * g


def _dot(a, b):
    return jnp.dot(a, b, preferred_element_type=F32)


def _dot_nt(a, b):
    return lax.dot_general(a, b, NT_DIMS, preferred_element_type=F32)


def _dot_tn(a, b):
    return lax.dot_general(a, b, TN_DIMS, preferred_element_type=F32)


def _const_spec(shape):
    zeros = (0,) * len(shape)
    return pl.BlockSpec(shape, lambda *_: zeros, pipeline_mode=pl.Buffered(1))


def _row_spec(tm, width):
    return pl.BlockSpec((tm, width), lambda i: (i, 0))


def _params(n_grid):
    return pltpu.CompilerParams(dimension_semantics=("arbitrary",) * n_grid,
                                vmem_limit_bytes=VMEM_LIMIT)


def _ffn_kernel(x_ref, gpre_ref, gpost_ref, wg_ref, wu_ref, wd_ref, o_ref, *, f_chunk):
    x = x_ref[...]
    h = _rms(x, gpre_ref[...]).astype(BF16)
    d_ff = wg_ref.shape[1]
    y = jnp.zeros(x.shape, F32)
    for c in range(d_ff // f_chunk):
        sl = slice(c * f_chunk, (c + 1) * f_chunk)
        g = _dot(h, wg_ref[:, sl])
        u = _dot(h, wu_ref[:, sl])
        a = (g * jax.nn.sigmoid(g) * u).astype(BF16)
        y = y + _dot(a, wd_ref[sl, :])
    o_ref[...] = x + 0.5 * _rms(y, gpost_ref[...])


def _ffn(x, g_pre, g_post, w_gate, w_up, w_down, tm):
    n, d = x.shape
    d_ff = w_gate.shape[1]
    f_chunk = d_ff // 2 if (d_ff // 2) % LANES == 0 else d_ff
    return pl.pallas_call(
        functools.partial(_ffn_kernel, f_chunk=f_chunk),
        grid=(n // tm,),
        in_specs=[_row_spec(tm, d), _const_spec((1, d)), _const_spec((1, d)),
                  _const_spec((d, d_ff)), _const_spec((d, d_ff)), _const_spec((d_ff, d))],
        out_specs=_row_spec(tm, d),
        out_shape=jax.ShapeDtypeStruct((n, d), F32),
        compiler_params=_params(1),
        name="ffn",
    )(x, g_pre, g_post, w_gate, w_up, w_down)


def _rope(blk, tc, ts1, ts2):
    return blk * tc + pltpu.roll(blk, 16, 1) * ts1 + pltpu.roll(blk, C_PAD - 16, 1) * ts2


def _mixer_in_kernel(x_ref, gpre_ref, win_ref, gcq_ref, gckv_ref, wuq_ref, wukv_ref,
                     tc_ref, ts1_ref, ts2_ref,
                     qa_ref, kab_ref, vab_ref, qb_ref, kbb_ref, vbb_ref,
                     ka_ref, va_ref, kb_ref, vb_ref, ckv_ref, krb_ref,
                     qc_ref, kc_ref, vc_ref, *, q_transposed):
    w = W_AB
    q_rank = gcq_ref.shape[1]
    kv_rank = gckv_ref.shape[1]
    u = _rms(x_ref[...], gpre_ref[...]).astype(BF16)
    proj = _dot(u, win_ref[...])
    qa, ka, va = proj[:, 0:w], proj[:, w:2 * w], proj[:, 2 * w:3 * w]
    qb, kb, vb = proj[:, 3 * w:4 * w], proj[:, 4 * w:5 * w], proj[:, 5 * w:6 * w]
    o = 6 * w
    cq = proj[:, o:o + q_rank]
    ckv = proj[:, o + q_rank:o + q_rank + kv_rank]
    krb = proj[:, o + q_rank + kv_rank:o + q_rank + kv_rank + C_PAD]
    qa_ref[...] = (qa * ATTN_SCALE).astype(BF16)
    qb_ref[...] = (qb * ATTN_SCALE).astype(BF16)
    ka_ref[...] = ka
    va_ref[...] = va
    kb_ref[...] = kb
    vb_ref[...] = vb
    kab_ref[...] = ka.astype(BF16)
    vab_ref[...] = va.astype(BF16)
    kbb_ref[...] = kb.astype(BF16)
    vbb_ref[...] = vb.astype(BF16)
    tc, ts1, ts2 = tc_ref[...], ts1_ref[...], ts2_ref[...]
    cqn = _rms(cq, gcq_ref[...]).astype(BF16)
    qall = _dot(cqn, wuq_ref[...])
    ckvn = _rms(ckv, gckv_ref[...])
    ckv_ref[...] = ckvn
    kv = _dot(ckvn.astype(BF16), wukv_ref[...])
    krot = _rope(krb, tc, ts1, ts2)
    krb_ref[...] = krot
    for h in range(H_C):
        sl = slice(h * C_PAD, (h + 1) * C_PAD)
        qh = _rope(qall[:, sl], tc, ts1, ts2) * (MLA_SCALE * LOG2E)
        if q_transposed:
            qc_ref[sl, :] = qh.T.astype(BF16)
        else:
            qc_ref[:, sl] = qh.astype(BF16)
        kc_ref[:, sl] = (kv[:, sl] + krot).astype(BF16)
    if not q_transposed:
        vc_ref[...] = kv[:, H_C * C_PAD:].astype(BF16)
    else:
        ones_row = (lax.broadcasted_iota(jnp.int32, (C_VX - C_V, x_ref.shape[0]), 0) == 0).astype(BF16)
        for p in range(H_C // 2):
            o = H_C * C_PAD + p * 2 * C_V
            vt = kv[:, o:o + 2 * C_V].T
            for e in range(2):
                r = (2 * p + e) * C_VX
                vc_ref[0, r:r + C_V, :] = vt[e * C_V:(e + 1) * C_V, :].astype(BF16)
                vc_ref[0, r + C_V:r + C_VX, :] = ones_row


def _mixer_in(x, g_pre, w_main, g_cq, g_ckv, w_uq, w_ukv, tabs, tm, q_transposed=False):
    n, d = x.shape
    q_rank, kv_rank = g_cq.shape[1], g_ckv.shape[1]
    wc = H_C * C_PAD
    wv = H_C * C_V
    bf = lambda width: jax.ShapeDtypeStruct((n, width), BF16)
    f32 = lambda width: jax.ShapeDtypeStruct((n, width), F32)
    q_shape = jax.ShapeDtypeStruct((wc, n), BF16) if q_transposed else bf(wc)
    q_spec = pl.BlockSpec((wc, tm), lambda i: (0, i)) if q_transposed else _row_spec(tm, wc)
    v_shape = jax.ShapeDtypeStruct((n // tm, H_C * C_VX, tm), BF16) if q_transposed else bf(wv)
    v_spec = (pl.BlockSpec((1, H_C * C_VX, tm), lambda i: (i, 0, 0)) if q_transposed
              else _row_spec(tm, wv))
    out_shape = ([bf(W_AB)] * 6 + [f32(W_AB)] * 4 + [f32(kv_rank), f32(C_PAD)]
                 + [q_shape, bf(wc), v_shape])
    out_specs = ([_row_spec(tm, W_AB)] * 10 + [_row_spec(tm, kv_rank), _row_spec(tm, C_PAD)]
                 + [q_spec, _row_spec(tm, wc), v_spec])
    return pl.pallas_call(
        functools.partial(_mixer_in_kernel, q_transposed=q_transposed),
        grid=(n // tm,),
        in_specs=[_row_spec(tm, d), _const_spec((1, d)), _const_spec(w_main.shape),
                  _const_spec((1, q_rank)), _const_spec((1, kv_rank)),
                  _const_spec(w_uq.shape), _const_spec(w_ukv.shape),
                  _row_spec(tm, C_PAD), _row_spec(tm, C_PAD), _row_spec(tm, C_PAD)],
        out_specs=out_specs,
        out_shape=out_shape,
        compiler_params=_params(1),
        name="mixer_in",
    )(x, g_pre, w_main, g_cq, g_ckv, w_uq, w_ukv, *tabs)


def _merge_kernel(x_ref, ya_ref, yb_ref, yc_ref, gpre_ref, gpost_ref, wgates_ref,
                  wa_ref, wb_ref, wc_ref, wout_ref, o_ref, *, yc_transposed):
    x = x_ref[...]
    d = x.shape[1]
    u = _rms(x, gpre_ref[...]).astype(BF16)
    m = jnp.zeros(x.shape, F32)
    for i, (y_ref, w_ref) in enumerate(((ya_ref, wa_ref), (yb_ref, wb_ref), (yc_ref, wc_ref))):
        gate = jax.nn.sigmoid(_dot(u, wgates_ref[:, i * d:(i + 1) * d]))
        branch_dot = _dot_tn if (yc_transposed and y_ref is yc_ref) else _dot
        m = m + gate * branch_dot(y_ref[...], w_ref[...])
    mixed = _dot(m.astype(BF16), wout_ref[...])
    o_ref[...] = x + _rms(mixed, gpost_ref[...])


def _merge(x, ya, yb, yc, g_pre, g_post, w_gates, w_a, w_b, w_c, w_out, tm, yc_transposed=False):
    n, d = x.shape
    yc_spec = (pl.BlockSpec((yc.shape[0], tm), lambda i: (0, i)) if yc_transposed
               else _row_spec(tm, yc.shape[1]))
    return pl.pallas_call(
        functools.partial(_merge_kernel, yc_transposed=yc_transposed),
        grid=(n // tm,),
        in_specs=[_row_spec(tm, d), _row_spec(tm, ya.shape[1]), _row_spec(tm, yb.shape[1]),
                  yc_spec, _const_spec((1, d)), _const_spec((1, d)),
                  _const_spec(w_gates.shape), _const_spec(w_a.shape), _const_spec(w_b.shape),
                  _const_spec(w_c.shape), _const_spec(w_out.shape)],
        out_specs=_row_spec(tm, d),
        out_shape=jax.ShapeDtypeStruct((n, d), F32),
        compiler_params=_params(1),
        name="merge",
    )(x, ya, yb, yc, g_pre, g_post, w_gates, w_a, w_b, w_c, w_out)


def _head_masked(q, width):
    lane_head = lax.broadcasted_iota(jnp.int32, q.shape, 1) // width
    zero = jnp.zeros_like(q)
    return [jnp.where(lane_head == h, q, zero) for h in range(q.shape[1] // width)]


def _strict_upper_ones(n):
    j = lax.broadcasted_iota(jnp.int32, (n, n), 0)
    s = lax.broadcasted_iota(jnp.int32, (n, n), 1)
    return (j > s).astype(BF16)


def _sb_terms(qm, kt, ones_ut, causal):
    z = _dot_nt(qm, kt)
    t = jnp.log1p(jnp.exp(-jnp.abs(z)))
    log_keep = -jnp.maximum(z, 0.0) - t
    log_beta = jnp.minimum(z, 0.0) - t
    if causal is not None:
        log_keep = jnp.where(causal, log_keep, 0.0)
    hi = log_keep.astype(BF16)
    lo = (log_keep - hi.astype(F32)).astype(BF16)
    tail = _dot(hi, ones_ut) + _dot(lo, ones_ut)
    return log_beta + tail, jnp.sum(log_keep, axis=1, keepdims=True)


def _sb_apply(terms, vt, carry, acc, causal):
    logw, total = terms
    w = jnp.exp(logw + carry)
    if causal is not None:
        w = jnp.where(causal, w, 0.0)
    return carry + total, acc + _dot(w.astype(BF16), vt)


def _sb_tile(qm, kt, vt, carry, acc, ones_ut, causal):
    return _sb_apply(_sb_terms(qm, kt, ones_ut, causal), vt, carry, acc, causal)


def _sb_alive(carry):
    return jnp.max(carry) > SB_DEAD


def _softmax_tile(s, vt, m, l, acc):
    m_new = jnp.maximum(m, jnp.max(s, axis=1, keepdims=True))
    alpha = jnp.exp2(m - m_new)
    p = jnp.exp2(s - m_new)
    l = alpha * l + jnp.sum(p, axis=1, keepdims=True)
    acc = alpha * acc + _dot(p.astype(BF16), vt)
    return m_new, l, acc


def _select_heads(parts, width):
    lane_head = lax.broadcasted_iota(jnp.int32, parts[0].shape, 1) // width
    out = parts[0]
    for h in range(1, len(parts)):
        out = jnp.where(lane_head == h, parts[h], out)
    return out


def _band_kernel(q_ref, k0_ref, k1_ref, k2_ref, v0_ref, v1_ref, v2_ref, bias_ref, o_ref):
    i = pl.program_id(0)
    tq = q_ref.shape[0]
    kwin = jnp.concatenate([k0_ref[...], k1_ref[...], k2_ref[...]], axis=0)
    vwin = jnp.concatenate([v0_ref[...], v1_ref[...], v2_ref[...]], axis=0)
    col = lax.broadcasted_iota(jnp.int32, (tq, 3 * tq), 1)
    in_range = col >= (2 - i) * tq
    parts = []
    for h, qm in enumerate(_head_masked(q_ref[...], HEAD_DIM)):
        s = jnp.where(in_range, _dot_nt(qm, kwin) + bias_ref[h], NEG)
        m = jnp.max(s, axis=1, keepdims=True)
        p = jnp.exp(s - m)
        l = jnp.sum(p, axis=1, keepdims=True)
        parts.append(_dot(p.astype(BF16), vwin) / l)
    o_ref[...] = _select_heads(parts, HEAD_DIM).astype(BF16)


def _band_attention(q, k, v, bias, tq):
    n, w = q.shape
    blk = lambda back: pl.BlockSpec((tq, w), lambda i: (jnp.maximum(i - back, 0), 0))
    return pl.pallas_call(
        _band_kernel,
        grid=(n // tq,),
        in_specs=[blk(0), blk(2), blk(1), blk(0), blk(2), blk(1), blk(0), _const_spec(bias.shape)],
        out_specs=blk(0),
        out_shape=jax.ShapeDtypeStruct((n, w), BF16),
        compiler_params=_params(1),
        name="band_attn",
    )(q, k, k, k, v, v, v, bias)


def _sb_kernel(q_ref, k_ref, v_ref, o_ref, *, tile):
    i = pl.program_id(0)
    row = lax.broadcasted_iota(jnp.int32, (tile, tile), 0)
    col = lax.broadcasted_iota(jnp.int32, (tile, tile), 1)
    causal = col < row
    ones_ut = _strict_upper_ones(tile)
    width = q_ref.shape[1]
    qms = _head_masked(q_ref[...], HEAD_DIM)
    rows = lambda ref, kb: ref[pl.ds(pl.multiple_of(kb * tile, tile), tile), :]

    prev = jnp.maximum(i - 1, 0)
    k_diag, k_prev = rows(k_ref, i), rows(k_ref, prev)
    terms_diag = [_sb_terms(qm, k_diag, ones_ut, causal) for qm in qms]
    terms_prev = [_sb_terms(qm, k_prev, ones_ut, None) for qm in qms]
    v_diag, v_prev = rows(v_ref, i), rows(v_ref, prev)
    parts = []
    for qm, t_diag, t_prev in zip(qms, terms_diag, terms_prev):
        carry, acc = _sb_apply(t_diag, v_diag, jnp.zeros((tile, 1), F32), jnp.zeros((tile, width), F32),
                               causal)
        carry, acc = _sb_apply(t_prev, v_prev, jnp.where(i > 0, carry, NEG), acc, None)
        state = lax.while_loop(
            lambda st: (st[0] < i - 1) & _sb_alive(st[1]),
            lambda st, qm=qm: (st[0] + 1,) + _sb_tile(qm, rows(k_ref, i - 2 - st[0]), rows(v_ref, i - 2 - st[0]),
                                                      st[1], st[2], ones_ut, None),
            (jnp.int32(0), carry, acc))
        parts.append(state[2])
    o_ref[...] = _select_heads(parts, HEAD_DIM).astype(BF16)


def _sb_attention(q, k, v, tile):
    n, w = q.shape
    return pl.pallas_call(
        functools.partial(_sb_kernel, tile=tile),
        grid=(n // tile,),
        in_specs=[_row_spec(tile, w), _const_spec((n, w)), _const_spec((n, w))],
        out_specs=_row_spec(tile, w),
        out_shape=jax.ShapeDtypeStruct((n, w), BF16),
        compiler_params=_params(1),
        name="sb_attn",
    )(q, k, v)


def _mla_kernel(qt_ref, k_ref, vt_ref, ot_ref, s_scr, p_scr, a_scr, m_scr, acc_scr, *, tile):
    i = pl.program_id(1)
    heads = range(2)

    def scores_into(slot, kb):
        for e in heads:
            s_scr[slot, e] = _dot(k_ref[kb, :, e * C_PAD:(e + 1) * C_PAD],
                                  qt_ref[e * C_PAD:(e + 1) * C_PAD, :])

    def add_values(slot, kb):
        for e in heads:
            pv = _dot(vt_ref[kb, e * C_VX:(e + 1) * C_VX, :], p_scr[slot, e])
            acc_scr[e] = a_scr[slot, e] * acc_scr[e] + pv

    def softmax_into(slot, mask):
        for e in heads:
            s = s_scr[slot, e]
            if mask is not None:
                s = jnp.where(mask, s, NEG)
            m = m_scr[e]
            m_new = jnp.maximum(m, jnp.max(s, axis=0, keepdims=True))
            m_scr[e] = m_new
            a_scr[slot, e] = jnp.exp2(m - m_new)
            p_scr[slot, e] = jnp.exp2(s - m_new).astype(BF16)

    def stage(t, j, issue_scores, mask):
        ahead = (j + 2) % MLA_SLOTS
        if issue_scores:
            scores_into(ahead, jnp.minimum(t + 2, i))
        add_values(ahead, jnp.maximum(t - 2, 0))
        softmax_into(j, mask)

    for slot in (2, 3):
        p_scr[slot] = jnp.zeros(p_scr.shape[1:], BF16)
        a_scr[slot] = jnp.ones(a_scr.shape[1:], F32)
    m_scr[...] = jnp.full(m_scr.shape, NEG, F32)
    acc_scr[...] = jnp.zeros(acc_scr.shape, F32)
    scores_into(0, 0)
    scores_into(1, jnp.minimum(1, i))

    def four_stages(g, _):
        for j in range(MLA_SLOTS):
            stage(MLA_SLOTS * g + j, j, True, None)
        return 0
    groups = i // MLA_SLOTS
    lax.fori_loop(0, groups, four_stages, 0)

    key = lax.broadcasted_iota(jnp.int32, (tile, tile), 0)
    qry = lax.broadcasted_iota(jnp.int32, (tile, tile), 1)
    visible = (key // CHUNK) <= (qry // CHUNK)
    for rest in range(MLA_SLOTS):
        @pl.when(i % MLA_SLOTS == rest)
        def _(rest=rest):
            base = MLA_SLOTS * groups
            for j in range(rest):
                stage(base + j, j, j + 2 <= rest, None)
            stage(i, rest, False, visible)
            add_values((rest - 1) % MLA_SLOTS, jnp.maximum(i - 1, 0))
            add_values(rest, i)
            ot_ref[...] = jnp.concatenate([acc_scr[e, :C_V] / acc_scr[e, C_V:C_V + 1] for e in heads],
                                          axis=0).astype(BF16)


def _mla_attention(qt, k, vt3, tile):
    n = k.shape[0]
    pairs = H_C // 2
    n_tiles = n // tile
    k3 = k.reshape(n_tiles, tile, H_C * C_PAD)
    assert vt3.shape == (n_tiles, H_C * C_VX, tile)
    return pl.pallas_call(
        functools.partial(_mla_kernel, tile=tile),
        grid=(pairs, n_tiles),
        in_specs=[pl.BlockSpec((2 * C_PAD, tile), lambda p, i: (p, i)),
                  pl.BlockSpec((n_tiles, tile, 2 * C_PAD), lambda p, i: (0, 0, p),
                               pipeline_mode=pl.Buffered(1)),
                  pl.BlockSpec((n_tiles, 2 * C_VX, tile), lambda p, i: (0, p, 0),
                               pipeline_mode=pl.Buffered(1))],
        out_specs=pl.BlockSpec((2 * C_V, tile), lambda p, i: (p, i)),
        out_shape=jax.ShapeDtypeStruct((H_C * C_V, n), BF16),
        scratch_shapes=[pltpu.VMEM((MLA_SLOTS, 2, tile, tile), F32), pltpu.VMEM((MLA_SLOTS, 2, tile, tile), BF16),
                        pltpu.VMEM((MLA_SLOTS, 2, 1, tile), F32), pltpu.VMEM((2, 1, tile), F32),
                        pltpu.VMEM((2, C_VX, tile), F32)],
        compiler_params=_params(2),
        name="mla_attn",
    )(qt, k3, vt3)


def _sample_attn_kernel(qa_ref, ka_ref, va_ref, cak_ref, cav_ref, biasc_ref, biasn_ref,
                        qb_ref, kb_ref, vb_ref, cbk_ref, cbv_ref,
                        qc_ref, kc_ref, vc_ref, ckv_ref, ckr_ref, wukv_ref,
                        ya_ref, yb_ref, yc_ref, *, tile):
    ds = qa_ref.shape[1]
    past = cbk_ref.shape[2]

    ka, va, cak, cav = ka_ref[0], va_ref[0], cak_ref[0, 0].astype(BF16), cav_ref[0, 0].astype(BF16)
    parts = []
    for h, qm in enumerate(_head_masked(qa_ref[0], HEAD_DIM)):
        s_c = _dot_nt(qm, cak) + biasc_ref[h]
        s_n = _dot_nt(qm, ka) + biasn_ref[h]
        m = jnp.maximum(jnp.max(s_c, axis=1, keepdims=True), jnp.max(s_n, axis=1, keepdims=True))
        p_c = jnp.exp(s_c - m)
        p_n = jnp.exp(s_n - m)
        l = jnp.sum(p_c, axis=1, keepdims=True) + jnp.sum(p_n, axis=1, keepdims=True)
        parts.append((_dot(p_c.astype(BF16), cav) + _dot(p_n.astype(BF16), va)) / l)
    ya_ref[0] = _select_heads(parts, HEAD_DIM).astype(BF16)

    row = lax.broadcasted_iota(jnp.int32, (ds, ds), 0)
    col = lax.broadcasted_iota(jnp.int32, (ds, ds), 1)
    causal = col < row
    ones_new = _strict_upper_ones(ds)
    ones_ut = _strict_upper_ones(tile)
    kb, vb = kb_ref[0], vb_ref[0]
    n_tiles = past // tile
    parts = []
    for qm in _head_masked(qb_ref[0], HEAD_DIM):
        state = _sb_tile(qm, kb, vb, jnp.zeros((ds, 1), F32), jnp.zeros((ds, W_AB), F32),
                         ones_new, causal)

        def cache_tile(st, qm=qm):
            start = pl.multiple_of((n_tiles - 1 - st[0]) * tile, tile)
            return (st[0] + 1,) + _sb_tile(qm, cbk_ref[0, 0, pl.ds(start, tile), :].astype(BF16),
                                           cbv_ref[0, 0, pl.ds(start, tile), :].astype(BF16),
                                           st[1], st[2], ones_ut, None)
        state = lax.while_loop(lambda st: (st[0] < n_tiles) & _sb_alive(st[1]), cache_tile,
                               (jnp.int32(0),) + state)
        parts.append(state[2])
    yb_ref[0] = _select_heads(parts, HEAD_DIM).astype(BF16)

    wk = H_C * C_PAD
    pairs = H_C // 2
    qc = qc_ref[0]
    qms = []
    for p in range(pairs):
        qms.extend(_head_masked(qc[:, p * 2 * C_PAD:(p + 1) * 2 * C_PAD], C_PAD))

    def attend(kall, vall, state):
        out = []
        for h, (m, l, acc) in enumerate(state):
            p = h // 2
            s = _dot_nt(qms[h], kall[:, p * 2 * C_PAD:(p + 1) * 2 * C_PAD])
            out.append(_softmax_tile(s, vall[:, p * 2 * C_V:(p + 1) * 2 * C_V], m, l, acc))
        return tuple(out)

    tile_c = SAMPLE_LATENT_TILE if past % SAMPLE_LATENT_TILE == 0 else tile

    place = (lax.broadcasted_iota(jnp.int32, (C_ROPE, C_PAD), 1)
             == lax.broadcasted_iota(jnp.int32, (C_ROPE, C_PAD), 0) + C_NOPE).astype(BF16)

    def cache_step(j, state):
        start = pl.multiple_of(j * tile_c, tile_c)
        lat = ckv_ref[0, 0, pl.ds(start, tile_c), :].astype(BF16)
        kv = _dot(lat, wukv_ref[...])
        krb = _dot(ckr_ref[0, 0, pl.ds(start, tile_c), :].astype(BF16), place)
        kall = jnp.concatenate([kv[:, h * C_PAD:(h + 1) * C_PAD] + krb for h in range(H_C)],
                               axis=1).astype(BF16)
        return attend(kall, kv[:, wk:].astype(BF16), state)

    init = tuple((jnp.full((ds, 1), NEG, F32), jnp.zeros((ds, 1), F32),
                  jnp.zeros((ds, 2 * C_V), F32)) for _ in range(H_C))
    state = lax.fori_loop(0, past // tile_c, cache_step, init)
    state = attend(kc_ref[0], vc_ref[0], state)
    outs = []
    for p in range(pairs):
        outs.append(_select_heads([state[2 * p + e][2] / state[2 * p + e][1] for e in range(2)], C_V))
    yc_ref[0] = jnp.concatenate(outs, axis=1).astype(BF16)


def _sample_attention(new, caches, l, bias_c, bias_n, w_ukv, tile):
    qa, ka, va, qb, kb, vb, qc, kc, vc = new
    cak, cav, cbk, cbv, ckv, ckr = caches
    nb, ds, _ = qa.shape
    req = lambda a: pl.BlockSpec((1,) + a.shape[1:], lambda b: (b, 0, 0))
    cache = lambda a: pl.BlockSpec((1, 1) + a.shape[2:], lambda b: (l, b, 0, 0))
    ins = [qa, ka, va, cak, cav, bias_c, bias_n, qb, kb, vb, cbk, cbv, qc, kc, vc, ckv, ckr, w_ukv]
    specs = [req(a) for a in ins]
    for idx in (3, 4, 10, 11, 15, 16):
        specs[idx] = cache(ins[idx])
    for idx in (5, 6, 17):
        specs[idx] = _const_spec(ins[idx].shape)
    out_shape = [jax.ShapeDtypeStruct((nb, ds, W_AB), BF16), jax.ShapeDtypeStruct((nb, ds, W_AB), BF16),
                 jax.ShapeDtypeStruct((nb, ds, H_C * C_V), BF16)]
    return pl.pallas_call(
        functools.partial(_sample_attn_kernel, tile=tile),
        grid=(nb,),
        in_specs=specs,
        out_specs=[req(s) for s in out_shape],
        out_shape=out_shape,
        compiler_params=_params(1),
        name="sample_attn",
    )(*ins)


def _rope_tables(pos):
    half = C_ROPE // 2
    freq = ROPE_THETA ** (-jnp.arange(half, dtype=F32) / half)
    ang = pos.astype(F32)[:, None] * freq[None, :]
    cos, sin = jnp.cos(ang), jnp.sin(ang)
    n = pos.shape[0]
    z = lambda w: jnp.zeros((n, w), F32)
    tc = jnp.concatenate([jnp.ones((n, C_NOPE), F32), cos, cos, z(C_PAD - C_NOPE - C_ROPE)], axis=1)
    ts1 = jnp.concatenate([z(C_NOPE + half), sin, z(C_PAD - C_NOPE - C_ROPE)], axis=1)
    ts2 = jnp.concatenate([z(C_NOPE), -sin, z(C_PAD - C_NOPE - half)], axis=1)
    return tc, ts1, ts2


def _band_bias(rel_bias, q0, nq, k0, nk):
    period = 1 << int(np.ceil(np.log2(nq + nk)))
    d = np.arange(period)
    d = np.where(d >= nk, d - period, d)
    idx = np.clip((q0 - k0) - d, -REL_CLIP, REL_CLIP) + REL_CLIP
    g = rel_bias.astype(F32)[:, idx]
    toe = jnp.tile(g, (1, nq))[:, :nq * (period - 1)].reshape(-1, nq, period - 1)[:, :, :nk]
    q_pos, k_pos = q0 + np.arange(nq), k0 + np.arange(nk)
    qc, kc = q_pos[:, None] // CHUNK, k_pos[None, :] // CHUNK
    vis = (kc <= qc) & (kc >= qc - BAND_CHUNKS)
    return jnp.where(jnp.asarray(vis)[None], toe, NEG)


def _cast_kernel(w_ref, o_ref):
    o_ref[...] = w_ref[0].astype(BF16)


def _layer_bf16(w, l):
    _, rows, cols = w.shape
    rb = rows // 4 if rows % 64 == 0 else rows
    return pl.pallas_call(
        _cast_kernel,
        grid=(rows // rb,),
        in_specs=[pl.BlockSpec((1, rb, cols), lambda i: (l, i, 0))],
        out_specs=pl.BlockSpec((rb, cols), lambda i: (i, 0)),
        out_shape=jax.ShapeDtypeStruct((rows, cols), BF16),
        compiler_params=_params(1),
        name="cast_bf16",
    )(w)


def _layer_weights(l, w_in, w_uq, w_uk, w_uv):
    d = w_in.shape[1]
    q_rank, kv_rank = w_uq.shape[1], w_uk.shape[1]
    o = 6 * W_AB + q_rank + kv_rank
    zc = lambda w: jnp.zeros((d, w), w_in.dtype)
    w_main = jnp.concatenate([w_in[l, :, :o], zc(C_NOPE), w_in[l, :, o:o + C_ROPE],
                              zc(C_PAD - C_NOPE - C_ROPE)], axis=1).astype(BF16)
    w_gates = w_in[l, :, o + C_ROPE:].astype(BF16)
    pad_heads = lambda w: jnp.pad(w, ((0, 0), (0, 0), (0, C_PAD - w.shape[2]))).reshape(w.shape[0], -1)
    w_uq_p = pad_heads(w_uq[l]).astype(BF16)
    w_ukv = jnp.concatenate([pad_heads(w_uk[l]), w_uv[l].reshape(kv_rank, -1)], axis=1).astype(BF16)
    return w_main, w_gates, w_uq_p, w_ukv


def kernel(x_prompt, x_sample, cache_a_k, cache_a_v, cache_b_k, cache_b_v, cache_c_kv, cache_c_kr, ffn1_norm_pre, ffn1_norm_post, ffn1_w_gate, ffn1_w_up, ffn1_w_down, mix_norm_pre, mix_norm_post, w_in, cq_norm, ckv_norm, w_uq, w_uk, w_uv, rel_bias_a, w_br_a, w_br_b, w_br_c, w_out, ffn2_norm_pre, ffn2_norm_post, ffn2_w_gate, ffn2_w_up, ffn2_w_down):
    batch, seq, d = x_prompt.shape
    nb, ds, _ = x_sample.shape
    depth = w_in.shape[0]
    past = cache_b_k.shape[2]
    win_cache = cache_a_k.shape[2]
    assert batch == 1, "prompt group is a single sequence"
    tile = 256
    tm = 512 if seq % 512 == 0 else tile
    tile_c = tm
    assert seq % tile == 0 and past % tile == 0 and tile % CHUNK == 0 and 2 * tile >= WIN_A
    assert past % CHUNK + ds <= CHUNK
    ns = nb * ds

    xp = x_prompt.reshape(seq, d)
    xs = x_sample.reshape(ns, d)
    tabs_p = _rope_tables(jnp.arange(seq))
    tabs_s = _rope_tables(jnp.tile(past + jnp.arange(ds), nb))

    row = lambda g, l: g[l][None, :]
    base = 2 * tile

    merged_heads = lambda c: c.reshape(c.shape[:3] + (-1,))
    caches = (merged_heads(cache_a_k), merged_heads(cache_a_v), merged_heads(cache_b_k),
              merged_heads(cache_b_v), cache_c_kv, cache_c_kr)
    states_p, states_s = [], []
    for l in range(depth):
        w_main, w_gates, w_uq_p, w_ukv = _layer_weights(l, w_in, w_uq, w_uk, w_uv)
        ffn1 = (row(ffn1_norm_pre, l), row(ffn1_norm_post, l), _layer_bf16(ffn1_w_gate, l),
                _layer_bf16(ffn1_w_up, l), _layer_bf16(ffn1_w_down, l))
        ffn2 = (row(ffn2_norm_pre, l), row(ffn2_norm_post, l), _layer_bf16(ffn2_w_gate, l),
                _layer_bf16(ffn2_w_up, l), _layer_bf16(ffn2_w_down, l))
        mix_in = (row(mix_norm_pre, l), w_main, row(cq_norm, l), row(ckv_norm, l), w_uq_p, w_ukv)
        mrg = (row(mix_norm_pre, l), row(mix_norm_post, l), w_gates, _layer_bf16(w_br_a, l),
               _layer_bf16(w_br_b, l), _layer_bf16(w_br_c, l), _layer_bf16(w_out, l))
        bias_p = _band_bias(rel_bias_a[l], base, tile, base - 2 * tile, 3 * tile)
        bias_c = _band_bias(rel_bias_a[l], past, ds, past - win_cache, win_cache)
        bias_n = _band_bias(rel_bias_a[l], past, ds, past, ds)

        xp = _ffn(xp, *ffn1, tm)
        (qa, kab, vab, qb, kbb, vbb, ka, va, kb, vb, ckv, krb, qc, kc, vc) = _mixer_in(
            xp, *mix_in, tabs_p, tm, q_transposed=True)
        ya = _band_attention(qa, kab, vab, bias_p, tile)
        yb = _sb_attention(qb, kbb, vbb, tile)
        yc = _mla_attention(qc, kc, vc, tile_c)
        xp = _merge(xp, ya, yb, yc, *mrg, tm, yc_transposed=True)
        xp = _ffn(xp, *ffn2, tm)
        win = min(WIN_A, seq)
        heads = lambda t, h: t.reshape(1, t.shape[0], h, t.shape[1] // h)
        states_p.append((heads(ka[-win:], H_A), heads(va[-win:], H_A), heads(kb, H_B), heads(vb, H_B),
                         ckv[None], krb[None, :, C_NOPE:C_NOPE + C_ROPE]))

        xs = _ffn(xs, *ffn1, ns)
        (qa, kab, vab, qb, kbb, vbb, ka, va, kb, vb, ckv, krb, qc, kc, vc) = _mixer_in(
            xs, *mix_in, tabs_s, ns)
        per_req = lambda t: t.reshape(nb, ds, t.shape[1])
        new = tuple(per_req(t) for t in (qa, kab, vab, qb, kbb, vbb, qc, kc, vc))
        ya, yb, yc = _sample_attention(new, caches, l, bias_c, bias_n, w_ukv, tile)
        xs = _merge(xs, ya.reshape(ns, -1), yb.reshape(ns, -1), yc.reshape(ns, -1), *mrg, ns)
        xs = _ffn(xs, *ffn2, ns)
        heads_s = lambda t, h: t.reshape(nb, ds, h, t.shape[1] // h)
        states_s.append((heads_s(ka, H_A), heads_s(va, H_A), heads_s(kb, H_B), heads_s(vb, H_B),
                         ckv.reshape(nb, ds, -1), krb[:, C_NOPE:C_NOPE + C_ROPE].reshape(nb, ds, C_ROPE)))

    stack = lambda states, i: jnp.stack([s[i] for s in states], axis=0)
    return ((xp.reshape(batch, seq, d), xs.reshape(nb, ds, d))
            + tuple(stack(states_p, i) for i in range(6))
            + tuple(stack(states_s, i) for i in range(6)))
```

```python
import functools

import numpy as np
import jax
import jax.numpy as jnp
from jax import lax
from jax.experimental import pallas as pl
from jax.experimental.pallas import tpu as pltpu

F32 = jnp.float32
BF16 = jnp.bfloat16

CHUNK = 64
BAND_CHUNKS = 8
WIN_A = BAND_CHUNKS * CHUNK
HEAD_DIM = 64
H_A = 4
H_B = 4
H_C = 8
C_NOPE = 64
C_ROPE = 32
C_V = 64
REL_CLIP = 128
ROPE_THETA = 10000.0
EPS = 1e-6
N_BRANCH = 3
W_AB = H_A * HEAD_DIM
C_PAD = 128
ATTN_SCALE = HEAD_DIM ** -0.5
MLA_SCALE = (C_NOPE + C_ROPE) ** -0.5
LOG2E = float(np.log2(np.e))
C_VX = C_V + 16
MLA_SLOTS = 4
SAMPLE_LATENT_TILE = 1024
NEG = -1e30
SB_DEAD = -104.0

LANES = 128
VMEM_LIMIT = 52 * 1024 * 1024

NT_DIMS = (((1,), (1,)), ((), ()))
TN_DIMS = (((0,), (0,)), ((), ()))


def _rms(x, g):
    return x * lax.rsqrt(jnp.mean(x * x, axis=-1, keepdims=True) + EPS) * g


def _dot(a, b):
    return jnp.dot(a, b, preferred_element_type=F32)


def _dot_nt(a, b):
    return lax.dot_general(a, b, NT_DIMS, preferred_element_type=F32)


def _dot_tn(a, b):
    return lax.dot_general(a, b, TN_DIMS, preferred_element_type=F32)


def _const_spec(shape):
    zeros = (0,) * len(shape)
    return pl.BlockSpec(shape, lambda *_: zeros, pipeline_mode=pl.Buffered(1))


def _row_spec(tm, width):
    return pl.BlockSpec((tm, width), lambda i: (i, 0))


def _params(n_grid):
    return pltpu.CompilerParams(dimension_semantics=("arbitrary",) * n_grid,
                                vmem_limit_bytes=VMEM_LIMIT)


def _ffn_kernel(x_ref, gpre_ref, gpost_ref, wg_ref, wu_ref, wd_ref, o_ref, *, f_chunk):
    x = x_ref[...]
    h = _rms(x, gpre_ref[...]).astype(BF16)
    d_ff = wg_ref.shape[1]
    y = jnp.zeros(x.shape, F32)
    for c in range(d_ff // f_chunk):
        sl = slice(c * f_chunk, (c + 1) * f_chunk)
        g = _dot(h, wg_ref[:, sl])
        u = _dot(h, wu_ref[:, sl])
        a = (g * jax.nn.sigmoid(g) * u).astype(BF16)
        y = y + _dot(a, wd_ref[sl, :])
    o_ref[...] = x + 0.5 * _rms(y, gpost_ref[...])


def _ffn(x, g_pre, g_post, w_gate, w_up, w_down, tm):
    n, d = x.shape
    d_ff = w_gate.shape[1]
    f_chunk = d_ff // 2 if (d_ff // 2) % LANES == 0 else d_ff
    return pl.pallas_call(
        functools.partial(_ffn_kernel, f_chunk=f_chunk),
        grid=(n // tm,),
        in_specs=[_row_spec(tm, d), _const_spec((1, d)), _const_spec((1, d)),
                  _const_spec((d, d_ff)), _const_spec((d, d_ff)), _const_spec((d_ff, d))],
        out_specs=_row_spec(tm, d),
        out_shape=jax.ShapeDtypeStruct((n, d), F32),
        compiler_params=_params(1),
        name="ffn",
    )(x, g_pre, g_post, w_gate, w_up, w_down)


def _rope(blk, tc, ts1, ts2):
    return blk * tc + pltpu.roll(blk, 16, 1) * ts1 + pltpu.roll(blk, C_PAD - 16, 1) * ts2


def _mixer_in_kernel(x_ref, gpre_ref, win_ref, gcq_ref, gckv_ref, wuq_ref, wukv_ref,
                     tc_ref, ts1_ref, ts2_ref,
                     qa_ref, kab_ref, vab_ref, qb_ref, kbb_ref, vbb_ref,
                     ka_ref, va_ref, kb_ref, vb_ref, ckv_ref, krb_ref,
                     qc_ref, kc_ref, vc_ref, *, q_transposed):
    w = W_AB
    q_rank = gcq_ref.shape[1]
    kv_rank = gckv_ref.shape[1]
    u = _rms(x_ref[...], gpre_ref[...]).astype(BF16)
    proj = _dot(u, win_ref[...])
    qa, ka, va = proj[:, 0:w], proj[:, w:2 * w], proj[:, 2 * w:3 * w]
    qb, kb, vb = proj[:, 3 * w:4 * w], proj[:, 4 * w:5 * w], proj[:, 5 * w:6 * w]
    o = 6 * w
    cq = proj[:, o:o + q_rank]
    ckv = proj[:, o + q_rank:o + q_rank + kv_rank]
    krb = proj[:, o + q_rank + kv_rank:o + q_rank + kv_rank + C_PAD]
    qa_ref[...] = (qa * ATTN_SCALE).astype(BF16)
    qb_ref[...] = (qb * ATTN_SCALE).astype(BF16)
    ka_ref[...] = ka
    va_ref[...] = va
    kb_ref[...] = kb
    vb_ref[...] = vb
    kab_ref[...] = ka.astype(BF16)
    vab_ref[...] = va.astype(BF16)
    kbb_ref[...] = kb.astype(BF16)
    vbb_ref[...] = vb.astype(BF16)
    tc, ts1, ts2 = tc_ref[...], ts1_ref[...], ts2_ref[...]
    cqn = _rms(cq, gcq_ref[...]).astype(BF16)
    qall = _dot(cqn, wuq_ref[...])
    ckvn = _rms(ckv, gckv_ref[...])
    ckv_ref[...] = ckvn
    kv = _dot(ckvn.astype(BF16), wukv_ref[...])
    krot = _rope(krb, tc, ts1, ts2)
    krb_ref[...] = krot
    for h in range(H_C):
        sl = slice(h * C_PAD, (h + 1) * C_PAD)
        qh = _rope(qall[:, sl], tc, ts1, ts2) * (MLA_SCALE * LOG2E)
        if q_transposed:
            qc_ref[sl, :] = qh.T.astype(BF16)
        else:
            qc_ref[:, sl] = qh.astype(BF16)
        kc_ref[:, sl] = (kv[:, sl] + krot).astype(BF16)
    if not q_transposed:
        vc_ref[...] = kv[:, H_C * C_PAD:].astype(BF16)
    else:
        ones_row = (lax.broadcasted_iota(jnp.int32, (C_VX - C_V, x_ref.shape[0]), 0) == 0).astype(BF16)
        for p in range(H_C // 2):
            o = H_C * C_PAD + p * 2 * C_V
            vt = kv[:, o:o + 2 * C_V].T
            for e in range(2):
                r = (2 * p + e) * C_VX
                vc_ref[0, r:r + C_V, :] = vt[e * C_V:(e + 1) * C_V, :].astype(BF16)
                vc_ref[0, r + C_V:r + C_VX, :] = ones_row


def _mixer_in(x, g_pre, w_main, g_cq, g_ckv, w_uq, w_ukv, tabs, tm, q_transposed=False):
    n, d = x.shape
    q_rank, kv_rank = g_cq.shape[1], g_ckv.shape[1]
    wc = H_C * C_PAD
    wv = H_C * C_V
    bf = lambda width: jax.ShapeDtypeStruct((n, width), BF16)
    f32 = lambda width: jax.ShapeDtypeStruct((n, width), F32)
    q_shape = jax.ShapeDtypeStruct((wc, n), BF16) if q_transposed else bf(wc)
    q_spec = pl.BlockSpec((wc, tm), lambda i: (0, i)) if q_transposed else _row_spec(tm, wc)
    v_shape = jax.ShapeDtypeStruct((n // tm, H_C * C_VX, tm), BF16) if q_transposed else bf(wv)
    v_spec = (pl.BlockSpec((1, H_C * C_VX, tm), lambda i: (i, 0, 0)) if q_transposed
              else _row_spec(tm, wv))
    out_shape = ([bf(W_AB)] * 6 + [f32(W_AB)] * 4 + [f32(kv_rank), f32(C_PAD)]
                 + [q_shape, bf(wc), v_shape])
    out_specs = ([_row_spec(tm, W_AB)] * 10 + [_row_spec(tm, kv_rank), _row_spec(tm, C_PAD)]
                 + [q_spec, _row_spec(tm, wc), v_spec])
    return pl.pallas_call(
        functools.partial(_mixer_in_kernel, q_transposed=q_transposed),
        grid=(n // tm,),
        in_specs=[_row_spec(tm, d), _const_spec((1, d)), _const_spec(w_main.shape),
                  _const_spec((1, q_rank)), _const_spec((1, kv_rank)),
                  _const_spec(w_uq.shape), _const_spec(w_ukv.shape),
                  _row_spec(tm, C_PAD), _row_spec(tm, C_PAD), _row_spec(tm, C_PAD)],
        out_specs=out_specs,
        out_shape=out_shape,
        compiler_params=_params(1),
        name="mixer_in",
    )(x, g_pre, w_main, g_cq, g_ckv, w_uq, w_ukv, *tabs)


def _merge_kernel(x_ref, ya_ref, yb_ref, yc_ref, gpre_ref, gpost_ref, wgates_ref,
                  wa_ref, wb_ref, wc_ref, wout_ref, o_ref, *, yc_transposed):
    x = x_ref[...]
    d = x.shape[1]
    u = _rms(x, gpre_ref[...]).astype(BF16)
    m = jnp.zeros(x.shape, F32)
    for i, (y_ref, w_ref) in enumerate(((ya_ref, wa_ref), (yb_ref, wb_ref), (yc_ref, wc_ref))):
        gate = jax.nn.sigmoid(_dot(u, wgates_ref[:, i * d:(i + 1) * d]))
        branch_dot = _dot_tn if (yc_transposed and y_ref is yc_ref) else _dot
        m = m + gate * branch_dot(y_ref[...], w_ref[...])
    mixed = _dot(m.astype(BF16), wout_ref[...])
    o_ref[...] = x + _rms(mixed, gpost_ref[...])


def _merge(x, ya, yb, yc, g_pre, g_post, w_gates, w_a, w_b, w_c, w_out, tm, yc_transposed=False):
    n, d = x.shape
    yc_spec = (pl.BlockSpec((yc.shape[0], tm), lambda i: (0, i)) if yc_transposed
               else _row_spec(tm, yc.shape[1]))
    return pl.pallas_call(
        functools.partial(_merge_kernel, yc_transposed=yc_transposed),
        grid=(n // tm,),
        in_specs=[_row_spec(tm, d), _row_spec(tm, ya.shape[1]), _row_spec(tm, yb.shape[1]),
                  yc_spec, _const_spec((1, d)), _const_spec((1, d)),
                  _const_spec(w_gates.shape), _const_spec(w_a.shape), _const_spec(w_b.shape),
                  _const_spec(w_c.shape), _const_spec(w_out.shape)],
        out_specs=_row_spec(tm, d),
        out_shape=jax.ShapeDtypeStruct((n, d), F32),
        compiler_params=_params(1),
        name="merge",
    )(x, ya, yb, yc, g_pre, g_post, w_gates, w_a, w_b, w_c, w_out)


def _head_masked(q, width):
    lane_head = lax.broadcasted_iota(jnp.int32, q.shape, 1) // width
    zero = jnp.zeros_like(q)
    return [jnp.where(lane_head == h, q, zero) for h in range(q.shape[1] // width)]


def _strict_upper_ones(n):
    j = lax.broadcasted_iota(jnp.int32, (n, n), 0)
    s = lax.broadcasted_iota(jnp.int32, (n, n), 1)
    return (j > s).astype(BF16)


def _sb_terms(qm, kt, ones_ut, causal):
    z = _dot_nt(qm, kt)
    t = jnp.log1p(jnp.exp(-jnp.abs(z)))
    log_keep = -jnp.maximum(z, 0.0) - t
    log_beta = jnp.minimum(z, 0.0) - t
    if causal is not None:
        log_keep = jnp.where(causal, log_keep, 0.0)
    hi = log_keep.astype(BF16)
    lo = (log_keep - hi.astype(F32)).astype(BF16)
    tail = _dot(hi, ones_ut) + _dot(lo, ones_ut)
    return log_beta + tail, jnp.sum(log_keep, axis=1, keepdims=True)


def _sb_apply(terms, vt, carry, acc, causal):
    logw, total = terms
    w = jnp.exp(logw + carry)
    if causal is not None:
        w = jnp.where(causal, w, 0.0)
    return carry + total, acc + _dot(w.astype(BF16), vt)


def _sb_tile(qm, kt, vt, carry, acc, ones_ut, causal):
    return _sb_apply(_sb_terms(qm, kt, ones_ut, causal), vt, carry, acc, causal)


def _sb_alive(carry):
    return jnp.max(carry) > SB_DEAD


def _softmax_tile(s, vt, m, l, acc):
    m_new = jnp.maximum(m, jnp.max(s, axis=1, keepdims=True))
    alpha = jnp.exp2(m - m_new)
    p = jnp.exp2(s - m_new)
    l = alpha * l + jnp.sum(p, axis=1, keepdims=True)
    acc = alpha * acc + _dot(p.astype(BF16), vt)
    return m_new, l, acc


def _select_heads(parts, width):
    lane_head = lax.broadcasted_iota(jnp.int32, parts[0].shape, 1) // width
    out = parts[0]
    for h in range(1, len(parts)):
        out = jnp.where(lane_head == h, parts[h], out)
    return out


def _band_kernel(q_ref, k0_ref, k1_ref, k2_ref, v0_ref, v1_ref, v2_ref, bias_ref, o_ref):
    i = pl.program_id(0)
    tq = q_ref.shape[0]
    kwin = jnp.concatenate([k0_ref[...], k1_ref[...], k2_ref[...]], axis=0)
    vwin = jnp.concatenate([v0_ref[...], v1_ref[...], v2_ref[...]], axis=0)
    col = lax.broadcasted_iota(jnp.int32, (tq, 3 * tq), 1)
    in_range = col >= (2 - i) * tq
    parts = []
    for h, qm in enumerate(_head_masked(q_ref[...], HEAD_DIM)):
        s = jnp.where(in_range, _dot_nt(qm, kwin) + bias_ref[h], NEG)
        m = jnp.max(s, axis=1, keepdims=True)
        p = jnp.exp(s - m)
        l = jnp.sum(p, axis=1, keepdims=True)
        parts.append(_dot(p.astype(BF16), vwin) / l)
    o_ref[...] = _select_heads(parts, HEAD_DIM).astype(BF16)


def _band_attention(q, k, v, bias, tq):
    n, w = q.shape
    blk = lambda back: pl.BlockSpec((tq, w), lambda i: (jnp.maximum(i - back, 0), 0))
    return pl.pallas_call(
        _band_kernel,
        grid=(n // tq,),
        in_specs=[blk(0), blk(2), blk(1), blk(0), blk(2), blk(1), blk(0), _const_spec(bias.shape)],
        out_specs=blk(0),
        out_shape=jax.ShapeDtypeStruct((n, w), BF16),
        compiler_params=_params(1),
        name="band_attn",
    )(q, k, k, k, v, v, v, bias)


def _sb_kernel(q_ref, k_ref, v_ref, o_ref, *, tile):
    i = pl.program_id(0)
    row = lax.broadcasted_iota(jnp.int32, (tile, tile), 0)
    col = lax.broadcasted_iota(jnp.int32, (tile, tile), 1)
    causal = col < row
    ones_ut = _strict_upper_ones(tile)
    width = q_ref.shape[1]
    qms = _head_masked(q_ref[...], HEAD_DIM)
    rows = lambda ref, kb: ref[pl.ds(pl.multiple_of(kb * tile, tile), tile), :]

    prev = jnp.maximum(i - 1, 0)
    k_diag, k_prev = rows(k_ref, i), rows(k_ref, prev)
    terms_diag = [_sb_terms(qm, k_diag, ones_ut, causal) for qm in qms]
    terms_prev = [_sb_terms(qm, k_prev, ones_ut, None) for qm in qms]
    v_diag, v_prev = rows(v_ref, i), rows(v_ref, prev)
    parts = []
    for qm, t_diag, t_prev in zip(qms, terms_diag, terms_prev):
        carry, acc = _sb_apply(t_diag, v_diag, jnp.zeros((tile, 1), F32), jnp.zeros((tile, width), F32),
                               causal)
        carry, acc = _sb_apply(t_prev, v_prev, jnp.where(i > 0, carry, NEG), acc, None)
        state = lax.while_loop(
            lambda st: (st[0] < i - 1) & _sb_alive(st[1]),
            lambda st, qm=qm: (st[0] + 1,) + _sb_tile(qm, rows(k_ref, i - 2 - st[0]), rows(v_ref, i - 2 - st[0]),
                                                      st[1], st[2], ones_ut, None),
            (jnp.int32(0), carry, acc))
        parts.append(state[2])
    o_ref[...] = _select_heads(parts, HEAD_DIM).astype(BF16)


def _sb_attention(q, k, v, tile):
    n, w = q.shape
    return pl.pallas_call(
        functools.partial(_sb_kernel, tile=tile),
        grid=(n // tile,),
        in_specs=[_row_spec(tile, w), _const_spec((n, w)), _const_spec((n, w))],
        out_specs=_row_spec(tile, w),
        out_shape=jax.ShapeDtypeStruct((n, w), BF16),
        compiler_params=_params(1),
        name="sb_attn",
    )(q, k, v)


def _mla_kernel(qt_ref, k_ref, vt_ref, ot_ref, s_scr, p_scr, a_scr, m_scr, acc_scr, *, tile):
    i = pl.program_id(1)
    heads = range(2)

    def scores_into(slot, kb):
        for e in heads:
            s_scr[slot, e] = _dot(k_ref[kb, :, e * C_PAD:(e + 1) * C_PAD],
                                  qt_ref[e * C_PAD:(e + 1) * C_PAD, :])

    def add_values(slot, kb):
        for e in heads:
            pv = _dot(vt_ref[kb, e * C_VX:(e + 1) * C_VX, :], p_scr[slot, e])
            acc_scr[e] = a_scr[slot, e] * acc_scr[e] + pv

    def softmax_into(slot, mask):
        for e in heads:
            s = s_scr[slot, e]
            if mask is not None:
                s = jnp.where(mask, s, NEG)
            m = m_scr[e]
            m_new = jnp.maximum(m, jnp.max(s, axis=0, keepdims=True))
            m_scr[e] = m_new
            a_scr[slot, e] = jnp.exp2(m - m_new)
            p_scr[slot, e] = jnp.exp2((s - m_new).astype(BF16))

    def stage(t, j, issue_scores, mask):
        ahead = (j + 2) % MLA_SLOTS
        if issue_scores:
            scores_into(ahead, jnp.minimum(t + 2, i))
        add_values(ahead, jnp.maximum(t - 2, 0))
        softmax_into(j, mask)

    for slot in (2, 3):
        p_scr[slot] = jnp.zeros(p_scr.shape[1:], BF16)
        a_scr[slot] = jnp.ones(a_scr.shape[1:], F32)
    m_scr[...] = jnp.full(m_scr.shape, NEG, F32)
    acc_scr[...] = jnp.zeros(acc_scr.shape, F32)
    scores_into(0, 0)
    scores_into(1, jnp.minimum(1, i))

    def four_stages(g, _):
        for j in range(MLA_SLOTS):
            stage(MLA_SLOTS * g + j, j, True, None)
        return 0
    groups = i // MLA_SLOTS
    lax.fori_loop(0, groups, four_stages, 0)

    key = lax.broadcasted_iota(jnp.int32, (tile, tile), 0)
    qry = lax.broadcasted_iota(jnp.int32, (tile, tile), 1)
    visible = (key // CHUNK) <= (qry // CHUNK)
    for rest in range(MLA_SLOTS):
        @pl.when(i % MLA_SLOTS == rest)
        def _(rest=rest):
            base = MLA_SLOTS * groups
            for j in range(rest):
                stage(base + j, j, j + 2 <= rest, None)
            stage(i, rest, False, visible)
            add_values((rest - 1) % MLA_SLOTS, jnp.maximum(i - 1, 0))
            add_values(rest, i)
            ot_ref[...] = jnp.concatenate([acc_scr[e, :C_V] / acc_scr[e, C_V:C_V + 1] for e in heads],
                                          axis=0).astype(BF16)


def _mla_attention(qt, k, vt3, tile):
    n = k.shape[0]
    pairs = H_C // 2
    n_tiles = n // tile
    k3 = k.reshape(n_tiles, tile, H_C * C_PAD)
    assert vt3.shape == (n_tiles, H_C * C_VX, tile)
    return pl.pallas_call(
        functools.partial(_mla_kernel, tile=tile),
        grid=(pairs, n_tiles),
        in_specs=[pl.BlockSpec((2 * C_PAD, tile), lambda p, i: (p, i)),
                  pl.BlockSpec((n_tiles, tile, 2 * C_PAD), lambda p, i: (0, 0, p),
                               pipeline_mode=pl.Buffered(1)),
                  pl.BlockSpec((n_tiles, 2 * C_VX, tile), lambda p, i: (0, p, 0),
                               pipeline_mode=pl.Buffered(1))],
        out_specs=pl.BlockSpec((2 * C_V, tile), lambda p, i: (p, i)),
        out_shape=jax.ShapeDtypeStruct((H_C * C_V, n), BF16),
        scratch_shapes=[pltpu.VMEM((MLA_SLOTS, 2, tile, tile), F32), pltpu.VMEM((MLA_SLOTS, 2, tile, tile), BF16),
                        pltpu.VMEM((MLA_SLOTS, 2, 1, tile), F32), pltpu.VMEM((2, 1, tile), F32),
                        pltpu.VMEM((2, C_VX, tile), F32)],
        compiler_params=_params(2),
        name="mla_attn",
    )(qt, k3, vt3)


def _sample_attn_kernel(qa_ref, ka_ref, va_ref, cak_ref, cav_ref, biasc_ref, biasn_ref,
                        qb_ref, kb_ref, vb_ref, cbk_ref, cbv_ref,
                        qc_ref, kc_ref, vc_ref, ckv_ref, ckr_ref, wukv_ref,
                        ya_ref, yb_ref, yc_ref, *, tile):
    ds = qa_ref.shape[1]
    past = cbk_ref.shape[2]

    ka, va, cak, cav = ka_ref[0], va_ref[0], cak_ref[0, 0].astype(BF16), cav_ref[0, 0].astype(BF16)
    parts = []
    for h, qm in enumerate(_head_masked(qa_ref[0], HEAD_DIM)):
        s_c = _dot_nt(qm, cak) + biasc_ref[h]
        s_n = _dot_nt(qm, ka) + biasn_ref[h]
        m = jnp.maximum(jnp.max(s_c, axis=1, keepdims=True), jnp.max(s_n, axis=1, keepdims=True))
        p_c = jnp.exp(s_c - m)
        p_n = jnp.exp(s_n - m)
        l = jnp.sum(p_c, axis=1, keepdims=True) + jnp.sum(p_n, axis=1, keepdims=True)
        parts.append((_dot(p_c.astype(BF16), cav) + _dot(p_n.astype(BF16), va)) / l)
    ya_ref[0] = _select_heads(parts, HEAD_DIM).astype(BF16)

    row = lax.broadcasted_iota(jnp.int32, (ds, ds), 0)
    col = lax.broadcasted_iota(jnp.int32, (ds, ds), 1)
    causal = col < row
    ones_new = _strict_upper_ones(ds)
    ones_ut = _strict_upper_ones(tile)
    kb, vb = kb_ref[0], vb_ref[0]
    n_tiles = past // tile
    parts = []
    for qm in _head_masked(qb_ref[0], HEAD_DIM):
        state = _sb_tile(qm, kb, vb, jnp.zeros((ds, 1), F32), jnp.zeros((ds, W_AB), F32),
                         ones_new, causal)

        def cache_tile(st, qm=qm):
            start = pl.multiple_of((n_tiles - 1 - st[0]) * tile, tile)
            return (st[0] + 1,) + _sb_tile(qm, cbk_ref[0, 0, pl.ds(start, tile), :].astype(BF16),
                                           cbv_ref[0, 0, pl.ds(start, tile), :].astype(BF16),
                                           st[1], st[2], ones_ut, None)
        state = lax.while_loop(lambda st: (st[0] < n_tiles) & _sb_alive(st[1]), cache_tile,
                               (jnp.int32(0),) + state)
        parts.append(state[2])
    yb_ref[0] = _select_heads(parts, HEAD_DIM).astype(BF16)

    wk = H_C * C_PAD
    pairs = H_C // 2
    qc = qc_ref[0]
    qms = []
    for p in range(pairs):
        qms.extend(_head_masked(qc[:, p * 2 * C_PAD:(p + 1) * 2 * C_PAD], C_PAD))

    def attend(kall, vall, state):
        out = []
        for h, (m, l, acc) in enumerate(state):
            p = h // 2
            s = _dot_nt(qms[h], kall[:, p * 2 * C_PAD:(p + 1) * 2 * C_PAD])
            out.append(_softmax_tile(s, vall[:, p * 2 * C_V:(p + 1) * 2 * C_V], m, l, acc))
        return tuple(out)

    tile_c = SAMPLE_LATENT_TILE if past % SAMPLE_LATENT_TILE == 0 else tile

    place = (lax.broadcasted_iota(jnp.int32, (C_ROPE, C_PAD), 1)
             == lax.broadcasted_iota(jnp.int32, (C_ROPE, C_PAD), 0) + C_NOPE).astype(BF16)

    def cache_step(j, state):
        start = pl.multiple_of(j * tile_c, tile_c)
        lat = ckv_ref[0, 0, pl.ds(start, tile_c), :].astype(BF16)
        kv = _dot(lat, wukv_ref[...])
        krb = _dot(ckr_ref[0, 0, pl.ds(start, tile_c), :].astype(BF16), place)
        kall = jnp.concatenate([kv[:, h * C_PAD:(h + 1) * C_PAD] + krb for h in range(H_C)],
                               axis=1).astype(BF16)
        return attend(kall, kv[:, wk:].astype(BF16), state)

    init = tuple((jnp.full((ds, 1), NEG, F32), jnp.zeros((ds, 1), F32),
                  jnp.zeros((ds, 2 * C_V), F32)) for _ in range(H_C))
    state = lax.fori_loop(0, past // tile_c, cache_step, init)
    state = attend(kc_ref[0], vc_ref[0], state)
    outs = []
    for p in range(pairs):
        outs.append(_select_heads([state[2 * p + e][2] / state[2 * p + e][1] for e in range(2)], C_V))
    yc_ref[0] = jnp.concatenate(outs, axis=1).astype(BF16)


def _sample_attention(new, caches, l, bias_c, bias_n, w_ukv, tile):
    qa, ka, va, qb, kb, vb, qc, kc, vc = new
    cak, cav, cbk, cbv, ckv, ckr = caches
    nb, ds, _ = qa.shape
    req = lambda a: pl.BlockSpec((1,) + a.shape[1:], lambda b: (b, 0, 0))
    cache = lambda a: pl.BlockSpec((1, 1) + a.shape[2:], lambda b: (l, b, 0, 0))
    ins = [qa, ka, va, cak, cav, bias_c, bias_n, qb, kb, vb, cbk, cbv, qc, kc, vc, ckv, ckr, w_ukv]
    specs = [req(a) for a in ins]
    for idx in (3, 4, 10, 11, 15, 16):
        specs[idx] = cache(ins[idx])
    for idx in (5, 6, 17):
        specs[idx] = _const_spec(ins[idx].shape)
    out_shape = [jax.ShapeDtypeStruct((nb, ds, W_AB), BF16), jax.ShapeDtypeStruct((nb, ds, W_AB), BF16),
                 jax.ShapeDtypeStruct((nb, ds, H_C * C_V), BF16)]
    return pl.pallas_call(
        functools.partial(_sample_attn_kernel, tile=tile),
        grid=(nb,),
        in_specs=specs,
        out_specs=[req(s) for s in out_shape],
        out_shape=out_shape,
        compiler_params=_params(1),
        name="sample_attn",
    )(*ins)


def _rope_tables(pos):
    half = C_ROPE // 2
    freq = ROPE_THETA ** (-jnp.arange(half, dtype=F32) / half)
    ang = pos.astype(F32)[:, None] * freq[None, :]
    cos, sin = jnp.cos(ang), jnp.sin(ang)
    n = pos.shape[0]
    z = lambda w: jnp.zeros((n, w), F32)
    tc = jnp.concatenate([jnp.ones((n, C_NOPE), F32), cos, cos, z(C_PAD - C_NOPE - C_ROPE)], axis=1)
    ts1 = jnp.concatenate([z(C_NOPE + half), sin, z(C_PAD - C_NOPE - C_ROPE)], axis=1)
    ts2 = jnp.concatenate([z(C_NOPE), -sin, z(C_PAD - C_NOPE - half)], axis=1)
    return tc, ts1, ts2


def _band_bias(rel_bias, q0, nq, k0, nk):
    period = 1 << int(np.ceil(np.log2(nq + nk)))
    d = np.arange(period)
    d = np.where(d >= nk, d - period, d)
    idx = np.clip((q0 - k0) - d, -REL_CLIP, REL_CLIP) + REL_CLIP
    g = rel_bias.astype(F32)[:, idx]
    toe = jnp.tile(g, (1, nq))[:, :nq * (period - 1)].reshape(-1, nq, period - 1)[:, :, :nk]
    q_pos, k_pos = q0 + np.arange(nq), k0 + np.arange(nk)
    qc, kc = q_pos[:, None] // CHUNK, k_pos[None, :] // CHUNK
    vis = (kc <= qc) & (kc >= qc - BAND_CHUNKS)
    return jnp.where(jnp.asarray(vis)[None], toe, NEG)


def _cast_kernel(w_ref, o_ref):
    o_ref[...] = w_ref[0].astype(BF16)


def _layer_bf16(w, l):
    _, rows, cols = w.shape
    rb = rows // 4 if rows % 64 == 0 else rows
    return pl.pallas_call(
        _cast_kernel,
        grid=(rows // rb,),
        in_specs=[pl.BlockSpec((1, rb, cols), lambda i: (l, i, 0))],
        out_specs=pl.BlockSpec((rb, cols), lambda i: (i, 0)),
        out_shape=jax.ShapeDtypeStruct((rows, cols), BF16),
        compiler_params=_params(1),
        name="cast_bf16",
    )(w)


def _layer_weights(l, w_in, w_uq, w_uk, w_uv):
    d = w_in.shape[1]
    q_rank, kv_rank = w_uq.shape[1], w_uk.shape[1]
    o = 6 * W_AB + q_rank + kv_rank
    zc = lambda w: jnp.zeros((d, w), w_in.dtype)
    w_main = jnp.concatenate([w_in[l, :, :o], zc(C_NOPE), w_in[l, :, o:o + C_ROPE],
                              zc(C_PAD - C_NOPE - C_ROPE)], axis=1).astype(BF16)
    w_gates = w_in[l, :, o + C_ROPE:].astype(BF16)
    pad_heads = lambda w: jnp.pad(w, ((0, 0), (0, 0), (0, C_PAD - w.shape[2]))).reshape(w.shape[0], -1)
    w_uq_p = pad_heads(w_uq[l]).astype(BF16)
    w_ukv = jnp.concatenate([pad_heads(w_uk[l]), w_uv[l].reshape(kv_rank, -1)], axis=1).astype(BF16)
    return w_main, w_gates, w_uq_p, w_ukv


def kernel(x_prompt, x_sample, cache_a_k, cache_a_v, cache_b_k, cache_b_v, cache_c_kv, cache_c_kr, ffn1_norm_pre, ffn1_norm_post, ffn1_w_gate, ffn1_w_up, ffn1_w_down, mix_norm_pre, mix_norm_post, w_in, cq_norm, ckv_norm, w_uq, w_uk, w_uv, rel_bias_a, w_br_a, w_br_b, w_br_c, w_out, ffn2_norm_pre, ffn2_norm_post, ffn2_w_gate, ffn2_w_up, ffn2_w_down):
    batch, seq, d = x_prompt.shape
    nb, ds, _ = x_sample.shape
    depth = w_in.shape[0]
    past = cache_b_k.shape[2]
    win_cache = cache_a_k.shape[2]
    assert batch == 1, "prompt group is a single sequence"
    tile = 256
    tm = 512 if seq % 512 == 0 else tile
    tile_c = tm
    assert seq % tile == 0 and past % tile == 0 and tile % CHUNK == 0 and 2 * tile >= WIN_A
    assert past % CHUNK + ds <= CHUNK
    ns = nb * ds

    xp = x_prompt.reshape(seq, d)
    xs = x_sample.reshape(ns, d)
    tabs_p = _rope_tables(jnp.arange(seq))
    tabs_s = _rope_tables(jnp.tile(past + jnp.arange(ds), nb))

    row = lambda g, l: g[l][None, :]
    base = 2 * tile

    merged_heads = lambda c: c.reshape(c.shape[:3] + (-1,))
    caches = (merged_heads(cache_a_k), merged_heads(cache_a_v), merged_heads(cache_b_k),
              merged_heads(cache_b_v), cache_c_kv, cache_c_kr)
    states_p, states_s = [], []
    for l in range(depth):
        w_main, w_gates, w_uq_p, w_ukv = _layer_weights(l, w_in, w_uq, w_uk, w_uv)
        ffn1 = (row(ffn1_norm_pre, l), row(ffn1_norm_post, l), _layer_bf16(ffn1_w_gate, l),
                _layer_bf16(ffn1_w_up, l), _layer_bf16(ffn1_w_down, l))
        ffn2 = (row(ffn2_norm_pre, l), row(ffn2_norm_post, l), _layer_bf16(ffn2_w_gate, l),
                _layer_bf16(ffn2_w_up, l), _layer_bf16(ffn2_w_down, l))
        mix_in = (row(mix_norm_pre, l), w_main, row(cq_norm, l), row(ckv_norm, l), w_uq_p, w_ukv)
        mrg = (row(mix_norm_pre, l), row(mix_norm_post, l), w_gates, _layer_bf16(w_br_a, l),
               _layer_bf16(w_br_b, l), _layer_bf16(w_br_c, l), _layer_bf16(w_out, l))
        bias_p = _band_bias(rel_bias_a[l], base, tile, base - 2 * tile, 3 * tile)
        bias_c = _band_bias(rel_bias_a[l], past, ds, past - win_cache, win_cache)
        bias_n = _band_bias(rel_bias_a[l], past, ds, past, ds)

        xp = _ffn(xp, *ffn1, tm)
        (qa, kab, vab, qb, kbb, vbb, ka, va, kb, vb, ckv, krb, qc, kc, vc) = _mixer_in(
            xp, *mix_in, tabs_p, tm, q_transposed=True)
        ya = _band_attention(qa, kab, vab, bias_p, tile)
        yb = _sb_attention(qb, kbb, vbb, tile)
        yc = _mla_attention(qc, kc, vc, tile_c)
        xp = _merge(xp, ya, yb, yc, *mrg, tm, yc_transposed=True)
        xp = _ffn(xp, *ffn2, tm)
        win = min(WIN_A, seq)
        heads = lambda t, h: t.reshape(1, t.shape[0], h, t.shape[1] // h)
        states_p.append((heads(ka[-win:], H_A), heads(va[-win:], H_A), heads(kb, H_B), heads(vb, H_B),
                         ckv[None], krb[None, :, C_NOPE:C_NOPE + C_ROPE]))

        xs = _ffn(xs, *ffn1, ns)
        (qa, kab, vab, qb, kbb, vbb, ka, va, kb, vb, ckv, krb, qc, kc, vc) = _mixer_in(
            xs, *mix_in, tabs_s, ns)
        per_req = lambda t: t.reshape(nb, ds, t.shape[1])
        new = tuple(per_req(t) for t in (qa, kab, vab, qb, kbb, vbb, qc, kc, vc))
        ya, yb, yc = _sample_attention(new, caches, l, bias_c, bias_n, w_ukv, tile)
        xs = _merge(xs, ya.reshape(ns, -1), yb.reshape(ns, -1), yc.reshape(ns, -1), *mrg, ns)
        xs = _ffn(xs, *ffn2, ns)
        heads_s = lambda t, h: t.reshape(nb, ds, h, t.shape[1] // h)
        states_s.append((heads_s(ka, H_A), heads_s(va, H_A), heads_s(kb, H_B), heads_s(vb, H_B),
                         ckv.reshape(nb, ds, -1), krb[:, C_NOPE:C_NOPE + C_ROPE].reshape(nb, ds, C_ROPE)))

    stack = lambda states, i: jnp.stack([s[i] for s in states], axis=0)
    return ((xp.reshape(batch, seq, d), xs.reshape(nb, ds, d))
            + tuple(stack(states_p, i) for i in range(6))
            + tuple(stack(states_s, i) for i in range(6)))
```

```python
import functools

import numpy as np
import jax
import jax.numpy as jnp
from jax import lax
from jax.experimental import pallas as pl
from jax.experimental.pallas import tpu as pltpu

F32 = jnp.float32
BF16 = jnp.bfloat16

CHUNK = 64
BAND_CHUNKS = 8
WIN_A = BAND_CHUNKS * CHUNK
HEAD_DIM = 64
H_A = 4
H_B = 4
H_C = 8
C_NOPE = 64
C_ROPE = 32
C_V = 64
REL_CLIP = 128
ROPE_THETA = 10000.0
EPS = 1e-6
N_BRANCH = 3
W_AB = H_A * HEAD_DIM
C_PAD = 128
ATTN_SCALE = HEAD_DIM ** -0.5
MLA_SCALE = (C_NOPE + C_ROPE) ** -0.5
LOG2E = float(np.log2(np.e))
C_VX = C_V + 16
C_ONE = C_NOPE + C_ROPE
NORM_MARGIN = 1.02
MLA_MIN_DENOM = 2.0 ** -60
MLA_UNROLL = 4
SAMPLE_LATENT_TILE = 1024
NEG = -1e30
SB_DEAD = -104.0

LANES = 128
MXU_WIDTH = 256
VMEM_LIMIT = 52 * 1024 * 1024

NT_DIMS = (((1,), (1,)), ((), ()))
TN_DIMS = (((0,), (0,)), ((), ()))


def _rms(x, g):
    return x * lax.rsqrt(jnp.mean(x * x, axis=-1, keepdims=True) + EPS) * g


def _dot(a, b):
    return jnp.dot(a, b, preferred_element_type=F32)


def _dot_nt(a, b):
    return lax.dot_general(a, b, NT_DIMS, preferred_element_type=F32)


def _dot_tn(a, b):
    return lax.dot_general(a, b, TN_DIMS, preferred_element_type=F32)


def _const_spec(shape):
    zeros = (0,) * len(shape)
    return pl.BlockSpec(shape, lambda *_: zeros, pipeline_mode=pl.Buffered(1))


def _row_spec(tm, width):
    return pl.BlockSpec((tm, width), lambda i: (i, 0))


def _params(n_grid):
    return pltpu.CompilerParams(dimension_semantics=("arbitrary",) * n_grid,
                                vmem_limit_bytes=VMEM_LIMIT)


def _ffn_kernel(x_ref, gpre_ref, gpost_ref, wg_ref, wu_ref, wd_ref, o_ref, *, f_cuts):
    x = x_ref[...]
    h = _rms(x, gpre_ref[...]).astype(BF16)
    y = jnp.zeros(x.shape, F32)
    for lo, hi in zip(f_cuts[:-1], f_cuts[1:]):
        sl = slice(lo, hi)
        g = _dot(h, wg_ref[:, sl])
        u = _dot(h, wu_ref[:, sl])
        a = (g * jax.nn.sigmoid(g) * u).astype(BF16)
        y = y + _dot(a, wd_ref[sl, :])
    o_ref[...] = x + 0.5 * _rms(y, gpost_ref[...])


def _ffn(x, g_pre, g_post, w_gate, w_up, w_down, tm):
    n, d = x.shape
    d_ff = w_gate.shape[1]
    cut = (d_ff // 2) // MXU_WIDTH * MXU_WIDTH
    f_cuts = (0, cut, d_ff) if 0 < cut < d_ff else (0, d_ff)
    return pl.pallas_call(
        functools.partial(_ffn_kernel, f_cuts=f_cuts),
        grid=(n // tm,),
        in_specs=[_row_spec(tm, d), _const_spec((1, d)), _const_spec((1, d)),
                  _const_spec((d, d_ff)), _const_spec((d, d_ff)), _const_spec((d_ff, d))],
        out_specs=_row_spec(tm, d),
        out_shape=jax.ShapeDtypeStruct((n, d), F32),
        compiler_params=_params(1),
        name="ffn",
    )(x, g_pre, g_post, w_gate, w_up, w_down)


def _rope(blk, tc, ts1, ts2):
    return blk * tc + pltpu.roll(blk, 16, 1) * ts1 + pltpu.roll(blk, C_PAD - 16, 1) * ts2


def _mixer_in_kernel(x_ref, gpre_ref, win_ref, gcq_ref, gckv_ref, wuq_ref, wukv_ref,
                     tc_ref, ts1_ref, ts2_ref,
                     qa_ref, kab_ref, vab_ref, qb_ref, kbb_ref, vbb_ref,
                     ka_ref, va_ref, kb_ref, vb_ref, ckv_ref, krb_ref,
                     qc_ref, kc_ref, vc_ref, kn_ref, *, q_transposed):
    w = W_AB
    q_rank = gcq_ref.shape[1]
    kv_rank = gckv_ref.shape[1]
    u = _rms(x_ref[...], gpre_ref[...]).astype(BF16)
    o = 6 * w
    lat = _dot(u, win_ref[:, o:])
    cq = lat[:, :q_rank]
    ckv = lat[:, q_rank:q_rank + kv_rank]
    krb = lat[:, q_rank + kv_rank:q_rank + kv_rank + C_PAD]
    proj = _dot(u, win_ref[:, :o])
    qa, ka, va = proj[:, 0:w], proj[:, w:2 * w], proj[:, 2 * w:3 * w]
    qb, kb, vb = proj[:, 3 * w:4 * w], proj[:, 4 * w:5 * w], proj[:, 5 * w:6 * w]
    qa_ref[...] = (qa * ATTN_SCALE).astype(BF16)
    qb_ref[...] = (qb * ATTN_SCALE).astype(BF16)
    ka_ref[...] = ka
    va_ref[...] = va
    kb_ref[...] = kb
    vb_ref[...] = vb
    kab_ref[...] = ka.astype(BF16)
    vab_ref[...] = va.astype(BF16)
    kbb_ref[...] = kb.astype(BF16)
    vbb_ref[...] = vb.astype(BF16)
    tc, ts1, ts2 = tc_ref[...], ts1_ref[...], ts2_ref[...]
    cqn = _rms(cq, gcq_ref[...]).astype(BF16)
    qall = _dot(cqn, wuq_ref[...])
    ckvn = _rms(ckv, gckv_ref[...])
    ckv_ref[...] = ckvn
    kv = _dot(ckvn.astype(BF16), wukv_ref[...])
    krot = _rope(krb, tc, ts1, ts2)
    krb_ref[...] = krot
    one_lane = (lax.broadcasted_iota(jnp.int32, (1, C_PAD), 1) == C_ONE).astype(F32)
    for h in range(H_C):
        sl = slice(h * C_PAD, (h + 1) * C_PAD)
        qh = _rope(qall[:, sl], tc, ts1, ts2) * (MLA_SCALE * LOG2E)
        if q_transposed:
            qt = qh.T
            norm = jnp.sqrt(jnp.sum(qt * qt, axis=0, keepdims=True)) * NORM_MARGIN
            row = lax.broadcasted_iota(jnp.int32, qt.shape, 0)
            qc_ref[sl, :] = jnp.where(row == C_ONE, norm, qt).astype(BF16)
        else:
            qc_ref[:, sl] = qh.astype(BF16)
        kh = kv[:, sl] + krot
        kc_ref[:, sl] = (kh + one_lane).astype(BF16)
        k_norm = jnp.sqrt(jnp.max(jnp.sum(kh * kh, axis=1, keepdims=True), axis=0, keepdims=True))
        kn_ref[0, :, sl] = jnp.broadcast_to(k_norm * NORM_MARGIN, (1, C_PAD))
    if not q_transposed:
        vc_ref[...] = kv[:, H_C * C_PAD:].astype(BF16)
    else:
        ones_row = (lax.broadcasted_iota(jnp.int32, (C_VX - C_V, x_ref.shape[0]), 0) == 0).astype(BF16)
        for p in range(H_C // 2):
            o = H_C * C_PAD + p * 2 * C_V
            vt = kv[:, o:o + 2 * C_V].T
            for e in range(2):
                r = (2 * p + e) * C_VX
                vc_ref[0, r:r + C_V, :] = vt[e * C_V:(e + 1) * C_V, :].astype(BF16)
                vc_ref[0, r + C_V:r + C_VX, :] = ones_row


def _mixer_in(x, g_pre, w_main, g_cq, g_ckv, w_uq, w_ukv, tabs, tm, q_transposed=False):
    n, d = x.shape
    q_rank, kv_rank = g_cq.shape[1], g_ckv.shape[1]
    wc = H_C * C_PAD
    wv = H_C * C_V
    bf = lambda width: jax.ShapeDtypeStruct((n, width), BF16)
    f32 = lambda width: jax.ShapeDtypeStruct((n, width), F32)
    q_shape = jax.ShapeDtypeStruct((wc, n), BF16) if q_transposed else bf(wc)
    q_spec = pl.BlockSpec((wc, tm), lambda i: (0, i)) if q_transposed else _row_spec(tm, wc)
    v_shape = jax.ShapeDtypeStruct((n // tm, H_C * C_VX, tm), BF16) if q_transposed else bf(wv)
    v_spec = (pl.BlockSpec((1, H_C * C_VX, tm), lambda i: (i, 0, 0)) if q_transposed
              else _row_spec(tm, wv))
    out_shape = ([bf(W_AB)] * 6 + [f32(W_AB)] * 4 + [f32(kv_rank), f32(C_PAD)]
                 + [q_shape, bf(wc), v_shape, jax.ShapeDtypeStruct((n // tm, 1, wc), F32)])
    out_specs = ([_row_spec(tm, W_AB)] * 10 + [_row_spec(tm, kv_rank), _row_spec(tm, C_PAD)]
                 + [q_spec, _row_spec(tm, wc), v_spec, pl.BlockSpec((1, 1, wc), lambda i: (i, 0, 0))])
    return pl.pallas_call(
        functools.partial(_mixer_in_kernel, q_transposed=q_transposed),
        grid=(n // tm,),
        in_specs=[_row_spec(tm, d), _const_spec((1, d)), _const_spec(w_main.shape),
                  _const_spec((1, q_rank)), _const_spec((1, kv_rank)),
                  _const_spec(w_uq.shape), _const_spec(w_ukv.shape),
                  _row_spec(tm, C_PAD), _row_spec(tm, C_PAD), _row_spec(tm, C_PAD)],
        out_specs=out_specs,
        out_shape=out_shape,
        compiler_params=_params(1),
        name="mixer_in",
    )(x, g_pre, w_main, g_cq, g_ckv, w_uq, w_ukv, *tabs)


def _merge_kernel(x_ref, ya_ref, yb_ref, yc_ref, gpre_ref, gpost_ref, wgates_ref,
                  wa_ref, wb_ref, wc_ref, wout_ref, o_ref, *, yc_transposed):
    x = x_ref[...]
    d = x.shape[1]
    u = _rms(x, gpre_ref[...]).astype(BF16)
    m = jnp.zeros(x.shape, F32)
    for i, (y_ref, w_ref) in enumerate(((ya_ref, wa_ref), (yb_ref, wb_ref), (yc_ref, wc_ref))):
        gate = jax.nn.sigmoid(_dot(u, wgates_ref[:, i * d:(i + 1) * d]))
        branch_dot = _dot_tn if (yc_transposed and y_ref is yc_ref) else _dot
        m = m + gate * branch_dot(y_ref[...], w_ref[...])
    mixed = _dot(m.astype(BF16), wout_ref[...])
    o_ref[...] = x + _rms(mixed, gpost_ref[...])


def _merge(x, ya, yb, yc, g_pre, g_post, w_gates, w_a, w_b, w_c, w_out, tm, yc_transposed=False):
    n, d = x.shape
    yc_spec = (pl.BlockSpec((yc.shape[0], tm), lambda i: (0, i)) if yc_transposed
               else _row_spec(tm, yc.shape[1]))
    return pl.pallas_call(
        functools.partial(_merge_kernel, yc_transposed=yc_transposed),
        grid=(n // tm,),
        in_specs=[_row_spec(tm, d), _row_spec(tm, ya.shape[1]), _row_spec(tm, yb.shape[1]),
                  yc_spec, _const_spec((1, d)), _const_spec((1, d)),
                  _const_spec(w_gates.shape), _const_spec(w_a.shape), _const_spec(w_b.shape),
                  _const_spec(w_c.shape), _const_spec(w_out.shape)],
        out_specs=_row_spec(tm, d),
        out_shape=jax.ShapeDtypeStruct((n, d), F32),
        compiler_params=_params(1),
        name="merge",
    )(x, ya, yb, yc, g_pre, g_post, w_gates, w_a, w_b, w_c, w_out)


def _head_masked(q, width):
    lane_head = lax.broadcasted_iota(jnp.int32, q.shape, 1) // width
    zero = jnp.zeros_like(q)
    return [jnp.where(lane_head == h, q, zero) for h in range(q.shape[1] // width)]


def _strict_upper_ones(n):
    j = lax.broadcasted_iota(jnp.int32, (n, n), 0)
    s = lax.broadcasted_iota(jnp.int32, (n, n), 1)
    return (j > s).astype(BF16)


def _sb_terms(qm, kt, ones_ut, causal):
    z = _dot_nt(qm, kt)
    t = jnp.log1p(jnp.exp(-jnp.abs(z)))
    log_keep = -jnp.maximum(z, 0.0) - t
    log_beta = jnp.minimum(z, 0.0) - t
    if causal is not None:
        log_keep = jnp.where(causal, log_keep, 0.0)
    hi = log_keep.astype(BF16)
    lo = (log_keep - hi.astype(F32)).astype(BF16)
    tail = _dot(hi, ones_ut) + _dot(lo, ones_ut)
    return log_beta + tail, jnp.sum(log_keep, axis=1, keepdims=True)


def _sb_apply(terms, vt, carry, acc, causal):
    logw, total = terms
    w = jnp.exp(logw + carry)
    if causal is not None:
        w = jnp.where(causal, w, 0.0)
    return carry + total, acc + _dot(w.astype(BF16), vt)


def _sb_tile(qm, kt, vt, carry, acc, ones_ut, causal):
    return _sb_apply(_sb_terms(qm, kt, ones_ut, causal), vt, carry, acc, causal)


def _sb_alive(carry):
    return jnp.max(carry) > SB_DEAD


def _softmax_tile(s, vt, m, l, acc):
    m_new = jnp.maximum(m, jnp.max(s, axis=1, keepdims=True))
    alpha = jnp.exp2(m - m_new)
    p = jnp.exp2(s - m_new)
    l = alpha * l + jnp.sum(p, axis=1, keepdims=True)
    acc = alpha * acc + _dot(p.astype(BF16), vt)
    return m_new, l, acc


def _select_heads(parts, width):
    lane_head = lax.broadcasted_iota(jnp.int32, parts[0].shape, 1) // width
    out = parts[0]
    for h in range(1, len(parts)):
        out = jnp.where(lane_head == h, parts[h], out)
    return out


def _band_kernel(q_ref, k0_ref, k1_ref, k2_ref, v0_ref, v1_ref, v2_ref, bias_ref, o_ref):
    i = pl.program_id(0)
    tq = q_ref.shape[0]
    kwin = jnp.concatenate([k0_ref[...], k1_ref[...], k2_ref[...]], axis=0)
    vwin = jnp.concatenate([v0_ref[...], v1_ref[...], v2_ref[...]], axis=0)
    col = lax.broadcasted_iota(jnp.int32, (tq, 3 * tq), 1)
    in_range = col >= (2 - i) * tq
    parts = []
    for h, qm in enumerate(_head_masked(q_ref[...], HEAD_DIM)):
        s = jnp.where(in_range, _dot_nt(qm, kwin) + bias_ref[h], NEG)
        m = jnp.max(s, axis=1, keepdims=True)
        p = jnp.exp(s - m)
        l = jnp.sum(p, axis=1, keepdims=True)
        parts.append(_dot(p.astype(BF16), vwin) / l)
    o_ref[...] = _select_heads(parts, HEAD_DIM).astype(BF16)


def _band_attention(q, k, v, bias, tq):
    n, w = q.shape
    blk = lambda back: pl.BlockSpec((tq, w), lambda i: (jnp.maximum(i - back, 0), 0))
    return pl.pallas_call(
        _band_kernel,
        grid=(n // tq,),
        in_specs=[blk(0), blk(2), blk(1), blk(0), blk(2), blk(1), blk(0), _const_spec(bias.shape)],
        out_specs=blk(0),
        out_shape=jax.ShapeDtypeStruct((n, w), BF16),
        compiler_params=_params(1),
        name="band_attn",
    )(q, k, k, k, v, v, v, bias)


def _sb_kernel(q_ref, k_ref, v_ref, o_ref, *, tile):
    i = pl.program_id(0)
    row = lax.broadcasted_iota(jnp.int32, (tile, tile), 0)
    col = lax.broadcasted_iota(jnp.int32, (tile, tile), 1)
    causal = col < row
    ones_ut = _strict_upper_ones(tile)
    width = q_ref.shape[1]
    qms = _head_masked(q_ref[...], HEAD_DIM)
    rows = lambda ref, kb: ref[pl.ds(pl.multiple_of(kb * tile, tile), tile), :]

    prev = jnp.maximum(i - 1, 0)
    k_diag, k_prev = rows(k_ref, i), rows(k_ref, prev)
    terms_diag = [_sb_terms(qm, k_diag, ones_ut, causal) for qm in qms]
    terms_prev = [_sb_terms(qm, k_prev, ones_ut, None) for qm in qms]
    v_diag, v_prev = rows(v_ref, i), rows(v_ref, prev)
    parts = []
    for qm, t_diag, t_prev in zip(qms, terms_diag, terms_prev):
        carry, acc = _sb_apply(t_diag, v_diag, jnp.zeros((tile, 1), F32), jnp.zeros((tile, width), F32),
                               causal)
        carry, acc = _sb_apply(t_prev, v_prev, jnp.where(i > 0, carry, NEG), acc, None)
        state = lax.while_loop(
            lambda st: (st[0] < i - 1) & _sb_alive(st[1]),
            lambda st, qm=qm: (st[0] + 1,) + _sb_tile(qm, rows(k_ref, i - 2 - st[0]), rows(v_ref, i - 2 - st[0]),
                                                      st[1], st[2], ones_ut, None),
            (jnp.int32(0), carry, acc))
        parts.append(state[2])
    o_ref[...] = _select_heads(parts, HEAD_DIM).astype(BF16)


def _sb_attention(q, k, v, tile):
    n, w = q.shape
    return pl.pallas_call(
        functools.partial(_sb_kernel, tile=tile),
        grid=(n // tile,),
        in_specs=[_row_spec(tile, w), _const_spec((n, w)), _const_spec((n, w))],
        out_specs=_row_spec(tile, w),
        out_shape=jax.ShapeDtypeStruct((n, w), BF16),
        compiler_params=_params(1),
        name="sb_attn",
    )(q, k, v)


def _mla_kernel(qt_ref, k_ref, vt_ref, kn_ref, ot_ref, qx_scr, p_scr, acc_scr):
    i = pl.program_id(1)
    heads = range(2)
    t = p_scr.shape[2]
    key = lax.broadcasted_iota(jnp.int32, (t, t), 0)
    qry = lax.broadcasted_iota(jnp.int32, (t, t), 1)
    visible = (key // CHUNK) <= (qry // CHUNK)
    row = lax.broadcasted_iota(jnp.int32, (C_PAD - C_ONE, t), 0)

    def load_q(shifted):
        k_bound = jnp.max(kn_ref[...], axis=0)
        for e in heads:
            lo = e * C_PAD
            qx_scr[lo:lo + C_ONE, :] = qt_ref[lo:lo + C_ONE, :]
            if shifted:
                bound = qt_ref[lo + C_ONE:lo + C_ONE + 1, :].astype(F32) * (k_bound[:, lo:lo + 1] * NORM_MARGIN)
                tail = jnp.where(row == 0, -bound, 0.0)
            else:
                tail = jnp.zeros(row.shape, F32)
            qx_scr[lo + C_ONE:lo + C_PAD, :] = tail.astype(BF16)

    def scores(kb, e):
        return _dot(k_ref[kb, :, e * C_PAD:(e + 1) * C_PAD], qx_scr[e * C_PAD:(e + 1) * C_PAD, :])

    def values(kb, e, p):
        return _dot(vt_ref[kb, e * C_VX:(e + 1) * C_VX, :], p)

    def write_out():
        ot_ref[...] = jnp.concatenate([acc_scr[e, :C_V] / acc_scr[e, C_V:C_V + 1] for e in heads],
                                      axis=0).astype(BF16)

    def fast_stage(kb, slot, mask):
        for e in heads:
            s = scores(kb, e)
            if mask is not None:
                s = jnp.where(mask, s, NEG)
            p_scr[slot, e] = jnp.exp2(s).astype(BF16)
        for e in heads:
            acc_scr[e] += values(kb, e, p_scr[slot, e])

    load_q(True)
    acc_scr[...] = jnp.zeros(acc_scr.shape, F32)

    def fast_group(g, _):
        for j in range(MLA_UNROLL):
            fast_stage(MLA_UNROLL * g + j, j, None)
        return 0
    groups = i // MLA_UNROLL
    lax.fori_loop(0, groups, fast_group, 0)

    def fast_single(kb, _):
        fast_stage(kb, 0, None)
        return 0
    lax.fori_loop(MLA_UNROLL * groups, i, fast_single, 0)
    fast_stage(i, 1, visible)
    denom = jnp.minimum(jnp.min(acc_scr[0, C_V:C_V + 1]), jnp.min(acc_scr[1, C_V:C_V + 1]))
    trusted = denom >= MLA_MIN_DENOM

    @pl.when(trusted)
    def _():
        write_out()

    @pl.when(jnp.logical_not(trusted))
    def _():
        load_q(False)
        acc_scr[...] = jnp.zeros(acc_scr.shape, F32)

        def step(kb, m):
            out = []
            for e in heads:
                s = jnp.where((kb < i) | visible, scores(kb, e), NEG)
                m_new = jnp.maximum(m[e], jnp.max(s, axis=0, keepdims=True))
                p = jnp.exp2(s - m_new).astype(BF16)
                acc_scr[e] = jnp.exp2(m[e] - m_new) * acc_scr[e] + values(kb, e, p)
                out.append(m_new)
            return tuple(out)
        lax.fori_loop(0, i + 1, step, tuple(jnp.full((1, t), NEG, F32) for _ in heads))
        write_out()


def _mla_attention(qt, k, vt3, kn):
    n = k.shape[0]
    pairs = H_C // 2
    n_tiles, _, t = vt3.shape
    assert n_tiles * t == n and kn.shape == (n_tiles, 1, H_C * C_PAD)
    k3 = k.reshape(n_tiles, t, H_C * C_PAD)
    return pl.pallas_call(
        _mla_kernel,
        grid=(pairs, n_tiles),
        in_specs=[pl.BlockSpec((2 * C_PAD, t), lambda p, i: (p, i)),
                  pl.BlockSpec((n_tiles, t, 2 * C_PAD), lambda p, i: (0, 0, p),
                               pipeline_mode=pl.Buffered(1)),
                  pl.BlockSpec((n_tiles, 2 * C_VX, t), lambda p, i: (0, p, 0),
                               pipeline_mode=pl.Buffered(1)),
                  pl.BlockSpec((n_tiles, 1, 2 * C_PAD), lambda p, i: (0, 0, p))],
        out_specs=pl.BlockSpec((2 * C_V, t), lambda p, i: (p, i)),
        out_shape=jax.ShapeDtypeStruct((H_C * C_V, n), BF16),
        scratch_shapes=[pltpu.VMEM((2 * C_PAD, t), BF16), pltpu.VMEM((MLA_UNROLL, 2, t, t), BF16),
                        pltpu.VMEM((2, C_VX, t), F32)],
        compiler_params=_params(2),
        name="mla_attn",
    )(qt, k3, vt3, kn)


def _sample_attn_kernel(qa_ref, ka_ref, va_ref, cak_ref, cav_ref, biasc_ref, biasn_ref,
                        qb_ref, kb_ref, vb_ref, cbk_ref, cbv_ref,
                        qc_ref, kc_ref, vc_ref, ckv_ref, ckr_ref, wukv_ref,
                        ya_ref, yb_ref, yc_ref, *, tile):
    ds = qa_ref.shape[1]
    past = cbk_ref.shape[2]

    ka, va, cak, cav = ka_ref[0], va_ref[0], cak_ref[0, 0].astype(BF16), cav_ref[0, 0].astype(BF16)
    parts = []
    for h, qm in enumerate(_head_masked(qa_ref[0], HEAD_DIM)):
        s_c = _dot_nt(qm, cak) + biasc_ref[h]
        s_n = _dot_nt(qm, ka) + biasn_ref[h]
        m = jnp.maximum(jnp.max(s_c, axis=1, keepdims=True), jnp.max(s_n, axis=1, keepdims=True))
        p_c = jnp.exp(s_c - m)
        p_n = jnp.exp(s_n - m)
        l = jnp.sum(p_c, axis=1, keepdims=True) + jnp.sum(p_n, axis=1, keepdims=True)
        parts.append((_dot(p_c.astype(BF16), cav) + _dot(p_n.astype(BF16), va)) / l)
    ya_ref[0] = _select_heads(parts, HEAD_DIM).astype(BF16)

    row = lax.broadcasted_iota(jnp.int32, (ds, ds), 0)
    col = lax.broadcasted_iota(jnp.int32, (ds, ds), 1)
    causal = col < row
    ones_new = _strict_upper_ones(ds)
    ones_ut = _strict_upper_ones(tile)
    kb, vb = kb_ref[0], vb_ref[0]
    n_tiles = past // tile
    parts = []
    for qm in _head_masked(qb_ref[0], HEAD_DIM):
        state = _sb_tile(qm, kb, vb, jnp.zeros((ds, 1), F32), jnp.zeros((ds, W_AB), F32),
                         ones_new, causal)

        def cache_tile(st, qm=qm):
            start = pl.multiple_of((n_tiles - 1 - st[0]) * tile, tile)
            return (st[0] + 1,) + _sb_tile(qm, cbk_ref[0, 0, pl.ds(start, tile), :].astype(BF16),
                                           cbv_ref[0, 0, pl.ds(start, tile), :].astype(BF16),
                                           st[1], st[2], ones_ut, None)
        state = lax.while_loop(lambda st: (st[0] < n_tiles) & _sb_alive(st[1]), cache_tile,
                               (jnp.int32(0),) + state)
        parts.append(state[2])
    yb_ref[0] = _select_heads(parts, HEAD_DIM).astype(BF16)

    wk = H_C * C_PAD
    pairs = H_C // 2
    qc = qc_ref[0]
    qms = []
    for p in range(pairs):
        qms.extend(_head_masked(qc[:, p * 2 * C_PAD:(p + 1) * 2 * C_PAD], C_PAD))

    def attend(kall, vall, state):
        out = []
        for h, (m, l, acc) in enumerate(state):
            p = h // 2
            s = _dot_nt(qms[h], kall[:, p * 2 * C_PAD:(p + 1) * 2 * C_PAD])
            out.append(_softmax_tile(s, vall[:, p * 2 * C_V:(p + 1) * 2 * C_V], m, l, acc))
        return tuple(out)

    tile_c = SAMPLE_LATENT_TILE if past % SAMPLE_LATENT_TILE == 0 else tile

    place = (lax.broadcasted_iota(jnp.int32, (C_ROPE, C_PAD), 1)
             == lax.broadcasted_iota(jnp.int32, (C_ROPE, C_PAD), 0) + C_NOPE).astype(BF16)

    def cache_step(j, state):
        start = pl.multiple_of(j * tile_c, tile_c)
        lat = ckv_ref[0, 0, pl.ds(start, tile_c), :].astype(BF16)
        kv = _dot(lat, wukv_ref[...])
        krb = _dot(ckr_ref[0, 0, pl.ds(start, tile_c), :].astype(BF16), place)
        kall = jnp.concatenate([kv[:, h * C_PAD:(h + 1) * C_PAD] + krb for h in range(H_C)],
                               axis=1).astype(BF16)
        return attend(kall, kv[:, wk:].astype(BF16), state)

    init = tuple((jnp.full((ds, 1), NEG, F32), jnp.zeros((ds, 1), F32),
                  jnp.zeros((ds, 2 * C_V), F32)) for _ in range(H_C))
    state = lax.fori_loop(0, past // tile_c, cache_step, init)
    state = attend(kc_ref[0], vc_ref[0], state)
    outs = []
    for p in range(pairs):
        outs.append(_select_heads([state[2 * p + e][2] / state[2 * p + e][1] for e in range(2)], C_V))
    yc_ref[0] = jnp.concatenate(outs, axis=1).astype(BF16)


def _sample_attention(new, caches, l, bias_c, bias_n, w_ukv, tile):
    qa, ka, va, qb, kb, vb, qc, kc, vc = new
    cak, cav, cbk, cbv, ckv, ckr = caches
    nb, ds, _ = qa.shape
    req = lambda a: pl.BlockSpec((1,) + a.shape[1:], lambda b: (b, 0, 0))
    cache = lambda a: pl.BlockSpec((1, 1) + a.shape[2:], lambda b: (l, b, 0, 0))
    ins = [qa, ka, va, cak, cav, bias_c, bias_n, qb, kb, vb, cbk, cbv, qc, kc, vc, ckv, ckr, w_ukv]
    specs = [req(a) for a in ins]
    for idx in (3, 4, 10, 11, 15, 16):
        specs[idx] = cache(ins[idx])
    for idx in (5, 6, 17):
        specs[idx] = _const_spec(ins[idx].shape)
    out_shape = [jax.ShapeDtypeStruct((nb, ds, W_AB), BF16), jax.ShapeDtypeStruct((nb, ds, W_AB), BF16),
                 jax.ShapeDtypeStruct((nb, ds, H_C * C_V), BF16)]
    return pl.pallas_call(
        functools.partial(_sample_attn_kernel, tile=tile),
        grid=(nb,),
        in_specs=specs,
        out_specs=[req(s) for s in out_shape],
        out_shape=out_shape,
        compiler_params=_params(1),
        name="sample_attn",
    )(*ins)


def _rope_tables(pos):
    half = C_ROPE // 2
    freq = ROPE_THETA ** (-jnp.arange(half, dtype=F32) / half)
    ang = pos.astype(F32)[:, None] * freq[None, :]
    cos, sin = jnp.cos(ang), jnp.sin(ang)
    n = pos.shape[0]
    z = lambda w: jnp.zeros((n, w), F32)
    tc = jnp.concatenate([jnp.ones((n, C_NOPE), F32), cos, cos, z(C_PAD - C_NOPE - C_ROPE)], axis=1)
    ts1 = jnp.concatenate([z(C_NOPE + half), sin, z(C_PAD - C_NOPE - C_ROPE)], axis=1)
    ts2 = jnp.concatenate([z(C_NOPE), -sin, z(C_PAD - C_NOPE - half)], axis=1)
    return tc, ts1, ts2


def _band_bias(rel_bias, q0, nq, k0, nk):
    period = 1 << int(np.ceil(np.log2(nq + nk)))
    d = np.arange(period)
    d = np.where(d >= nk, d - period, d)
    idx = np.clip((q0 - k0) - d, -REL_CLIP, REL_CLIP) + REL_CLIP
    g = rel_bias.astype(F32)[:, idx]
    toe = jnp.tile(g, (1, nq))[:, :nq * (period - 1)].reshape(-1, nq, period - 1)[:, :, :nk]
    q_pos, k_pos = q0 + np.arange(nq), k0 + np.arange(nk)
    qc, kc = q_pos[:, None] // CHUNK, k_pos[None, :] // CHUNK
    vis = (kc <= qc) & (kc >= qc - BAND_CHUNKS)
    return jnp.where(jnp.asarray(vis)[None], toe, NEG)


def _cast_kernel(w_ref, o_ref):
    o_ref[...] = w_ref[0].astype(BF16)


def _layer_bf16(w, l):
    _, rows, cols = w.shape
    rb = rows // 4 if rows % 64 == 0 else rows
    return pl.pallas_call(
        _cast_kernel,
        grid=(rows // rb,),
        in_specs=[pl.BlockSpec((1, rb, cols), lambda i: (l, i, 0))],
        out_specs=pl.BlockSpec((rb, cols), lambda i: (i, 0)),
        out_shape=jax.ShapeDtypeStruct((rows, cols), BF16),
        compiler_params=_params(1),
        name="cast_bf16",
    )(w)


def _layer_weights(l, w_in, w_uq, w_uk, w_uv):
    d = w_in.shape[1]
    q_rank, kv_rank = w_uq.shape[1], w_uk.shape[1]
    o = 6 * W_AB + q_rank + kv_rank
    zc = lambda w: jnp.zeros((d, w), w_in.dtype)
    w_main = jnp.concatenate([w_in[l, :, :o], zc(C_NOPE), w_in[l, :, o:o + C_ROPE],
                              zc(C_PAD - C_NOPE - C_ROPE)], axis=1).astype(BF16)
    w_gates = w_in[l, :, o + C_ROPE:].astype(BF16)
    pad_heads = lambda w: jnp.pad(w, ((0, 0), (0, 0), (0, C_PAD - w.shape[2]))).reshape(w.shape[0], -1)
    w_uq_p = pad_heads(w_uq[l]).astype(BF16)
    w_ukv = jnp.concatenate([pad_heads(w_uk[l]), w_uv[l].reshape(kv_rank, -1)], axis=1).astype(BF16)
    return w_main, w_gates, w_uq_p, w_ukv


def kernel(x_prompt, x_sample, cache_a_k, cache_a_v, cache_b_k, cache_b_v, cache_c_kv, cache_c_kr, ffn1_norm_pre, ffn1_norm_post, ffn1_w_gate, ffn1_w_up, ffn1_w_down, mix_norm_pre, mix_norm_post, w_in, cq_norm, ckv_norm, w_uq, w_uk, w_uv, rel_bias_a, w_br_a, w_br_b, w_br_c, w_out, ffn2_norm_pre, ffn2_norm_post, ffn2_w_gate, ffn2_w_up, ffn2_w_down):
    batch, seq, d = x_prompt.shape
    nb, ds, _ = x_sample.shape
    depth = w_in.shape[0]
    past = cache_b_k.shape[2]
    win_cache = cache_a_k.shape[2]
    assert batch == 1, "prompt group is a single sequence"
    tile = 256
    tm = 512 if seq % 512 == 0 else tile
    assert seq % tile == 0 and past % tile == 0 and tile % CHUNK == 0 and 2 * tile >= WIN_A
    assert past % CHUNK + ds <= CHUNK
    ns = nb * ds

    xp = x_prompt.reshape(seq, d)
    xs = x_sample.reshape(ns, d)
    tabs_p = _rope_tables(jnp.arange(seq))
    tabs_s = _rope_tables(jnp.tile(past + jnp.arange(ds), nb))

    row = lambda g, l: g[l][None, :]
    base = 2 * tile

    merged_heads = lambda c: c.reshape(c.shape[:3] + (-1,))
    caches = (merged_heads(cache_a_k), merged_heads(cache_a_v), merged_heads(cache_b_k),
              merged_heads(cache_b_v), cache_c_kv, cache_c_kr)
    states_p, states_s = [], []
    for l in range(depth):
        w_main, w_gates, w_uq_p, w_ukv = _layer_weights(l, w_in, w_uq, w_uk, w_uv)
        ffn1 = (row(ffn1_norm_pre, l), row(ffn1_norm_post, l), _layer_bf16(ffn1_w_gate, l),
                _layer_bf16(ffn1_w_up, l), _layer_bf16(ffn1_w_down, l))
        ffn2 = (row(ffn2_norm_pre, l), row(ffn2_norm_post, l), _layer_bf16(ffn2_w_gate, l),
                _layer_bf16(ffn2_w_up, l), _layer_bf16(ffn2_w_down, l))
        mix_in = (row(mix_norm_pre, l), w_main, row(cq_norm, l), row(ckv_norm, l), w_uq_p, w_ukv)
        mrg = (row(mix_norm_pre, l), row(mix_norm_post, l), w_gates, _layer_bf16(w_br_a, l),
               _layer_bf16(w_br_b, l), _layer_bf16(w_br_c, l), _layer_bf16(w_out, l))
        bias_p = _band_bias(rel_bias_a[l], base, tile, base - 2 * tile, 3 * tile)
        bias_c = _band_bias(rel_bias_a[l], past, ds, past - win_cache, win_cache)
        bias_n = _band_bias(rel_bias_a[l], past, ds, past, ds)

        xp = _ffn(xp, *ffn1, tm)
        (qa, kab, vab, qb, kbb, vbb, ka, va, kb, vb, ckv, krb, qc, kc, vc, kn) = _mixer_in(
            xp, *mix_in, tabs_p, tm, q_transposed=True)
        ya = _band_attention(qa, kab, vab, bias_p, tile)
        yb = _sb_attention(qb, kbb, vbb, tile)
        yc = _mla_attention(qc, kc, vc, kn)
        xp = _merge(xp, ya, yb, yc, *mrg, tm, yc_transposed=True)
        xp = _ffn(xp, *ffn2, tm)
        win = min(WIN_A, seq)
        heads = lambda t, h: t.reshape(1, t.shape[0], h, t.shape[1] // h)
        states_p.append((heads(ka[-win:], H_A), heads(va[-win:], H_A), heads(kb, H_B), heads(vb, H_B),
                         ckv[None], krb[None, :, C_NOPE:C_NOPE + C_ROPE]))

        xs = _ffn(xs, *ffn1, ns)
        (qa, kab, vab, qb, kbb, vbb, ka, va, kb, vb, ckv, krb, qc, kc, vc, _) = _mixer_in(
            xs, *mix_in, tabs_s, ns)
        per_req = lambda t: t.reshape(nb, ds, t.shape[1])
        new = tuple(per_req(t) for t in (qa, kab, vab, qb, kbb, vbb, qc, kc, vc))
        ya, yb, yc = _sample_attention(new, caches, l, bias_c, bias_n, w_ukv, tile)
        xs = _merge(xs, ya.reshape(ns, -1), yb.reshape(ns, -1), yc.reshape(ns, -1), *mrg, ns)
        xs = _ffn(xs, *ffn2, ns)
        heads_s = lambda t, h: t.reshape(nb, ds, h, t.shape[1] // h)
        states_s.append((heads_s(ka, H_A), heads_s(va, H_A), heads_s(kb, H_B), heads_s(vb, H_B),
                         ckv.reshape(nb, ds, -1), krb[:, C_NOPE:C_NOPE + C_ROPE].reshape(nb, ds, C_ROPE)))

    stack = lambda states, i: jnp.stack([s[i] for s in states], axis=0)
    return ((xp.reshape(batch, seq, d), xs.reshape(nb, ds, d))
            + tuple(stack(states_p, i) for i in range(6))
            + tuple(stack(states_s, i) for i in range(6)))
```

```python
import functools

import numpy as np
import jax
import jax.numpy as jnp
from jax import lax
from jax.experimental import pallas as pl
from jax.experimental.pallas import tpu as pltpu

F32 = jnp.float32
BF16 = jnp.bfloat16

CHUNK = 64
BAND_CHUNKS = 8
WIN_A = BAND_CHUNKS * CHUNK
HEAD_DIM = 64
H_A = 4
H_B = 4
H_C = 8
C_NOPE = 64
C_ROPE = 32
C_V = 64
REL_CLIP = 128
ROPE_THETA = 10000.0
EPS = 1e-6
N_BRANCH = 3
W_AB = H_A * HEAD_DIM
C_PAD = 128
ATTN_SCALE = HEAD_DIM ** -0.5
MLA_SCALE = (C_NOPE + C_ROPE) ** -0.5
LOG2E = float(np.log2(np.e))
C_VX = C_V + 16
C_ONE = C_NOPE + C_ROPE
NORM_MARGIN = 1.02
MLA_MIN_DENOM = 2.0 ** -60
MLA_UNROLL = 4
SAMPLE_LATENT_TILE = 1024
NEG = -1e30
SB_DEAD = -150.0

LANES = 128
MXU_WIDTH = 256
VMEM_LIMIT = 52 * 1024 * 1024

NT_DIMS = (((1,), (1,)), ((), ()))
TN_DIMS = (((0,), (0,)), ((), ()))


def _rms(x, g):
    return x * lax.rsqrt(jnp.mean(x * x, axis=-1, keepdims=True) + EPS) * g


def _dot(a, b):
    return jnp.dot(a, b, preferred_element_type=F32)


def _dot_nt(a, b):
    return lax.dot_general(a, b, NT_DIMS, preferred_element_type=F32)


def _dot_tn(a, b):
    return lax.dot_general(a, b, TN_DIMS, preferred_element_type=F32)


def _const_spec(shape):
    zeros = (0,) * len(shape)
    return pl.BlockSpec(shape, lambda *_: zeros, pipeline_mode=pl.Buffered(1))


def _row_spec(tm, width):
    return pl.BlockSpec((tm, width), lambda i: (i, 0))


def _params(n_grid):
    return pltpu.CompilerParams(dimension_semantics=("arbitrary",) * n_grid,
                                vmem_limit_bytes=VMEM_LIMIT)


def _ffn_kernel(x_ref, gpre_ref, gpost_ref, wg_ref, wu_ref, wd_ref, o_ref, *, f_cuts):
    x = x_ref[...]
    h = _rms(x, gpre_ref[...]).astype(BF16)
    y = jnp.zeros(x.shape, F32)
    for lo, hi in zip(f_cuts[:-1], f_cuts[1:]):
        sl = slice(lo, hi)
        g = _dot(h, wg_ref[:, sl])
        u = _dot(h, wu_ref[:, sl])
        a = (g * jax.nn.sigmoid(g) * u).astype(BF16)
        y = y + _dot(a, wd_ref[sl, :])
    o_ref[...] = x + 0.5 * _rms(y, gpost_ref[...])


def _ffn(x, g_pre, g_post, w_gate, w_up, w_down, tm):
    n, d = x.shape
    d_ff = w_gate.shape[1]
    cut = (d_ff // 2) // MXU_WIDTH * MXU_WIDTH
    f_cuts = (0, cut, d_ff) if 0 < cut < d_ff else (0, d_ff)
    return pl.pallas_call(
        functools.partial(_ffn_kernel, f_cuts=f_cuts),
        grid=(n // tm,),
        in_specs=[_row_spec(tm, d), _const_spec((1, d)), _const_spec((1, d)),
                  _const_spec((d, d_ff)), _const_spec((d, d_ff)), _const_spec((d_ff, d))],
        out_specs=_row_spec(tm, d),
        out_shape=jax.ShapeDtypeStruct((n, d), F32),
        compiler_params=_params(1),
        name="ffn",
    )(x, g_pre, g_post, w_gate, w_up, w_down)


def _rope(blk, tc, ts1, ts2):
    return blk * tc + pltpu.roll(blk, 16, 1) * ts1 + pltpu.roll(blk, C_PAD - 16, 1) * ts2


def _mixer_in_kernel(x_ref, gpre_ref, win_ref, gcq_ref, gckv_ref, wuq_ref, wukv_ref,
                     tc_ref, ts1_ref, ts2_ref,
                     qa_ref, kab_ref, vab_ref, qb_ref, kbb_ref, vbb_ref,
                     ka_ref, va_ref, kb_ref, vb_ref, ckv_ref, krb_ref,
                     qc_ref, kc_ref, vc_ref, kn_ref, *, q_transposed):
    w = W_AB
    q_rank = gcq_ref.shape[1]
    kv_rank = gckv_ref.shape[1]
    u = _rms(x_ref[...], gpre_ref[...]).astype(BF16)
    o = 6 * w
    lat = _dot(u, win_ref[:, o:])
    cq = lat[:, :q_rank]
    ckv = lat[:, q_rank:q_rank + kv_rank]
    krb = lat[:, q_rank + kv_rank:q_rank + kv_rank + C_PAD]
    proj = _dot(u, win_ref[:, :o])
    qa, ka, va = proj[:, 0:w], proj[:, w:2 * w], proj[:, 2 * w:3 * w]
    qb, kb, vb = proj[:, 3 * w:4 * w], proj[:, 4 * w:5 * w], proj[:, 5 * w:6 * w]
    qa_ref[...] = (qa * ATTN_SCALE).astype(BF16)
    qb_ref[...] = (qb * (ATTN_SCALE * LOG2E)).astype(BF16)
    ka_ref[...] = ka
    va_ref[...] = va
    kb_ref[...] = kb
    vb_ref[...] = vb
    kab_ref[...] = ka.astype(BF16)
    vab_ref[...] = va.astype(BF16)
    kbb_ref[...] = kb.astype(BF16)
    vbb_ref[...] = vb.astype(BF16)
    tc, ts1, ts2 = tc_ref[...], ts1_ref[...], ts2_ref[...]
    cqn = _rms(cq, gcq_ref[...]).astype(BF16)
    qall = _dot(cqn, wuq_ref[...])
    ckvn = _rms(ckv, gckv_ref[...])
    ckv_ref[...] = ckvn
    kv = _dot(ckvn.astype(BF16), wukv_ref[...])
    krot = _rope(krb, tc, ts1, ts2)
    krb_ref[...] = krot
    one_lane = (lax.broadcasted_iota(jnp.int32, (1, C_PAD), 1) == C_ONE).astype(F32)
    for h in range(H_C):
        sl = slice(h * C_PAD, (h + 1) * C_PAD)
        qh = _rope(qall[:, sl], tc, ts1, ts2) * (MLA_SCALE * LOG2E)
        if q_transposed:
            qt = qh.T
            norm = jnp.sqrt(jnp.sum(qt * qt, axis=0, keepdims=True)) * NORM_MARGIN
            row = lax.broadcasted_iota(jnp.int32, qt.shape, 0)
            qc_ref[sl, :] = jnp.where(row == C_ONE, norm, qt).astype(BF16)
        else:
            qc_ref[:, sl] = qh.astype(BF16)
        kh = kv[:, sl] + krot
        kc_ref[:, sl] = (kh + one_lane).astype(BF16)
        k_norm = jnp.sqrt(jnp.max(jnp.sum(kh * kh, axis=1, keepdims=True), axis=0, keepdims=True))
        kn_ref[0, :, sl] = jnp.broadcast_to(k_norm * NORM_MARGIN, (1, C_PAD))
    if not q_transposed:
        vc_ref[...] = kv[:, H_C * C_PAD:].astype(BF16)
    else:
        ones_row = (lax.broadcasted_iota(jnp.int32, (C_VX - C_V, x_ref.shape[0]), 0) == 0).astype(BF16)
        for p in range(H_C // 2):
            o = H_C * C_PAD + p * 2 * C_V
            vt = kv[:, o:o + 2 * C_V].T
            for e in range(2):
                r = (2 * p + e) * C_VX
                vc_ref[0, r:r + C_V, :] = vt[e * C_V:(e + 1) * C_V, :].astype(BF16)
                vc_ref[0, r + C_V:r + C_VX, :] = ones_row


def _mixer_in(x, g_pre, w_main, g_cq, g_ckv, w_uq, w_ukv, tabs, tm, q_transposed=False):
    n, d = x.shape
    q_rank, kv_rank = g_cq.shape[1], g_ckv.shape[1]
    wc = H_C * C_PAD
    wv = H_C * C_V
    bf = lambda width: jax.ShapeDtypeStruct((n, width), BF16)
    f32 = lambda width: jax.ShapeDtypeStruct((n, width), F32)
    q_shape = jax.ShapeDtypeStruct((wc, n), BF16) if q_transposed else bf(wc)
    q_spec = pl.BlockSpec((wc, tm), lambda i: (0, i)) if q_transposed else _row_spec(tm, wc)
    v_shape = jax.ShapeDtypeStruct((n // tm, H_C * C_VX, tm), BF16) if q_transposed else bf(wv)
    v_spec = (pl.BlockSpec((1, H_C * C_VX, tm), lambda i: (i, 0, 0)) if q_transposed
              else _row_spec(tm, wv))
    out_shape = ([bf(W_AB)] * 6 + [f32(W_AB)] * 4 + [f32(kv_rank), f32(C_PAD)]
                 + [q_shape, bf(wc), v_shape, jax.ShapeDtypeStruct((n // tm, 1, wc), F32)])
    out_specs = ([_row_spec(tm, W_AB)] * 10 + [_row_spec(tm, kv_rank), _row_spec(tm, C_PAD)]
                 + [q_spec, _row_spec(tm, wc), v_spec, pl.BlockSpec((1, 1, wc), lambda i: (i, 0, 0))])
    return pl.pallas_call(
        functools.partial(_mixer_in_kernel, q_transposed=q_transposed),
        grid=(n // tm,),
        in_specs=[_row_spec(tm, d), _const_spec((1, d)), _const_spec(w_main.shape),
                  _const_spec((1, q_rank)), _const_spec((1, kv_rank)),
                  _const_spec(w_uq.shape), _const_spec(w_ukv.shape),
                  _row_spec(tm, C_PAD), _row_spec(tm, C_PAD), _row_spec(tm, C_PAD)],
        out_specs=out_specs,
        out_shape=out_shape,
        compiler_params=_params(1),
        name="mixer_in",
    )(x, g_pre, w_main, g_cq, g_ckv, w_uq, w_ukv, *tabs)


def _merge_kernel(x_ref, ya_ref, yb_ref, yc_ref, gpre_ref, gpost_ref, wgates_ref,
                  wa_ref, wb_ref, wc_ref, wout_ref, o_ref, *, yc_transposed):
    x = x_ref[...]
    d = x.shape[1]
    u = _rms(x, gpre_ref[...]).astype(BF16)
    m = jnp.zeros(x.shape, F32)
    for i, (y_ref, w_ref) in enumerate(((ya_ref, wa_ref), (yb_ref, wb_ref), (yc_ref, wc_ref))):
        gate = jax.nn.sigmoid(_dot(u, wgates_ref[:, i * d:(i + 1) * d]))
        branch_dot = _dot_tn if (yc_transposed and y_ref is yc_ref) else _dot
        m = m + gate * branch_dot(y_ref[...], w_ref[...])
    mixed = _dot(m.astype(BF16), wout_ref[...])
    o_ref[...] = x + _rms(mixed, gpost_ref[...])


def _merge(x, ya, yb, yc, g_pre, g_post, w_gates, w_a, w_b, w_c, w_out, tm, yc_transposed=False):
    n, d = x.shape
    yc_spec = (pl.BlockSpec((yc.shape[0], tm), lambda i: (0, i)) if yc_transposed
               else _row_spec(tm, yc.shape[1]))
    return pl.pallas_call(
        functools.partial(_merge_kernel, yc_transposed=yc_transposed),
        grid=(n // tm,),
        in_specs=[_row_spec(tm, d), _row_spec(tm, ya.shape[1]), _row_spec(tm, yb.shape[1]),
                  yc_spec, _const_spec((1, d)), _const_spec((1, d)),
                  _const_spec(w_gates.shape), _const_spec(w_a.shape), _const_spec(w_b.shape),
                  _const_spec(w_c.shape), _const_spec(w_out.shape)],
        out_specs=_row_spec(tm, d),
        out_shape=jax.ShapeDtypeStruct((n, d), F32),
        compiler_params=_params(1),
        name="merge",
    )(x, ya, yb, yc, g_pre, g_post, w_gates, w_a, w_b, w_c, w_out)


def _head_masked(q, width):
    lane_head = lax.broadcasted_iota(jnp.int32, q.shape, 1) // width
    zero = jnp.zeros_like(q)
    return [jnp.where(lane_head == h, q, zero) for h in range(q.shape[1] // width)]


def _strict_upper_ones(n):
    j = lax.broadcasted_iota(jnp.int32, (n, n), 0)
    s = lax.broadcasted_iota(jnp.int32, (n, n), 1)
    return (j > s).astype(BF16)


def _sb_terms(qm, kt, ones_ut, causal):
    z = _dot_nt(qm, kt)
    t = jnp.log2(1.0 + jnp.exp2(-jnp.abs(z)))
    log_keep = -jnp.maximum(z, 0.0) - t
    log_beta = jnp.minimum(z, 0.0) - t
    if causal is not None:
        log_keep = jnp.where(causal, log_keep, 0.0)
    hi = log_keep.astype(BF16)
    lo = (log_keep - hi.astype(F32)).astype(BF16)
    tail = _dot(hi, ones_ut) + _dot(lo, ones_ut)
    return log_beta + tail, jnp.sum(log_keep, axis=1, keepdims=True)


def _sb_apply(terms, vt, carry, acc, causal):
    logw, total = terms
    w = jnp.exp2(logw + carry)
    if causal is not None:
        w = jnp.where(causal, w, 0.0)
    return carry + total, acc + _dot(w.astype(BF16), vt)


def _sb_tile(qm, kt, vt, carry, acc, ones_ut, causal):
    return _sb_apply(_sb_terms(qm, kt, ones_ut, causal), vt, carry, acc, causal)


def _sb_alive(carry):
    return jnp.max(carry) > SB_DEAD


def _select_heads(parts, width):
    lane_head = lax.broadcasted_iota(jnp.int32, parts[0].shape, 1) // width
    out = parts[0]
    for h in range(1, len(parts)):
        out = jnp.where(lane_head == h, parts[h], out)
    return out


def _band_kernel(q_ref, k0_ref, k1_ref, k2_ref, v0_ref, v1_ref, v2_ref, bias_ref, o_ref):
    i = pl.program_id(0)
    tq = q_ref.shape[0]
    kwin = jnp.concatenate([k0_ref[...], k1_ref[...], k2_ref[...]], axis=0)
    vwin = jnp.concatenate([v0_ref[...], v1_ref[...], v2_ref[...]], axis=0)
    col = lax.broadcasted_iota(jnp.int32, (tq, 3 * tq), 1)
    in_range = col >= (2 - i) * tq
    parts = []
    for h, qm in enumerate(_head_masked(q_ref[...], HEAD_DIM)):
        s = jnp.where(in_range, _dot_nt(qm, kwin) + bias_ref[h], NEG)
        m = jnp.max(s, axis=1, keepdims=True)
        p = jnp.exp(s - m)
        l = jnp.sum(p, axis=1, keepdims=True)
        parts.append(_dot(p.astype(BF16), vwin) / l)
    o_ref[...] = _select_heads(parts, HEAD_DIM).astype(BF16)


def _band_attention(q, k, v, bias, tq):
    n, w = q.shape
    blk = lambda back: pl.BlockSpec((tq, w), lambda i: (jnp.maximum(i - back, 0), 0))
    return pl.pallas_call(
        _band_kernel,
        grid=(n // tq,),
        in_specs=[blk(0), blk(2), blk(1), blk(0), blk(2), blk(1), blk(0), _const_spec(bias.shape)],
        out_specs=blk(0),
        out_shape=jax.ShapeDtypeStruct((n, w), BF16),
        compiler_params=_params(1),
        name="band_attn",
    )(q, k, k, k, v, v, v, bias)


def _sb_kernel(q_ref, k_ref, v_ref, o_ref, *, tile):
    i = pl.program_id(0)
    row = lax.broadcasted_iota(jnp.int32, (tile, tile), 0)
    col = lax.broadcasted_iota(jnp.int32, (tile, tile), 1)
    causal = col < row
    ones_ut = _strict_upper_ones(tile)
    width = q_ref.shape[1]
    qms = _head_masked(q_ref[...], HEAD_DIM)
    rows = lambda ref, kb: ref[pl.ds(pl.multiple_of(kb * tile, tile), tile), :]

    prev = jnp.maximum(i - 1, 0)
    k_diag, k_prev = rows(k_ref, i), rows(k_ref, prev)
    terms_diag = [_sb_terms(qm, k_diag, ones_ut, causal) for qm in qms]
    terms_prev = [_sb_terms(qm, k_prev, ones_ut, None) for qm in qms]
    v_diag, v_prev = rows(v_ref, i), rows(v_ref, prev)
    parts = []
    for qm, t_diag, t_prev in zip(qms, terms_diag, terms_prev):
        carry, acc = _sb_apply(t_diag, v_diag, jnp.zeros((tile, 1), F32), jnp.zeros((tile, width), F32),
                               causal)
        carry, acc = _sb_apply(t_prev, v_prev, jnp.where(i > 0, carry, NEG), acc, None)
        state = lax.while_loop(
            lambda st: (st[0] < i - 1) & _sb_alive(st[1]),
            lambda st, qm=qm: (st[0] + 1,) + _sb_tile(qm, rows(k_ref, i - 2 - st[0]), rows(v_ref, i - 2 - st[0]),
                                                      st[1], st[2], ones_ut, None),
            (jnp.int32(0), carry, acc))
        parts.append(state[2])
    o_ref[...] = _select_heads(parts, HEAD_DIM).astype(BF16)


def _sb_attention(q, k, v, tile):
    n, w = q.shape
    return pl.pallas_call(
        functools.partial(_sb_kernel, tile=tile),
        grid=(n // tile,),
        in_specs=[_row_spec(tile, w), _const_spec((n, w)), _const_spec((n, w))],
        out_specs=_row_spec(tile, w),
        out_shape=jax.ShapeDtypeStruct((n, w), BF16),
        compiler_params=_params(1),
        name="sb_attn",
    )(q, k, v)


def _mla_kernel(qt_ref, k_ref, vt_ref, kn_ref, ot_ref, qx_scr, p_scr, acc_scr):
    i = pl.program_id(1)
    heads = range(2)
    t = p_scr.shape[2]
    key = lax.broadcasted_iota(jnp.int32, (t, t), 0)
    qry = lax.broadcasted_iota(jnp.int32, (t, t), 1)
    visible = (key // CHUNK) <= (qry // CHUNK)
    row = lax.broadcasted_iota(jnp.int32, (C_PAD - C_ONE, t), 0)

    def load_q(shifted):
        k_bound = jnp.max(kn_ref[...], axis=0)
        for e in heads:
            lo = e * C_PAD
            qx_scr[lo:lo + C_ONE, :] = qt_ref[lo:lo + C_ONE, :]
            if shifted:
                bound = qt_ref[lo + C_ONE:lo + C_ONE + 1, :].astype(F32) * (k_bound[:, lo:lo + 1] * NORM_MARGIN)
                tail = jnp.where(row == 0, -bound, 0.0)
            else:
                tail = jnp.zeros(row.shape, F32)
            qx_scr[lo + C_ONE:lo + C_PAD, :] = tail.astype(BF16)

    def scores(kb, e):
        return _dot(k_ref[kb, :, e * C_PAD:(e + 1) * C_PAD], qx_scr[e * C_PAD:(e + 1) * C_PAD, :])

    def values(kb, e, p):
        return _dot(vt_ref[kb, e * C_VX:(e + 1) * C_VX, :], p)

    def write_out():
        ot_ref[...] = jnp.concatenate([acc_scr[e, :C_V] / acc_scr[e, C_V:C_V + 1] for e in heads],
                                      axis=0).astype(BF16)

    def fast_stage(kb, slot, mask):
        for e in heads:
            s = scores(kb, e)
            if mask is not None:
                s = jnp.where(mask, s, NEG)
            p_scr[slot, e] = jnp.exp2(s).astype(BF16)
        for e in heads:
            acc_scr[e] += values(kb, e, p_scr[slot, e])

    load_q(True)
    acc_scr[...] = jnp.zeros(acc_scr.shape, F32)

    def fast_group(g, _):
        for j in range(MLA_UNROLL):
            fast_stage(MLA_UNROLL * g + j, j, None)
        return 0
    groups = i // MLA_UNROLL
    lax.fori_loop(0, groups, fast_group, 0)

    def fast_single(kb, _):
        fast_stage(kb, 0, None)
        return 0
    lax.fori_loop(MLA_UNROLL * groups, i, fast_single, 0)
    fast_stage(i, 1, visible)
    denom = jnp.minimum(jnp.min(acc_scr[0, C_V:C_V + 1]), jnp.min(acc_scr[1, C_V:C_V + 1]))
    trusted = denom >= MLA_MIN_DENOM

    @pl.when(trusted)
    def _():
        write_out()

    @pl.when(jnp.logical_not(trusted))
    def _():
        load_q(False)
        acc_scr[...] = jnp.zeros(acc_scr.shape, F32)

        def step(kb, m):
            out = []
            for e in heads:
                s = jnp.where((kb < i) | visible, scores(kb, e), NEG)
                m_new = jnp.maximum(m[e], jnp.max(s, axis=0, keepdims=True))
                p = jnp.exp2(s - m_new).astype(BF16)
                acc_scr[e] = jnp.exp2(m[e] - m_new) * acc_scr[e] + values(kb, e, p)
                out.append(m_new)
            return tuple(out)
        lax.fori_loop(0, i + 1, step, tuple(jnp.full((1, t), NEG, F32) for _ in heads))
        write_out()


def _mla_attention(qt, k, vt3, kn):
    n = k.shape[0]
    pairs = H_C // 2
    n_tiles, _, t = vt3.shape
    assert n_tiles * t == n and kn.shape == (n_tiles, 1, H_C * C_PAD)
    k3 = k.reshape(n_tiles, t, H_C * C_PAD)
    return pl.pallas_call(
        _mla_kernel,
        grid=(pairs, n_tiles),
        in_specs=[pl.BlockSpec((2 * C_PAD, t), lambda p, i: (p, i)),
                  pl.BlockSpec((n_tiles, t, 2 * C_PAD), lambda p, i: (0, 0, p),
                               pipeline_mode=pl.Buffered(1)),
                  pl.BlockSpec((n_tiles, 2 * C_VX, t), lambda p, i: (0, p, 0),
                               pipeline_mode=pl.Buffered(1)),
                  pl.BlockSpec((n_tiles, 1, 2 * C_PAD), lambda p, i: (0, 0, p))],
        out_specs=pl.BlockSpec((2 * C_V, t), lambda p, i: (p, i)),
        out_shape=jax.ShapeDtypeStruct((H_C * C_V, n), BF16),
        scratch_shapes=[pltpu.VMEM((2 * C_PAD, t), BF16), pltpu.VMEM((MLA_UNROLL, 2, t, t), BF16),
                        pltpu.VMEM((2, C_VX, t), F32)],
        compiler_params=_params(2),
        name="mla_attn",
    )(qt, k3, vt3, kn)


def _sample_attn_kernel(qa_ref, ka_ref, va_ref, cak_ref, cav_ref, biasc_ref, biasn_ref,
                        qb_ref, kb_ref, vb_ref, cbk_ref, cbv_ref,
                        qc_ref, ckvn_ref, krbn_ref, ckv_ref, ckr_ref, wukt_ref, wuvp_ref,
                        ya_ref, yb_ref, yc_ref, s_scr, *, tile):
    ds = qa_ref.shape[1]
    past = cbk_ref.shape[2]

    ka, va, cak, cav = ka_ref[0], va_ref[0], cak_ref[0, 0].astype(BF16), cav_ref[0, 0].astype(BF16)
    parts = []
    for h, qm in enumerate(_head_masked(qa_ref[0], HEAD_DIM)):
        s_c = _dot_nt(qm, cak) + biasc_ref[h]
        s_n = _dot_nt(qm, ka) + biasn_ref[h]
        m = jnp.maximum(jnp.max(s_c, axis=1, keepdims=True), jnp.max(s_n, axis=1, keepdims=True))
        p_c = jnp.exp(s_c - m)
        p_n = jnp.exp(s_n - m)
        l = jnp.sum(p_c, axis=1, keepdims=True) + jnp.sum(p_n, axis=1, keepdims=True)
        parts.append((_dot(p_c.astype(BF16), cav) + _dot(p_n.astype(BF16), va)) / l)
    ya_ref[0] = _select_heads(parts, HEAD_DIM).astype(BF16)

    row = lax.broadcasted_iota(jnp.int32, (ds, ds), 0)
    col = lax.broadcasted_iota(jnp.int32, (ds, ds), 1)
    causal = col < row
    ones_new = _strict_upper_ones(ds)
    ones_ut = _strict_upper_ones(tile)
    kb, vb = kb_ref[0], vb_ref[0]
    n_tiles = past // tile
    parts = []
    for qm in _head_masked(qb_ref[0], HEAD_DIM):
        state = _sb_tile(qm, kb, vb, jnp.zeros((ds, 1), F32), jnp.zeros((ds, W_AB), F32),
                         ones_new, causal)

        def cache_tile(st, qm=qm):
            start = pl.multiple_of((n_tiles - 1 - st[0]) * tile, tile)
            return (st[0] + 1,) + _sb_tile(qm, cbk_ref[0, 0, pl.ds(start, tile), :].astype(BF16),
                                           cbv_ref[0, 0, pl.ds(start, tile), :].astype(BF16),
                                           st[1], st[2], ones_ut, None)
        state = lax.while_loop(lambda st: (st[0] < n_tiles) & _sb_alive(st[1]), cache_tile,
                               (jnp.int32(0),) + state)
        parts.append(state[2])
    yb_ref[0] = _select_heads(parts, HEAD_DIM).astype(BF16)

    kv_rank = ckv_ref.shape[3]
    qc = qc_ref[0]
    rope_lanes = lax.broadcasted_iota(jnp.int32, (ds, C_PAD), 1) >= C_NOPE
    q_rows = []
    for h in range(H_C):
        qh = qc[:, h * C_PAD:(h + 1) * C_PAD]
        q_lat = _dot(qh, wukt_ref[h]).astype(BF16)
        q_rows.append(jnp.concatenate([q_lat, jnp.where(rope_lanes, qh, jnp.zeros_like(qh))], axis=1))
    q_all = jnp.concatenate(q_rows, axis=0)

    place = (lax.broadcasted_iota(jnp.int32, (C_ROPE, C_PAD), 1)
             == lax.broadcasted_iota(jnp.int32, (C_ROPE, C_PAD), 0) + C_NOPE).astype(BF16)
    tile_c = SAMPLE_LATENT_TILE if past % SAMPLE_LATENT_TILE == 0 else tile
    cache_tiles = range(past // tile_c)
    latent = lambda j: ckv_ref[0, 0, j * tile_c:(j + 1) * tile_c, :].astype(BF16)

    lat_new = ckvn_ref[0].astype(BF16)
    s_new = _dot_nt(q_all, jnp.concatenate([lat_new, krbn_ref[0].astype(BF16)], axis=1))
    m = jnp.max(s_new, axis=1, keepdims=True)
    for j in cache_tiles:
        k_rope = _dot(ckr_ref[0, 0, j * tile_c:(j + 1) * tile_c, :].astype(BF16), place).astype(BF16)
        s = _dot_nt(q_all, jnp.concatenate([latent(j), k_rope], axis=1))
        s_scr[:, j * tile_c:(j + 1) * tile_c] = s
        m = jnp.maximum(m, jnp.max(s, axis=1, keepdims=True))
    p = jnp.exp2(s_new - m)
    denom = jnp.sum(p, axis=1, keepdims=True)
    o_lat = _dot(p.astype(BF16), lat_new)
    for j in cache_tiles:
        p = jnp.exp2(s_scr[:, j * tile_c:(j + 1) * tile_c] - m)
        denom = denom + jnp.sum(p, axis=1, keepdims=True)
        o_lat = o_lat + _dot(p.astype(BF16), latent(j))
    o_lat = (o_lat / denom).astype(BF16)
    out = jnp.zeros((ds, H_C * C_V), F32)
    for h in range(H_C):
        out = out + _dot(o_lat[h * ds:(h + 1) * ds], wuvp_ref[h])
    yc_ref[0] = out.astype(BF16)


def _sample_attention(new, caches, l, bias_c, bias_n, w_ukt, w_uvp, tile):
    qa, ka, va, qb, kb, vb, qc, ckvn, krbn = new
    cak, cav, cbk, cbv, ckv, ckr = caches
    nb, ds, _ = qa.shape
    past = ckv.shape[2]
    req = lambda a: pl.BlockSpec((1,) + a.shape[1:], lambda b: (b, 0, 0))
    cache = lambda a: pl.BlockSpec((1, 1) + a.shape[2:], lambda b: (l, b, 0, 0))
    ins = [qa, ka, va, cak, cav, bias_c, bias_n, qb, kb, vb, cbk, cbv, qc, ckvn, krbn, ckv, ckr,
           w_ukt, w_uvp]
    specs = [req(a) for a in ins]
    for idx in (3, 4, 10, 11, 15, 16):
        specs[idx] = cache(ins[idx])
    for idx in (5, 6, 17, 18):
        specs[idx] = _const_spec(ins[idx].shape)
    out_shape = [jax.ShapeDtypeStruct((nb, ds, W_AB), BF16), jax.ShapeDtypeStruct((nb, ds, W_AB), BF16),
                 jax.ShapeDtypeStruct((nb, ds, H_C * C_V), BF16)]
    return pl.pallas_call(
        functools.partial(_sample_attn_kernel, tile=tile),
        grid=(nb,),
        in_specs=specs,
        out_specs=[req(s) for s in out_shape],
        out_shape=out_shape,
        scratch_shapes=[pltpu.VMEM((H_C * ds, past), F32)],
        compiler_params=_params(1),
        name="sample_attn",
    )(*ins)


def _rope_tables(pos):
    half = C_ROPE // 2
    freq = ROPE_THETA ** (-jnp.arange(half, dtype=F32) / half)
    ang = pos.astype(F32)[:, None] * freq[None, :]
    cos, sin = jnp.cos(ang), jnp.sin(ang)
    n = pos.shape[0]
    z = lambda w: jnp.zeros((n, w), F32)
    tc = jnp.concatenate([jnp.ones((n, C_NOPE), F32), cos, cos, z(C_PAD - C_NOPE - C_ROPE)], axis=1)
    ts1 = jnp.concatenate([z(C_NOPE + half), sin, z(C_PAD - C_NOPE - C_ROPE)], axis=1)
    ts2 = jnp.concatenate([z(C_NOPE), -sin, z(C_PAD - C_NOPE - half)], axis=1)
    return tc, ts1, ts2


def _band_bias(rel_bias, q0, nq, k0, nk):
    period = 1 << int(np.ceil(np.log2(nq + nk)))
    d = np.arange(period)
    d = np.where(d >= nk, d - period, d)
    idx = np.clip((q0 - k0) - d, -REL_CLIP, REL_CLIP) + REL_CLIP
    g = rel_bias.astype(F32)[:, idx]
    toe = jnp.tile(g, (1, nq))[:, :nq * (period - 1)].reshape(-1, nq, period - 1)[:, :, :nk]
    q_pos, k_pos = q0 + np.arange(nq), k0 + np.arange(nk)
    qc, kc = q_pos[:, None] // CHUNK, k_pos[None, :] // CHUNK
    vis = (kc <= qc) & (kc >= qc - BAND_CHUNKS)
    return jnp.where(jnp.asarray(vis)[None], toe, NEG)


def _cast_kernel(w_ref, o_ref):
    o_ref[...] = w_ref[0].astype(BF16)


def _layer_bf16(w, l):
    _, rows, cols = w.shape
    rb = rows // 4 if rows % 64 == 0 else rows
    return pl.pallas_call(
        _cast_kernel,
        grid=(rows // rb,),
        in_specs=[pl.BlockSpec((1, rb, cols), lambda i: (l, i, 0))],
        out_specs=pl.BlockSpec((rb, cols), lambda i: (i, 0)),
        out_shape=jax.ShapeDtypeStruct((rows, cols), BF16),
        compiler_params=_params(1),
        name="cast_bf16",
    )(w)


def _layer_weights(l, w_in, w_uq, w_uk, w_uv):
    d = w_in.shape[1]
    q_rank, kv_rank = w_uq.shape[1], w_uk.shape[1]
    o = 6 * W_AB + q_rank + kv_rank
    zc = lambda w: jnp.zeros((d, w), w_in.dtype)
    w_main = jnp.concatenate([w_in[l, :, :o], zc(C_NOPE), w_in[l, :, o:o + C_ROPE],
                              zc(C_PAD - C_NOPE - C_ROPE)], axis=1).astype(BF16)
    w_gates = w_in[l, :, o + C_ROPE:].astype(BF16)
    pad_heads = lambda w: jnp.pad(w, ((0, 0), (0, 0), (0, C_PAD - w.shape[2]))).reshape(w.shape[0], -1)
    w_uq_p = pad_heads(w_uq[l]).astype(BF16)
    w_ukv = jnp.concatenate([pad_heads(w_uk[l]), w_uv[l].reshape(kv_rank, -1)], axis=1).astype(BF16)
    w_ukt = jnp.pad(jnp.transpose(w_uk[l], (1, 2, 0)), ((0, 0), (0, C_PAD - C_NOPE), (0, 0))).astype(BF16)
    w_uvp = jnp.einsum('rhd,hg->hrgd', w_uv[l], jnp.eye(H_C, dtype=w_uv.dtype)).reshape(
        H_C, kv_rank, H_C * C_V).astype(BF16)
    return w_main, w_gates, w_uq_p, w_ukv, w_ukt, w_uvp


def kernel(x_prompt, x_sample, cache_a_k, cache_a_v, cache_b_k, cache_b_v, cache_c_kv, cache_c_kr, ffn1_norm_pre, ffn1_norm_post, ffn1_w_gate, ffn1_w_up, ffn1_w_down, mix_norm_pre, mix_norm_post, w_in, cq_norm, ckv_norm, w_uq, w_uk, w_uv, rel_bias_a, w_br_a, w_br_b, w_br_c, w_out, ffn2_norm_pre, ffn2_norm_post, ffn2_w_gate, ffn2_w_up, ffn2_w_down):
    batch, seq, d = x_prompt.shape
    nb, ds, _ = x_sample.shape
    depth = w_in.shape[0]
    past = cache_b_k.shape[2]
    win_cache = cache_a_k.shape[2]
    assert batch == 1, "prompt group is a single sequence"
    tile = 256
    tm = 512 if seq % 512 == 0 else tile
    assert seq % tile == 0 and past % tile == 0 and tile % CHUNK == 0 and 2 * tile >= WIN_A
    assert past % CHUNK + ds <= CHUNK
    ns = nb * ds

    xp = x_prompt.reshape(seq, d)
    xs = x_sample.reshape(ns, d)
    tabs_p = _rope_tables(jnp.arange(seq))
    tabs_s = _rope_tables(jnp.tile(past + jnp.arange(ds), nb))

    row = lambda g, l: g[l][None, :]
    base = 2 * tile

    merged_heads = lambda c: c.reshape(c.shape[:3] + (-1,))
    caches = (merged_heads(cache_a_k), merged_heads(cache_a_v), merged_heads(cache_b_k),
              merged_heads(cache_b_v), cache_c_kv, cache_c_kr)
    states_p, states_s = [], []
    for l in range(depth):
        w_main, w_gates, w_uq_p, w_ukv, w_ukt, w_uvp = _layer_weights(l, w_in, w_uq, w_uk, w_uv)
        ffn1 = (row(ffn1_norm_pre, l), row(ffn1_norm_post, l), _layer_bf16(ffn1_w_gate, l),
                _layer_bf16(ffn1_w_up, l), _layer_bf16(ffn1_w_down, l))
        ffn2 = (row(ffn2_norm_pre, l), row(ffn2_norm_post, l), _layer_bf16(ffn2_w_gate, l),
                _layer_bf16(ffn2_w_up, l), _layer_bf16(ffn2_w_down, l))
        mix_in = (row(mix_norm_pre, l), w_main, row(cq_norm, l), row(ckv_norm, l), w_uq_p, w_ukv)
        mrg = (row(mix_norm_pre, l), row(mix_norm_post, l), w_gates, _layer_bf16(w_br_a, l),
               _layer_bf16(w_br_b, l), _layer_bf16(w_br_c, l), _layer_bf16(w_out, l))
        bias_p = _band_bias(rel_bias_a[l], base, tile, base - 2 * tile, 3 * tile)
        bias_c = _band_bias(rel_bias_a[l], past, ds, past - win_cache, win_cache)
        bias_n = _band_bias(rel_bias_a[l], past, ds, past, ds)

        xp = _ffn(xp, *ffn1, tm)
        (qa, kab, vab, qb, kbb, vbb, ka, va, kb, vb, ckv, krb, qc, kc, vc, kn) = _mixer_in(
            xp, *mix_in, tabs_p, tm, q_transposed=True)
        ya = _band_attention(qa, kab, vab, bias_p, tile)
        yb = _sb_attention(qb, kbb, vbb, tile)
        yc = _mla_attention(qc, kc, vc, kn)
        xp = _merge(xp, ya, yb, yc, *mrg, tm, yc_transposed=True)
        xp = _ffn(xp, *ffn2, tm)
        win = min(WIN_A, seq)
        heads = lambda t, h: t.reshape(1, t.shape[0], h, t.shape[1] // h)
        states_p.append((heads(ka[-win:], H_A), heads(va[-win:], H_A), heads(kb, H_B), heads(vb, H_B),
                         ckv[None], krb[None, :, C_NOPE:C_NOPE + C_ROPE]))

        xs = _ffn(xs, *ffn1, ns)
        (qa, kab, vab, qb, kbb, vbb, ka, va, kb, vb, ckv, krb, qc, kc, vc, _) = _mixer_in(
            xs, *mix_in, tabs_s, ns)
        per_req = lambda t: t.reshape(nb, ds, t.shape[1])
        new = tuple(per_req(t) for t in (qa, kab, vab, qb, kbb, vbb, qc, ckv, krb))
        ya, yb, yc = _sample_attention(new, caches, l, bias_c, bias_n, w_ukt, w_uvp, tile)
        xs = _merge(xs, ya.reshape(ns, -1), yb.reshape(ns, -1), yc.reshape(ns, -1), *mrg, ns)
        xs = _ffn(xs, *ffn2, ns)
        heads_s = lambda t, h: t.reshape(nb, ds, h, t.shape[1] // h)
        states_s.append((heads_s(ka, H_A), heads_s(va, H_A), heads_s(kb, H_B), heads_s(vb, H_B),
                         ckv.reshape(nb, ds, -1), krb[:, C_NOPE:C_NOPE + C_ROPE].reshape(nb, ds, C_ROPE)))

    stack = lambda states, i: jnp.stack([s[i] for s in states], axis=0)
    return ((xp.reshape(batch, seq, d), xs.reshape(nb, ds, d))
            + tuple(stack(states_p, i) for i in range(6))
            + tuple(stack(states_s, i) for i in range(6)))
```

```python
import functools

import numpy as np
import jax
import jax.numpy as jnp
from jax import lax
from jax.experimental import pallas as pl
from jax.experimental.pallas import tpu as pltpu

F32 = jnp.float32
BF16 = jnp.bfloat16

CHUNK = 64
BAND_CHUNKS = 8
WIN_A = BAND_CHUNKS * CHUNK
HEAD_DIM = 64
H_A = 4
H_B = 4
H_C = 8
C_NOPE = 64
C_ROPE = 32
C_V = 64
REL_CLIP = 128
ROPE_THETA = 10000.0
EPS = 1e-6
N_BRANCH = 3
W_AB = H_A * HEAD_DIM
C_PAD = 128
ATTN_SCALE = HEAD_DIM ** -0.5
MLA_SCALE = (C_NOPE + C_ROPE) ** -0.5
LOG2E = float(np.log2(np.e))
C_VX = C_V + 16
C_ONE = C_NOPE + C_ROPE
NORM_MARGIN = 1.02
MLA_MIN_DENOM = 2.0 ** -60
MLA_UNROLL = 4
SAMPLE_LATENT_TILE = 1024
NEG = -1e30
SB_DEAD = -150.0

LANES = 128
MXU_WIDTH = 256
VMEM_LIMIT = 52 * 1024 * 1024

NT_DIMS = (((1,), (1,)), ((), ()))
TN_DIMS = (((0,), (0,)), ((), ()))


def _rms(x, g):
    return x * lax.rsqrt(jnp.mean(x * x, axis=-1, keepdims=True) + EPS) * g


def _dot(a, b):
    return jnp.dot(a, b, preferred_element_type=F32)


def _dot_nt(a, b):
    return lax.dot_general(a, b, NT_DIMS, preferred_element_type=F32)


def _dot_tn(a, b):
    return lax.dot_general(a, b, TN_DIMS, preferred_element_type=F32)


def _const_spec(shape):
    zeros = (0,) * len(shape)
    return pl.BlockSpec(shape, lambda *_: zeros, pipeline_mode=pl.Buffered(1))


def _row_spec(tm, width):
    return pl.BlockSpec((tm, width), lambda i: (i, 0))


def _params(n_grid):
    return pltpu.CompilerParams(dimension_semantics=("arbitrary",) * n_grid,
                                vmem_limit_bytes=VMEM_LIMIT)


def _ffn_kernel(x_ref, gpre_ref, gpost_ref, wg_ref, wu_ref, wd_ref, o_ref, *, f_cuts):
    x = x_ref[...]
    h = _rms(x, gpre_ref[...]).astype(BF16)
    y = jnp.zeros(x.shape, F32)
    for lo, hi in zip(f_cuts[:-1], f_cuts[1:]):
        sl = slice(lo, hi)
        g = _dot(h, wg_ref[:, sl])
        u = _dot(h, wu_ref[:, sl])
        a = (g * jax.nn.sigmoid(g) * u).astype(BF16)
        y = y + _dot(a, wd_ref[sl, :])
    o_ref[...] = x + 0.5 * _rms(y, gpost_ref[...])


def _ffn(x, g_pre, g_post, w_gate, w_up, w_down, tm):
    n, d = x.shape
    d_ff = w_gate.shape[1]
    cut = (d_ff // 2) // MXU_WIDTH * MXU_WIDTH
    f_cuts = (0, cut, d_ff) if 0 < cut < d_ff else (0, d_ff)
    return pl.pallas_call(
        functools.partial(_ffn_kernel, f_cuts=f_cuts),
        grid=(n // tm,),
        in_specs=[_row_spec(tm, d), _const_spec((1, d)), _const_spec((1, d)),
                  _const_spec((d, d_ff)), _const_spec((d, d_ff)), _const_spec((d_ff, d))],
        out_specs=_row_spec(tm, d),
        out_shape=jax.ShapeDtypeStruct((n, d), F32),
        compiler_params=_params(1),
        name="ffn",
    )(x, g_pre, g_post, w_gate, w_up, w_down)


def _rope(blk, tc, ts1, ts2):
    return blk * tc + pltpu.roll(blk, 16, 1) * ts1 + pltpu.roll(blk, C_PAD - 16, 1) * ts2


def _mixer_in_kernel(x_ref, gpre_ref, win_ref, gcq_ref, gckv_ref, wuq_ref, wukv_ref,
                     tc_ref, ts1_ref, ts2_ref,
                     qa_ref, kab_ref, vab_ref, qb_ref, kbb_ref, vbb_ref,
                     ka_ref, va_ref, kb_ref, vb_ref, ckv_ref, krb_ref,
                     qc_ref, kc_ref, vc_ref, kn_ref, *, q_transposed):
    w = W_AB
    q_rank = gcq_ref.shape[1]
    kv_rank = gckv_ref.shape[1]
    u = _rms(x_ref[...], gpre_ref[...]).astype(BF16)
    o = 6 * w
    lat = _dot(u, win_ref[:, o:])
    cq = lat[:, :q_rank]
    ckv = lat[:, q_rank:q_rank + kv_rank]
    krb = lat[:, q_rank + kv_rank:q_rank + kv_rank + C_PAD]
    proj = _dot(u, win_ref[:, :o])
    qa, ka, va = proj[:, 0:w], proj[:, w:2 * w], proj[:, 2 * w:3 * w]
    qb, kb, vb = proj[:, 3 * w:4 * w], proj[:, 4 * w:5 * w], proj[:, 5 * w:6 * w]
    qa_ref[...] = (qa * ATTN_SCALE).astype(BF16)
    qb_ref[...] = (qb * (ATTN_SCALE * LOG2E)).astype(BF16)
    ka_ref[...] = ka
    va_ref[...] = va
    kb_ref[...] = kb
    vb_ref[...] = vb
    kab_ref[...] = ka.astype(BF16)
    vab_ref[...] = va.astype(BF16)
    kbb_ref[...] = kb.astype(BF16)
    vbb_ref[...] = vb.astype(BF16)
    tc, ts1, ts2 = tc_ref[...], ts1_ref[...], ts2_ref[...]
    cqn = _rms(cq, gcq_ref[...]).astype(BF16)
    qall = _dot(cqn, wuq_ref[...])
    ckvn = _rms(ckv, gckv_ref[...])
    ckv_ref[...] = ckvn
    kv = _dot(ckvn.astype(BF16), wukv_ref[...])
    krot = _rope(krb, tc, ts1, ts2)
    krb_ref[...] = krot
    one_lane = (lax.broadcasted_iota(jnp.int32, (1, C_PAD), 1) == C_ONE).astype(F32)
    for h in range(H_C):
        sl = slice(h * C_PAD, (h + 1) * C_PAD)
        qh = _rope(qall[:, sl], tc, ts1, ts2) * (MLA_SCALE * LOG2E)
        if q_transposed:
            qt = qh.T
            norm = jnp.sqrt(jnp.sum(qt * qt, axis=0, keepdims=True)) * NORM_MARGIN
            row = lax.broadcasted_iota(jnp.int32, qt.shape, 0)
            qc_ref[sl, :] = jnp.where(row == C_ONE, norm, qt).astype(BF16)
        else:
            qc_ref[:, sl] = qh.astype(BF16)
        kh = kv[:, sl] + krot
        kc_ref[:, sl] = (kh + one_lane).astype(BF16)
        k_norm = jnp.sqrt(jnp.max(jnp.sum(kh * kh, axis=1, keepdims=True), axis=0, keepdims=True))
        kn_ref[0, :, sl] = jnp.broadcast_to(k_norm * NORM_MARGIN, (1, C_PAD))
    if not q_transposed:
        vc_ref[...] = kv[:, H_C * C_PAD:].astype(BF16)
    else:
        ones_row = (lax.broadcasted_iota(jnp.int32, (C_VX - C_V, x_ref.shape[0]), 0) == 0).astype(BF16)
        for p in range(H_C // 2):
            o = H_C * C_PAD + p * 2 * C_V
            vt = kv[:, o:o + 2 * C_V].T
            for e in range(2):
                r = (2 * p + e) * C_VX
                vc_ref[0, r:r + C_V, :] = vt[e * C_V:(e + 1) * C_V, :].astype(BF16)
                vc_ref[0, r + C_V:r + C_VX, :] = ones_row


def _mixer_in(x, g_pre, w_main, g_cq, g_ckv, w_uq, w_ukv, tabs, tm, q_transposed=False):
    n, d = x.shape
    q_rank, kv_rank = g_cq.shape[1], g_ckv.shape[1]
    wc = H_C * C_PAD
    wv = H_C * C_V
    bf = lambda width: jax.ShapeDtypeStruct((n, width), BF16)
    f32 = lambda width: jax.ShapeDtypeStruct((n, width), F32)
    q_shape = jax.ShapeDtypeStruct((wc, n), BF16) if q_transposed else bf(wc)
    q_spec = pl.BlockSpec((wc, tm), lambda i: (0, i)) if q_transposed else _row_spec(tm, wc)
    v_shape = jax.ShapeDtypeStruct((n // tm, H_C * C_VX, tm), BF16) if q_transposed else bf(wv)
    v_spec = (pl.BlockSpec((1, H_C * C_VX, tm), lambda i: (i, 0, 0)) if q_transposed
              else _row_spec(tm, wv))
    out_shape = ([bf(W_AB)] * 6 + [f32(W_AB)] * 4 + [f32(kv_rank), f32(C_PAD)]
                 + [q_shape, bf(wc), v_shape, jax.ShapeDtypeStruct((n // tm, 1, wc), F32)])
    out_specs = ([_row_spec(tm, W_AB)] * 10 + [_row_spec(tm, kv_rank), _row_spec(tm, C_PAD)]
                 + [q_spec, _row_spec(tm, wc), v_spec, pl.BlockSpec((1, 1, wc), lambda i: (i, 0, 0))])
    return pl.pallas_call(
        functools.partial(_mixer_in_kernel, q_transposed=q_transposed),
        grid=(n // tm,),
        in_specs=[_row_spec(tm, d), _const_spec((1, d)), _const_spec(w_main.shape),
                  _const_spec((1, q_rank)), _const_spec((1, kv_rank)),
                  _const_spec(w_uq.shape), _const_spec(w_ukv.shape),
                  _row_spec(tm, C_PAD), _row_spec(tm, C_PAD), _row_spec(tm, C_PAD)],
        out_specs=out_specs,
        out_shape=out_shape,
        compiler_params=_params(1),
        name="mixer_in",
    )(x, g_pre, w_main, g_cq, g_ckv, w_uq, w_ukv, *tabs)


def _merge_kernel(x_ref, ya_ref, yb_ref, yc_ref, gpre_ref, gpost_ref, wgates_ref,
                  wa_ref, wb_ref, wc_ref, wout_ref, o_ref, *, yc_transposed):
    x = x_ref[...]
    d = x.shape[1]
    u = _rms(x, gpre_ref[...]).astype(BF16)
    m = jnp.zeros(x.shape, F32)
    for i, (y_ref, w_ref) in enumerate(((ya_ref, wa_ref), (yb_ref, wb_ref), (yc_ref, wc_ref))):
        gate = jax.nn.sigmoid(_dot(u, wgates_ref[:, i * d:(i + 1) * d]))
        branch_dot = _dot_tn if (yc_transposed and y_ref is yc_ref) else _dot
        m = m + gate * branch_dot(y_ref[...], w_ref[...])
    mixed = _dot(m.astype(BF16), wout_ref[...])
    o_ref[...] = x + _rms(mixed, gpost_ref[...])


def _merge(x, ya, yb, yc, g_pre, g_post, w_gates, w_a, w_b, w_c, w_out, tm, yc_transposed=False):
    n, d = x.shape
    yc_spec = (pl.BlockSpec((yc.shape[0], tm), lambda i: (0, i)) if yc_transposed
               else _row_spec(tm, yc.shape[1]))
    return pl.pallas_call(
        functools.partial(_merge_kernel, yc_transposed=yc_transposed),
        grid=(n // tm,),
        in_specs=[_row_spec(tm, d), _row_spec(tm, ya.shape[1]), _row_spec(tm, yb.shape[1]),
                  yc_spec, _const_spec((1, d)), _const_spec((1, d)),
                  _const_spec(w_gates.shape), _const_spec(w_a.shape), _const_spec(w_b.shape),
                  _const_spec(w_c.shape), _const_spec(w_out.shape)],
        out_specs=_row_spec(tm, d),
        out_shape=jax.ShapeDtypeStruct((n, d), F32),
        compiler_params=_params(1),
        name="merge",
    )(x, ya, yb, yc, g_pre, g_post, w_gates, w_a, w_b, w_c, w_out)


def _head_masked(q, width):
    lane_head = lax.broadcasted_iota(jnp.int32, q.shape, 1) // width
    zero = jnp.zeros_like(q)
    return [jnp.where(lane_head == h, q, zero) for h in range(q.shape[1] // width)]


def _strict_upper_ones(n):
    j = lax.broadcasted_iota(jnp.int32, (n, n), 0)
    s = lax.broadcasted_iota(jnp.int32, (n, n), 1)
    return (j > s).astype(BF16)


def _sb_terms(qm, kt, ones_ut, causal):
    z = _dot_nt(qm, kt)
    t = jnp.log2(1.0 + jnp.exp2(-jnp.abs(z)))
    log_keep = -jnp.maximum(z, 0.0) - t
    log_beta = jnp.minimum(z, 0.0) - t
    if causal is not None:
        log_keep = jnp.where(causal, log_keep, 0.0)
    hi = log_keep.astype(BF16)
    lo = (log_keep - hi.astype(F32)).astype(BF16)
    tail = _dot(hi, ones_ut) + _dot(lo, ones_ut)
    return log_beta + tail, jnp.sum(log_keep, axis=1, keepdims=True)


def _sb_apply(terms, vt, carry, acc, causal):
    logw, total = terms
    w = jnp.exp2(logw + carry)
    if causal is not None:
        w = jnp.where(causal, w, 0.0)
    return carry + total, acc + _dot(w.astype(BF16), vt)


def _sb_tile(qm, kt, vt, carry, acc, ones_ut, causal):
    return _sb_apply(_sb_terms(qm, kt, ones_ut, causal), vt, carry, acc, causal)


def _sb_alive(carry):
    return jnp.max(carry) > SB_DEAD


def _select_heads(parts, width):
    lane_head = lax.broadcasted_iota(jnp.int32, parts[0].shape, 1) // width
    out = parts[0]
    for h in range(1, len(parts)):
        out = jnp.where(lane_head == h, parts[h], out)
    return out


def _band_kernel(q_ref, k0_ref, k1_ref, k2_ref, v0_ref, v1_ref, v2_ref, bias_ref, o_ref):
    i = pl.program_id(0)
    tq = q_ref.shape[0]
    kwin = jnp.concatenate([k0_ref[...], k1_ref[...], k2_ref[...]], axis=0)
    vwin = jnp.concatenate([v0_ref[...], v1_ref[...], v2_ref[...]], axis=0)
    col = lax.broadcasted_iota(jnp.int32, (tq, 3 * tq), 1)
    in_range = col >= (2 - i) * tq
    parts = []
    for h, qm in enumerate(_head_masked(q_ref[...], HEAD_DIM)):
        s = jnp.where(in_range, _dot_nt(qm, kwin) + bias_ref[h], NEG)
        m = jnp.max(s, axis=1, keepdims=True)
        p = jnp.exp(s - m)
        l = jnp.sum(p, axis=1, keepdims=True)
        parts.append(_dot(p.astype(BF16), vwin) / l)
    o_ref[...] = _select_heads(parts, HEAD_DIM).astype(BF16)


def _band_attention(q, k, v, bias, tq):
    n, w = q.shape
    blk = lambda back: pl.BlockSpec((tq, w), lambda i: (jnp.maximum(i - back, 0), 0))
    return pl.pallas_call(
        _band_kernel,
        grid=(n // tq,),
        in_specs=[blk(0), blk(2), blk(1), blk(0), blk(2), blk(1), blk(0), _const_spec(bias.shape)],
        out_specs=blk(0),
        out_shape=jax.ShapeDtypeStruct((n, w), BF16),
        compiler_params=_params(1),
        name="band_attn",
    )(q, k, k, k, v, v, v, bias)


def _sb_kernel(q_ref, k_ref, v_ref, o_ref, *, tile):
    i = pl.program_id(0)
    row = lax.broadcasted_iota(jnp.int32, (tile, tile), 0)
    col = lax.broadcasted_iota(jnp.int32, (tile, tile), 1)
    causal = col < row
    ones_ut = _strict_upper_ones(tile)
    width = q_ref.shape[1]
    qms = _head_masked(q_ref[...], HEAD_DIM)
    rows = lambda ref, kb: ref[pl.ds(pl.multiple_of(kb * tile, tile), tile), :]

    prev = jnp.maximum(i - 1, 0)
    k_diag, k_prev = rows(k_ref, i), rows(k_ref, prev)
    terms_diag = [_sb_terms(qm, k_diag, ones_ut, causal) for qm in qms]
    terms_prev = [_sb_terms(qm, k_prev, ones_ut, None) for qm in qms]
    v_diag, v_prev = rows(v_ref, i), rows(v_ref, prev)
    parts = []
    for qm, t_diag, t_prev in zip(qms, terms_diag, terms_prev):
        carry, acc = _sb_apply(t_diag, v_diag, jnp.zeros((tile, 1), F32), jnp.zeros((tile, width), F32),
                               causal)
        carry, acc = _sb_apply(t_prev, v_prev, jnp.where(i > 0, carry, NEG), acc, None)
        state = lax.while_loop(
            lambda st: (st[0] < i - 1) & _sb_alive(st[1]),
            lambda st, qm=qm: (st[0] + 1,) + _sb_tile(qm, rows(k_ref, i - 2 - st[0]), rows(v_ref, i - 2 - st[0]),
                                                      st[1], st[2], ones_ut, None),
            (jnp.int32(0), carry, acc))
        parts.append(state[2])
    o_ref[...] = _select_heads(parts, HEAD_DIM).astype(BF16)


def _sb_attention(q, k, v, tile):
    n, w = q.shape
    return pl.pallas_call(
        functools.partial(_sb_kernel, tile=tile),
        grid=(n // tile,),
        in_specs=[_row_spec(tile, w), _const_spec((n, w)), _const_spec((n, w))],
        out_specs=_row_spec(tile, w),
        out_shape=jax.ShapeDtypeStruct((n, w), BF16),
        compiler_params=_params(1),
        name="sb_attn",
    )(q, k, v)


def _mla_kernel(qt_ref, k_ref, vt_ref, kn_ref, ot_ref, qx_scr, p_scr, acc_scr):
    i = pl.program_id(1)
    heads = range(2)
    tk, tq = p_scr.shape[2:]
    per_q = tq // tk
    plain = per_q * i
    key = lax.broadcasted_iota(jnp.int32, (tk, tq), 0)
    qry = lax.broadcasted_iota(jnp.int32, (tk, tq), 1)
    row = lax.broadcasted_iota(jnp.int32, (C_PAD - C_ONE, tq), 0)

    def visible(d):
        return ((d * tk + key) // CHUNK) <= (qry // CHUNK)

    def load_q(shifted):
        k_bound = jnp.max(kn_ref[...], axis=0)
        for e in heads:
            lo = e * C_PAD
            qx_scr[lo:lo + C_ONE, :] = qt_ref[lo:lo + C_ONE, :]
            if shifted:
                bound = qt_ref[lo + C_ONE:lo + C_ONE + 1, :].astype(F32) * (k_bound[:, lo:lo + 1] * NORM_MARGIN)
                tail = jnp.where(row == 0, -bound, 0.0)
            else:
                tail = jnp.zeros(row.shape, F32)
            qx_scr[lo + C_ONE:lo + C_PAD, :] = tail.astype(BF16)

    def scores(kb, e):
        return _dot(k_ref[kb, :, e * C_PAD:(e + 1) * C_PAD], qx_scr[e * C_PAD:(e + 1) * C_PAD, :])

    def values(kb, e, p):
        return _dot(vt_ref[kb, e * C_VX:(e + 1) * C_VX, :], p)

    def write_out():
        ot_ref[...] = jnp.concatenate([acc_scr[e, :C_V] / acc_scr[e, C_V:C_V + 1] for e in heads],
                                      axis=0).astype(BF16)

    def fast_stage(kb, slot, mask):
        for e in heads:
            s = scores(kb, e)
            if mask is not None:
                s = jnp.where(mask, s, NEG)
            p_scr[slot, e] = jnp.exp2(s).astype(BF16)
        for e in heads:
            acc_scr[e] += values(kb, e, p_scr[slot, e])

    load_q(True)
    acc_scr[...] = jnp.zeros(acc_scr.shape, F32)

    def fast_group(g, _):
        for j in range(MLA_UNROLL):
            fast_stage(MLA_UNROLL * g + j, j, None)
        return 0
    groups = plain // MLA_UNROLL
    lax.fori_loop(0, groups, fast_group, 0)

    def fast_single(kb, _):
        fast_stage(kb, 0, None)
        return 0
    lax.fori_loop(MLA_UNROLL * groups, plain, fast_single, 0)
    for d in range(per_q):
        fast_stage(plain + d, 1 + d % (MLA_UNROLL - 1), visible(d))
    denom = jnp.minimum(jnp.min(acc_scr[0, C_V:C_V + 1]), jnp.min(acc_scr[1, C_V:C_V + 1]))
    trusted = denom >= MLA_MIN_DENOM

    @pl.when(trusted)
    def _():
        write_out()

    @pl.when(jnp.logical_not(trusted))
    def _():
        load_q(False)
        acc_scr[...] = jnp.zeros(acc_scr.shape, F32)

        def step(kb, m):
            out = []
            for e in heads:
                s = jnp.where(visible(kb - plain), scores(kb, e), NEG)
                m_new = jnp.maximum(m[e], jnp.max(s, axis=0, keepdims=True))
                p = jnp.exp2(s - m_new).astype(BF16)
                acc_scr[e] = jnp.exp2(m[e] - m_new) * acc_scr[e] + values(kb, e, p)
                out.append(m_new)
            return tuple(out)
        lax.fori_loop(0, plain + per_q, step, tuple(jnp.full((1, tq), NEG, F32) for _ in heads))
        write_out()


def _mla_attention(qt, k, vt3, kn, tq):
    n = k.shape[0]
    pairs = H_C // 2
    n_tiles, _, tk = vt3.shape
    assert n_tiles * tk == n and kn.shape == (n_tiles, 1, H_C * C_PAD)
    assert tq % tk == 0 and n % tq == 0 and tq // tk < MLA_UNROLL
    k3 = k.reshape(n_tiles, tk, H_C * C_PAD)
    return pl.pallas_call(
        _mla_kernel,
        grid=(pairs, n // tq),
        in_specs=[pl.BlockSpec((2 * C_PAD, tq), lambda p, i: (p, i)),
                  pl.BlockSpec((n_tiles, tk, 2 * C_PAD), lambda p, i: (0, 0, p),
                               pipeline_mode=pl.Buffered(1)),
                  pl.BlockSpec((n_tiles, 2 * C_VX, tk), lambda p, i: (0, p, 0),
                               pipeline_mode=pl.Buffered(1)),
                  pl.BlockSpec((n_tiles, 1, 2 * C_PAD), lambda p, i: (0, 0, p))],
        out_specs=pl.BlockSpec((2 * C_V, tq), lambda p, i: (p, i)),
        out_shape=jax.ShapeDtypeStruct((H_C * C_V, n), BF16),
        scratch_shapes=[pltpu.VMEM((2 * C_PAD, tq), BF16), pltpu.VMEM((MLA_UNROLL, 2, tk, tq), BF16),
                        pltpu.VMEM((2, C_VX, tq), F32)],
        compiler_params=_params(2),
        name="mla_attn",
    )(qt, k3, vt3, kn)


def _sample_attn_kernel(qa_ref, ka_ref, va_ref, cak_ref, cav_ref, biasc_ref, biasn_ref,
                        qb_ref, kb_ref, vb_ref, cbk_ref, cbv_ref,
                        qc_ref, ckvn_ref, krbn_ref, ckv_ref, ckr_ref, wukt_ref, wuvp_ref,
                        ya_ref, yb_ref, yc_ref, s_scr, *, tile):
    ds = qa_ref.shape[1]
    past = cbk_ref.shape[2]

    ka, va, cak, cav = ka_ref[0], va_ref[0], cak_ref[0, 0].astype(BF16), cav_ref[0, 0].astype(BF16)
    parts = []
    for h, qm in enumerate(_head_masked(qa_ref[0], HEAD_DIM)):
        s_c = _dot_nt(qm, cak) + biasc_ref[h]
        s_n = _dot_nt(qm, ka) + biasn_ref[h]
        m = jnp.maximum(jnp.max(s_c, axis=1, keepdims=True), jnp.max(s_n, axis=1, keepdims=True))
        p_c = jnp.exp(s_c - m)
        p_n = jnp.exp(s_n - m)
        l = jnp.sum(p_c, axis=1, keepdims=True) + jnp.sum(p_n, axis=1, keepdims=True)
        parts.append((_dot(p_c.astype(BF16), cav) + _dot(p_n.astype(BF16), va)) / l)
    ya_ref[0] = _select_heads(parts, HEAD_DIM).astype(BF16)

    row = lax.broadcasted_iota(jnp.int32, (ds, ds), 0)
    col = lax.broadcasted_iota(jnp.int32, (ds, ds), 1)
    causal = col < row
    ones_new = _strict_upper_ones(ds)
    ones_ut = _strict_upper_ones(tile)
    kb, vb = kb_ref[0], vb_ref[0]
    n_tiles = past // tile
    parts = []
    for qm in _head_masked(qb_ref[0], HEAD_DIM):
        state = _sb_tile(qm, kb, vb, jnp.zeros((ds, 1), F32), jnp.zeros((ds, W_AB), F32),
                         ones_new, causal)

        def cache_tile(st, qm=qm):
            start = pl.multiple_of((n_tiles - 1 - st[0]) * tile, tile)
            return (st[0] + 1,) + _sb_tile(qm, cbk_ref[0, 0, pl.ds(start, tile), :].astype(BF16),
                                           cbv_ref[0, 0, pl.ds(start, tile), :].astype(BF16),
                                           st[1], st[2], ones_ut, None)
        state = lax.while_loop(lambda st: (st[0] < n_tiles) & _sb_alive(st[1]), cache_tile,
                               (jnp.int32(0),) + state)
        parts.append(state[2])
    yb_ref[0] = _select_heads(parts, HEAD_DIM).astype(BF16)

    kv_rank = ckv_ref.shape[3]
    qc = qc_ref[0]
    rope_lanes = lax.broadcasted_iota(jnp.int32, (ds, C_PAD), 1) >= C_NOPE
    q_rows = []
    for h in range(H_C):
        qh = qc[:, h * C_PAD:(h + 1) * C_PAD]
        q_lat = _dot(qh, wukt_ref[h]).astype(BF16)
        q_rows.append(jnp.concatenate([q_lat, jnp.where(rope_lanes, qh, jnp.zeros_like(qh))], axis=1))
    q_all = jnp.concatenate(q_rows, axis=0)

    place = (lax.broadcasted_iota(jnp.int32, (C_ROPE, C_PAD), 1)
             == lax.broadcasted_iota(jnp.int32, (C_ROPE, C_PAD), 0) + C_NOPE).astype(BF16)
    tile_c = SAMPLE_LATENT_TILE if past % SAMPLE_LATENT_TILE == 0 else tile
    cache_tiles = range(past // tile_c)
    latent = lambda j: ckv_ref[0, 0, j * tile_c:(j + 1) * tile_c, :].astype(BF16)

    lat_new = ckvn_ref[0].astype(BF16)
    s_new = _dot_nt(q_all, jnp.concatenate([lat_new, krbn_ref[0].astype(BF16)], axis=1))
    m = jnp.max(s_new, axis=1, keepdims=True)
    for j in cache_tiles:
        k_rope = _dot(ckr_ref[0, 0, j * tile_c:(j + 1) * tile_c, :].astype(BF16), place).astype(BF16)
        s = _dot_nt(q_all, jnp.concatenate([latent(j), k_rope], axis=1))
        s_scr[:, j * tile_c:(j + 1) * tile_c] = s
        m = jnp.maximum(m, jnp.max(s, axis=1, keepdims=True))
    p = jnp.exp2(s_new - m)
    denom = jnp.sum(p, axis=1, keepdims=True)
    o_lat = _dot(p.astype(BF16), lat_new)
    for j in cache_tiles:
        p = jnp.exp2(s_scr[:, j * tile_c:(j + 1) * tile_c] - m)
        denom = denom + jnp.sum(p, axis=1, keepdims=True)
        o_lat = o_lat + _dot(p.astype(BF16), latent(j))
    o_lat = (o_lat / denom).astype(BF16)
    out = jnp.zeros((ds, H_C * C_V), F32)
    for h in range(H_C):
        out = out + _dot(o_lat[h * ds:(h + 1) * ds], wuvp_ref[h])
    yc_ref[0] = out.astype(BF16)


def _sample_attention(new, caches, l, bias_c, bias_n, w_ukt, w_uvp, tile):
    qa, ka, va, qb, kb, vb, qc, ckvn, krbn = new
    cak, cav, cbk, cbv, ckv, ckr = caches
    nb, ds, _ = qa.shape
    past = ckv.shape[2]
    req = lambda a: pl.BlockSpec((1,) + a.shape[1:], lambda b: (b, 0, 0))
    cache = lambda a: pl.BlockSpec((1, 1) + a.shape[2:], lambda b: (l, b, 0, 0))
    ins = [qa, ka, va, cak, cav, bias_c, bias_n, qb, kb, vb, cbk, cbv, qc, ckvn, krbn, ckv, ckr,
           w_ukt, w_uvp]
    specs = [req(a) for a in ins]
    for idx in (3, 4, 10, 11, 15, 16):
        specs[idx] = cache(ins[idx])
    for idx in (5, 6, 17, 18):
        specs[idx] = _const_spec(ins[idx].shape)
    out_shape = [jax.ShapeDtypeStruct((nb, ds, W_AB), BF16), jax.ShapeDtypeStruct((nb, ds, W_AB), BF16),
                 jax.ShapeDtypeStruct((nb, ds, H_C * C_V), BF16)]
    return pl.pallas_call(
        functools.partial(_sample_attn_kernel, tile=tile),
        grid=(nb,),
        in_specs=specs,
        out_specs=[req(s) for s in out_shape],
        out_shape=out_shape,
        scratch_shapes=[pltpu.VMEM((H_C * ds, past), F32)],
        compiler_params=_params(1),
        name="sample_attn",
    )(*ins)


def _rope_tables(pos):
    half = C_ROPE // 2
    freq = ROPE_THETA ** (-jnp.arange(half, dtype=F32) / half)
    ang = pos.astype(F32)[:, None] * freq[None, :]
    cos, sin = jnp.cos(ang), jnp.sin(ang)
    n = pos.shape[0]
    z = lambda w: jnp.zeros((n, w), F32)
    tc = jnp.concatenate([jnp.ones((n, C_NOPE), F32), cos, cos, z(C_PAD - C_NOPE - C_ROPE)], axis=1)
    ts1 = jnp.concatenate([z(C_NOPE + half), sin, z(C_PAD - C_NOPE - C_ROPE)], axis=1)
    ts2 = jnp.concatenate([z(C_NOPE), -sin, z(C_PAD - C_NOPE - half)], axis=1)
    return tc, ts1, ts2


def _band_bias(rel_bias, q0, nq, k0, nk):
    period = 1 << int(np.ceil(np.log2(nq + nk)))
    d = np.arange(period)
    d = np.where(d >= nk, d - period, d)
    idx = np.clip((q0 - k0) - d, -REL_CLIP, REL_CLIP) + REL_CLIP
    g = rel_bias.astype(F32)[:, idx]
    toe = jnp.tile(g, (1, nq))[:, :nq * (period - 1)].reshape(-1, nq, period - 1)[:, :, :nk]
    q_pos, k_pos = q0 + np.arange(nq), k0 + np.arange(nk)
    qc, kc = q_pos[:, None] // CHUNK, k_pos[None, :] // CHUNK
    vis = (kc <= qc) & (kc >= qc - BAND_CHUNKS)
    return jnp.where(jnp.asarray(vis)[None], toe, NEG)


def _cast_kernel(w_ref, o_ref):
    o_ref[...] = w_ref[0].astype(BF16)


def _layer_bf16(w, l):
    _, rows, cols = w.shape
    rb = rows // 4 if rows % 64 == 0 else rows
    return pl.pallas_call(
        _cast_kernel,
        grid=(rows // rb,),
        in_specs=[pl.BlockSpec((1, rb, cols), lambda i: (l, i, 0))],
        out_specs=pl.BlockSpec((rb, cols), lambda i: (i, 0)),
        out_shape=jax.ShapeDtypeStruct((rows, cols), BF16),
        compiler_params=_params(1),
        name="cast_bf16",
    )(w)


def _layer_weights(l, w_in, w_uq, w_uk, w_uv):
    d = w_in.shape[1]
    q_rank, kv_rank = w_uq.shape[1], w_uk.shape[1]
    o = 6 * W_AB + q_rank + kv_rank
    zc = lambda w: jnp.zeros((d, w), w_in.dtype)
    w_main = jnp.concatenate([w_in[l, :, :o], zc(C_NOPE), w_in[l, :, o:o + C_ROPE],
                              zc(C_PAD - C_NOPE - C_ROPE)], axis=1).astype(BF16)
    w_gates = w_in[l, :, o + C_ROPE:].astype(BF16)
    pad_heads = lambda w: jnp.pad(w, ((0, 0), (0, 0), (0, C_PAD - w.shape[2]))).reshape(w.shape[0], -1)
    w_uq_p = pad_heads(w_uq[l]).astype(BF16)
    w_ukv = jnp.concatenate([pad_heads(w_uk[l]), w_uv[l].reshape(kv_rank, -1)], axis=1).astype(BF16)
    w_ukt = jnp.pad(jnp.transpose(w_uk[l], (1, 2, 0)), ((0, 0), (0, C_PAD - C_NOPE), (0, 0))).astype(BF16)
    w_uvp = jnp.einsum('rhd,hg->hrgd', w_uv[l], jnp.eye(H_C, dtype=w_uv.dtype)).reshape(
        H_C, kv_rank, H_C * C_V).astype(BF16)
    return w_main, w_gates, w_uq_p, w_ukv, w_ukt, w_uvp


def kernel(x_prompt, x_sample, cache_a_k, cache_a_v, cache_b_k, cache_b_v, cache_c_kv, cache_c_kr, ffn1_norm_pre, ffn1_norm_post, ffn1_w_gate, ffn1_w_up, ffn1_w_down, mix_norm_pre, mix_norm_post, w_in, cq_norm, ckv_norm, w_uq, w_uk, w_uv, rel_bias_a, w_br_a, w_br_b, w_br_c, w_out, ffn2_norm_pre, ffn2_norm_post, ffn2_w_gate, ffn2_w_up, ffn2_w_down):
    batch, seq, d = x_prompt.shape
    nb, ds, _ = x_sample.shape
    depth = w_in.shape[0]
    past = cache_b_k.shape[2]
    win_cache = cache_a_k.shape[2]
    assert batch == 1, "prompt group is a single sequence"
    tile = 256
    tm = 512 if seq % 512 == 0 else tile
    assert seq % tile == 0 and past % tile == 0 and tile % CHUNK == 0 and 2 * tile >= WIN_A
    assert past % CHUNK + ds <= CHUNK
    ns = nb * ds

    xp = x_prompt.reshape(seq, d)
    xs = x_sample.reshape(ns, d)
    tabs_p = _rope_tables(jnp.arange(seq))
    tabs_s = _rope_tables(jnp.tile(past + jnp.arange(ds), nb))

    row = lambda g, l: g[l][None, :]
    base = 2 * tile

    merged_heads = lambda c: c.reshape(c.shape[:3] + (-1,))
    caches = (merged_heads(cache_a_k), merged_heads(cache_a_v), merged_heads(cache_b_k),
              merged_heads(cache_b_v), cache_c_kv, cache_c_kr)
    states_p, states_s = [], []
    for l in range(depth):
        w_main, w_gates, w_uq_p, w_ukv, w_ukt, w_uvp = _layer_weights(l, w_in, w_uq, w_uk, w_uv)
        ffn1 = (row(ffn1_norm_pre, l), row(ffn1_norm_post, l), _layer_bf16(ffn1_w_gate, l),
                _layer_bf16(ffn1_w_up, l), _layer_bf16(ffn1_w_down, l))
        ffn2 = (row(ffn2_norm_pre, l), row(ffn2_norm_post, l), _layer_bf16(ffn2_w_gate, l),
                _layer_bf16(ffn2_w_up, l), _layer_bf16(ffn2_w_down, l))
        mix_in = (row(mix_norm_pre, l), w_main, row(cq_norm, l), row(ckv_norm, l), w_uq_p, w_ukv)
        mrg = (row(mix_norm_pre, l), row(mix_norm_post, l), w_gates, _layer_bf16(w_br_a, l),
               _layer_bf16(w_br_b, l), _layer_bf16(w_br_c, l), _layer_bf16(w_out, l))
        bias_p = _band_bias(rel_bias_a[l], base, tile, base - 2 * tile, 3 * tile)
        bias_c = _band_bias(rel_bias_a[l], past, ds, past - win_cache, win_cache)
        bias_n = _band_bias(rel_bias_a[l], past, ds, past, ds)

        xp = _ffn(xp, *ffn1, tm)
        (qa, kab, vab, qb, kbb, vbb, ka, va, kb, vb, ckv, krb, qc, kc, vc, kn) = _mixer_in(
            xp, *mix_in, tabs_p, tm, q_transposed=True)
        ya = _band_attention(qa, kab, vab, bias_p, tile)
        yb = _sb_attention(qb, kbb, vbb, tile)
        yc = _mla_attention(qc, kc, vc, kn, 2 * tm if seq % (2 * tm) == 0 else tm)
        xp = _merge(xp, ya, yb, yc, *mrg, tm, yc_transposed=True)
        xp = _ffn(xp, *ffn2, tm)
        win = min(WIN_A, seq)
        heads = lambda t, h: t.reshape(1, t.shape[0], h, t.shape[1] // h)
        states_p.append((heads(ka[-win:], H_A), heads(va[-win:], H_A), heads(kb, H_B), heads(vb, H_B),
                         ckv[None], krb[None, :, C_NOPE:C_NOPE + C_ROPE]))

        xs = _ffn(xs, *ffn1, ns)
        (qa, kab, vab, qb, kbb, vbb, ka, va, kb, vb, ckv, krb, qc, kc, vc, _) = _mixer_in(
            xs, *mix_in, tabs_s, ns)
        per_req = lambda t: t.reshape(nb, ds, t.shape[1])
        new = tuple(per_req(t) for t in (qa, kab, vab, qb, kbb, vbb, qc, ckv, krb))
        ya, yb, yc = _sample_attention(new, caches, l, bias_c, bias_n, w_ukt, w_uvp, tile)
        xs = _merge(xs, ya.reshape(ns, -1), yb.reshape(ns, -1), yc.reshape(ns, -1), *mrg, ns)
        xs = _ffn(xs, *ffn2, ns)
        heads_s = lambda t, h: t.reshape(nb, ds, h, t.shape[1] // h)
        states_s.append((heads_s(ka, H_A), heads_s(va, H_A), heads_s(kb, H_B), heads_s(vb, H_B),
                         ckv.reshape(nb, ds, -1), krb[:, C_NOPE:C_NOPE + C_ROPE].reshape(nb, ds, C_ROPE)))

    stack = lambda states, i: jnp.stack([s[i] for s in states], axis=0)
    return ((xp.reshape(batch, seq, d), xs.reshape(nb, ds, d))
            + tuple(stack(states_p, i) for i in range(6))
            + tuple(stack(states_s, i) for i in range(6)))
```

```python
import functools

import numpy as np
import jax
import jax.numpy as jnp
from jax import lax
from jax.experimental import pallas as pl
from jax.experimental.pallas import tpu as pltpu

F32 = jnp.float32
BF16 = jnp.bfloat16

CHUNK = 64
BAND_CHUNKS = 8
WIN_A = BAND_CHUNKS * CHUNK
HEAD_DIM = 64
H_A = 4
H_B = 4
H_C = 8
C_NOPE = 64
C_ROPE = 32
C_V = 64
REL_CLIP = 128
ROPE_THETA = 10000.0
EPS = 1e-6
N_BRANCH = 3
W_AB = H_A * HEAD_DIM
C_PAD = 128
ATTN_SCALE = HEAD_DIM ** -0.5
MLA_SCALE = (C_NOPE + C_ROPE) ** -0.5
LOG2E = float(np.log2(np.e))
C_ONE = C_NOPE + C_ROPE
NORM_MARGIN = 1.02
MLA_MIN_DENOM = 2.0 ** -60
MLA_UNROLL = 4
SAMPLE_LATENT_TILE = 1024
NEG = -1e30
SB_DEAD = -150.0

LANES = 128
MXU_WIDTH = 256
VMEM_LIMIT = 52 * 1024 * 1024

NT_DIMS = (((1,), (1,)), ((), ()))
TN_DIMS = (((0,), (0,)), ((), ()))


def _rms(x, g):
    return x * lax.rsqrt(jnp.mean(x * x, axis=-1, keepdims=True) + EPS) * g


def _dot(a, b):
    return jnp.dot(a, b, preferred_element_type=F32)


def _dot_nt(a, b):
    return lax.dot_general(a, b, NT_DIMS, preferred_element_type=F32)


def _dot_tn(a, b):
    return lax.dot_general(a, b, TN_DIMS, preferred_element_type=F32)


def _const_spec(shape):
    zeros = (0,) * len(shape)
    return pl.BlockSpec(shape, lambda *_: zeros, pipeline_mode=pl.Buffered(1))


def _row_spec(tm, width):
    return pl.BlockSpec((tm, width), lambda i: (i, 0))


def _params(n_grid):
    return pltpu.CompilerParams(dimension_semantics=("arbitrary",) * n_grid,
                                vmem_limit_bytes=VMEM_LIMIT)


def _ffn_kernel(x_ref, gpre_ref, gpost_ref, wg_ref, wu_ref, wd_ref, o_ref, *, f_cuts):
    x = x_ref[...]
    h = _rms(x, gpre_ref[...]).astype(BF16)
    y = jnp.zeros(x.shape, F32)
    for lo, hi in zip(f_cuts[:-1], f_cuts[1:]):
        sl = slice(lo, hi)
        g = _dot(h, wg_ref[:, sl])
        u = _dot(h, wu_ref[:, sl])
        a = (g * jax.nn.sigmoid(g) * u).astype(BF16)
        y = y + _dot(a, wd_ref[sl, :])
    o_ref[...] = x + 0.5 * _rms(y, gpost_ref[...])


def _ffn(x, g_pre, g_post, w_gate, w_up, w_down, tm):
    n, d = x.shape
    d_ff = w_gate.shape[1]
    cut = (d_ff // 2) // MXU_WIDTH * MXU_WIDTH
    f_cuts = (0, cut, d_ff) if 0 < cut < d_ff else (0, d_ff)
    return pl.pallas_call(
        functools.partial(_ffn_kernel, f_cuts=f_cuts),
        grid=(n // tm,),
        in_specs=[_row_spec(tm, d), _const_spec((1, d)), _const_spec((1, d)),
                  _const_spec((d, d_ff)), _const_spec((d, d_ff)), _const_spec((d_ff, d))],
        out_specs=_row_spec(tm, d),
        out_shape=jax.ShapeDtypeStruct((n, d), F32),
        compiler_params=_params(1),
        name="ffn",
    )(x, g_pre, g_post, w_gate, w_up, w_down)


def _rope(blk, tc, ts1, ts2):
    return blk * tc + pltpu.roll(blk, 16, 1) * ts1 + pltpu.roll(blk, C_PAD - 16, 1) * ts2


def _mixer_in_kernel(x_ref, gpre_ref, win_ref, gcq_ref, gckv_ref, wuq_ref, wukv_ref,
                     tc_ref, ts1_ref, ts2_ref,
                     qa_ref, kab_ref, vab_ref, qb_ref, kbb_ref, vbb_ref,
                     ka_ref, va_ref, kb_ref, vb_ref, ckv_ref, krb_ref,
                     qc_ref, kc_ref, vc_ref, kn_ref, *, q_transposed):
    w = W_AB
    q_rank = gcq_ref.shape[1]
    kv_rank = gckv_ref.shape[1]
    u = _rms(x_ref[...], gpre_ref[...]).astype(BF16)
    o = 6 * w
    lat = _dot(u, win_ref[:, o:])
    cq = lat[:, :q_rank]
    ckv = lat[:, q_rank:q_rank + kv_rank]
    krb = lat[:, q_rank + kv_rank:q_rank + kv_rank + C_PAD]
    proj = _dot(u, win_ref[:, :o])
    qa, ka, va = proj[:, 0:w], proj[:, w:2 * w], proj[:, 2 * w:3 * w]
    qb, kb, vb = proj[:, 3 * w:4 * w], proj[:, 4 * w:5 * w], proj[:, 5 * w:6 * w]
    qa_ref[...] = (qa * (ATTN_SCALE * LOG2E)).astype(BF16)
    qb_ref[...] = (qb * (ATTN_SCALE * LOG2E)).astype(BF16)
    ka_ref[...] = ka
    va_ref[...] = va
    kb_ref[...] = kb
    vb_ref[...] = vb
    kab_ref[...] = ka.astype(BF16)
    vab_ref[...] = va.astype(BF16)
    kbb_ref[...] = kb.astype(BF16)
    vbb_ref[...] = vb.astype(BF16)
    tc, ts1, ts2 = tc_ref[...], ts1_ref[...], ts2_ref[...]
    cqn = _rms(cq, gcq_ref[...]).astype(BF16)
    qall = _dot(cqn, wuq_ref[...])
    ckvn = _rms(ckv, gckv_ref[...])
    ckv_ref[...] = ckvn
    kv = _dot(ckvn.astype(BF16), wukv_ref[...])
    krot = _rope(krb, tc, ts1, ts2)
    krb_ref[...] = krot
    one_lane = (lax.broadcasted_iota(jnp.int32, (1, C_PAD), 1) == C_ONE).astype(F32)
    for h in range(H_C):
        sl = slice(h * C_PAD, (h + 1) * C_PAD)
        qh = _rope(qall[:, sl], tc, ts1, ts2) * (MLA_SCALE * LOG2E)
        if q_transposed:
            qt = qh.T
            norm = jnp.sqrt(jnp.sum(qt * qt, axis=0, keepdims=True)) * NORM_MARGIN
            row = lax.broadcasted_iota(jnp.int32, qt.shape, 0)
            qc_ref[sl, :] = jnp.where(row == C_ONE, norm, qt).astype(BF16)
        else:
            qc_ref[:, sl] = qh.astype(BF16)
        kh = kv[:, sl] + krot
        kc_ref[:, sl] = (kh + one_lane).astype(BF16)
        k_norm = jnp.sqrt(jnp.max(jnp.sum(kh * kh, axis=1, keepdims=True), axis=0, keepdims=True))
        kn_ref[0, :, sl] = jnp.broadcast_to(k_norm * NORM_MARGIN, (1, C_PAD))
    if not q_transposed:
        vc_ref[...] = kv[:, H_C * C_PAD:].astype(BF16)
    else:
        for p in range(H_C // 2):
            o = H_C * C_PAD + p * 2 * C_V
            vc_ref[0, p * 2 * C_V:(p + 1) * 2 * C_V, :] = kv[:, o:o + 2 * C_V].T.astype(BF16)


def _mixer_in(x, g_pre, w_main, g_cq, g_ckv, w_uq, w_ukv, tabs, tm, q_transposed=False):
    n, d = x.shape
    q_rank, kv_rank = g_cq.shape[1], g_ckv.shape[1]
    wc = H_C * C_PAD
    wv = H_C * C_V
    bf = lambda width: jax.ShapeDtypeStruct((n, width), BF16)
    f32 = lambda width: jax.ShapeDtypeStruct((n, width), F32)
    q_shape = jax.ShapeDtypeStruct((wc, n), BF16) if q_transposed else bf(wc)
    q_spec = pl.BlockSpec((wc, tm), lambda i: (0, i)) if q_transposed else _row_spec(tm, wc)
    v_shape = jax.ShapeDtypeStruct((n // tm, wv, tm), BF16) if q_transposed else bf(wv)
    v_spec = (pl.BlockSpec((1, wv, tm), lambda i: (i, 0, 0)) if q_transposed
              else _row_spec(tm, wv))
    out_shape = ([bf(W_AB)] * 6 + [f32(W_AB)] * 4 + [f32(kv_rank), f32(C_PAD)]
                 + [q_shape, bf(wc), v_shape, jax.ShapeDtypeStruct((n // tm, 1, wc), F32)])
    out_specs = ([_row_spec(tm, W_AB)] * 10 + [_row_spec(tm, kv_rank), _row_spec(tm, C_PAD)]
                 + [q_spec, _row_spec(tm, wc), v_spec, pl.BlockSpec((1, 1, wc), lambda i: (i, 0, 0))])
    return pl.pallas_call(
        functools.partial(_mixer_in_kernel, q_transposed=q_transposed),
        grid=(n // tm,),
        in_specs=[_row_spec(tm, d), _const_spec((1, d)), _const_spec(w_main.shape),
                  _const_spec((1, q_rank)), _const_spec((1, kv_rank)),
                  _const_spec(w_uq.shape), _const_spec(w_ukv.shape),
                  _row_spec(tm, C_PAD), _row_spec(tm, C_PAD), _row_spec(tm, C_PAD)],
        out_specs=out_specs,
        out_shape=out_shape,
        compiler_params=_params(1),
        name="mixer_in",
    )(x, g_pre, w_main, g_cq, g_ckv, w_uq, w_ukv, *tabs)


def _merge_kernel(x_ref, ya_ref, yb_ref, yc_ref, gpre_ref, gpost_ref, wgates_ref,
                  wa_ref, wb_ref, wc_ref, wout_ref, o_ref, *, yc_transposed):
    x = x_ref[...]
    d = x.shape[1]
    u = _rms(x, gpre_ref[...]).astype(BF16)
    m = jnp.zeros(x.shape, F32)
    for i, (y_ref, w_ref) in enumerate(((ya_ref, wa_ref), (yb_ref, wb_ref), (yc_ref, wc_ref))):
        gate = jax.nn.sigmoid(_dot(u, wgates_ref[:, i * d:(i + 1) * d]))
        branch_dot = _dot_tn if (yc_transposed and y_ref is yc_ref) else _dot
        m = m + gate * branch_dot(y_ref[...], w_ref[...])
    mixed = _dot(m.astype(BF16), wout_ref[...])
    o_ref[...] = x + _rms(mixed, gpost_ref[...])


def _merge(x, ya, yb, yc, g_pre, g_post, w_gates, w_a, w_b, w_c, w_out, tm, yc_transposed=False):
    n, d = x.shape
    yc_spec = (pl.BlockSpec((yc.shape[0], tm), lambda i: (0, i)) if yc_transposed
               else _row_spec(tm, yc.shape[1]))
    return pl.pallas_call(
        functools.partial(_merge_kernel, yc_transposed=yc_transposed),
        grid=(n // tm,),
        in_specs=[_row_spec(tm, d), _row_spec(tm, ya.shape[1]), _row_spec(tm, yb.shape[1]),
                  yc_spec, _const_spec((1, d)), _const_spec((1, d)),
                  _const_spec(w_gates.shape), _const_spec(w_a.shape), _const_spec(w_b.shape),
                  _const_spec(w_c.shape), _const_spec(w_out.shape)],
        out_specs=_row_spec(tm, d),
        out_shape=jax.ShapeDtypeStruct((n, d), F32),
        compiler_params=_params(1),
        name="merge",
    )(x, ya, yb, yc, g_pre, g_post, w_gates, w_a, w_b, w_c, w_out)


def _head_masked(q, width):
    lane_head = lax.broadcasted_iota(jnp.int32, q.shape, 1) // width
    zero = jnp.zeros_like(q)
    return [jnp.where(lane_head == h, q, zero) for h in range(q.shape[1] // width)]


def _strict_upper_ones(n):
    j = lax.broadcasted_iota(jnp.int32, (n, n), 0)
    s = lax.broadcasted_iota(jnp.int32, (n, n), 1)
    return (j > s).astype(BF16)


def _sb_terms(qm, kt, ones_ut, causal):
    z = _dot_nt(qm, kt)
    t = jnp.log2(1.0 + jnp.exp2(-jnp.abs(z)))
    log_keep = -jnp.maximum(z, 0.0) - t
    log_beta = jnp.minimum(z, 0.0) - t
    if causal is not None:
        log_keep = jnp.where(causal, log_keep, 0.0)
    hi = log_keep.astype(BF16)
    lo = (log_keep - hi.astype(F32)).astype(BF16)
    tail = _dot(hi, ones_ut) + _dot(lo, ones_ut)
    return log_beta + tail, jnp.sum(log_keep, axis=1, keepdims=True)


def _sb_apply(terms, vt, carry, acc, causal):
    logw, total = terms
    w = jnp.exp2(logw + carry)
    if causal is not None:
        w = jnp.where(causal, w, 0.0)
    return carry + total, acc + _dot(w.astype(BF16), vt)


def _sb_tile(qm, kt, vt, carry, acc, ones_ut, causal):
    return _sb_apply(_sb_terms(qm, kt, ones_ut, causal), vt, carry, acc, causal)


def _sb_alive(carry):
    return jnp.max(carry) > SB_DEAD


def _select_heads(parts, width):
    lane_head = lax.broadcasted_iota(jnp.int32, parts[0].shape, 1) // width
    out = parts[0]
    for h in range(1, len(parts)):
        out = jnp.where(lane_head == h, parts[h], out)
    return out


def _band_kernel(q_ref, k0_ref, k1_ref, k2_ref, v0_ref, v1_ref, v2_ref, bias_ref, o_ref):
    i = pl.program_id(0)
    tq = q_ref.shape[0]
    kwin = jnp.concatenate([k0_ref[...], k1_ref[...], k2_ref[...]], axis=0)
    vwin = jnp.concatenate([v0_ref[...], v1_ref[...], v2_ref[...]], axis=0)

    def attend(clip_window):
        parts = []
        for h, qm in enumerate(_head_masked(q_ref[...], HEAD_DIM)):
            s = _dot_nt(qm, kwin) + bias_ref[h]
            if clip_window:
                col = lax.broadcasted_iota(jnp.int32, s.shape, 1)
                s = jnp.where(col >= (2 - i) * tq, s, NEG)
            m = jnp.max(s, axis=1, keepdims=True)
            p = jnp.exp2(s - m)
            l = jnp.sum(p, axis=1, keepdims=True)
            parts.append(_dot(p.astype(BF16), vwin) / l)
        o_ref[...] = _select_heads(parts, HEAD_DIM).astype(BF16)

    @pl.when(i < 2)
    def _():
        attend(True)

    @pl.when(i >= 2)
    def _():
        attend(False)


def _band_attention(q, k, v, bias, tq):
    n, w = q.shape
    blk = lambda back: pl.BlockSpec((tq, w), lambda i: (jnp.maximum(i - back, 0), 0))
    return pl.pallas_call(
        _band_kernel,
        grid=(n // tq,),
        in_specs=[blk(0), blk(2), blk(1), blk(0), blk(2), blk(1), blk(0), _const_spec(bias.shape)],
        out_specs=blk(0),
        out_shape=jax.ShapeDtypeStruct((n, w), BF16),
        compiler_params=_params(1),
        name="band_attn",
    )(q, k, k, k, v, v, v, bias)


def _sb_kernel(q_ref, k_ref, v_ref, o_ref, *, tile):
    i = pl.program_id(0)
    row = lax.broadcasted_iota(jnp.int32, (tile, tile), 0)
    col = lax.broadcasted_iota(jnp.int32, (tile, tile), 1)
    causal = col < row
    ones_ut = _strict_upper_ones(tile)
    width = q_ref.shape[1]
    qms = _head_masked(q_ref[...], HEAD_DIM)
    rows = lambda ref, kb: ref[pl.ds(pl.multiple_of(kb * tile, tile), tile), :]

    prev = jnp.maximum(i - 1, 0)
    k_diag, k_prev = rows(k_ref, i), rows(k_ref, prev)
    terms_diag = [_sb_terms(qm, k_diag, ones_ut, causal) for qm in qms]
    terms_prev = [_sb_terms(qm, k_prev, ones_ut, None) for qm in qms]
    v_diag, v_prev = rows(v_ref, i), rows(v_ref, prev)
    parts = []
    for qm, t_diag, t_prev in zip(qms, terms_diag, terms_prev):
        carry, acc = _sb_apply(t_diag, v_diag, jnp.zeros((tile, 1), F32), jnp.zeros((tile, width), F32),
                               causal)
        carry, acc = _sb_apply(t_prev, v_prev, jnp.where(i > 0, carry, NEG), acc, None)
        state = lax.while_loop(
            lambda st: (st[0] < i - 1) & _sb_alive(st[1]),
            lambda st, qm=qm: (st[0] + 1,) + _sb_tile(qm, rows(k_ref, i - 2 - st[0]), rows(v_ref, i - 2 - st[0]),
                                                      st[1], st[2], ones_ut, None),
            (jnp.int32(0), carry, acc))
        parts.append(state[2])
    o_ref[...] = _select_heads(parts, HEAD_DIM).astype(BF16)


def _sb_attention(q, k, v, tile):
    n, w = q.shape
    return pl.pallas_call(
        functools.partial(_sb_kernel, tile=tile),
        grid=(n // tile,),
        in_specs=[_row_spec(tile, w), _const_spec((n, w)), _const_spec((n, w))],
        out_specs=_row_spec(tile, w),
        out_shape=jax.ShapeDtypeStruct((n, w), BF16),
        compiler_params=_params(1),
        name="sb_attn",
    )(q, k, v)


def _mla_kernel(qt_ref, k_ref, vt_ref, kn_ref, ot_ref, qx_scr, p_scr, acc_scr, l_scr):
    i = pl.program_id(1)
    heads = range(2)
    tk, tq = p_scr.shape[2:]
    per_q = tq // tk
    plain = per_q * i
    key = lax.broadcasted_iota(jnp.int32, (tk, tq), 0)
    qry = lax.broadcasted_iota(jnp.int32, (tk, tq), 1)
    row = lax.broadcasted_iota(jnp.int32, (C_PAD - C_ONE, tq), 0)

    def visible(d):
        return ((d * tk + key) // CHUNK) <= (qry // CHUNK)

    def load_q(shifted):
        k_bound = jnp.max(kn_ref[...], axis=0)
        for e in heads:
            lo = e * C_PAD
            qx_scr[lo:lo + C_ONE, :] = qt_ref[lo:lo + C_ONE, :]
            if shifted:
                bound = qt_ref[lo + C_ONE:lo + C_ONE + 1, :].astype(F32) * (k_bound[:, lo:lo + 1] * NORM_MARGIN)
                tail = jnp.where(row == 0, -bound, 0.0)
            else:
                tail = jnp.zeros(row.shape, F32)
            qx_scr[lo + C_ONE:lo + C_PAD, :] = tail.astype(BF16)

    def scores(kb, e):
        return _dot(k_ref[kb, :, e * C_PAD:(e + 1) * C_PAD], qx_scr[e * C_PAD:(e + 1) * C_PAD, :])

    def values(kb, e, p):
        return _dot(vt_ref[kb, e * C_V:(e + 1) * C_V, :], p)

    def write_out():
        ot_ref[...] = jnp.concatenate([acc_scr[e] / l_scr[e] for e in heads], axis=0).astype(BF16)

    def reset():
        acc_scr[...] = jnp.zeros(acc_scr.shape, F32)
        l_scr[...] = jnp.zeros(l_scr.shape, F32)

    def fast_stage(kb, slot, mask):
        for e in heads:
            s = scores(kb, e)
            if mask is not None:
                s = jnp.where(mask, s, NEG)
            p = jnp.exp2(s)
            l_scr[e] += jnp.sum(p, axis=0, keepdims=True)
            p_scr[slot, e] = p.astype(BF16)
        for e in heads:
            acc_scr[e] += values(kb, e, p_scr[slot, e])

    load_q(True)
    reset()

    def fast_group(g, _):
        for j in range(MLA_UNROLL):
            fast_stage(MLA_UNROLL * g + j, j, None)
        return 0
    groups = plain // MLA_UNROLL
    lax.fori_loop(0, groups, fast_group, 0)

    def fast_single(kb, _):
        fast_stage(kb, 0, None)
        return 0
    lax.fori_loop(MLA_UNROLL * groups, plain, fast_single, 0)
    for d in range(per_q):
        fast_stage(plain + d, (1 + d) % MLA_UNROLL, visible(d))
    denom = jnp.minimum(jnp.min(l_scr[0]), jnp.min(l_scr[1]))
    trusted = denom >= MLA_MIN_DENOM

    @pl.when(trusted)
    def _():
        write_out()

    @pl.when(jnp.logical_not(trusted))
    def _():
        load_q(False)
        reset()

        def step(kb, m):
            out = []
            for e in heads:
                s = jnp.where(visible(kb - plain), scores(kb, e), NEG)
                m_new = jnp.maximum(m[e], jnp.max(s, axis=0, keepdims=True))
                alpha = jnp.exp2(m[e] - m_new)
                p = jnp.exp2(s - m_new)
                l_scr[e] = alpha * l_scr[e] + jnp.sum(p, axis=0, keepdims=True)
                acc_scr[e] = alpha * acc_scr[e] + values(kb, e, p.astype(BF16))
                out.append(m_new)
            return tuple(out)
        lax.fori_loop(0, plain + per_q, step, tuple(jnp.full((1, tq), NEG, F32) for _ in heads))
        write_out()


def _mla_attention(qt, k, vt3, kn, tq):
    n = k.shape[0]
    pairs = H_C // 2
    n_tiles, _, tk = vt3.shape
    assert n_tiles * tk == n and kn.shape == (n_tiles, 1, H_C * C_PAD)
    assert tq % tk == 0 and n % tq == 0
    k3 = k.reshape(n_tiles, tk, H_C * C_PAD)
    return pl.pallas_call(
        _mla_kernel,
        grid=(pairs, n // tq),
        in_specs=[pl.BlockSpec((2 * C_PAD, tq), lambda p, i: (p, i)),
                  pl.BlockSpec((n_tiles, tk, 2 * C_PAD), lambda p, i: (0, 0, p),
                               pipeline_mode=pl.Buffered(1)),
                  pl.BlockSpec((n_tiles, 2 * C_V, tk), lambda p, i: (0, p, 0),
                               pipeline_mode=pl.Buffered(1)),
                  pl.BlockSpec((n_tiles, 1, 2 * C_PAD), lambda p, i: (0, 0, p))],
        out_specs=pl.BlockSpec((2 * C_V, tq), lambda p, i: (p, i)),
        out_shape=jax.ShapeDtypeStruct((H_C * C_V, n), BF16),
        scratch_shapes=[pltpu.VMEM((2 * C_PAD, tq), BF16), pltpu.VMEM((MLA_UNROLL, 2, tk, tq), BF16),
                        pltpu.VMEM((2, C_V, tq), F32), pltpu.VMEM((2, 1, tq), F32)],
        compiler_params=_params(2),
        name="mla_attn",
    )(qt, k3, vt3, kn)


def _sample_attn_kernel(qa_ref, ka_ref, va_ref, cak_ref, cav_ref, biasc_ref, biasn_ref,
                        qb_ref, kb_ref, vb_ref, cbk_ref, cbv_ref,
                        qc_ref, ckvn_ref, krbn_ref, ckv_ref, ckr_ref, wukt_ref, wuvp_ref,
                        ya_ref, yb_ref, yc_ref, s_scr, *, tile):
    ds = qa_ref.shape[1]
    past = cbk_ref.shape[2]

    ka, va, cak, cav = ka_ref[0], va_ref[0], cak_ref[0, 0].astype(BF16), cav_ref[0, 0].astype(BF16)
    parts = []
    for h, qm in enumerate(_head_masked(qa_ref[0], HEAD_DIM)):
        s_c = _dot_nt(qm, cak) + biasc_ref[h]
        s_n = _dot_nt(qm, ka) + biasn_ref[h]
        m = jnp.maximum(jnp.max(s_c, axis=1, keepdims=True), jnp.max(s_n, axis=1, keepdims=True))
        p_c = jnp.exp2(s_c - m)
        p_n = jnp.exp2(s_n - m)
        l = jnp.sum(p_c, axis=1, keepdims=True) + jnp.sum(p_n, axis=1, keepdims=True)
        parts.append((_dot(p_c.astype(BF16), cav) + _dot(p_n.astype(BF16), va)) / l)
    ya_ref[0] = _select_heads(parts, HEAD_DIM).astype(BF16)

    row = lax.broadcasted_iota(jnp.int32, (ds, ds), 0)
    col = lax.broadcasted_iota(jnp.int32, (ds, ds), 1)
    causal = col < row
    ones_new = _strict_upper_ones(ds)
    ones_ut = _strict_upper_ones(tile)
    kb, vb = kb_ref[0], vb_ref[0]
    n_tiles = past // tile
    parts = []
    for qm in _head_masked(qb_ref[0], HEAD_DIM):
        state = _sb_tile(qm, kb, vb, jnp.zeros((ds, 1), F32), jnp.zeros((ds, W_AB), F32),
                         ones_new, causal)

        def cache_tile(st, qm=qm):
            start = pl.multiple_of((n_tiles - 1 - st[0]) * tile, tile)
            return (st[0] + 1,) + _sb_tile(qm, cbk_ref[0, 0, pl.ds(start, tile), :].astype(BF16),
                                           cbv_ref[0, 0, pl.ds(start, tile), :].astype(BF16),
                                           st[1], st[2], ones_ut, None)
        state = lax.while_loop(lambda st: (st[0] < n_tiles) & _sb_alive(st[1]), cache_tile,
                               (jnp.int32(0),) + state)
        parts.append(state[2])
    yb_ref[0] = _select_heads(parts, HEAD_DIM).astype(BF16)

    kv_rank = ckv_ref.shape[3]
    qc = qc_ref[0]
    rope_lanes = lax.broadcasted_iota(jnp.int32, (ds, C_PAD), 1) >= C_NOPE
    q_rows = []
    for h in range(H_C):
        qh = qc[:, h * C_PAD:(h + 1) * C_PAD]
        q_lat = _dot(qh, wukt_ref[h]).astype(BF16)
        q_rows.append(jnp.concatenate([q_lat, jnp.where(rope_lanes, qh, jnp.zeros_like(qh))], axis=1))
    q_all = jnp.concatenate(q_rows, axis=0)

    place = (lax.broadcasted_iota(jnp.int32, (C_ROPE, C_PAD), 1)
             == lax.broadcasted_iota(jnp.int32, (C_ROPE, C_PAD), 0) + C_NOPE).astype(BF16)
    tile_c = SAMPLE_LATENT_TILE if past % SAMPLE_LATENT_TILE == 0 else tile
    cache_tiles = range(past // tile_c)
    latent = lambda j: ckv_ref[0, 0, j * tile_c:(j + 1) * tile_c, :].astype(BF16)

    lat_new = ckvn_ref[0].astype(BF16)
    s_new = _dot_nt(q_all, jnp.concatenate([lat_new, krbn_ref[0].astype(BF16)], axis=1))
    m = jnp.max(s_new, axis=1, keepdims=True)
    for j in cache_tiles:
        k_rope = _dot(ckr_ref[0, 0, j * tile_c:(j + 1) * tile_c, :].astype(BF16), place).astype(BF16)
        s = _dot_nt(q_all, jnp.concatenate([latent(j), k_rope], axis=1))
        s_scr[:, j * tile_c:(j + 1) * tile_c] = s
        m = jnp.maximum(m, jnp.max(s, axis=1, keepdims=True))
    p = jnp.exp2(s_new - m)
    denom = jnp.sum(p, axis=1, keepdims=True)
    o_lat = _dot(p.astype(BF16), lat_new)
    for j in cache_tiles:
        p = jnp.exp2(s_scr[:, j * tile_c:(j + 1) * tile_c] - m)
        denom = denom + jnp.sum(p, axis=1, keepdims=True)
        o_lat = o_lat + _dot(p.astype(BF16), latent(j))
    o_lat = (o_lat / denom).astype(BF16)
    out = jnp.zeros((ds, H_C * C_V), F32)
    for h in range(H_C):
        out = out + _dot(o_lat[h * ds:(h + 1) * ds], wuvp_ref[h])
    yc_ref[0] = out.astype(BF16)


def _sample_attention(new, caches, l, bias_c, bias_n, w_ukt, w_uvp, tile):
    qa, ka, va, qb, kb, vb, qc, ckvn, krbn = new
    cak, cav, cbk, cbv, ckv, ckr = caches
    nb, ds, _ = qa.shape
    past = ckv.shape[2]
    req = lambda a: pl.BlockSpec((1,) + a.shape[1:], lambda b: (b, 0, 0))
    cache = lambda a: pl.BlockSpec((1, 1) + a.shape[2:], lambda b: (l, b, 0, 0))
    ins = [qa, ka, va, cak, cav, bias_c, bias_n, qb, kb, vb, cbk, cbv, qc, ckvn, krbn, ckv, ckr,
           w_ukt, w_uvp]
    specs = [req(a) for a in ins]
    for idx in (3, 4, 10, 11, 15, 16):
        specs[idx] = cache(ins[idx])
    for idx in (5, 6, 17, 18):
        specs[idx] = _const_spec(ins[idx].shape)
    out_shape = [jax.ShapeDtypeStruct((nb, ds, W_AB), BF16), jax.ShapeDtypeStruct((nb, ds, W_AB), BF16),
                 jax.ShapeDtypeStruct((nb, ds, H_C * C_V), BF16)]
    return pl.pallas_call(
        functools.partial(_sample_attn_kernel, tile=tile),
        grid=(nb,),
        in_specs=specs,
        out_specs=[req(s) for s in out_shape],
        out_shape=out_shape,
        scratch_shapes=[pltpu.VMEM((H_C * ds, past), F32)],
        compiler_params=_params(1),
        name="sample_attn",
    )(*ins)


def _rope_tables(pos):
    half = C_ROPE // 2
    freq = ROPE_THETA ** (-jnp.arange(half, dtype=F32) / half)
    ang = pos.astype(F32)[:, None] * freq[None, :]
    cos, sin = jnp.cos(ang), jnp.sin(ang)
    n = pos.shape[0]
    z = lambda w: jnp.zeros((n, w), F32)
    tc = jnp.concatenate([jnp.ones((n, C_NOPE), F32), cos, cos, z(C_PAD - C_NOPE - C_ROPE)], axis=1)
    ts1 = jnp.concatenate([z(C_NOPE + half), sin, z(C_PAD - C_NOPE - C_ROPE)], axis=1)
    ts2 = jnp.concatenate([z(C_NOPE), -sin, z(C_PAD - C_NOPE - half)], axis=1)
    return tc, ts1, ts2


def _band_bias(rel_bias, q0, nq, k0, nk):
    period = 1 << int(np.ceil(np.log2(nq + nk)))
    d = np.arange(period)
    d = np.where(d >= nk, d - period, d)
    idx = np.clip((q0 - k0) - d, -REL_CLIP, REL_CLIP) + REL_CLIP
    g = rel_bias.astype(F32)[:, idx]
    toe = jnp.tile(g, (1, nq))[:, :nq * (period - 1)].reshape(-1, nq, period - 1)[:, :, :nk]
    q_pos, k_pos = q0 + np.arange(nq), k0 + np.arange(nk)
    qc, kc = q_pos[:, None] // CHUNK, k_pos[None, :] // CHUNK
    vis = (kc <= qc) & (kc >= qc - BAND_CHUNKS)
    return jnp.where(jnp.asarray(vis)[None], toe * LOG2E, NEG)


def _cast_kernel(w_ref, o_ref):
    o_ref[...] = w_ref[0].astype(BF16)


def _layer_bf16(w, l):
    _, rows, cols = w.shape
    rb = rows // 4 if rows % 64 == 0 else rows
    return pl.pallas_call(
        _cast_kernel,
        grid=(rows // rb,),
        in_specs=[pl.BlockSpec((1, rb, cols), lambda i: (l, i, 0))],
        out_specs=pl.BlockSpec((rb, cols), lambda i: (i, 0)),
        out_shape=jax.ShapeDtypeStruct((rows, cols), BF16),
        compiler_params=_params(1),
        name="cast_bf16",
    )(w)


def _layer_weights(l, w_in, w_uq, w_uk, w_uv):
    d = w_in.shape[1]
    q_rank, kv_rank = w_uq.shape[1], w_uk.shape[1]
    o = 6 * W_AB + q_rank + kv_rank
    zc = lambda w: jnp.zeros((d, w), w_in.dtype)
    w_main = jnp.concatenate([w_in[l, :, :o], zc(C_NOPE), w_in[l, :, o:o + C_ROPE],
                              zc(C_PAD - C_NOPE - C_ROPE)], axis=1).astype(BF16)
    w_gates = w_in[l, :, o + C_ROPE:].astype(BF16)
    pad_heads = lambda w: jnp.pad(w, ((0, 0), (0, 0), (0, C_PAD - w.shape[2]))).reshape(w.shape[0], -1)
    w_uq_p = pad_heads(w_uq[l]).astype(BF16)
    w_ukv = jnp.concatenate([pad_heads(w_uk[l]), w_uv[l].reshape(kv_rank, -1)], axis=1).astype(BF16)
    w_ukt = jnp.pad(jnp.transpose(w_uk[l], (1, 2, 0)), ((0, 0), (0, C_PAD - C_NOPE), (0, 0))).astype(BF16)
    w_uvp = jnp.einsum('rhd,hg->hrgd', w_uv[l], jnp.eye(H_C, dtype=w_uv.dtype)).reshape(
        H_C, kv_rank, H_C * C_V).astype(BF16)
    return w_main, w_gates, w_uq_p, w_ukv, w_ukt, w_uvp


def kernel(x_prompt, x_sample, cache_a_k, cache_a_v, cache_b_k, cache_b_v, cache_c_kv, cache_c_kr, ffn1_norm_pre, ffn1_norm_post, ffn1_w_gate, ffn1_w_up, ffn1_w_down, mix_norm_pre, mix_norm_post, w_in, cq_norm, ckv_norm, w_uq, w_uk, w_uv, rel_bias_a, w_br_a, w_br_b, w_br_c, w_out, ffn2_norm_pre, ffn2_norm_post, ffn2_w_gate, ffn2_w_up, ffn2_w_down):
    batch, seq, d = x_prompt.shape
    nb, ds, _ = x_sample.shape
    depth = w_in.shape[0]
    past = cache_b_k.shape[2]
    win_cache = cache_a_k.shape[2]
    assert batch == 1, "prompt group is a single sequence"
    tile = 256
    tm = 512 if seq % 512 == 0 else tile
    assert seq % tile == 0 and past % tile == 0 and tile % CHUNK == 0 and 2 * tile >= WIN_A
    assert past % CHUNK + ds <= CHUNK
    ns = nb * ds

    xp = x_prompt.reshape(seq, d)
    xs = x_sample.reshape(ns, d)
    tabs_p = _rope_tables(jnp.arange(seq))
    tabs_s = _rope_tables(jnp.tile(past + jnp.arange(ds), nb))

    row = lambda g, l: g[l][None, :]
    base = 2 * tile

    merged_heads = lambda c: c.reshape(c.shape[:3] + (-1,))
    caches = (merged_heads(cache_a_k), merged_heads(cache_a_v), merged_heads(cache_b_k),
              merged_heads(cache_b_v), cache_c_kv, cache_c_kr)
    states_p, states_s = [], []
    for l in range(depth):
        w_main, w_gates, w_uq_p, w_ukv, w_ukt, w_uvp = _layer_weights(l, w_in, w_uq, w_uk, w_uv)
        ffn1 = (row(ffn1_norm_pre, l), row(ffn1_norm_post, l), _layer_bf16(ffn1_w_gate, l),
                _layer_bf16(ffn1_w_up, l), _layer_bf16(ffn1_w_down, l))
        ffn2 = (row(ffn2_norm_pre, l), row(ffn2_norm_post, l), _layer_bf16(ffn2_w_gate, l),
                _layer_bf16(ffn2_w_up, l), _layer_bf16(ffn2_w_down, l))
        mix_in = (row(mix_norm_pre, l), w_main, row(cq_norm, l), row(ckv_norm, l), w_uq_p, w_ukv)
        mrg = (row(mix_norm_pre, l), row(mix_norm_post, l), w_gates, _layer_bf16(w_br_a, l),
               _layer_bf16(w_br_b, l), _layer_bf16(w_br_c, l), _layer_bf16(w_out, l))
        bias_p = _band_bias(rel_bias_a[l], base, tile, base - 2 * tile, 3 * tile)
        bias_c = _band_bias(rel_bias_a[l], past, ds, past - win_cache, win_cache)
        bias_n = _band_bias(rel_bias_a[l], past, ds, past, ds)

        xp = _ffn(xp, *ffn1, tm)
        (qa, kab, vab, qb, kbb, vbb, ka, va, kb, vb, ckv, krb, qc, kc, vc, kn) = _mixer_in(
            xp, *mix_in, tabs_p, tm, q_transposed=True)
        ya = _band_attention(qa, kab, vab, bias_p, tile)
        yb = _sb_attention(qb, kbb, vbb, tile)
        yc = _mla_attention(qc, kc, vc, kn, 2 * tm if seq % (2 * tm) == 0 else tm)
        xp = _merge(xp, ya, yb, yc, *mrg, tm, yc_transposed=True)
        xp = _ffn(xp, *ffn2, tm)
        win = min(WIN_A, seq)
        heads = lambda t, h: t.reshape(1, t.shape[0], h, t.shape[1] // h)
        states_p.append((heads(ka[-win:], H_A), heads(va[-win:], H_A), heads(kb, H_B), heads(vb, H_B),
                         ckv[None], krb[None, :, C_NOPE:C_NOPE + C_ROPE]))

        xs = _ffn(xs, *ffn1, ns)
        (qa, kab, vab, qb, kbb, vbb, ka, va, kb, vb, ckv, krb, qc, kc, vc, _) = _mixer_in(
            xs, *mix_in, tabs_s, ns)
        per_req = lambda t: t.reshape(nb, ds, t.shape[1])
        new = tuple(per_req(t) for t in (qa, kab, vab, qb, kbb, vbb, qc, ckv, krb))
        ya, yb, yc = _sample_attention(new, caches, l, bias_c, bias_n, w_ukt, w_uvp, tile)
        xs = _merge(xs, ya.reshape(ns, -1), yb.reshape(ns, -1), yc.reshape(ns, -1), *mrg, ns)
        xs = _ffn(xs, *ffn2, ns)
        heads_s = lambda t, h: t.reshape(nb, ds, h, t.shape[1] // h)
        states_s.append((heads_s(ka, H_A), heads_s(va, H_A), heads_s(kb, H_B), heads_s(vb, H_B),
                         ckv.reshape(nb, ds, -1), krb[:, C_NOPE:C_NOPE + C_ROPE].reshape(nb, ds, C_ROPE)))

    stack = lambda states, i: jnp.stack([s[i] for s in states], axis=0)
    return ((xp.reshape(batch, seq, d), xs.reshape(nb, ds, d))
            + tuple(stack(states_p, i) for i in range(6))
            + tuple(stack(states_s, i) for i in range(6)))
```

```python
import functools

import numpy as np
import jax
import jax.numpy as jnp
from jax import lax
from jax.experimental import pallas as pl
from jax.experimental.pallas import tpu as pltpu

F32 = jnp.float32
BF16 = jnp.bfloat16

CHUNK = 64
BAND_CHUNKS = 8
WIN_A = BAND_CHUNKS * CHUNK
HEAD_DIM = 64
H_A = 4
H_B = 4
H_C = 8
C_NOPE = 64
C_ROPE = 32
C_V = 64
REL_CLIP = 128
ROPE_THETA = 10000.0
EPS = 1e-6
N_BRANCH = 3
W_AB = H_A * HEAD_DIM
C_PAD = 128
ATTN_SCALE = HEAD_DIM ** -0.5
MLA_SCALE = (C_NOPE + C_ROPE) ** -0.5
LOG2E = float(np.log2(np.e))
C_ONE = C_NOPE + C_ROPE
NORM_MARGIN = 1.02
MLA_MIN_DENOM = 2.0 ** -60
MLA_UNROLL = 4
SAMPLE_LATENT_TILE = 1024
NEG = -1e30
SB_DEAD = -150.0

LANES = 128
MXU_WIDTH = 256
VMEM_LIMIT = 52 * 1024 * 1024

NT_DIMS = (((1,), (1,)), ((), ()))
TN_DIMS = (((0,), (0,)), ((), ()))


def _rms(x, g):
    return x * lax.rsqrt(jnp.mean(x * x, axis=-1, keepdims=True) + EPS) * g


def _dot(a, b):
    return jnp.dot(a, b, preferred_element_type=F32)


def _dot_nt(a, b):
    return lax.dot_general(a, b, NT_DIMS, preferred_element_type=F32)


def _dot_tn(a, b):
    return lax.dot_general(a, b, TN_DIMS, preferred_element_type=F32)


def _const_spec(shape):
    zeros = (0,) * len(shape)
    return pl.BlockSpec(shape, lambda *_: zeros, pipeline_mode=pl.Buffered(1))


def _row_spec(tm, width):
    return pl.BlockSpec((tm, width), lambda i: (i, 0))


def _params(n_grid):
    return pltpu.CompilerParams(dimension_semantics=("arbitrary",) * n_grid,
                                vmem_limit_bytes=VMEM_LIMIT)


def _ffn_kernel(x_ref, gpre_ref, gpost_ref, wg_ref, wu_ref, wd_ref, o_ref, *, f_cuts):
    x = x_ref[...]
    h = _rms(x, gpre_ref[...]).astype(BF16)
    y = jnp.zeros(x.shape, F32)
    for lo, hi in zip(f_cuts[:-1], f_cuts[1:]):
        sl = slice(lo, hi)
        g = _dot(h, wg_ref[:, sl])
        u = _dot(h, wu_ref[:, sl])
        a = (g * jax.nn.sigmoid(g) * u).astype(BF16)
        y = y + _dot(a, wd_ref[sl, :])
    o_ref[...] = x + 0.5 * _rms(y, gpost_ref[...])


def _ffn(x, g_pre, g_post, w_gate, w_up, w_down, tm):
    n, d = x.shape
    d_ff = w_gate.shape[1]
    cut = (d_ff // 2) // MXU_WIDTH * MXU_WIDTH
    f_cuts = (0, cut, d_ff) if 0 < cut < d_ff else (0, d_ff)
    return pl.pallas_call(
        functools.partial(_ffn_kernel, f_cuts=f_cuts),
        grid=(n // tm,),
        in_specs=[_row_spec(tm, d), _const_spec((1, d)), _const_spec((1, d)),
                  _const_spec((d, d_ff)), _const_spec((d, d_ff)), _const_spec((d_ff, d))],
        out_specs=_row_spec(tm, d),
        out_shape=jax.ShapeDtypeStruct((n, d), F32),
        compiler_params=_params(1),
        name="ffn",
    )(x, g_pre, g_post, w_gate, w_up, w_down)


def _rope(blk, tc, ts1, ts2):
    return blk * tc + pltpu.roll(blk, 16, 1) * ts1 + pltpu.roll(blk, C_PAD - 16, 1) * ts2


def _mixer_in_kernel(x_ref, gpre_ref, win_ref, gcq_ref, gckv_ref, wuq_ref, wukv_ref,
                     tc_ref, ts1_ref, ts2_ref,
                     qa_ref, kab_ref, vab_ref, qb_ref, kbb_ref, vbb_ref,
                     ka_ref, va_ref, kb_ref, vb_ref, ckv_ref, krb_ref,
                     qc_ref, kc_ref, vc_ref, kn_ref, *, q_transposed):
    w = W_AB
    q_rank = gcq_ref.shape[1]
    kv_rank = gckv_ref.shape[1]
    u = _rms(x_ref[...], gpre_ref[...]).astype(BF16)
    o = 6 * w
    lat = _dot(u, win_ref[:, o:])
    cq = lat[:, :q_rank]
    ckv = lat[:, q_rank:q_rank + kv_rank]
    krb = lat[:, q_rank + kv_rank:q_rank + kv_rank + C_PAD]
    proj = _dot(u, win_ref[:, :o])
    qa, ka, va = proj[:, 0:w], proj[:, w:2 * w], proj[:, 2 * w:3 * w]
    qb, kb, vb = proj[:, 3 * w:4 * w], proj[:, 4 * w:5 * w], proj[:, 5 * w:6 * w]
    qa_ref[...] = (qa * (ATTN_SCALE * LOG2E)).astype(BF16)
    qb_ref[...] = (qb * (ATTN_SCALE * LOG2E)).astype(BF16)
    ka_ref[...] = ka
    va_ref[...] = va
    kb_ref[...] = kb
    vb_ref[...] = vb
    kab_ref[...] = ka.astype(BF16)
    vab_ref[...] = va.astype(BF16)
    kbb_ref[...] = kb.astype(BF16)
    vbb_ref[...] = vb.astype(BF16)
    tc, ts1, ts2 = tc_ref[...], ts1_ref[...], ts2_ref[...]
    cqn = _rms(cq, gcq_ref[...]).astype(BF16)
    qall = _dot(cqn, wuq_ref[...])
    ckvn = _rms(ckv, gckv_ref[...])
    ckv_ref[...] = ckvn
    kv = _dot(ckvn.astype(BF16), wukv_ref[...])
    krot = _rope(krb, tc, ts1, ts2)
    krb_ref[...] = krot
    one_lane = (lax.broadcasted_iota(jnp.int32, (1, C_PAD), 1) == C_ONE).astype(F32)
    for h in range(H_C):
        sl = slice(h * C_PAD, (h + 1) * C_PAD)
        qh = _rope(qall[:, sl], tc, ts1, ts2) * (MLA_SCALE * LOG2E)
        if q_transposed:
            qt = qh.T
            norm = jnp.sqrt(jnp.sum(qt * qt, axis=0, keepdims=True)) * NORM_MARGIN
            row = lax.broadcasted_iota(jnp.int32, qt.shape, 0)
            qc_ref[sl, :] = jnp.where(row == C_ONE, norm, qt).astype(BF16)
        else:
            qc_ref[:, sl] = qh.astype(BF16)
        kh = kv[:, sl] + krot
        kc_ref[:, sl] = (kh + one_lane).astype(BF16)
        k_sq = _dot((kh * kh).astype(BF16), jnp.ones((C_PAD, C_PAD), BF16))
        kn_ref[0, :, sl] = jnp.sqrt(jnp.max(k_sq, axis=0, keepdims=True)) * NORM_MARGIN
    if not q_transposed:
        vc_ref[...] = kv[:, H_C * C_PAD:].astype(BF16)
    else:
        for p in range(H_C // 2):
            o = H_C * C_PAD + p * 2 * C_V
            vc_ref[0, p * 2 * C_V:(p + 1) * 2 * C_V, :] = kv[:, o:o + 2 * C_V].T.astype(BF16)


def _mixer_in(x, g_pre, w_main, g_cq, g_ckv, w_uq, w_ukv, tabs, tm, q_transposed=False):
    n, d = x.shape
    q_rank, kv_rank = g_cq.shape[1], g_ckv.shape[1]
    wc = H_C * C_PAD
    wv = H_C * C_V
    bf = lambda width: jax.ShapeDtypeStruct((n, width), BF16)
    f32 = lambda width: jax.ShapeDtypeStruct((n, width), F32)
    q_shape = jax.ShapeDtypeStruct((wc, n), BF16) if q_transposed else bf(wc)
    q_spec = pl.BlockSpec((wc, tm), lambda i: (0, i)) if q_transposed else _row_spec(tm, wc)
    v_shape = jax.ShapeDtypeStruct((n // tm, wv, tm), BF16) if q_transposed else bf(wv)
    v_spec = (pl.BlockSpec((1, wv, tm), lambda i: (i, 0, 0)) if q_transposed
              else _row_spec(tm, wv))
    out_shape = ([bf(W_AB)] * 6 + [f32(W_AB)] * 4 + [f32(kv_rank), f32(C_PAD)]
                 + [q_shape, bf(wc), v_shape, jax.ShapeDtypeStruct((n // tm, 1, wc), F32)])
    out_specs = ([_row_spec(tm, W_AB)] * 10 + [_row_spec(tm, kv_rank), _row_spec(tm, C_PAD)]
                 + [q_spec, _row_spec(tm, wc), v_spec, pl.BlockSpec((1, 1, wc), lambda i: (i, 0, 0))])
    return pl.pallas_call(
        functools.partial(_mixer_in_kernel, q_transposed=q_transposed),
        grid=(n // tm,),
        in_specs=[_row_spec(tm, d), _const_spec((1, d)), _const_spec(w_main.shape),
                  _const_spec((1, q_rank)), _const_spec((1, kv_rank)),
                  _const_spec(w_uq.shape), _const_spec(w_ukv.shape),
                  _row_spec(tm, C_PAD), _row_spec(tm, C_PAD), _row_spec(tm, C_PAD)],
        out_specs=out_specs,
        out_shape=out_shape,
        compiler_params=_params(1),
        name="mixer_in",
    )(x, g_pre, w_main, g_cq, g_ckv, w_uq, w_ukv, *tabs)


def _merge_kernel(x_ref, ya_ref, yb_ref, yc_ref, gpre_ref, gpost_ref, wgates_ref,
                  wa_ref, wb_ref, wc_ref, wout_ref, o_ref, *, yc_transposed):
    x = x_ref[...]
    d = x.shape[1]
    u = _rms(x, gpre_ref[...]).astype(BF16)
    m = jnp.zeros(x.shape, F32)
    for i, (y_ref, w_ref) in enumerate(((ya_ref, wa_ref), (yb_ref, wb_ref), (yc_ref, wc_ref))):
        gate = jax.nn.sigmoid(_dot(u, wgates_ref[:, i * d:(i + 1) * d]))
        branch_dot = _dot_tn if (yc_transposed and y_ref is yc_ref) else _dot
        m = m + gate * branch_dot(y_ref[...], w_ref[...])
    mixed = _dot(m.astype(BF16), wout_ref[...])
    o_ref[...] = x + _rms(mixed, gpost_ref[...])


def _merge(x, ya, yb, yc, g_pre, g_post, w_gates, w_a, w_b, w_c, w_out, tm, yc_transposed=False):
    n, d = x.shape
    yc_spec = (pl.BlockSpec((yc.shape[0], tm), lambda i: (0, i)) if yc_transposed
               else _row_spec(tm, yc.shape[1]))
    return pl.pallas_call(
        functools.partial(_merge_kernel, yc_transposed=yc_transposed),
        grid=(n // tm,),
        in_specs=[_row_spec(tm, d), _row_spec(tm, ya.shape[1]), _row_spec(tm, yb.shape[1]),
                  yc_spec, _const_spec((1, d)), _const_spec((1, d)),
                  _const_spec(w_gates.shape), _const_spec(w_a.shape), _const_spec(w_b.shape),
                  _const_spec(w_c.shape), _const_spec(w_out.shape)],
        out_specs=_row_spec(tm, d),
        out_shape=jax.ShapeDtypeStruct((n, d), F32),
        compiler_params=_params(1),
        name="merge",
    )(x, ya, yb, yc, g_pre, g_post, w_gates, w_a, w_b, w_c, w_out)


def _head_masked(q, width):
    lane_head = lax.broadcasted_iota(jnp.int32, q.shape, 1) // width
    zero = jnp.zeros_like(q)
    return [jnp.where(lane_head == h, q, zero) for h in range(q.shape[1] // width)]


def _strict_upper_ones(n):
    j = lax.broadcasted_iota(jnp.int32, (n, n), 0)
    s = lax.broadcasted_iota(jnp.int32, (n, n), 1)
    return (j > s).astype(BF16)


def _sb_terms(qm, kt, ones_ut, causal):
    z = _dot_nt(qm, kt)
    t = jnp.log2(1.0 + jnp.exp2(-jnp.abs(z)))
    log_keep = -jnp.maximum(z, 0.0) - t
    log_beta = jnp.minimum(z, 0.0) - t
    if causal is not None:
        log_keep = jnp.where(causal, log_keep, 0.0)
    hi = log_keep.astype(BF16)
    lo = (log_keep - hi.astype(F32)).astype(BF16)
    tail = _dot(hi, ones_ut) + _dot(lo, ones_ut)
    return log_beta + tail, jnp.sum(log_keep, axis=1, keepdims=True)


def _sb_apply(terms, vt, carry, acc, causal):
    logw, total = terms
    w = jnp.exp2(logw + carry)
    if causal is not None:
        w = jnp.where(causal, w, 0.0)
    return carry + total, acc + _dot(w.astype(BF16), vt)


def _sb_tile(qm, kt, vt, carry, acc, ones_ut, causal):
    return _sb_apply(_sb_terms(qm, kt, ones_ut, causal), vt, carry, acc, causal)


def _sb_alive(carry):
    return jnp.max(carry) > SB_DEAD


def _select_heads(parts, width):
    lane_head = lax.broadcasted_iota(jnp.int32, parts[0].shape, 1) // width
    out = parts[0]
    for h in range(1, len(parts)):
        out = jnp.where(lane_head == h, parts[h], out)
    return out


def _band_kernel(q_ref, k0_ref, k1_ref, k2_ref, v0_ref, v1_ref, v2_ref, bias_ref, o_ref):
    i = pl.program_id(0)
    tq = q_ref.shape[0]
    kwin = jnp.concatenate([k0_ref[...], k1_ref[...], k2_ref[...]], axis=0)
    vwin = jnp.concatenate([v0_ref[...], v1_ref[...], v2_ref[...]], axis=0)

    def attend(clip_window):
        parts = []
        for h, qm in enumerate(_head_masked(q_ref[...], HEAD_DIM)):
            s = _dot_nt(qm, kwin) + bias_ref[h]
            if clip_window:
                col = lax.broadcasted_iota(jnp.int32, s.shape, 1)
                s = jnp.where(col >= (2 - i) * tq, s, NEG)
            m = jnp.max(s, axis=1, keepdims=True)
            p = jnp.exp2(s - m)
            l = jnp.sum(p, axis=1, keepdims=True)
            parts.append(_dot(p.astype(BF16), vwin) / l)
        o_ref[...] = _select_heads(parts, HEAD_DIM).astype(BF16)

    @pl.when(i < 2)
    def _():
        attend(True)

    @pl.when(i >= 2)
    def _():
        attend(False)


def _band_attention(q, k, v, bias, tq):
    n, w = q.shape
    blk = lambda back: pl.BlockSpec((tq, w), lambda i: (jnp.maximum(i - back, 0), 0))
    return pl.pallas_call(
        _band_kernel,
        grid=(n // tq,),
        in_specs=[blk(0), blk(2), blk(1), blk(0), blk(2), blk(1), blk(0), _const_spec(bias.shape)],
        out_specs=blk(0),
        out_shape=jax.ShapeDtypeStruct((n, w), BF16),
        compiler_params=_params(1),
        name="band_attn",
    )(q, k, k, k, v, v, v, bias)


def _sb_kernel(q_ref, k_ref, v_ref, o_ref, *, tile):
    i = pl.program_id(0)
    row = lax.broadcasted_iota(jnp.int32, (tile, tile), 0)
    col = lax.broadcasted_iota(jnp.int32, (tile, tile), 1)
    causal = col < row
    ones_ut = _strict_upper_ones(tile)
    width = q_ref.shape[1]
    qms = _head_masked(q_ref[...], HEAD_DIM)
    rows = lambda ref, kb: ref[pl.ds(pl.multiple_of(kb * tile, tile), tile), :]

    prev = jnp.maximum(i - 1, 0)
    k_diag, k_prev = rows(k_ref, i), rows(k_ref, prev)
    terms_diag = [_sb_terms(qm, k_diag, ones_ut, causal) for qm in qms]
    terms_prev = [_sb_terms(qm, k_prev, ones_ut, None) for qm in qms]
    v_diag, v_prev = rows(v_ref, i), rows(v_ref, prev)
    parts = []
    for qm, t_diag, t_prev in zip(qms, terms_diag, terms_prev):
        carry, acc = _sb_apply(t_diag, v_diag, jnp.zeros((tile, 1), F32), jnp.zeros((tile, width), F32),
                               causal)
        carry, acc = _sb_apply(t_prev, v_prev, jnp.where(i > 0, carry, NEG), acc, None)
        state = lax.while_loop(
            lambda st: (st[0] < i - 1) & _sb_alive(st[1]),
            lambda st, qm=qm: (st[0] + 1,) + _sb_tile(qm, rows(k_ref, i - 2 - st[0]), rows(v_ref, i - 2 - st[0]),
                                                      st[1], st[2], ones_ut, None),
            (jnp.int32(0), carry, acc))
        parts.append(state[2])
    o_ref[...] = _select_heads(parts, HEAD_DIM).astype(BF16)


def _sb_attention(q, k, v, tile):
    n, w = q.shape
    return pl.pallas_call(
        functools.partial(_sb_kernel, tile=tile),
        grid=(n // tile,),
        in_specs=[_row_spec(tile, w), _const_spec((n, w)), _const_spec((n, w))],
        out_specs=_row_spec(tile, w),
        out_shape=jax.ShapeDtypeStruct((n, w), BF16),
        compiler_params=_params(1),
        name="sb_attn",
    )(q, k, v)


def _mla_kernel(qt_ref, k_ref, vt_ref, kn_ref, ot_ref, qx_scr, p_scr, acc_scr, l_scr):
    i = pl.program_id(1)
    heads = range(2)
    tk, tq = p_scr.shape[2:]
    per_q = tq // tk
    plain = per_q * i
    key = lax.broadcasted_iota(jnp.int32, (tk, tq), 0)
    qry = lax.broadcasted_iota(jnp.int32, (tk, tq), 1)
    row = lax.broadcasted_iota(jnp.int32, (C_PAD - C_ONE, tq), 0)

    def visible(d):
        return ((d * tk + key) // CHUNK) <= (qry // CHUNK)

    def load_q(shifted):
        k_bound = jnp.max(kn_ref[...], axis=0)
        for e in heads:
            lo = e * C_PAD
            qx_scr[lo:lo + C_ONE, :] = qt_ref[lo:lo + C_ONE, :]
            if shifted:
                bound = qt_ref[lo + C_ONE:lo + C_ONE + 1, :].astype(F32) * (k_bound[:, lo:lo + 1] * NORM_MARGIN)
                tail = jnp.where(row == 0, -bound, 0.0)
            else:
                tail = jnp.zeros(row.shape, F32)
            qx_scr[lo + C_ONE:lo + C_PAD, :] = tail.astype(BF16)

    def scores(kb, e):
        return _dot(k_ref[kb, :, e * C_PAD:(e + 1) * C_PAD], qx_scr[e * C_PAD:(e + 1) * C_PAD, :])

    def values(kb, e, p):
        return _dot(vt_ref[kb, e * C_V:(e + 1) * C_V, :], p)

    def write_out():
        ot_ref[...] = jnp.concatenate([acc_scr[e] / l_scr[e] for e in heads], axis=0).astype(BF16)

    def reset():
        acc_scr[...] = jnp.zeros(acc_scr.shape, F32)
        l_scr[...] = jnp.zeros(l_scr.shape, F32)

    def fast_stage(kb, slot, mask):
        for e in heads:
            s = scores(kb, e)
            if mask is not None:
                s = jnp.where(mask, s, NEG)
            p = jnp.exp2(s)
            l_scr[e] += jnp.sum(p, axis=0, keepdims=True)
            p_scr[slot, e] = p.astype(BF16)
        for e in heads:
            acc_scr[e] += values(kb, e, p_scr[slot, e])

    load_q(True)
    reset()

    def fast_group(g, _):
        for j in range(MLA_UNROLL):
            fast_stage(MLA_UNROLL * g + j, j, None)
        return 0
    groups = plain // MLA_UNROLL
    lax.fori_loop(0, groups, fast_group, 0)

    def fast_single(kb, _):
        fast_stage(kb, 0, None)
        return 0
    lax.fori_loop(MLA_UNROLL * groups, plain, fast_single, 0)
    for d in range(per_q):
        fast_stage(plain + d, (1 + d) % MLA_UNROLL, visible(d))
    denom = jnp.minimum(jnp.min(l_scr[0]), jnp.min(l_scr[1]))
    trusted = denom >= MLA_MIN_DENOM

    @pl.when(trusted)
    def _():
        write_out()

    @pl.when(jnp.logical_not(trusted))
    def _():
        load_q(False)
        reset()

        def step(kb, m):
            out = []
            for e in heads:
                s = jnp.where(visible(kb - plain), scores(kb, e), NEG)
                m_new = jnp.maximum(m[e], jnp.max(s, axis=0, keepdims=True))
                alpha = jnp.exp2(m[e] - m_new)
                p = jnp.exp2(s - m_new)
                l_scr[e] = alpha * l_scr[e] + jnp.sum(p, axis=0, keepdims=True)
                acc_scr[e] = alpha * acc_scr[e] + values(kb, e, p.astype(BF16))
                out.append(m_new)
            return tuple(out)
        lax.fori_loop(0, plain + per_q, step, tuple(jnp.full((1, tq), NEG, F32) for _ in heads))
        write_out()


def _mla_attention(qt, k, vt3, kn, tq):
    n = k.shape[0]
    pairs = H_C // 2
    n_tiles, _, tk = vt3.shape
    assert n_tiles * tk == n and kn.shape == (n_tiles, 1, H_C * C_PAD)
    assert tq % tk == 0 and n % tq == 0
    k3 = k.reshape(n_tiles, tk, H_C * C_PAD)
    return pl.pallas_call(
        _mla_kernel,
        grid=(pairs, n // tq),
        in_specs=[pl.BlockSpec((2 * C_PAD, tq), lambda p, i: (p, i)),
                  pl.BlockSpec((n_tiles, tk, 2 * C_PAD), lambda p, i: (0, 0, p)),
                  pl.BlockSpec((n_tiles, 2 * C_V, tk), lambda p, i: (0, p, 0)),
                  pl.BlockSpec((n_tiles, 1, 2 * C_PAD), lambda p, i: (0, 0, p))],
        out_specs=pl.BlockSpec((2 * C_V, tq), lambda p, i: (p, i)),
        out_shape=jax.ShapeDtypeStruct((H_C * C_V, n), BF16),
        scratch_shapes=[pltpu.VMEM((2 * C_PAD, tq), BF16), pltpu.VMEM((MLA_UNROLL, 2, tk, tq), BF16),
                        pltpu.VMEM((2, C_V, tq), F32), pltpu.VMEM((2, 1, tq), F32)],
        compiler_params=_params(2),
        name="mla_attn",
    )(qt, k3, vt3, kn)


def _sample_attn_kernel(qa_ref, ka_ref, va_ref, cak_ref, cav_ref, biasc_ref, biasn_ref,
                        qb_ref, kb_ref, vb_ref, cbk_ref, cbv_ref,
                        qc_ref, ckvn_ref, krbn_ref, ckv_ref, ckr_ref, wukt_ref, wuvp_ref,
                        ya_ref, yb_ref, yc_ref, s_scr, *, tile):
    ds = qa_ref.shape[1]
    past = cbk_ref.shape[2]

    ka, va, cak, cav = ka_ref[0], va_ref[0], cak_ref[0, 0].astype(BF16), cav_ref[0, 0].astype(BF16)
    parts = []
    for h, qm in enumerate(_head_masked(qa_ref[0], HEAD_DIM)):
        s_c = _dot_nt(qm, cak) + biasc_ref[h]
        s_n = _dot_nt(qm, ka) + biasn_ref[h]
        m = jnp.maximum(jnp.max(s_c, axis=1, keepdims=True), jnp.max(s_n, axis=1, keepdims=True))
        p_c = jnp.exp2(s_c - m)
        p_n = jnp.exp2(s_n - m)
        l = jnp.sum(p_c, axis=1, keepdims=True) + jnp.sum(p_n, axis=1, keepdims=True)
        parts.append((_dot(p_c.astype(BF16), cav) + _dot(p_n.astype(BF16), va)) / l)
    ya_ref[0] = _select_heads(parts, HEAD_DIM).astype(BF16)

    row = lax.broadcasted_iota(jnp.int32, (ds, ds), 0)
    col = lax.broadcasted_iota(jnp.int32, (ds, ds), 1)
    causal = col < row
    ones_new = _strict_upper_ones(ds)
    ones_ut = _strict_upper_ones(tile)
    kb, vb = kb_ref[0], vb_ref[0]
    n_tiles = past // tile
    parts = []
    for qm in _head_masked(qb_ref[0], HEAD_DIM):
        state = _sb_tile(qm, kb, vb, jnp.zeros((ds, 1), F32), jnp.zeros((ds, W_AB), F32),
                         ones_new, causal)

        def cache_tile(st, qm=qm):
            start = pl.multiple_of((n_tiles - 1 - st[0]) * tile, tile)
            return (st[0] + 1,) + _sb_tile(qm, cbk_ref[0, 0, pl.ds(start, tile), :].astype(BF16),
                                           cbv_ref[0, 0, pl.ds(start, tile), :].astype(BF16),
                                           st[1], st[2], ones_ut, None)
        state = lax.while_loop(lambda st: (st[0] < n_tiles) & _sb_alive(st[1]), cache_tile,
                               (jnp.int32(0),) + state)
        parts.append(state[2])
    yb_ref[0] = _select_heads(parts, HEAD_DIM).astype(BF16)

    kv_rank = ckv_ref.shape[3]
    qc = qc_ref[0]
    rope_lanes = lax.broadcasted_iota(jnp.int32, (ds, C_PAD), 1) >= C_NOPE
    q_rows = []
    for h in range(H_C):
        qh = qc[:, h * C_PAD:(h + 1) * C_PAD]
        q_lat = _dot(qh, wukt_ref[h]).astype(BF16)
        q_rows.append(jnp.concatenate([q_lat, jnp.where(rope_lanes, qh, jnp.zeros_like(qh))], axis=1))
    q_all = jnp.concatenate(q_rows, axis=0)

    place = (lax.broadcasted_iota(jnp.int32, (C_ROPE, C_PAD), 1)
             == lax.broadcasted_iota(jnp.int32, (C_ROPE, C_PAD), 0) + C_NOPE).astype(BF16)
    tile_c = SAMPLE_LATENT_TILE if past % SAMPLE_LATENT_TILE == 0 else tile
    cache_tiles = range(past // tile_c)
    latent = lambda j: ckv_ref[0, 0, j * tile_c:(j + 1) * tile_c, :].astype(BF16)

    lat_new = ckvn_ref[0].astype(BF16)
    s_new = _dot_nt(q_all, jnp.concatenate([lat_new, krbn_ref[0].astype(BF16)], axis=1))
    m = jnp.max(s_new, axis=1, keepdims=True)
    for j in cache_tiles:
        k_rope = _dot(ckr_ref[0, 0, j * tile_c:(j + 1) * tile_c, :].astype(BF16), place).astype(BF16)
        s = _dot_nt(q_all, jnp.concatenate([latent(j), k_rope], axis=1))
        s_scr[:, j * tile_c:(j + 1) * tile_c] = s
        m = jnp.maximum(m, jnp.max(s, axis=1, keepdims=True))
    p = jnp.exp2(s_new - m)
    denom = jnp.sum(p, axis=1, keepdims=True)
    o_lat = _dot(p.astype(BF16), lat_new)
    for j in cache_tiles:
        p = jnp.exp2(s_scr[:, j * tile_c:(j + 1) * tile_c] - m)
        denom = denom + jnp.sum(p, axis=1, keepdims=True)
        o_lat = o_lat + _dot(p.astype(BF16), latent(j))
    o_lat = (o_lat / denom).astype(BF16)
    out = jnp.zeros((ds, H_C * C_V), F32)
    for h in range(H_C):
        out = out + _dot(o_lat[h * ds:(h + 1) * ds], wuvp_ref[h])
    yc_ref[0] = out.astype(BF16)


def _sample_attention(new, caches, l, bias_c, bias_n, w_ukt, w_uvp, tile):
    qa, ka, va, qb, kb, vb, qc, ckvn, krbn = new
    cak, cav, cbk, cbv, ckv, ckr = caches
    nb, ds, _ = qa.shape
    past = ckv.shape[2]
    req = lambda a: pl.BlockSpec((1,) + a.shape[1:], lambda b: (b, 0, 0))
    cache = lambda a: pl.BlockSpec((1, 1) + a.shape[2:], lambda b: (l, b, 0, 0))
    ins = [qa, ka, va, cak, cav, bias_c, bias_n, qb, kb, vb, cbk, cbv, qc, ckvn, krbn, ckv, ckr,
           w_ukt, w_uvp]
    specs = [req(a) for a in ins]
    for idx in (3, 4, 10, 11, 15, 16):
        specs[idx] = cache(ins[idx])
    for idx in (5, 6, 17, 18):
        specs[idx] = _const_spec(ins[idx].shape)
    out_shape = [jax.ShapeDtypeStruct((nb, ds, W_AB), BF16), jax.ShapeDtypeStruct((nb, ds, W_AB), BF16),
                 jax.ShapeDtypeStruct((nb, ds, H_C * C_V), BF16)]
    return pl.pallas_call(
        functools.partial(_sample_attn_kernel, tile=tile),
        grid=(nb,),
        in_specs=specs,
        out_specs=[req(s) for s in out_shape],
        out_shape=out_shape,
        scratch_shapes=[pltpu.VMEM((H_C * ds, past), F32)],
        compiler_params=_params(1),
        name="sample_attn",
    )(*ins)


def _rope_tables(pos):
    half = C_ROPE // 2
    freq = ROPE_THETA ** (-jnp.arange(half, dtype=F32) / half)
    ang = pos.astype(F32)[:, None] * freq[None, :]
    cos, sin = jnp.cos(ang), jnp.sin(ang)
    n = pos.shape[0]
    z = lambda w: jnp.zeros((n, w), F32)
    tc = jnp.concatenate([jnp.ones((n, C_NOPE), F32), cos, cos, z(C_PAD - C_NOPE - C_ROPE)], axis=1)
    ts1 = jnp.concatenate([z(C_NOPE + half), sin, z(C_PAD - C_NOPE - C_ROPE)], axis=1)
    ts2 = jnp.concatenate([z(C_NOPE), -sin, z(C_PAD - C_NOPE - half)], axis=1)
    return tc, ts1, ts2


def _band_bias(rel_bias, q0, nq, k0, nk):
    period = 1 << int(np.ceil(np.log2(nq + nk)))
    d = np.arange(period)
    d = np.where(d >= nk, d - period, d)
    idx = np.clip((q0 - k0) - d, -REL_CLIP, REL_CLIP) + REL_CLIP
    g = rel_bias.astype(F32)[:, idx]
    toe = jnp.tile(g, (1, nq))[:, :nq * (period - 1)].reshape(-1, nq, period - 1)[:, :, :nk]
    q_pos, k_pos = q0 + np.arange(nq), k0 + np.arange(nk)
    qc, kc = q_pos[:, None] // CHUNK, k_pos[None, :] // CHUNK
    vis = (kc <= qc) & (kc >= qc - BAND_CHUNKS)
    return jnp.where(jnp.asarray(vis)[None], toe * LOG2E, NEG)


def _cast_kernel(w_ref, o_ref):
    o_ref[...] = w_ref[0].astype(BF16)


def _layer_bf16(w, l):
    _, rows, cols = w.shape
    rb = rows // 4 if rows % 64 == 0 else rows
    return pl.pallas_call(
        _cast_kernel,
        grid=(rows // rb,),
        in_specs=[pl.BlockSpec((1, rb, cols), lambda i: (l, i, 0))],
        out_specs=pl.BlockSpec((rb, cols), lambda i: (i, 0)),
        out_shape=jax.ShapeDtypeStruct((rows, cols), BF16),
        compiler_params=_params(1),
        name="cast_bf16",
    )(w)


def _layer_weights(l, w_in, w_uq, w_uk, w_uv):
    d = w_in.shape[1]
    q_rank, kv_rank = w_uq.shape[1], w_uk.shape[1]
    o = 6 * W_AB + q_rank + kv_rank
    zc = lambda w: jnp.zeros((d, w), w_in.dtype)
    w_main = jnp.concatenate([w_in[l, :, :o], zc(C_NOPE), w_in[l, :, o:o + C_ROPE],
                              zc(C_PAD - C_NOPE - C_ROPE)], axis=1).astype(BF16)
    w_gates = w_in[l, :, o + C_ROPE:].astype(BF16)
    pad_heads = lambda w: jnp.pad(w, ((0, 0), (0, 0), (0, C_PAD - w.shape[2]))).reshape(w.shape[0], -1)
    w_uq_p = pad_heads(w_uq[l]).astype(BF16)
    w_ukv = jnp.concatenate([pad_heads(w_uk[l]), w_uv[l].reshape(kv_rank, -1)], axis=1).astype(BF16)
    w_ukt = jnp.pad(jnp.transpose(w_uk[l], (1, 2, 0)), ((0, 0), (0, C_PAD - C_NOPE), (0, 0))).astype(BF16)
    w_uvp = jnp.einsum('rhd,hg->hrgd', w_uv[l], jnp.eye(H_C, dtype=w_uv.dtype)).reshape(
        H_C, kv_rank, H_C * C_V).astype(BF16)
    return w_main, w_gates, w_uq_p, w_ukv, w_ukt, w_uvp


def kernel(x_prompt, x_sample, cache_a_k, cache_a_v, cache_b_k, cache_b_v, cache_c_kv, cache_c_kr, ffn1_norm_pre, ffn1_norm_post, ffn1_w_gate, ffn1_w_up, ffn1_w_down, mix_norm_pre, mix_norm_post, w_in, cq_norm, ckv_norm, w_uq, w_uk, w_uv, rel_bias_a, w_br_a, w_br_b, w_br_c, w_out, ffn2_norm_pre, ffn2_norm_post, ffn2_w_gate, ffn2_w_up, ffn2_w_down):
    batch, seq, d = x_prompt.shape
    nb, ds, _ = x_sample.shape
    depth = w_in.shape[0]
    past = cache_b_k.shape[2]
    win_cache = cache_a_k.shape[2]
    assert batch == 1, "prompt group is a single sequence"
    tile = 256
    tm = 512 if seq % 512 == 0 else tile
    assert seq % tile == 0 and past % tile == 0 and tile % CHUNK == 0 and 2 * tile >= WIN_A
    assert past % CHUNK + ds <= CHUNK
    ns = nb * ds

    xp = x_prompt.reshape(seq, d)
    xs = x_sample.reshape(ns, d)
    tabs_p = _rope_tables(jnp.arange(seq))
    tabs_s = _rope_tables(jnp.tile(past + jnp.arange(ds), nb))

    row = lambda g, l: g[l][None, :]
    base = 2 * tile

    merged_heads = lambda c: c.reshape(c.shape[:3] + (-1,))
    caches = (merged_heads(cache_a_k), merged_heads(cache_a_v), merged_heads(cache_b_k),
              merged_heads(cache_b_v), cache_c_kv, cache_c_kr)
    states_p, states_s = [], []
    for l in range(depth):
        w_main, w_gates, w_uq_p, w_ukv, w_ukt, w_uvp = _layer_weights(l, w_in, w_uq, w_uk, w_uv)
        ffn1 = (row(ffn1_norm_pre, l), row(ffn1_norm_post, l), _layer_bf16(ffn1_w_gate, l),
                _layer_bf16(ffn1_w_up, l), _layer_bf16(ffn1_w_down, l))
        ffn2 = (row(ffn2_norm_pre, l), row(ffn2_norm_post, l), _layer_bf16(ffn2_w_gate, l),
                _layer_bf16(ffn2_w_up, l), _layer_bf16(ffn2_w_down, l))
        mix_in = (row(mix_norm_pre, l), w_main, row(cq_norm, l), row(ckv_norm, l), w_uq_p, w_ukv)
        mrg = (row(mix_norm_pre, l), row(mix_norm_post, l), w_gates, _layer_bf16(w_br_a, l),
               _layer_bf16(w_br_b, l), _layer_bf16(w_br_c, l), _layer_bf16(w_out, l))
        bias_p = _band_bias(rel_bias_a[l], base, tile, base - 2 * tile, 3 * tile)
        bias_c = _band_bias(rel_bias_a[l], past, ds, past - win_cache, win_cache)
        bias_n = _band_bias(rel_bias_a[l], past, ds, past, ds)

        xp = _ffn(xp, *ffn1, tm)
        (qa, kab, vab, qb, kbb, vbb, ka, va, kb, vb, ckv, krb, qc, kc, vc, kn) = _mixer_in(
            xp, *mix_in, tabs_p, tm, q_transposed=True)
        ya = _band_attention(qa, kab, vab, bias_p, tile)
        yb = _sb_attention(qb, kbb, vbb, tile)
        yc = _mla_attention(qc, kc, vc, kn, 2 * tm if seq % (2 * tm) == 0 else tm)
        xp = _merge(xp, ya, yb, yc, *mrg, tm, yc_transposed=True)
        xp = _ffn(xp, *ffn2, tm)
        win = min(WIN_A, seq)
        heads = lambda t, h: t.reshape(1, t.shape[0], h, t.shape[1] // h)
        states_p.append((heads(ka[-win:], H_A), heads(va[-win:], H_A), heads(kb, H_B), heads(vb, H_B),
                         ckv[None], krb[None, :, C_NOPE:C_NOPE + C_ROPE]))

        xs = _ffn(xs, *ffn1, ns)
        (qa, kab, vab, qb, kbb, vbb, ka, va, kb, vb, ckv, krb, qc, kc, vc, _) = _mixer_in(
            xs, *mix_in, tabs_s, ns)
        per_req = lambda t: t.reshape(nb, ds, t.shape[1])
        new = tuple(per_req(t) for t in (qa, kab, vab, qb, kbb, vbb, qc, ckv, krb))
        ya, yb, yc = _sample_attention(new, caches, l, bias_c, bias_n, w_ukt, w_uvp, tile)
        xs = _merge(xs, ya.reshape(ns, -1), yb.reshape(ns, -1), yc.reshape(ns, -1), *mrg, ns)
        xs = _ffn(xs, *ffn2, ns)
        heads_s = lambda t, h: t.reshape(nb, ds, h, t.shape[1] // h)
        states_s.append((heads_s(ka, H_A), heads_s(va, H_A), heads_s(kb, H_B), heads_s(vb, H_B),
                         ckv.reshape(nb, ds, -1), krb[:, C_NOPE:C_NOPE + C_ROPE].reshape(nb, ds, C_ROPE)))

    stack = lambda states, i: jnp.stack([s[i] for s in states], axis=0)
    return ((xp.reshape(batch, seq, d), xs.reshape(nb, ds, d))
            + tuple(stack(states_p, i) for i in range(6))
            + tuple(stack(states_s, i) for i in range(6)))
```

```python
import functools

import numpy as np
import jax
import jax.numpy as jnp
from jax import lax
from jax.experimental import pallas as pl
from jax.experimental.pallas import tpu as pltpu

F32 = jnp.float32
BF16 = jnp.bfloat16

CHUNK = 64
BAND_CHUNKS = 8
WIN_A = BAND_CHUNKS * CHUNK
HEAD_DIM = 64
H_A = 4
H_B = 4
H_C = 8
C_NOPE = 64
C_ROPE = 32
C_V = 64
REL_CLIP = 128
ROPE_THETA = 10000.0
EPS = 1e-6
N_BRANCH = 3
W_AB = H_A * HEAD_DIM
C_PAD = 128
ATTN_SCALE = HEAD_DIM ** -0.5
MLA_SCALE = (C_NOPE + C_ROPE) ** -0.5
LOG2E = float(np.log2(np.e))
C_ONE = C_NOPE + C_ROPE
NORM_MARGIN = 1.02
MLA_MIN_DENOM = 2.0 ** -60
MLA_UNROLL = 4
SAMPLE_LATENT_TILE = 1024
NEG = -1e30
SB_ALWAYS = 2
SB_DEAD = -150.0

LANES = 128
MXU_WIDTH = 256
VMEM_LIMIT = 52 * 1024 * 1024

NT_DIMS = (((1,), (1,)), ((), ()))
TN_DIMS = (((0,), (0,)), ((), ()))


def _rms(x, g):
    return x * lax.rsqrt(jnp.mean(x * x, axis=-1, keepdims=True) + EPS) * g


def _dot(a, b):
    return jnp.dot(a, b, preferred_element_type=F32)


def _dot_nt(a, b):
    return lax.dot_general(a, b, NT_DIMS, preferred_element_type=F32)


def _dot_tn(a, b):
    return lax.dot_general(a, b, TN_DIMS, preferred_element_type=F32)


def _const_spec(shape):
    zeros = (0,) * len(shape)
    return pl.BlockSpec(shape, lambda *_: zeros, pipeline_mode=pl.Buffered(1))


def _row_spec(tm, width):
    return pl.BlockSpec((tm, width), lambda i: (i, 0))


def _params(n_grid):
    return pltpu.CompilerParams(dimension_semantics=("arbitrary",) * n_grid,
                                vmem_limit_bytes=VMEM_LIMIT)


def _ffn_kernel(x_ref, gpre_ref, gpost_ref, wg_ref, wu_ref, wd_ref, o_ref, *, f_cuts):
    x = x_ref[...]
    h = _rms(x, gpre_ref[...]).astype(BF16)
    y = jnp.zeros(x.shape, F32)
    for lo, hi in zip(f_cuts[:-1], f_cuts[1:]):
        sl = slice(lo, hi)
        g = _dot(h, wg_ref[:, sl])
        u = _dot(h, wu_ref[:, sl])
        a = (g * jax.nn.sigmoid(g) * u).astype(BF16)
        y = y + _dot(a, wd_ref[sl, :])
    o_ref[...] = x + 0.5 * _rms(y, gpost_ref[...])


def _ffn(x, g_pre, g_post, w_gate, w_up, w_down, tm):
    n, d = x.shape
    d_ff = w_gate.shape[1]
    cut = (d_ff // 2) // MXU_WIDTH * MXU_WIDTH
    f_cuts = (0, cut, d_ff) if 0 < cut < d_ff else (0, d_ff)
    return pl.pallas_call(
        functools.partial(_ffn_kernel, f_cuts=f_cuts),
        grid=(n // tm,),
        in_specs=[_row_spec(tm, d), _const_spec((1, d)), _const_spec((1, d)),
                  _const_spec((d, d_ff)), _const_spec((d, d_ff)), _const_spec((d_ff, d))],
        out_specs=_row_spec(tm, d),
        out_shape=jax.ShapeDtypeStruct((n, d), F32),
        compiler_params=_params(1),
        name="ffn",
    )(x, g_pre, g_post, w_gate, w_up, w_down)


def _rope(blk, tc, ts1, ts2):
    return blk * tc + pltpu.roll(blk, 16, 1) * ts1 + pltpu.roll(blk, C_PAD - 16, 1) * ts2


def _mixer_in_kernel(x_ref, gpre_ref, win_ref, gcq_ref, gckv_ref, wuq_ref, wukv_ref,
                     tc_ref, ts1_ref, ts2_ref,
                     qa_ref, kab_ref, vab_ref, qb_ref, kbb_ref, vbb_ref,
                     ka_ref, va_ref, kb_ref, vb_ref, ckv_ref, krb_ref,
                     qc_ref, kc_ref, vc_ref, kn_ref, *, q_transposed):
    w = W_AB
    q_rank = gcq_ref.shape[1]
    kv_rank = gckv_ref.shape[1]
    u = _rms(x_ref[...], gpre_ref[...]).astype(BF16)
    o = 6 * w
    lat = _dot(u, win_ref[:, o:])
    cq = lat[:, :q_rank]
    ckv = lat[:, q_rank:q_rank + kv_rank]
    krb = lat[:, q_rank + kv_rank:q_rank + kv_rank + C_PAD]
    proj = _dot(u, win_ref[:, :o])
    qa, ka, va = proj[:, 0:w], proj[:, w:2 * w], proj[:, 2 * w:3 * w]
    qb, kb, vb = proj[:, 3 * w:4 * w], proj[:, 4 * w:5 * w], proj[:, 5 * w:6 * w]
    qa_ref[...] = (qa * (ATTN_SCALE * LOG2E)).astype(BF16)
    qb_ref[...] = (qb * (ATTN_SCALE * LOG2E)).astype(BF16)
    ka_ref[...] = ka
    va_ref[...] = va
    kb_ref[...] = kb
    vb_ref[...] = vb
    kab_ref[...] = ka.astype(BF16)
    vab_ref[...] = va.astype(BF16)
    kbb_ref[...] = kb.astype(BF16)
    vbb_ref[...] = vb.astype(BF16)
    tc, ts1, ts2 = tc_ref[...], ts1_ref[...], ts2_ref[...]
    cqn = _rms(cq, gcq_ref[...]).astype(BF16)
    qall = _dot(cqn, wuq_ref[...])
    ckvn = _rms(ckv, gckv_ref[...])
    ckv_ref[...] = ckvn
    kv = _dot(ckvn.astype(BF16), wukv_ref[...])
    krot = _rope(krb, tc, ts1, ts2)
    krb_ref[...] = krot
    one_lane = (lax.broadcasted_iota(jnp.int32, (1, C_PAD), 1) == C_ONE).astype(F32)
    for h in range(H_C):
        sl = slice(h * C_PAD, (h + 1) * C_PAD)
        qh = _rope(qall[:, sl], tc, ts1, ts2) * (MLA_SCALE * LOG2E)
        if q_transposed:
            qt = qh.T
            norm = jnp.sqrt(jnp.sum(qt * qt, axis=0, keepdims=True)) * NORM_MARGIN
            row = lax.broadcasted_iota(jnp.int32, qt.shape, 0)
            qc_ref[sl, :] = jnp.where(row == C_ONE, norm, qt).astype(BF16)
        else:
            qc_ref[:, sl] = qh.astype(BF16)
        kh = kv[:, sl] + krot
        kc_ref[:, sl] = (kh + one_lane).astype(BF16)
        k_sq = _dot((kh * kh).astype(BF16), jnp.ones((C_PAD, C_PAD), BF16))
        kn_ref[0, :, sl] = jnp.sqrt(jnp.max(k_sq, axis=0, keepdims=True)) * NORM_MARGIN
    if not q_transposed:
        vc_ref[...] = kv[:, H_C * C_PAD:].astype(BF16)
    else:
        for p in range(H_C // 2):
            o = H_C * C_PAD + p * 2 * C_V
            vc_ref[0, p * 2 * C_V:(p + 1) * 2 * C_V, :] = kv[:, o:o + 2 * C_V].T.astype(BF16)


def _mixer_in(x, g_pre, w_main, g_cq, g_ckv, w_uq, w_ukv, tabs, tm, q_transposed=False):
    n, d = x.shape
    q_rank, kv_rank = g_cq.shape[1], g_ckv.shape[1]
    wc = H_C * C_PAD
    wv = H_C * C_V
    bf = lambda width: jax.ShapeDtypeStruct((n, width), BF16)
    f32 = lambda width: jax.ShapeDtypeStruct((n, width), F32)
    q_shape = jax.ShapeDtypeStruct((wc, n), BF16) if q_transposed else bf(wc)
    q_spec = pl.BlockSpec((wc, tm), lambda i: (0, i)) if q_transposed else _row_spec(tm, wc)
    v_shape = jax.ShapeDtypeStruct((n // tm, wv, tm), BF16) if q_transposed else bf(wv)
    v_spec = (pl.BlockSpec((1, wv, tm), lambda i: (i, 0, 0)) if q_transposed
              else _row_spec(tm, wv))
    out_shape = ([bf(W_AB)] * 6 + [f32(W_AB)] * 4 + [f32(kv_rank), f32(C_PAD)]
                 + [q_shape, bf(wc), v_shape, jax.ShapeDtypeStruct((n // tm, 1, wc), F32)])
    out_specs = ([_row_spec(tm, W_AB)] * 10 + [_row_spec(tm, kv_rank), _row_spec(tm, C_PAD)]
                 + [q_spec, _row_spec(tm, wc), v_spec, pl.BlockSpec((1, 1, wc), lambda i: (i, 0, 0))])
    return pl.pallas_call(
        functools.partial(_mixer_in_kernel, q_transposed=q_transposed),
        grid=(n // tm,),
        in_specs=[_row_spec(tm, d), _const_spec((1, d)), _const_spec(w_main.shape),
                  _const_spec((1, q_rank)), _const_spec((1, kv_rank)),
                  _const_spec(w_uq.shape), _const_spec(w_ukv.shape),
                  _row_spec(tm, C_PAD), _row_spec(tm, C_PAD), _row_spec(tm, C_PAD)],
        out_specs=out_specs,
        out_shape=out_shape,
        compiler_params=_params(1),
        name="mixer_in",
    )(x, g_pre, w_main, g_cq, g_ckv, w_uq, w_ukv, *tabs)


def _merge_kernel(x_ref, ya_ref, yb_ref, yc_ref, gpre_ref, gpost_ref, wgates_ref,
                  wa_ref, wb_ref, wc_ref, wout_ref, o_ref, *, yc_transposed):
    x = x_ref[...]
    d = x.shape[1]
    u = _rms(x, gpre_ref[...]).astype(BF16)
    m = jnp.zeros(x.shape, F32)
    for i, (y_ref, w_ref) in enumerate(((ya_ref, wa_ref), (yb_ref, wb_ref), (yc_ref, wc_ref))):
        gate = jax.nn.sigmoid(_dot(u, wgates_ref[:, i * d:(i + 1) * d]))
        branch_dot = _dot_tn if (yc_transposed and y_ref is yc_ref) else _dot
        m = m + gate * branch_dot(y_ref[...], w_ref[...])
    mixed = _dot(m.astype(BF16), wout_ref[...])
    o_ref[...] = x + _rms(mixed, gpost_ref[...])


def _merge(x, ya, yb, yc, g_pre, g_post, w_gates, w_a, w_b, w_c, w_out, tm, yc_transposed=False):
    n, d = x.shape
    yc_spec = (pl.BlockSpec((yc.shape[0], tm), lambda i: (0, i)) if yc_transposed
               else _row_spec(tm, yc.shape[1]))
    return pl.pallas_call(
        functools.partial(_merge_kernel, yc_transposed=yc_transposed),
        grid=(n // tm,),
        in_specs=[_row_spec(tm, d), _row_spec(tm, ya.shape[1]), _row_spec(tm, yb.shape[1]),
                  yc_spec, _const_spec((1, d)), _const_spec((1, d)),
                  _const_spec(w_gates.shape), _const_spec(w_a.shape), _const_spec(w_b.shape),
                  _const_spec(w_c.shape), _const_spec(w_out.shape)],
        out_specs=_row_spec(tm, d),
        out_shape=jax.ShapeDtypeStruct((n, d), F32),
        compiler_params=_params(1),
        name="merge",
    )(x, ya, yb, yc, g_pre, g_post, w_gates, w_a, w_b, w_c, w_out)


def _head_masked(q, width):
    lane_head = lax.broadcasted_iota(jnp.int32, q.shape, 1) // width
    zero = jnp.zeros_like(q)
    return [jnp.where(lane_head == h, q, zero) for h in range(q.shape[1] // width)]


def _strict_upper_ones(n):
    j = lax.broadcasted_iota(jnp.int32, (n, n), 0)
    s = lax.broadcasted_iota(jnp.int32, (n, n), 1)
    return (j > s).astype(BF16)


def _sb_terms(qm, kt, ones_ut, causal):
    z = _dot_nt(qm, kt)
    t = jnp.log2(1.0 + jnp.exp2(-jnp.abs(z)))
    log_keep = -jnp.maximum(z, 0.0) - t
    log_beta = jnp.minimum(z, 0.0) - t
    if causal is not None:
        log_keep = jnp.where(causal, log_keep, 0.0)
    hi = log_keep.astype(BF16)
    lo = (log_keep - hi.astype(F32)).astype(BF16)
    tail = _dot(hi, ones_ut) + _dot(lo, ones_ut)
    return log_beta + tail, jnp.sum(log_keep, axis=1, keepdims=True)


def _sb_apply(terms, vt, carry, acc, causal):
    logw, total = terms
    w = jnp.exp2(logw + carry)
    if causal is not None:
        w = jnp.where(causal, w, 0.0)
    return carry + total, acc + _dot(w.astype(BF16), vt)


def _sb_tile(qm, kt, vt, carry, acc, ones_ut, causal):
    return _sb_apply(_sb_terms(qm, kt, ones_ut, causal), vt, carry, acc, causal)


def _sb_alive(carry):
    return jnp.max(carry) > SB_DEAD


def _select_heads(parts, width):
    lane_head = lax.broadcasted_iota(jnp.int32, parts[0].shape, 1) // width
    out = parts[0]
    for h in range(1, len(parts)):
        out = jnp.where(lane_head == h, parts[h], out)
    return out


def _band_sb_kernel(qa_ref, k0_ref, k1_ref, k2_ref, v0_ref, v1_ref, v2_ref, bias_ref,
                    q_ref, k_ref, v_ref, oa_ref, o_ref, *, tile):
    i = pl.program_id(0)
    kwin = jnp.concatenate([k0_ref[...], k1_ref[...], k2_ref[...]], axis=0)
    vwin = jnp.concatenate([v0_ref[...], v1_ref[...], v2_ref[...]], axis=0)
    in_window = lax.broadcasted_iota(jnp.int32, (tile, 3 * tile), 1) >= (2 - i) * tile
    band = []
    for h, qm in enumerate(_head_masked(qa_ref[...], HEAD_DIM)):
        s = jnp.where(in_window, _dot_nt(qm, kwin) + bias_ref[h], NEG)
        m = jnp.max(s, axis=1, keepdims=True)
        p = jnp.exp2(s - m)
        l = jnp.sum(p, axis=1, keepdims=True)
        band.append(_dot(p.astype(BF16), vwin) / l)
    oa_ref[...] = _select_heads(band, HEAD_DIM).astype(BF16)

    row = lax.broadcasted_iota(jnp.int32, (tile, tile), 0)
    col = lax.broadcasted_iota(jnp.int32, (tile, tile), 1)
    causal = col < row
    ones_ut = _strict_upper_ones(tile)
    width = q_ref.shape[1]
    qms = _head_masked(q_ref[...], HEAD_DIM)
    rows = lambda ref, kb: ref[pl.ds(pl.multiple_of(kb * tile, tile), tile), :]

    tiles = [jnp.maximum(i - d, 0) for d in range(SB_ALWAYS)]
    terms = [[_sb_terms(qm, rows(k_ref, kb), ones_ut, causal if d == 0 else None) for qm in qms]
             for d, kb in enumerate(tiles)]
    values = [rows(v_ref, kb) for kb in tiles]
    parts = []
    for h, qm in enumerate(qms):
        carry, acc = jnp.zeros((tile, 1), F32), jnp.zeros((tile, width), F32)
        for d in range(SB_ALWAYS):
            carry_in = carry if d == 0 else jnp.where(i >= d, carry, NEG)
            carry, acc = _sb_apply(terms[d][h], values[d], carry_in, acc, causal if d == 0 else None)
        first = i - SB_ALWAYS
        state = lax.while_loop(
            lambda st: (st[0] <= first) & _sb_alive(st[1]),
            lambda st, qm=qm: (st[0] + 1,) + _sb_tile(qm, rows(k_ref, first - st[0]), rows(v_ref, first - st[0]),
                                                      st[1], st[2], ones_ut, None),
            (jnp.int32(0), carry, acc))
        parts.append(state[2])
    o_ref[...] = _select_heads(parts, HEAD_DIM).astype(BF16)


def _band_sb_attention(qa, ka, va, bias, q, k, v, tile):
    n, w = q.shape
    blk = lambda back: pl.BlockSpec((tile, w), lambda i: (jnp.maximum(i - back, 0), 0))
    out = jax.ShapeDtypeStruct((n, w), BF16)
    return pl.pallas_call(
        functools.partial(_band_sb_kernel, tile=tile),
        grid=(n // tile,),
        in_specs=[blk(0), blk(2), blk(1), blk(0), blk(2), blk(1), blk(0), _const_spec(bias.shape),
                  blk(0), _const_spec((n, w)), _const_spec((n, w))],
        out_specs=[blk(0), blk(0)],
        out_shape=[out, out],
        compiler_params=_params(1),
        name="band_sb_attn",
    )(qa, ka, ka, ka, va, va, va, bias, q, k, v)


def _mla_kernel(qt_ref, k_ref, vt_ref, kn_ref, ot_ref, qx_scr, p_scr, acc_scr, l_scr):
    i = pl.program_id(1)
    heads = range(2)
    tk, tq = p_scr.shape[2:]
    per_q = tq // tk
    plain = per_q * i
    key = lax.broadcasted_iota(jnp.int32, (tk, tq), 0)
    qry = lax.broadcasted_iota(jnp.int32, (tk, tq), 1)
    row = lax.broadcasted_iota(jnp.int32, (C_PAD - C_ONE, tq), 0)

    def visible(d):
        return ((d * tk + key) // CHUNK) <= (qry // CHUNK)

    def load_q(shifted):
        k_bound = jnp.max(kn_ref[...], axis=0)
        for e in heads:
            lo = e * C_PAD
            qx_scr[lo:lo + C_ONE, :] = qt_ref[lo:lo + C_ONE, :]
            if shifted:
                bound = qt_ref[lo + C_ONE:lo + C_ONE + 1, :].astype(F32) * (k_bound[:, lo:lo + 1] * NORM_MARGIN)
                tail = jnp.where(row == 0, -bound, 0.0)
            else:
                tail = jnp.zeros(row.shape, F32)
            qx_scr[lo + C_ONE:lo + C_PAD, :] = tail.astype(BF16)

    def scores(kb, e):
        return _dot(k_ref[kb, :, e * C_PAD:(e + 1) * C_PAD], qx_scr[e * C_PAD:(e + 1) * C_PAD, :])

    def values(kb, e, p):
        return _dot(vt_ref[kb, e * C_V:(e + 1) * C_V, :], p)

    def write_out():
        ot_ref[...] = jnp.concatenate([acc_scr[e] / l_scr[e] for e in heads], axis=0).astype(BF16)

    def reset():
        acc_scr[...] = jnp.zeros(acc_scr.shape, F32)
        l_scr[...] = jnp.zeros(l_scr.shape, F32)

    def fast_stage(kb, slot, mask):
        for e in heads:
            s = scores(kb, e)
            if mask is not None:
                s = jnp.where(mask, s, NEG)
            p = jnp.exp2(s)
            l_scr[e] += jnp.sum(p, axis=0, keepdims=True)
            p_scr[slot, e] = p.astype(BF16)
        for e in heads:
            acc_scr[e] += values(kb, e, p_scr[slot, e])

    load_q(True)
    reset()

    def fast_group(g, _):
        for j in range(MLA_UNROLL):
            fast_stage(MLA_UNROLL * g + j, j, None)
        return 0
    groups = plain // MLA_UNROLL
    lax.fori_loop(0, groups, fast_group, 0)

    def fast_single(kb, _):
        fast_stage(kb, 0, None)
        return 0
    lax.fori_loop(MLA_UNROLL * groups, plain, fast_single, 0)
    for d in range(per_q):
        fast_stage(plain + d, (1 + d) % MLA_UNROLL, visible(d))
    denom = jnp.minimum(jnp.min(l_scr[0]), jnp.min(l_scr[1]))
    trusted = denom >= MLA_MIN_DENOM

    @pl.when(trusted)
    def _():
        write_out()

    @pl.when(jnp.logical_not(trusted))
    def _():
        load_q(False)
        reset()

        def step(kb, m):
            out = []
            for e in heads:
                s = jnp.where(visible(kb - plain), scores(kb, e), NEG)
                m_new = jnp.maximum(m[e], jnp.max(s, axis=0, keepdims=True))
                alpha = jnp.exp2(m[e] - m_new)
                p = jnp.exp2(s - m_new)
                l_scr[e] = alpha * l_scr[e] + jnp.sum(p, axis=0, keepdims=True)
                acc_scr[e] = alpha * acc_scr[e] + values(kb, e, p.astype(BF16))
                out.append(m_new)
            return tuple(out)
        lax.fori_loop(0, plain + per_q, step, tuple(jnp.full((1, tq), NEG, F32) for _ in heads))
        write_out()


def _mla_attention(qt, k, vt3, kn, tq):
    n = k.shape[0]
    pairs = H_C // 2
    n_tiles, _, tk = vt3.shape
    assert n_tiles * tk == n and kn.shape == (n_tiles, 1, H_C * C_PAD)
    assert tq % tk == 0 and n % tq == 0
    k3 = k.reshape(n_tiles, tk, H_C * C_PAD)
    return pl.pallas_call(
        _mla_kernel,
        grid=(pairs, n // tq),
        in_specs=[pl.BlockSpec((2 * C_PAD, tq), lambda p, i: (p, i)),
                  pl.BlockSpec((n_tiles, tk, 2 * C_PAD), lambda p, i: (0, 0, p)),
                  pl.BlockSpec((n_tiles, 2 * C_V, tk), lambda p, i: (0, p, 0)),
                  pl.BlockSpec((n_tiles, 1, 2 * C_PAD), lambda p, i: (0, 0, p))],
        out_specs=pl.BlockSpec((2 * C_V, tq), lambda p, i: (p, i)),
        out_shape=jax.ShapeDtypeStruct((H_C * C_V, n), BF16),
        scratch_shapes=[pltpu.VMEM((2 * C_PAD, tq), BF16), pltpu.VMEM((MLA_UNROLL, 2, tk, tq), BF16),
                        pltpu.VMEM((2, C_V, tq), F32), pltpu.VMEM((2, 1, tq), F32)],
        compiler_params=_params(2),
        name="mla_attn",
    )(qt, k3, vt3, kn)


def _sample_attn_kernel(qa_ref, ka_ref, va_ref, cak_ref, cav_ref, biasc_ref, biasn_ref,
                        qb_ref, kb_ref, vb_ref, cbk_ref, cbv_ref,
                        qc_ref, ckvn_ref, krbn_ref, ckv_ref, ckr_ref, wukt_ref, wuvp_ref,
                        ya_ref, yb_ref, yc_ref, s_scr, *, tile):
    ds = qa_ref.shape[1]
    past = cbk_ref.shape[2]

    ka, va, cak, cav = ka_ref[0], va_ref[0], cak_ref[0, 0].astype(BF16), cav_ref[0, 0].astype(BF16)
    parts = []
    for h, qm in enumerate(_head_masked(qa_ref[0], HEAD_DIM)):
        s_c = _dot_nt(qm, cak) + biasc_ref[h]
        s_n = _dot_nt(qm, ka) + biasn_ref[h]
        m = jnp.maximum(jnp.max(s_c, axis=1, keepdims=True), jnp.max(s_n, axis=1, keepdims=True))
        p_c = jnp.exp2(s_c - m)
        p_n = jnp.exp2(s_n - m)
        l = jnp.sum(p_c, axis=1, keepdims=True) + jnp.sum(p_n, axis=1, keepdims=True)
        parts.append((_dot(p_c.astype(BF16), cav) + _dot(p_n.astype(BF16), va)) / l)
    ya_ref[0] = _select_heads(parts, HEAD_DIM).astype(BF16)

    row = lax.broadcasted_iota(jnp.int32, (ds, ds), 0)
    col = lax.broadcasted_iota(jnp.int32, (ds, ds), 1)
    causal = col < row
    ones_new = _strict_upper_ones(ds)
    ones_ut = _strict_upper_ones(tile)
    kb, vb = kb_ref[0], vb_ref[0]
    n_tiles = past // tile
    parts = []
    for qm in _head_masked(qb_ref[0], HEAD_DIM):
        state = _sb_tile(qm, kb, vb, jnp.zeros((ds, 1), F32), jnp.zeros((ds, W_AB), F32),
                         ones_new, causal)

        def cache_tile(st, qm=qm):
            start = pl.multiple_of((n_tiles - 1 - st[0]) * tile, tile)
            return (st[0] + 1,) + _sb_tile(qm, cbk_ref[0, 0, pl.ds(start, tile), :].astype(BF16),
                                           cbv_ref[0, 0, pl.ds(start, tile), :].astype(BF16),
                                           st[1], st[2], ones_ut, None)
        state = lax.while_loop(lambda st: (st[0] < n_tiles) & _sb_alive(st[1]), cache_tile,
                               (jnp.int32(0),) + state)
        parts.append(state[2])
    yb_ref[0] = _select_heads(parts, HEAD_DIM).astype(BF16)

    kv_rank = ckv_ref.shape[3]
    qc = qc_ref[0]
    rope_lanes = lax.broadcasted_iota(jnp.int32, (ds, C_PAD), 1) >= C_NOPE
    q_rows = []
    for h in range(H_C):
        qh = qc[:, h * C_PAD:(h + 1) * C_PAD]
        q_lat = _dot(qh, wukt_ref[h]).astype(BF16)
        q_rows.append(jnp.concatenate([q_lat, jnp.where(rope_lanes, qh, jnp.zeros_like(qh))], axis=1))
    q_all = jnp.concatenate(q_rows, axis=0)

    place = (lax.broadcasted_iota(jnp.int32, (C_ROPE, C_PAD), 1)
             == lax.broadcasted_iota(jnp.int32, (C_ROPE, C_PAD), 0) + C_NOPE).astype(BF16)
    tile_c = SAMPLE_LATENT_TILE if past % SAMPLE_LATENT_TILE == 0 else tile
    cache_tiles = range(past // tile_c)
    latent = lambda j: ckv_ref[0, 0, j * tile_c:(j + 1) * tile_c, :].astype(BF16)

    lat_new = ckvn_ref[0].astype(BF16)
    s_new = _dot_nt(q_all, jnp.concatenate([lat_new, krbn_ref[0].astype(BF16)], axis=1))
    m = jnp.max(s_new, axis=1, keepdims=True)
    for j in cache_tiles:
        k_rope = _dot(ckr_ref[0, 0, j * tile_c:(j + 1) * tile_c, :].astype(BF16), place).astype(BF16)
        s = _dot_nt(q_all, jnp.concatenate([latent(j), k_rope], axis=1))
        s_scr[:, j * tile_c:(j + 1) * tile_c] = s
        m = jnp.maximum(m, jnp.max(s, axis=1, keepdims=True))
    p = jnp.exp2(s_new - m)
    denom = jnp.sum(p, axis=1, keepdims=True)
    o_lat = _dot(p.astype(BF16), lat_new)
    for j in cache_tiles:
        p = jnp.exp2(s_scr[:, j * tile_c:(j + 1) * tile_c] - m)
        denom = denom + jnp.sum(p, axis=1, keepdims=True)
        o_lat = o_lat + _dot(p.astype(BF16), latent(j))
    o_lat = (o_lat / denom).astype(BF16)
    out = jnp.zeros((ds, H_C * C_V), F32)
    for h in range(H_C):
        out = out + _dot(o_lat[h * ds:(h + 1) * ds], wuvp_ref[h])
    yc_ref[0] = out.astype(BF16)


def _sample_attention(new, caches, l, bias_c, bias_n, w_ukt, w_uvp, tile):
    qa, ka, va, qb, kb, vb, qc, ckvn, krbn = new
    cak, cav, cbk, cbv, ckv, ckr = caches
    nb, ds, _ = qa.shape
    past = ckv.shape[2]
    req = lambda a: pl.BlockSpec((1,) + a.shape[1:], lambda b: (b, 0, 0))
    cache = lambda a: pl.BlockSpec((1, 1) + a.shape[2:], lambda b: (l, b, 0, 0))
    ins = [qa, ka, va, cak, cav, bias_c, bias_n, qb, kb, vb, cbk, cbv, qc, ckvn, krbn, ckv, ckr,
           w_ukt, w_uvp]
    specs = [req(a) for a in ins]
    for idx in (3, 4, 10, 11, 15, 16):
        specs[idx] = cache(ins[idx])
    for idx in (5, 6, 17, 18):
        specs[idx] = _const_spec(ins[idx].shape)
    out_shape = [jax.ShapeDtypeStruct((nb, ds, W_AB), BF16), jax.ShapeDtypeStruct((nb, ds, W_AB), BF16),
                 jax.ShapeDtypeStruct((nb, ds, H_C * C_V), BF16)]
    return pl.pallas_call(
        functools.partial(_sample_attn_kernel, tile=tile),
        grid=(nb,),
        in_specs=specs,
        out_specs=[req(s) for s in out_shape],
        out_shape=out_shape,
        scratch_shapes=[pltpu.VMEM((H_C * ds, past), F32)],
        compiler_params=_params(1),
        name="sample_attn",
    )(*ins)


def _rope_tables(pos):
    half = C_ROPE // 2
    freq = ROPE_THETA ** (-jnp.arange(half, dtype=F32) / half)
    ang = pos.astype(F32)[:, None] * freq[None, :]
    cos, sin = jnp.cos(ang), jnp.sin(ang)
    n = pos.shape[0]
    z = lambda w: jnp.zeros((n, w), F32)
    tc = jnp.concatenate([jnp.ones((n, C_NOPE), F32), cos, cos, z(C_PAD - C_NOPE - C_ROPE)], axis=1)
    ts1 = jnp.concatenate([z(C_NOPE + half), sin, z(C_PAD - C_NOPE - C_ROPE)], axis=1)
    ts2 = jnp.concatenate([z(C_NOPE), -sin, z(C_PAD - C_NOPE - half)], axis=1)
    return tc, ts1, ts2


def _band_bias(rel_bias, q0, nq, k0, nk):
    period = 1 << int(np.ceil(np.log2(nq + nk)))
    d = np.arange(period)
    d = np.where(d >= nk, d - period, d)
    idx = np.clip((q0 - k0) - d, -REL_CLIP, REL_CLIP) + REL_CLIP
    g = rel_bias.astype(F32)[:, idx]
    toe = jnp.tile(g, (1, nq))[:, :nq * (period - 1)].reshape(-1, nq, period - 1)[:, :, :nk]
    q_pos, k_pos = q0 + np.arange(nq), k0 + np.arange(nk)
    qc, kc = q_pos[:, None] // CHUNK, k_pos[None, :] // CHUNK
    vis = (kc <= qc) & (kc >= qc - BAND_CHUNKS)
    return jnp.where(jnp.asarray(vis)[None], toe * LOG2E, NEG)


def _cast_kernel(w_ref, o_ref):
    o_ref[...] = w_ref[0].astype(BF16)


def _layer_bf16(w, l):
    _, rows, cols = w.shape
    rb = rows // 4 if rows % 64 == 0 else rows
    return pl.pallas_call(
        _cast_kernel,
        grid=(rows // rb,),
        in_specs=[pl.BlockSpec((1, rb, cols), lambda i: (l, i, 0))],
        out_specs=pl.BlockSpec((rb, cols), lambda i: (i, 0)),
        out_shape=jax.ShapeDtypeStruct((rows, cols), BF16),
        compiler_params=_params(1),
        name="cast_bf16",
    )(w)


def _layer_weights(l, w_in, w_uq, w_uk, w_uv):
    d = w_in.shape[1]
    q_rank, kv_rank = w_uq.shape[1], w_uk.shape[1]
    o = 6 * W_AB + q_rank + kv_rank
    zc = lambda w: jnp.zeros((d, w), w_in.dtype)
    w_main = jnp.concatenate([w_in[l, :, :o], zc(C_NOPE), w_in[l, :, o:o + C_ROPE],
                              zc(C_PAD - C_NOPE - C_ROPE)], axis=1).astype(BF16)
    w_gates = w_in[l, :, o + C_ROPE:].astype(BF16)
    pad_heads = lambda w: jnp.pad(w, ((0, 0), (0, 0), (0, C_PAD - w.shape[2]))).reshape(w.shape[0], -1)
    w_uq_p = pad_heads(w_uq[l]).astype(BF16)
    w_ukv = jnp.concatenate([pad_heads(w_uk[l]), w_uv[l].reshape(kv_rank, -1)], axis=1).astype(BF16)
    w_ukt = jnp.pad(jnp.transpose(w_uk[l], (1, 2, 0)), ((0, 0), (0, C_PAD - C_NOPE), (0, 0))).astype(BF16)
    w_uvp = jnp.einsum('rhd,hg->hrgd', w_uv[l], jnp.eye(H_C, dtype=w_uv.dtype)).reshape(
        H_C, kv_rank, H_C * C_V).astype(BF16)
    return w_main, w_gates, w_uq_p, w_ukv, w_ukt, w_uvp


def kernel(x_prompt, x_sample, cache_a_k, cache_a_v, cache_b_k, cache_b_v, cache_c_kv, cache_c_kr, ffn1_norm_pre, ffn1_norm_post, ffn1_w_gate, ffn1_w_up, ffn1_w_down, mix_norm_pre, mix_norm_post, w_in, cq_norm, ckv_norm, w_uq, w_uk, w_uv, rel_bias_a, w_br_a, w_br_b, w_br_c, w_out, ffn2_norm_pre, ffn2_norm_post, ffn2_w_gate, ffn2_w_up, ffn2_w_down):
    batch, seq, d = x_prompt.shape
    nb, ds, _ = x_sample.shape
    depth = w_in.shape[0]
    past = cache_b_k.shape[2]
    win_cache = cache_a_k.shape[2]
    assert batch == 1, "prompt group is a single sequence"
    tile = 256
    tm = 512 if seq % 512 == 0 else tile
    assert seq % tile == 0 and past % tile == 0 and tile % CHUNK == 0 and 2 * tile >= WIN_A
    assert past % CHUNK + ds <= CHUNK
    ns = nb * ds

    xp = x_prompt.reshape(seq, d)
    xs = x_sample.reshape(ns, d)
    tabs_p = _rope_tables(jnp.arange(seq))
    tabs_s = _rope_tables(jnp.tile(past + jnp.arange(ds), nb))

    row = lambda g, l: g[l][None, :]
    base = 2 * tile

    merged_heads = lambda c: c.reshape(c.shape[:3] + (-1,))
    caches = (merged_heads(cache_a_k), merged_heads(cache_a_v), merged_heads(cache_b_k),
              merged_heads(cache_b_v), cache_c_kv, cache_c_kr)
    states_p, states_s = [], []
    for l in range(depth):
        w_main, w_gates, w_uq_p, w_ukv, w_ukt, w_uvp = _layer_weights(l, w_in, w_uq, w_uk, w_uv)
        ffn1 = (row(ffn1_norm_pre, l), row(ffn1_norm_post, l), _layer_bf16(ffn1_w_gate, l),
                _layer_bf16(ffn1_w_up, l), _layer_bf16(ffn1_w_down, l))
        ffn2 = (row(ffn2_norm_pre, l), row(ffn2_norm_post, l), _layer_bf16(ffn2_w_gate, l),
                _layer_bf16(ffn2_w_up, l), _layer_bf16(ffn2_w_down, l))
        mix_in = (row(mix_norm_pre, l), w_main, row(cq_norm, l), row(ckv_norm, l), w_uq_p, w_ukv)
        mrg = (row(mix_norm_pre, l), row(mix_norm_post, l), w_gates, _layer_bf16(w_br_a, l),
               _layer_bf16(w_br_b, l), _layer_bf16(w_br_c, l), _layer_bf16(w_out, l))
        bias_p = _band_bias(rel_bias_a[l], base, tile, base - 2 * tile, 3 * tile)
        bias_c = _band_bias(rel_bias_a[l], past, ds, past - win_cache, win_cache)
        bias_n = _band_bias(rel_bias_a[l], past, ds, past, ds)

        xp = _ffn(xp, *ffn1, tm)
        (qa, kab, vab, qb, kbb, vbb, ka, va, kb, vb, ckv, krb, qc, kc, vc, kn) = _mixer_in(
            xp, *mix_in, tabs_p, tm, q_transposed=True)
        ya, yb = _band_sb_attention(qa, kab, vab, bias_p, qb, kbb, vbb, tile)
        yc = _mla_attention(qc, kc, vc, kn, 2 * tm if seq % (2 * tm) == 0 else tm)
        xp = _merge(xp, ya, yb, yc, *mrg, tm, yc_transposed=True)
        xp = _ffn(xp, *ffn2, tm)
        win = min(WIN_A, seq)
        heads = lambda t, h: t.reshape(1, t.shape[0], h, t.shape[1] // h)
        states_p.append((heads(ka[-win:], H_A), heads(va[-win:], H_A), heads(kb, H_B), heads(vb, H_B),
                         ckv[None], krb[None, :, C_NOPE:C_NOPE + C_ROPE]))

        xs = _ffn(xs, *ffn1, ns)
        (qa, kab, vab, qb, kbb, vbb, ka, va, kb, vb, ckv, krb, qc, kc, vc, _) = _mixer_in(
            xs, *mix_in, tabs_s, ns)
        per_req = lambda t: t.reshape(nb, ds, t.shape[1])
        new = tuple(per_req(t) for t in (qa, kab, vab, qb, kbb, vbb, qc, ckv, krb))
        ya, yb, yc = _sample_attention(new, caches, l, bias_c, bias_n, w_ukt, w_uvp, tile)
        xs = _merge(xs, ya.reshape(ns, -1), yb.reshape(ns, -1), yc.reshape(ns, -1), *mrg, ns)
        xs = _ffn(xs, *ffn2, ns)
        heads_s = lambda t, h: t.reshape(nb, ds, h, t.shape[1] // h)
        states_s.append((heads_s(ka, H_A), heads_s(va, H_A), heads_s(kb, H_B), heads_s(vb, H_B),
                         ckv.reshape(nb, ds, -1), krb[:, C_NOPE:C_NOPE + C_ROPE].reshape(nb, ds, C_ROPE)))

    stack = lambda states, i: jnp.stack([s[i] for s in states], axis=0)
    return ((xp.reshape(batch, seq, d), xs.reshape(nb, ds, d))
            + tuple(stack(states_p, i) for i in range(6))
            + tuple(stack(states_s, i) for i in range(6)))
```

```python
import functools

import numpy as np
import jax
import jax.numpy as jnp
from jax import lax
from jax.experimental import pallas as pl
from jax.experimental.pallas import tpu as pltpu

F32 = jnp.float32
BF16 = jnp.bfloat16

CHUNK = 64
BAND_CHUNKS = 8
WIN_A = BAND_CHUNKS * CHUNK
HEAD_DIM = 64
H_A = 4
H_B = 4
H_C = 8
C_NOPE = 64
C_ROPE = 32
C_V = 64
REL_CLIP = 128
ROPE_THETA = 10000.0
EPS = 1e-6
N_BRANCH = 3
W_AB = H_A * HEAD_DIM
C_PAD = 128
ATTN_SCALE = HEAD_DIM ** -0.5
MLA_SCALE = (C_NOPE + C_ROPE) ** -0.5
LOG2E = float(np.log2(np.e))
C_ONE = C_NOPE + C_ROPE
NORM_MARGIN = 1.02
MLA_MIN_DENOM = 2.0 ** -60
MLA_UNROLL = 4
SAMPLE_LATENT_TILE = 1024
NEG = -1e30
BAND_SB_SUB = 2
SB_ALWAYS = 2
SB_DEAD = -150.0

LANES = 128
MXU_WIDTH = 256
VMEM_LIMIT = 52 * 1024 * 1024

NT_DIMS = (((1,), (1,)), ((), ()))
TN_DIMS = (((0,), (0,)), ((), ()))


def _rms(x, g):
    return x * lax.rsqrt(jnp.mean(x * x, axis=-1, keepdims=True) + EPS) * g


def _dot(a, b):
    return jnp.dot(a, b, preferred_element_type=F32)


def _dot_nt(a, b):
    return lax.dot_general(a, b, NT_DIMS, preferred_element_type=F32)


def _dot_tn(a, b):
    return lax.dot_general(a, b, TN_DIMS, preferred_element_type=F32)


def _const_spec(shape):
    zeros = (0,) * len(shape)
    return pl.BlockSpec(shape, lambda *_: zeros, pipeline_mode=pl.Buffered(1))


def _row_spec(tm, width):
    return pl.BlockSpec((tm, width), lambda i: (i, 0))


def _params(n_grid):
    return pltpu.CompilerParams(dimension_semantics=("arbitrary",) * n_grid,
                                vmem_limit_bytes=VMEM_LIMIT)


def _ffn_kernel(x_ref, gpre_ref, gpost_ref, wg_ref, wu_ref, wd_ref, o_ref, *, f_cuts):
    x = x_ref[...]
    h = _rms(x, gpre_ref[...]).astype(BF16)
    y = jnp.zeros(x.shape, F32)
    for lo, hi in zip(f_cuts[:-1], f_cuts[1:]):
        sl = slice(lo, hi)
        g = _dot(h, wg_ref[:, sl])
        u = _dot(h, wu_ref[:, sl])
        a = (g * jax.nn.sigmoid(g) * u).astype(BF16)
        y = y + _dot(a, wd_ref[sl, :])
    o_ref[...] = x + 0.5 * _rms(y, gpost_ref[...])


def _ffn(x, g_pre, g_post, w_gate, w_up, w_down, tm):
    n, d = x.shape
    d_ff = w_gate.shape[1]
    cut = (d_ff // 2) // MXU_WIDTH * MXU_WIDTH
    f_cuts = (0, cut, d_ff) if 0 < cut < d_ff else (0, d_ff)
    return pl.pallas_call(
        functools.partial(_ffn_kernel, f_cuts=f_cuts),
        grid=(n // tm,),
        in_specs=[_row_spec(tm, d), _const_spec((1, d)), _const_spec((1, d)),
                  _const_spec((d, d_ff)), _const_spec((d, d_ff)), _const_spec((d_ff, d))],
        out_specs=_row_spec(tm, d),
        out_shape=jax.ShapeDtypeStruct((n, d), F32),
        compiler_params=_params(1),
        name="ffn",
    )(x, g_pre, g_post, w_gate, w_up, w_down)


def _rope(blk, tc, ts1, ts2):
    return blk * tc + pltpu.roll(blk, 16, 1) * ts1 + pltpu.roll(blk, C_PAD - 16, 1) * ts2


def _mixer_in_kernel(x_ref, gpre_ref, win_ref, gcq_ref, gckv_ref, wuq_ref, wukv_ref,
                     tc_ref, ts1_ref, ts2_ref,
                     qa_ref, kab_ref, vab_ref, qb_ref, kbb_ref, vbb_ref,
                     ka_ref, va_ref, kb_ref, vb_ref, ckv_ref, krb_ref,
                     qc_ref, kc_ref, vc_ref, kn_ref, *, q_transposed):
    w = W_AB
    q_rank = gcq_ref.shape[1]
    kv_rank = gckv_ref.shape[1]
    u = _rms(x_ref[...], gpre_ref[...]).astype(BF16)
    o = 6 * w
    lat = _dot(u, win_ref[:, o:])
    cq = lat[:, :q_rank]
    ckv = lat[:, q_rank:q_rank + kv_rank]
    krb = lat[:, q_rank + kv_rank:q_rank + kv_rank + C_PAD]
    proj = _dot(u, win_ref[:, :o])
    qa, ka, va = proj[:, 0:w], proj[:, w:2 * w], proj[:, 2 * w:3 * w]
    qb, kb, vb = proj[:, 3 * w:4 * w], proj[:, 4 * w:5 * w], proj[:, 5 * w:6 * w]
    qa_ref[...] = (qa * (ATTN_SCALE * LOG2E)).astype(BF16)
    qb_ref[...] = (qb * (ATTN_SCALE * LOG2E)).astype(BF16)
    ka_ref[...] = ka
    va_ref[...] = va
    kb_ref[...] = kb
    vb_ref[...] = vb
    kab_ref[...] = ka.astype(BF16)
    vab_ref[...] = va.astype(BF16)
    kbb_ref[...] = kb.astype(BF16)
    vbb_ref[...] = vb.astype(BF16)
    tc, ts1, ts2 = tc_ref[...], ts1_ref[...], ts2_ref[...]
    cqn = _rms(cq, gcq_ref[...]).astype(BF16)
    qall = _dot(cqn, wuq_ref[...])
    ckvn = _rms(ckv, gckv_ref[...])
    ckv_ref[...] = ckvn
    kv = _dot(ckvn.astype(BF16), wukv_ref[...])
    krot = _rope(krb, tc, ts1, ts2)
    krb_ref[...] = krot
    one_lane = (lax.broadcasted_iota(jnp.int32, (1, C_PAD), 1) == C_ONE).astype(F32)
    for h in range(H_C):
        sl = slice(h * C_PAD, (h + 1) * C_PAD)
        qh = _rope(qall[:, sl], tc, ts1, ts2) * (MLA_SCALE * LOG2E)
        if q_transposed:
            qt = qh.T
            norm = jnp.sqrt(jnp.sum(qt * qt, axis=0, keepdims=True)) * NORM_MARGIN
            row = lax.broadcasted_iota(jnp.int32, qt.shape, 0)
            qc_ref[sl, :] = jnp.where(row == C_ONE, norm, qt).astype(BF16)
        else:
            qc_ref[:, sl] = qh.astype(BF16)
        kh = kv[:, sl] + krot
        kc_ref[:, sl] = (kh + one_lane).astype(BF16)
        k_sq = _dot((kh * kh).astype(BF16), jnp.ones((C_PAD, C_PAD), BF16))
        kn_ref[0, :, sl] = jnp.sqrt(jnp.max(k_sq, axis=0, keepdims=True)) * NORM_MARGIN
    if not q_transposed:
        vc_ref[...] = kv[:, H_C * C_PAD:].astype(BF16)
    else:
        for p in range(H_C // 2):
            o = H_C * C_PAD + p * 2 * C_V
            vc_ref[0, p * 2 * C_V:(p + 1) * 2 * C_V, :] = kv[:, o:o + 2 * C_V].T.astype(BF16)


def _mixer_in(x, g_pre, w_main, g_cq, g_ckv, w_uq, w_ukv, tabs, tm, q_transposed=False):
    n, d = x.shape
    q_rank, kv_rank = g_cq.shape[1], g_ckv.shape[1]
    wc = H_C * C_PAD
    wv = H_C * C_V
    bf = lambda width: jax.ShapeDtypeStruct((n, width), BF16)
    f32 = lambda width: jax.ShapeDtypeStruct((n, width), F32)
    q_shape = jax.ShapeDtypeStruct((wc, n), BF16) if q_transposed else bf(wc)
    q_spec = pl.BlockSpec((wc, tm), lambda i: (0, i)) if q_transposed else _row_spec(tm, wc)
    v_shape = jax.ShapeDtypeStruct((n // tm, wv, tm), BF16) if q_transposed else bf(wv)
    v_spec = (pl.BlockSpec((1, wv, tm), lambda i: (i, 0, 0)) if q_transposed
              else _row_spec(tm, wv))
    out_shape = ([bf(W_AB)] * 6 + [f32(W_AB)] * 4 + [f32(kv_rank), f32(C_PAD)]
                 + [q_shape, bf(wc), v_shape, jax.ShapeDtypeStruct((n // tm, 1, wc), F32)])
    out_specs = ([_row_spec(tm, W_AB)] * 10 + [_row_spec(tm, kv_rank), _row_spec(tm, C_PAD)]
                 + [q_spec, _row_spec(tm, wc), v_spec, pl.BlockSpec((1, 1, wc), lambda i: (i, 0, 0))])
    return pl.pallas_call(
        functools.partial(_mixer_in_kernel, q_transposed=q_transposed),
        grid=(n // tm,),
        in_specs=[_row_spec(tm, d), _const_spec((1, d)), _const_spec(w_main.shape),
                  _const_spec((1, q_rank)), _const_spec((1, kv_rank)),
                  _const_spec(w_uq.shape), _const_spec(w_ukv.shape),
                  _row_spec(tm, C_PAD), _row_spec(tm, C_PAD), _row_spec(tm, C_PAD)],
        out_specs=out_specs,
        out_shape=out_shape,
        compiler_params=_params(1),
        name="mixer_in",
    )(x, g_pre, w_main, g_cq, g_ckv, w_uq, w_ukv, *tabs)


def _merge_kernel(x_ref, ya_ref, yb_ref, yc_ref, gpre_ref, gpost_ref, wgates_ref,
                  wa_ref, wb_ref, wc_ref, wout_ref, o_ref, *, yc_transposed):
    x = x_ref[...]
    d = x.shape[1]
    u = _rms(x, gpre_ref[...]).astype(BF16)
    m = jnp.zeros(x.shape, F32)
    for i, (y_ref, w_ref) in enumerate(((ya_ref, wa_ref), (yb_ref, wb_ref), (yc_ref, wc_ref))):
        gate = jax.nn.sigmoid(_dot(u, wgates_ref[:, i * d:(i + 1) * d]))
        branch_dot = _dot_tn if (yc_transposed and y_ref is yc_ref) else _dot
        m = m + gate * branch_dot(y_ref[...], w_ref[...])
    mixed = _dot(m.astype(BF16), wout_ref[...])
    o_ref[...] = x + _rms(mixed, gpost_ref[...])


def _merge(x, ya, yb, yc, g_pre, g_post, w_gates, w_a, w_b, w_c, w_out, tm, yc_transposed=False):
    n, d = x.shape
    yc_spec = (pl.BlockSpec((yc.shape[0], tm), lambda i: (0, i)) if yc_transposed
               else _row_spec(tm, yc.shape[1]))
    return pl.pallas_call(
        functools.partial(_merge_kernel, yc_transposed=yc_transposed),
        grid=(n // tm,),
        in_specs=[_row_spec(tm, d), _row_spec(tm, ya.shape[1]), _row_spec(tm, yb.shape[1]),
                  yc_spec, _const_spec((1, d)), _const_spec((1, d)),
                  _const_spec(w_gates.shape), _const_spec(w_a.shape), _const_spec(w_b.shape),
                  _const_spec(w_c.shape), _const_spec(w_out.shape)],
        out_specs=_row_spec(tm, d),
        out_shape=jax.ShapeDtypeStruct((n, d), F32),
        compiler_params=_params(1),
        name="merge",
    )(x, ya, yb, yc, g_pre, g_post, w_gates, w_a, w_b, w_c, w_out)


def _head_masked(q, width):
    lane_head = lax.broadcasted_iota(jnp.int32, q.shape, 1) // width
    zero = jnp.zeros_like(q)
    return [jnp.where(lane_head == h, q, zero) for h in range(q.shape[1] // width)]


def _strict_upper_ones(n):
    j = lax.broadcasted_iota(jnp.int32, (n, n), 0)
    s = lax.broadcasted_iota(jnp.int32, (n, n), 1)
    return (j > s).astype(BF16)


def _sb_terms(qm, kt, ones_ut, causal):
    z = _dot_nt(qm, kt)
    t = jnp.log2(1.0 + jnp.exp2(-jnp.abs(z)))
    log_keep = -jnp.maximum(z, 0.0) - t
    log_beta = jnp.minimum(z, 0.0) - t
    if causal is not None:
        log_keep = jnp.where(causal, log_keep, 0.0)
    hi = log_keep.astype(BF16)
    lo = (log_keep - hi.astype(F32)).astype(BF16)
    tail = _dot(hi, ones_ut) + _dot(lo, ones_ut)
    return log_beta + tail, jnp.sum(log_keep, axis=1, keepdims=True)


def _sb_apply(terms, vt, carry, acc, causal):
    logw, total = terms
    w = jnp.exp2(logw + carry)
    if causal is not None:
        w = jnp.where(causal, w, 0.0)
    return carry + total, acc + _dot(w.astype(BF16), vt)


def _sb_tile(qm, kt, vt, carry, acc, ones_ut, causal):
    return _sb_apply(_sb_terms(qm, kt, ones_ut, causal), vt, carry, acc, causal)


def _sb_alive(carry):
    return jnp.max(carry) > SB_DEAD


def _select_heads(parts, width):
    lane_head = lax.broadcasted_iota(jnp.int32, parts[0].shape, 1) // width
    out = parts[0]
    for h in range(1, len(parts)):
        out = jnp.where(lane_head == h, parts[h], out)
    return out


def _band_sb_kernel(qa_ref, ka0_ref, ka1_ref, va0_ref, va1_ref, bias_ref, q_ref, k_ref, v_ref,
                    oa_ref, o_ref, *, tile):
    i = pl.program_id(0)
    sub = BAND_SB_SUB
    k_band = jnp.concatenate([ka0_ref[...], ka1_ref[...]], axis=0)
    v_band = jnp.concatenate([va0_ref[...], va1_ref[...]], axis=0)
    col = lax.broadcasted_iota(jnp.int32, (tile, 3 * tile), 1)
    for t in range(sub):
        kwin = k_band[(sub - 2 + t) * tile:(sub + 1 + t) * tile]
        vwin = v_band[(sub - 2 + t) * tile:(sub + 1 + t) * tile]
        in_window = col >= (2 - (sub * i + t)) * tile
        band = []
        for h, qm in enumerate(_head_masked(qa_ref[t * tile:(t + 1) * tile, :], HEAD_DIM)):
            s = jnp.where(in_window, _dot_nt(qm, kwin) + bias_ref[h], NEG)
            m = jnp.max(s, axis=1, keepdims=True)
            p = jnp.exp2(s - m)
            l = jnp.sum(p, axis=1, keepdims=True)
            band.append(_dot(p.astype(BF16), vwin) / l)
        oa_ref[t * tile:(t + 1) * tile, :] = _select_heads(band, HEAD_DIM).astype(BF16)

    row = lax.broadcasted_iota(jnp.int32, (tile, tile), 0)
    col = lax.broadcasted_iota(jnp.int32, (tile, tile), 1)
    causal = col < row
    ones_ut = _strict_upper_ones(tile)
    width = q_ref.shape[1]
    rows = lambda ref, kb: ref[pl.ds(pl.multiple_of(kb * tile, tile), tile), :]

    work = []
    for t in range(sub):
        it = sub * i + t
        qms = _head_masked(q_ref[t * tile:(t + 1) * tile, :], HEAD_DIM)
        tiles = [jnp.maximum(it - d, 0) for d in range(SB_ALWAYS)]
        terms = [[_sb_terms(qm, rows(k_ref, kb), ones_ut, causal if d == 0 else None) for qm in qms]
                 for d, kb in enumerate(tiles)]
        values = [rows(v_ref, kb) for kb in tiles]
        states = []
        for h in range(len(qms)):
            carry, acc = jnp.zeros((tile, 1), F32), jnp.zeros((tile, width), F32)
            for d in range(SB_ALWAYS):
                carry_in = carry if d == 0 else jnp.where(it >= d, carry, NEG)
                carry, acc = _sb_apply(terms[d][h], values[d], carry_in, acc, causal if d == 0 else None)
            states.append((carry, acc))
        work.append((it, qms, states))
    for t, (it, qms, states) in enumerate(work):
        first = it - SB_ALWAYS
        parts = []
        for qm, (carry, acc) in zip(qms, states):
            state = lax.while_loop(
                lambda st, first=first: (st[0] <= first) & _sb_alive(st[1]),
                lambda st, qm=qm, first=first: (st[0] + 1,) + _sb_tile(
                    qm, rows(k_ref, first - st[0]), rows(v_ref, first - st[0]), st[1], st[2], ones_ut, None),
                (jnp.int32(0), carry, acc))
            parts.append(state[2])
        o_ref[t * tile:(t + 1) * tile, :] = _select_heads(parts, HEAD_DIM).astype(BF16)


def _band_sb_attention(qa, ka, va, bias, q, k, v, tile):
    n, w = q.shape
    rows = BAND_SB_SUB * tile
    assert BAND_SB_SUB >= 2 and n % rows == 0
    blk = lambda back: pl.BlockSpec((rows, w), lambda i: (jnp.maximum(i - back, 0), 0))
    out = jax.ShapeDtypeStruct((n, w), BF16)
    return pl.pallas_call(
        functools.partial(_band_sb_kernel, tile=tile),
        grid=(n // rows,),
        in_specs=[blk(0), blk(1), blk(0), blk(1), blk(0), _const_spec(bias.shape),
                  blk(0), _const_spec((n, w)), _const_spec((n, w))],
        out_specs=[blk(0), blk(0)],
        out_shape=[out, out],
        compiler_params=_params(1),
        name="band_sb_attn",
    )(qa, ka, ka, va, va, bias, q, k, v)


def _mla_kernel(qt_ref, k_ref, vt_ref, kn_ref, ot_ref, qx_scr, p_scr, acc_scr, l_scr):
    i = pl.program_id(1)
    heads = range(2)
    tk, tq = p_scr.shape[2:]
    per_q = tq // tk
    plain = per_q * i
    key = lax.broadcasted_iota(jnp.int32, (tk, tq), 0)
    qry = lax.broadcasted_iota(jnp.int32, (tk, tq), 1)
    row = lax.broadcasted_iota(jnp.int32, (C_PAD - C_ONE, tq), 0)

    def visible(d):
        return ((d * tk + key) // CHUNK) <= (qry // CHUNK)

    def load_q(shifted):
        k_bound = jnp.max(kn_ref[...], axis=0)
        for e in heads:
            lo = e * C_PAD
            qx_scr[lo:lo + C_ONE, :] = qt_ref[lo:lo + C_ONE, :]
            if shifted:
                bound = qt_ref[lo + C_ONE:lo + C_ONE + 1, :].astype(F32) * (k_bound[:, lo:lo + 1] * NORM_MARGIN)
                tail = jnp.where(row == 0, -bound, 0.0)
            else:
                tail = jnp.zeros(row.shape, F32)
            qx_scr[lo + C_ONE:lo + C_PAD, :] = tail.astype(BF16)

    def scores(kb, e):
        return _dot(k_ref[kb, :, e * C_PAD:(e + 1) * C_PAD], qx_scr[e * C_PAD:(e + 1) * C_PAD, :])

    def values(kb, e, p):
        return _dot(vt_ref[kb, e * C_V:(e + 1) * C_V, :], p)

    def write_out():
        ot_ref[...] = jnp.concatenate([acc_scr[e] / l_scr[e] for e in heads], axis=0).astype(BF16)

    def reset():
        acc_scr[...] = jnp.zeros(acc_scr.shape, F32)
        l_scr[...] = jnp.zeros(l_scr.shape, F32)

    def fast_stage(kb, slot, mask):
        for e in heads:
            s = scores(kb, e)
            if mask is not None:
                s = jnp.where(mask, s, NEG)
            p = jnp.exp2(s)
            l_scr[e] += jnp.sum(p, axis=0, keepdims=True)
            p_scr[slot, e] = p.astype(BF16)
        for e in heads:
            acc_scr[e] += values(kb, e, p_scr[slot, e])

    load_q(True)
    reset()

    def fast_group(g, _):
        for j in range(MLA_UNROLL):
            fast_stage(MLA_UNROLL * g + j, j, None)
        return 0
    groups = plain // MLA_UNROLL
    lax.fori_loop(0, groups, fast_group, 0)

    def fast_single(kb, _):
        fast_stage(kb, 0, None)
        return 0
    lax.fori_loop(MLA_UNROLL * groups, plain, fast_single, 0)
    for d in range(per_q):
        fast_stage(plain + d, (1 + d) % MLA_UNROLL, visible(d))
    denom = jnp.minimum(jnp.min(l_scr[0]), jnp.min(l_scr[1]))
    trusted = denom >= MLA_MIN_DENOM

    @pl.when(trusted)
    def _():
        write_out()

    @pl.when(jnp.logical_not(trusted))
    def _():
        load_q(False)
        reset()

        def step(kb, m):
            out = []
            for e in heads:
                s = jnp.where(visible(kb - plain), scores(kb, e), NEG)
                m_new = jnp.maximum(m[e], jnp.max(s, axis=0, keepdims=True))
                alpha = jnp.exp2(m[e] - m_new)
                p = jnp.exp2(s - m_new)
                l_scr[e] = alpha * l_scr[e] + jnp.sum(p, axis=0, keepdims=True)
                acc_scr[e] = alpha * acc_scr[e] + values(kb, e, p.astype(BF16))
                out.append(m_new)
            return tuple(out)
        lax.fori_loop(0, plain + per_q, step, tuple(jnp.full((1, tq), NEG, F32) for _ in heads))
        write_out()


def _mla_attention(qt, k, vt3, kn, tq):
    n = k.shape[0]
    pairs = H_C // 2
    n_tiles, _, tk = vt3.shape
    assert n_tiles * tk == n and kn.shape == (n_tiles, 1, H_C * C_PAD)
    assert tq % tk == 0 and n % tq == 0
    k3 = k.reshape(n_tiles, tk, H_C * C_PAD)
    return pl.pallas_call(
        _mla_kernel,
        grid=(pairs, n // tq),
        in_specs=[pl.BlockSpec((2 * C_PAD, tq), lambda p, i: (p, i)),
                  pl.BlockSpec((n_tiles, tk, 2 * C_PAD), lambda p, i: (0, 0, p)),
                  pl.BlockSpec((n_tiles, 2 * C_V, tk), lambda p, i: (0, p, 0)),
                  pl.BlockSpec((n_tiles, 1, 2 * C_PAD), lambda p, i: (0, 0, p))],
        out_specs=pl.BlockSpec((2 * C_V, tq), lambda p, i: (p, i)),
        out_shape=jax.ShapeDtypeStruct((H_C * C_V, n), BF16),
        scratch_shapes=[pltpu.VMEM((2 * C_PAD, tq), BF16), pltpu.VMEM((MLA_UNROLL, 2, tk, tq), BF16),
                        pltpu.VMEM((2, C_V, tq), F32), pltpu.VMEM((2, 1, tq), F32)],
        compiler_params=_params(2),
        name="mla_attn",
    )(qt, k3, vt3, kn)


def _sample_attn_kernel(qa_ref, ka_ref, va_ref, cak_ref, cav_ref, biasc_ref, biasn_ref,
                        qb_ref, kb_ref, vb_ref, cbk_ref, cbv_ref,
                        qc_ref, ckvn_ref, krbn_ref, ckv_ref, ckr_ref, wukt_ref, wuvp_ref,
                        ya_ref, yb_ref, yc_ref, s_scr, *, tile):
    ds = qa_ref.shape[1]
    past = cbk_ref.shape[2]

    ka, va, cak, cav = ka_ref[0], va_ref[0], cak_ref[0, 0].astype(BF16), cav_ref[0, 0].astype(BF16)
    parts = []
    for h, qm in enumerate(_head_masked(qa_ref[0], HEAD_DIM)):
        s_c = _dot_nt(qm, cak) + biasc_ref[h]
        s_n = _dot_nt(qm, ka) + biasn_ref[h]
        m = jnp.maximum(jnp.max(s_c, axis=1, keepdims=True), jnp.max(s_n, axis=1, keepdims=True))
        p_c = jnp.exp2(s_c - m)
        p_n = jnp.exp2(s_n - m)
        l = jnp.sum(p_c, axis=1, keepdims=True) + jnp.sum(p_n, axis=1, keepdims=True)
        parts.append((_dot(p_c.astype(BF16), cav) + _dot(p_n.astype(BF16), va)) / l)
    ya_ref[0] = _select_heads(parts, HEAD_DIM).astype(BF16)

    row = lax.broadcasted_iota(jnp.int32, (ds, ds), 0)
    col = lax.broadcasted_iota(jnp.int32, (ds, ds), 1)
    causal = col < row
    ones_new = _strict_upper_ones(ds)
    ones_ut = _strict_upper_ones(tile)
    kb, vb = kb_ref[0], vb_ref[0]
    n_tiles = past // tile
    parts = []
    for qm in _head_masked(qb_ref[0], HEAD_DIM):
        state = _sb_tile(qm, kb, vb, jnp.zeros((ds, 1), F32), jnp.zeros((ds, W_AB), F32),
                         ones_new, causal)

        def cache_tile(st, qm=qm):
            start = pl.multiple_of((n_tiles - 1 - st[0]) * tile, tile)
            return (st[0] + 1,) + _sb_tile(qm, cbk_ref[0, 0, pl.ds(start, tile), :].astype(BF16),
                                           cbv_ref[0, 0, pl.ds(start, tile), :].astype(BF16),
                                           st[1], st[2], ones_ut, None)
        state = lax.while_loop(lambda st: (st[0] < n_tiles) & _sb_alive(st[1]), cache_tile,
                               (jnp.int32(0),) + state)
        parts.append(state[2])
    yb_ref[0] = _select_heads(parts, HEAD_DIM).astype(BF16)

    kv_rank = ckv_ref.shape[3]
    qc = qc_ref[0]
    rope_lanes = lax.broadcasted_iota(jnp.int32, (ds, C_PAD), 1) >= C_NOPE
    q_rows = []
    for h in range(H_C):
        qh = qc[:, h * C_PAD:(h + 1) * C_PAD]
        q_lat = _dot(qh, wukt_ref[h]).astype(BF16)
        q_rows.append(jnp.concatenate([q_lat, jnp.where(rope_lanes, qh, jnp.zeros_like(qh))], axis=1))
    q_all = jnp.concatenate(q_rows, axis=0)

    place = (lax.broadcasted_iota(jnp.int32, (C_ROPE, C_PAD), 1)
             == lax.broadcasted_iota(jnp.int32, (C_ROPE, C_PAD), 0) + C_NOPE).astype(BF16)
    tile_c = SAMPLE_LATENT_TILE if past % SAMPLE_LATENT_TILE == 0 else tile
    cache_tiles = range(past // tile_c)
    latent = lambda j: ckv_ref[0, 0, j * tile_c:(j + 1) * tile_c, :].astype(BF16)

    lat_new = ckvn_ref[0].astype(BF16)
    s_new = _dot_nt(q_all, jnp.concatenate([lat_new, krbn_ref[0].astype(BF16)], axis=1))
    m = jnp.max(s_new, axis=1, keepdims=True)
    for j in cache_tiles:
        k_rope = _dot(ckr_ref[0, 0, j * tile_c:(j + 1) * tile_c, :].astype(BF16), place).astype(BF16)
        s = _dot_nt(q_all, jnp.concatenate([latent(j), k_rope], axis=1))
        s_scr[:, j * tile_c:(j + 1) * tile_c] = s
        m = jnp.maximum(m, jnp.max(s, axis=1, keepdims=True))
    p = jnp.exp2(s_new - m)
    denom = jnp.sum(p, axis=1, keepdims=True)
    o_lat = _dot(p.astype(BF16), lat_new)
    for j in cache_tiles:
        p = jnp.exp2(s_scr[:, j * tile_c:(j + 1) * tile_c] - m)
        denom = denom + jnp.sum(p, axis=1, keepdims=True)
        o_lat = o_lat + _dot(p.astype(BF16), latent(j))
    o_lat = (o_lat / denom).astype(BF16)
    out = jnp.zeros((ds, H_C * C_V), F32)
    for h in range(H_C):
        out = out + _dot(o_lat[h * ds:(h + 1) * ds], wuvp_ref[h])
    yc_ref[0] = out.astype(BF16)


def _sample_attention(new, caches, l, bias_c, bias_n, w_ukt, w_uvp, tile):
    qa, ka, va, qb, kb, vb, qc, ckvn, krbn = new
    cak, cav, cbk, cbv, ckv, ckr = caches
    nb, ds, _ = qa.shape
    past = ckv.shape[2]
    req = lambda a: pl.BlockSpec((1,) + a.shape[1:], lambda b: (b, 0, 0))
    cache = lambda a: pl.BlockSpec((1, 1) + a.shape[2:], lambda b: (l, b, 0, 0))
    ins = [qa, ka, va, cak, cav, bias_c, bias_n, qb, kb, vb, cbk, cbv, qc, ckvn, krbn, ckv, ckr,
           w_ukt, w_uvp]
    specs = [req(a) for a in ins]
    for idx in (3, 4, 10, 11, 15, 16):
        specs[idx] = cache(ins[idx])
    for idx in (5, 6, 17, 18):
        specs[idx] = _const_spec(ins[idx].shape)
    out_shape = [jax.ShapeDtypeStruct((nb, ds, W_AB), BF16), jax.ShapeDtypeStruct((nb, ds, W_AB), BF16),
                 jax.ShapeDtypeStruct((nb, ds, H_C * C_V), BF16)]
    return pl.pallas_call(
        functools.partial(_sample_attn_kernel, tile=tile),
        grid=(nb,),
        in_specs=specs,
        out_specs=[req(s) for s in out_shape],
        out_shape=out_shape,
        scratch_shapes=[pltpu.VMEM((H_C * ds, past), F32)],
        compiler_params=_params(1),
        name="sample_attn",
    )(*ins)


def _rope_tables(pos):
    half = C_ROPE // 2
    freq = ROPE_THETA ** (-jnp.arange(half, dtype=F32) / half)
    ang = pos.astype(F32)[:, None] * freq[None, :]
    cos, sin = jnp.cos(ang), jnp.sin(ang)
    n = pos.shape[0]
    z = lambda w: jnp.zeros((n, w), F32)
    tc = jnp.concatenate([jnp.ones((n, C_NOPE), F32), cos, cos, z(C_PAD - C_NOPE - C_ROPE)], axis=1)
    ts1 = jnp.concatenate([z(C_NOPE + half), sin, z(C_PAD - C_NOPE - C_ROPE)], axis=1)
    ts2 = jnp.concatenate([z(C_NOPE), -sin, z(C_PAD - C_NOPE - half)], axis=1)
    return tc, ts1, ts2


def _band_bias(rel_bias, q0, nq, k0, nk):
    period = 1 << int(np.ceil(np.log2(nq + nk)))
    d = np.arange(period)
    d = np.where(d >= nk, d - period, d)
    idx = np.clip((q0 - k0) - d, -REL_CLIP, REL_CLIP) + REL_CLIP
    g = rel_bias.astype(F32)[:, idx]
    toe = jnp.tile(g, (1, nq))[:, :nq * (period - 1)].reshape(-1, nq, period - 1)[:, :, :nk]
    q_pos, k_pos = q0 + np.arange(nq), k0 + np.arange(nk)
    qc, kc = q_pos[:, None] // CHUNK, k_pos[None, :] // CHUNK
    vis = (kc <= qc) & (kc >= qc - BAND_CHUNKS)
    return jnp.where(jnp.asarray(vis)[None], toe * LOG2E, NEG)


def _cast_kernel(w_ref, o_ref):
    o_ref[...] = w_ref[0].astype(BF16)


def _layer_bf16(w, l):
    _, rows, cols = w.shape
    rb = rows // 4 if rows % 64 == 0 else rows
    return pl.pallas_call(
        _cast_kernel,
        grid=(rows // rb,),
        in_specs=[pl.BlockSpec((1, rb, cols), lambda i: (l, i, 0))],
        out_specs=pl.BlockSpec((rb, cols), lambda i: (i, 0)),
        out_shape=jax.ShapeDtypeStruct((rows, cols), BF16),
        compiler_params=_params(1),
        name="cast_bf16",
    )(w)


def _layer_weights(l, w_in, w_uq, w_uk, w_uv):
    d = w_in.shape[1]
    q_rank, kv_rank = w_uq.shape[1], w_uk.shape[1]
    o = 6 * W_AB + q_rank + kv_rank
    zc = lambda w: jnp.zeros((d, w), w_in.dtype)
    w_main = jnp.concatenate([w_in[l, :, :o], zc(C_NOPE), w_in[l, :, o:o + C_ROPE],
                              zc(C_PAD - C_NOPE - C_ROPE)], axis=1).astype(BF16)
    w_gates = w_in[l, :, o + C_ROPE:].astype(BF16)
    pad_heads = lambda w: jnp.pad(w, ((0, 0), (0, 0), (0, C_PAD - w.shape[2]))).reshape(w.shape[0], -1)
    w_uq_p = pad_heads(w_uq[l]).astype(BF16)
    w_ukv = jnp.concatenate([pad_heads(w_uk[l]), w_uv[l].reshape(kv_rank, -1)], axis=1).astype(BF16)
    w_ukt = jnp.pad(jnp.transpose(w_uk[l], (1, 2, 0)), ((0, 0), (0, C_PAD - C_NOPE), (0, 0))).astype(BF16)
    w_uvp = jnp.einsum('rhd,hg->hrgd', w_uv[l], jnp.eye(H_C, dtype=w_uv.dtype)).reshape(
        H_C, kv_rank, H_C * C_V).astype(BF16)
    return w_main, w_gates, w_uq_p, w_ukv, w_ukt, w_uvp


def kernel(x_prompt, x_sample, cache_a_k, cache_a_v, cache_b_k, cache_b_v, cache_c_kv, cache_c_kr, ffn1_norm_pre, ffn1_norm_post, ffn1_w_gate, ffn1_w_up, ffn1_w_down, mix_norm_pre, mix_norm_post, w_in, cq_norm, ckv_norm, w_uq, w_uk, w_uv, rel_bias_a, w_br_a, w_br_b, w_br_c, w_out, ffn2_norm_pre, ffn2_norm_post, ffn2_w_gate, ffn2_w_up, ffn2_w_down):
    batch, seq, d = x_prompt.shape
    nb, ds, _ = x_sample.shape
    depth = w_in.shape[0]
    past = cache_b_k.shape[2]
    win_cache = cache_a_k.shape[2]
    assert batch == 1, "prompt group is a single sequence"
    tile = 256
    tm = 512 if seq % 512 == 0 else tile
    assert seq % tile == 0 and past % tile == 0 and tile % CHUNK == 0 and 2 * tile >= WIN_A
    assert past % CHUNK + ds <= CHUNK
    ns = nb * ds

    xp = x_prompt.reshape(seq, d)
    xs = x_sample.reshape(ns, d)
    tabs_p = _rope_tables(jnp.arange(seq))
    tabs_s = _rope_tables(jnp.tile(past + jnp.arange(ds), nb))

    row = lambda g, l: g[l][None, :]
    base = 2 * tile

    merged_heads = lambda c: c.reshape(c.shape[:3] + (-1,))
    caches = (merged_heads(cache_a_k), merged_heads(cache_a_v), merged_heads(cache_b_k),
              merged_heads(cache_b_v), cache_c_kv, cache_c_kr)
    states_p, states_s = [], []
    for l in range(depth):
        w_main, w_gates, w_uq_p, w_ukv, w_ukt, w_uvp = _layer_weights(l, w_in, w_uq, w_uk, w_uv)
        ffn1 = (row(ffn1_norm_pre, l), row(ffn1_norm_post, l), _layer_bf16(ffn1_w_gate, l),
                _layer_bf16(ffn1_w_up, l), _layer_bf16(ffn1_w_down, l))
        ffn2 = (row(ffn2_norm_pre, l), row(ffn2_norm_post, l), _layer_bf16(ffn2_w_gate, l),
                _layer_bf16(ffn2_w_up, l), _layer_bf16(ffn2_w_down, l))
        mix_in = (row(mix_norm_pre, l), w_main, row(cq_norm, l), row(ckv_norm, l), w_uq_p, w_ukv)
        mrg = (row(mix_norm_pre, l), row(mix_norm_post, l), w_gates, _layer_bf16(w_br_a, l),
               _layer_bf16(w_br_b, l), _layer_bf16(w_br_c, l), _layer_bf16(w_out, l))
        bias_p = _band_bias(rel_bias_a[l], base, tile, base - 2 * tile, 3 * tile)
        bias_c = _band_bias(rel_bias_a[l], past, ds, past - win_cache, win_cache)
        bias_n = _band_bias(rel_bias_a[l], past, ds, past, ds)

        xp = _ffn(xp, *ffn1, tm)
        (qa, kab, vab, qb, kbb, vbb, ka, va, kb, vb, ckv, krb, qc, kc, vc, kn) = _mixer_in(
            xp, *mix_in, tabs_p, tm, q_transposed=True)
        ya, yb = _band_sb_attention(qa, kab, vab, bias_p, qb, kbb, vbb, tile)
        yc = _mla_attention(qc, kc, vc, kn, 2 * tm if seq % (2 * tm) == 0 else tm)
        xp = _merge(xp, ya, yb, yc, *mrg, tm, yc_transposed=True)
        xp = _ffn(xp, *ffn2, tm)
        win = min(WIN_A, seq)
        heads = lambda t, h: t.reshape(1, t.shape[0], h, t.shape[1] // h)
        states_p.append((heads(ka[-win:], H_A), heads(va[-win:], H_A), heads(kb, H_B), heads(vb, H_B),
                         ckv[None], krb[None, :, C_NOPE:C_NOPE + C_ROPE]))

        xs = _ffn(xs, *ffn1, ns)
        (qa, kab, vab, qb, kbb, vbb, ka, va, kb, vb, ckv, krb, qc, kc, vc, _) = _mixer_in(
            xs, *mix_in, tabs_s, ns)
        per_req = lambda t: t.reshape(nb, ds, t.shape[1])
        new = tuple(per_req(t) for t in (qa, kab, vab, qb, kbb, vbb, qc, ckv, krb))
        ya, yb, yc = _sample_attention(new, caches, l, bias_c, bias_n, w_ukt, w_uvp, tile)
        xs = _merge(xs, ya.reshape(ns, -1), yb.reshape(ns, -1), yc.reshape(ns, -1), *mrg, ns)
        xs = _ffn(xs, *ffn2, ns)
        heads_s = lambda t, h: t.reshape(nb, ds, h, t.shape[1] // h)
        states_s.append((heads_s(ka, H_A), heads_s(va, H_A), heads_s(kb, H_B), heads_s(vb, H_B),
                         ckv.reshape(nb, ds, -1), krb[:, C_NOPE:C_NOPE + C_ROPE].reshape(nb, ds, C_ROPE)))

    stack = lambda states, i: jnp.stack([s[i] for s in states], axis=0)
    return ((xp.reshape(batch, seq, d), xs.reshape(nb, ds, d))
            + tuple(stack(states_p, i) for i in range(6))
            + tuple(stack(states_s, i) for i in range(6)))
```

```python
import functools

import numpy as np
import jax
import jax.numpy as jnp
from jax import lax
from jax.experimental import pallas as pl
from jax.experimental.pallas import tpu as pltpu

F32 = jnp.float32
BF16 = jnp.bfloat16

CHUNK = 64
BAND_CHUNKS = 8
WIN_A = BAND_CHUNKS * CHUNK
HEAD_DIM = 64
H_A = 4
H_B = 4
H_C = 8
C_NOPE = 64
C_ROPE = 32
C_V = 64
REL_CLIP = 128
ROPE_THETA = 10000.0
EPS = 1e-6
N_BRANCH = 3
W_AB = H_A * HEAD_DIM
C_PAD = 128
ATTN_SCALE = HEAD_DIM ** -0.5
MLA_SCALE = (C_NOPE + C_ROPE) ** -0.5
LOG2E = float(np.log2(np.e))
C_ONE = C_NOPE + C_ROPE
NORM_MARGIN = 1.02
MLA_MIN_DENOM = 2.0 ** -60
MLA_UNROLL = 4
SAMPLE_LATENT_TILE = 1024
NEG = -1e30
BAND_SB_SUB = 2
SB_ALWAYS = 2
SB_DEAD = -150.0

LANES = 128
MXU_WIDTH = 256
VMEM_LIMIT = 52 * 1024 * 1024

NT_DIMS = (((1,), (1,)), ((), ()))
TN_DIMS = (((0,), (0,)), ((), ()))


def _rms(x, g):
    return x * lax.rsqrt(jnp.mean(x * x, axis=-1, keepdims=True) + EPS) * g


def _dot(a, b):
    return jnp.dot(a, b, preferred_element_type=F32)


def _dot_nt(a, b):
    return lax.dot_general(a, b, NT_DIMS, preferred_element_type=F32)


def _dot_tn(a, b):
    return lax.dot_general(a, b, TN_DIMS, preferred_element_type=F32)


def _const_spec(shape):
    zeros = (0,) * len(shape)
    return pl.BlockSpec(shape, lambda *_: zeros, pipeline_mode=pl.Buffered(1))


def _row_spec(tm, width):
    return pl.BlockSpec((tm, width), lambda i: (i, 0))


def _params(n_grid):
    return pltpu.CompilerParams(dimension_semantics=("arbitrary",) * n_grid,
                                vmem_limit_bytes=VMEM_LIMIT)


def _ffn_kernel(x_ref, gpre_ref, gpost_ref, wg_ref, wu_ref, wd_ref, o_ref, *, f_cuts):
    x = x_ref[...]
    h = _rms(x, gpre_ref[...]).astype(BF16)
    y = jnp.zeros(x.shape, F32)
    for lo, hi in zip(f_cuts[:-1], f_cuts[1:]):
        sl = slice(lo, hi)
        g = _dot(h, wg_ref[:, sl])
        u = _dot(h, wu_ref[:, sl])
        a = (g * jax.nn.sigmoid(g) * u).astype(BF16)
        y = y + _dot(a, wd_ref[sl, :])
    o_ref[...] = x + 0.5 * _rms(y, gpost_ref[...])


def _ffn(x, g_pre, g_post, w_gate, w_up, w_down, tm):
    n, d = x.shape
    d_ff = w_gate.shape[1]
    cut = (d_ff // 2) // MXU_WIDTH * MXU_WIDTH
    f_cuts = (0, cut, d_ff) if 0 < cut < d_ff else (0, d_ff)
    return pl.pallas_call(
        functools.partial(_ffn_kernel, f_cuts=f_cuts),
        grid=(n // tm,),
        in_specs=[_row_spec(tm, d), _const_spec((1, d)), _const_spec((1, d)),
                  _const_spec((d, d_ff)), _const_spec((d, d_ff)), _const_spec((d_ff, d))],
        out_specs=_row_spec(tm, d),
        out_shape=jax.ShapeDtypeStruct((n, d), F32),
        compiler_params=_params(1),
        name="ffn",
    )(x, g_pre, g_post, w_gate, w_up, w_down)


def _rope(blk, tc, ts1, ts2):
    return blk * tc + pltpu.roll(blk, 16, 1) * ts1 + pltpu.roll(blk, C_PAD - 16, 1) * ts2


def _mixer_in_kernel(x_ref, gpre_ref, win_ref, gcq_ref, gckv_ref, wuq_ref, wukv_ref,
                     tc_ref, ts1_ref, ts2_ref,
                     qa_ref, kab_ref, vab_ref, qb_ref, kbb_ref, vbb_ref,
                     ka_ref, va_ref, kb_ref, vb_ref, ckv_ref, krb_ref,
                     qc_ref, kc_ref, vc_ref, kn_ref, *, q_transposed):
    w = W_AB
    q_rank = gcq_ref.shape[1]
    kv_rank = gckv_ref.shape[1]
    u = _rms(x_ref[...], gpre_ref[...]).astype(BF16)
    o = 6 * w
    lat = _dot(u, win_ref[:, o:])
    cq = lat[:, :q_rank]
    ckv = lat[:, q_rank:q_rank + kv_rank]
    krb = lat[:, q_rank + kv_rank:q_rank + kv_rank + C_PAD]
    proj = _dot(u, win_ref[:, :o])
    qa, ka, va = proj[:, 0:w], proj[:, w:2 * w], proj[:, 2 * w:3 * w]
    qb, kb, vb = proj[:, 3 * w:4 * w], proj[:, 4 * w:5 * w], proj[:, 5 * w:6 * w]
    qa_ref[...] = (qa * (ATTN_SCALE * LOG2E)).astype(BF16)
    qb_ref[...] = (qb * (ATTN_SCALE * LOG2E)).astype(BF16)
    ka_ref[...] = ka
    va_ref[...] = va
    kb_ref[...] = kb
    vb_ref[...] = vb
    kab_ref[...] = ka.astype(BF16)
    vab_ref[...] = va.astype(BF16)
    kbb_ref[...] = kb.astype(BF16)
    vbb_ref[...] = vb.astype(BF16)
    tc, ts1, ts2 = tc_ref[...], ts1_ref[...], ts2_ref[...]
    cqn = _rms(cq, gcq_ref[...]).astype(BF16)
    qall = _dot(cqn, wuq_ref[...])
    ckvn = _rms(ckv, gckv_ref[...])
    ckv_ref[...] = ckvn
    kv = _dot(ckvn.astype(BF16), wukv_ref[...])
    krot = _rope(krb, tc, ts1, ts2)
    krb_ref[...] = krot
    one_lane = (lax.broadcasted_iota(jnp.int32, (1, C_PAD), 1) == C_ONE).astype(F32)
    for h in range(H_C):
        sl = slice(h * C_PAD, (h + 1) * C_PAD)
        qh = _rope(qall[:, sl], tc, ts1, ts2) * (MLA_SCALE * LOG2E)
        if q_transposed:
            qt = qh.T
            norm = jnp.sqrt(jnp.sum(qt * qt, axis=0, keepdims=True)) * NORM_MARGIN
            row = lax.broadcasted_iota(jnp.int32, qt.shape, 0)
            qc_ref[sl, :] = jnp.where(row == C_ONE, norm, qt).astype(BF16)
        else:
            qc_ref[:, sl] = qh.astype(BF16)
        kh = kv[:, sl] + krot
        kc_ref[:, sl] = (kh + one_lane).astype(BF16)
        k_sq = _dot((kh * kh).astype(BF16), jnp.ones((C_PAD, C_PAD), BF16))
        kn_ref[0, :, sl] = jnp.sqrt(jnp.max(k_sq, axis=0, keepdims=True)) * NORM_MARGIN
    if not q_transposed:
        vc_ref[...] = kv[:, H_C * C_PAD:].astype(BF16)
    else:
        for p in range(H_C // 2):
            o = H_C * C_PAD + p * 2 * C_V
            vc_ref[0, p * 2 * C_V:(p + 1) * 2 * C_V, :] = kv[:, o:o + 2 * C_V].T.astype(BF16)


def _mixer_in(x, g_pre, w_main, g_cq, g_ckv, w_uq, w_ukv, tabs, tm, q_transposed=False):
    n, d = x.shape
    q_rank, kv_rank = g_cq.shape[1], g_ckv.shape[1]
    wc = H_C * C_PAD
    wv = H_C * C_V
    bf = lambda width: jax.ShapeDtypeStruct((n, width), BF16)
    f32 = lambda width: jax.ShapeDtypeStruct((n, width), F32)
    q_shape = jax.ShapeDtypeStruct((wc, n), BF16) if q_transposed else bf(wc)
    q_spec = pl.BlockSpec((wc, tm), lambda i: (0, i)) if q_transposed else _row_spec(tm, wc)
    v_shape = jax.ShapeDtypeStruct((n // tm, wv, tm), BF16) if q_transposed else bf(wv)
    v_spec = (pl.BlockSpec((1, wv, tm), lambda i: (i, 0, 0)) if q_transposed
              else _row_spec(tm, wv))
    out_shape = ([bf(W_AB)] * 6 + [f32(W_AB)] * 4 + [f32(kv_rank), f32(C_PAD)]
                 + [q_shape, bf(wc), v_shape, jax.ShapeDtypeStruct((n // tm, 1, wc), F32)])
    out_specs = ([_row_spec(tm, W_AB)] * 10 + [_row_spec(tm, kv_rank), _row_spec(tm, C_PAD)]
                 + [q_spec, _row_spec(tm, wc), v_spec, pl.BlockSpec((1, 1, wc), lambda i: (i, 0, 0))])
    return pl.pallas_call(
        functools.partial(_mixer_in_kernel, q_transposed=q_transposed),
        grid=(n // tm,),
        in_specs=[_row_spec(tm, d), _const_spec((1, d)), _const_spec(w_main.shape),
                  _const_spec((1, q_rank)), _const_spec((1, kv_rank)),
                  _const_spec(w_uq.shape), _const_spec(w_ukv.shape),
                  _row_spec(tm, C_PAD), _row_spec(tm, C_PAD), _row_spec(tm, C_PAD)],
        out_specs=out_specs,
        out_shape=out_shape,
        compiler_params=_params(1),
        name="mixer_in",
    )(x, g_pre, w_main, g_cq, g_ckv, w_uq, w_ukv, *tabs)


def _merge_kernel(x_ref, ya_ref, yb_ref, yc_ref, gpre_ref, gpost_ref, wgates_ref,
                  wa_ref, wb_ref, wc_ref, wout_ref, o_ref, *, yc_transposed):
    x = x_ref[...]
    d = x.shape[1]
    u = _rms(x, gpre_ref[...]).astype(BF16)
    m = jnp.zeros(x.shape, F32)
    for i, (y_ref, w_ref) in enumerate(((ya_ref, wa_ref), (yb_ref, wb_ref), (yc_ref, wc_ref))):
        gate = jax.nn.sigmoid(_dot(u, wgates_ref[:, i * d:(i + 1) * d]))
        branch_dot = _dot_tn if (yc_transposed and y_ref is yc_ref) else _dot
        m = m + gate * branch_dot(y_ref[...], w_ref[...])
    mixed = _dot(m.astype(BF16), wout_ref[...])
    o_ref[...] = x + _rms(mixed, gpost_ref[...])


def _merge(x, ya, yb, yc, g_pre, g_post, w_gates, w_a, w_b, w_c, w_out, tm, yc_transposed=False):
    n, d = x.shape
    yc_spec = (pl.BlockSpec((yc.shape[0], tm), lambda i: (0, i)) if yc_transposed
               else _row_spec(tm, yc.shape[1]))
    return pl.pallas_call(
        functools.partial(_merge_kernel, yc_transposed=yc_transposed),
        grid=(n // tm,),
        in_specs=[_row_spec(tm, d), _row_spec(tm, ya.shape[1]), _row_spec(tm, yb.shape[1]),
                  yc_spec, _const_spec((1, d)), _const_spec((1, d)),
                  _const_spec(w_gates.shape), _const_spec(w_a.shape), _const_spec(w_b.shape),
                  _const_spec(w_c.shape), _const_spec(w_out.shape)],
        out_specs=_row_spec(tm, d),
        out_shape=jax.ShapeDtypeStruct((n, d), F32),
        compiler_params=_params(1),
        name="merge",
    )(x, ya, yb, yc, g_pre, g_post, w_gates, w_a, w_b, w_c, w_out)


def _head_masked(q, width):
    lane_head = lax.broadcasted_iota(jnp.int32, q.shape, 1) // width
    zero = jnp.zeros_like(q)
    return [jnp.where(lane_head == h, q, zero) for h in range(q.shape[1] // width)]


def _strict_upper_ones(n):
    j = lax.broadcasted_iota(jnp.int32, (n, n), 0)
    s = lax.broadcasted_iota(jnp.int32, (n, n), 1)
    return (j > s).astype(BF16)


def _sb_terms(qm, kt, ones_ut, causal):
    z = _dot_nt(qm, kt)
    t = jnp.log2(1.0 + jnp.exp2(-jnp.abs(z)))
    log_keep = -jnp.maximum(z, 0.0) - t
    log_beta = jnp.minimum(z, 0.0) - t
    if causal is not None:
        log_keep = jnp.where(causal, log_keep, 0.0)
    hi = log_keep.astype(BF16)
    lo = (log_keep - hi.astype(F32)).astype(BF16)
    tail = _dot(hi, ones_ut) + _dot(lo, ones_ut)
    return log_beta + tail, jnp.sum(log_keep, axis=1, keepdims=True)


def _sb_apply(terms, vt, carry, acc, causal):
    logw, total = terms
    w = jnp.exp2(logw + carry)
    if causal is not None:
        w = jnp.where(causal, w, 0.0)
    return carry + total, acc + _dot(w.astype(BF16), vt)


def _sb_tile(qm, kt, vt, carry, acc, ones_ut, causal):
    return _sb_apply(_sb_terms(qm, kt, ones_ut, causal), vt, carry, acc, causal)


def _sb_alive(carry):
    return jnp.max(carry) > SB_DEAD


def _select_heads(parts, width):
    lane_head = lax.broadcasted_iota(jnp.int32, parts[0].shape, 1) // width
    out = parts[0]
    for h in range(1, len(parts)):
        out = jnp.where(lane_head == h, parts[h], out)
    return out


def _band_sb_kernel(qa_ref, ka0_ref, ka1_ref, va0_ref, va1_ref, bias_ref, q_ref, k_ref, v_ref,
                    oa_ref, o_ref, *, tile):
    i = pl.program_id(0)
    sub = BAND_SB_SUB
    k_band = jnp.concatenate([ka0_ref[...], ka1_ref[...]], axis=0)
    v_band = jnp.concatenate([va0_ref[...], va1_ref[...]], axis=0)
    col = lax.broadcasted_iota(jnp.int32, (tile, 3 * tile), 1)
    for t in range(sub):
        kwin = k_band[(sub - 2 + t) * tile:(sub + 1 + t) * tile]
        vwin = v_band[(sub - 2 + t) * tile:(sub + 1 + t) * tile]
        in_window = col >= (2 - (sub * i + t)) * tile
        band = []
        for h, qm in enumerate(_head_masked(qa_ref[t * tile:(t + 1) * tile, :], HEAD_DIM)):
            s = jnp.where(in_window, _dot_nt(qm, kwin) + bias_ref[h], NEG)
            m = jnp.max(s, axis=1, keepdims=True)
            p = jnp.exp2(s - m)
            l = jnp.sum(p, axis=1, keepdims=True)
            band.append(_dot(p.astype(BF16), vwin) / l)
        oa_ref[t * tile:(t + 1) * tile, :] = _select_heads(band, HEAD_DIM).astype(BF16)

    row = lax.broadcasted_iota(jnp.int32, (tile, tile), 0)
    col = lax.broadcasted_iota(jnp.int32, (tile, tile), 1)
    causal = col < row
    ones_ut = _strict_upper_ones(tile)
    width = q_ref.shape[1]
    rows = lambda ref, kb: ref[pl.ds(pl.multiple_of(kb * tile, tile), tile), :]

    work = []
    for t in range(sub):
        it = sub * i + t
        qms = _head_masked(q_ref[t * tile:(t + 1) * tile, :], HEAD_DIM)
        tiles = [jnp.maximum(it - d, 0) for d in range(SB_ALWAYS)]
        terms = [[_sb_terms(qm, rows(k_ref, kb), ones_ut, causal if d == 0 else None) for qm in qms]
                 for d, kb in enumerate(tiles)]
        values = [rows(v_ref, kb) for kb in tiles]
        states = []
        for h in range(len(qms)):
            carry, acc = jnp.zeros((tile, 1), F32), jnp.zeros((tile, width), F32)
            for d in range(SB_ALWAYS):
                carry_in = carry if d == 0 else jnp.where(it >= d, carry, NEG)
                carry, acc = _sb_apply(terms[d][h], values[d], carry_in, acc, causal if d == 0 else None)
            states.append((carry, acc))
        work.append((it, qms, states))
    for t, (it, qms, states) in enumerate(work):
        first = it - SB_ALWAYS
        parts = []
        for qm, (carry, acc) in zip(qms, states):
            state = lax.while_loop(
                lambda st, first=first: (st[0] <= first) & _sb_alive(st[1]),
                lambda st, qm=qm, first=first: (st[0] + 1,) + _sb_tile(
                    qm, rows(k_ref, first - st[0]), rows(v_ref, first - st[0]), st[1], st[2], ones_ut, None),
                (jnp.int32(0), carry, acc))
            parts.append(state[2])
        o_ref[t * tile:(t + 1) * tile, :] = _select_heads(parts, HEAD_DIM).astype(BF16)


def _band_sb_attention(qa, ka, va, bias, q, k, v, tile):
    n, w = q.shape
    rows = BAND_SB_SUB * tile
    assert BAND_SB_SUB >= 2 and n % rows == 0
    blk = lambda back: pl.BlockSpec((rows, w), lambda i: (jnp.maximum(i - back, 0), 0))
    out = jax.ShapeDtypeStruct((n, w), BF16)
    return pl.pallas_call(
        functools.partial(_band_sb_kernel, tile=tile),
        grid=(n // rows,),
        in_specs=[blk(0), blk(1), blk(0), blk(1), blk(0), _const_spec(bias.shape),
                  blk(0), _const_spec((n, w)), _const_spec((n, w))],
        out_specs=[blk(0), blk(0)],
        out_shape=[out, out],
        compiler_params=_params(1),
        name="band_sb_attn",
    )(qa, ka, ka, va, va, bias, q, k, v)


def _mla_kernel(qt_ref, k_ref, vt_ref, kn_ref, ot_ref, qx_scr, p_scr, acc_scr, l_scr):
    i = pl.program_id(1)
    heads = range(2)
    tk, tq = p_scr.shape[2:]
    per_q = tq // tk
    plain = per_q * i
    key = lax.broadcasted_iota(jnp.int32, (tk, tq), 0)
    qry = lax.broadcasted_iota(jnp.int32, (tk, tq), 1)
    row = lax.broadcasted_iota(jnp.int32, (C_PAD - C_ONE, tq), 0)

    def visible(d):
        return ((d * tk + key) // CHUNK) <= (qry // CHUNK)

    def load_q(shifted):
        k_bound = jnp.max(kn_ref[...], axis=0)
        for e in heads:
            lo = e * C_PAD
            qx_scr[lo:lo + C_ONE, :] = qt_ref[lo:lo + C_ONE, :]
            if shifted:
                bound = qt_ref[lo + C_ONE:lo + C_ONE + 1, :].astype(F32) * (k_bound[:, lo:lo + 1] * NORM_MARGIN)
                tail = jnp.where(row == 0, -bound, 0.0)
            else:
                tail = jnp.zeros(row.shape, F32)
            qx_scr[lo + C_ONE:lo + C_PAD, :] = tail.astype(BF16)

    def scores(kb, e):
        return _dot(k_ref[kb, :, e * C_PAD:(e + 1) * C_PAD], qx_scr[e * C_PAD:(e + 1) * C_PAD, :])

    def values(kb, e, p):
        return _dot(vt_ref[kb, e * C_V:(e + 1) * C_V, :], p)

    def write_out():
        ot_ref[...] = jnp.concatenate([acc_scr[e] / l_scr[e] for e in heads], axis=0).astype(BF16)

    def reset():
        acc_scr[...] = jnp.zeros(acc_scr.shape, F32)
        l_scr[...] = jnp.zeros(l_scr.shape, F32)

    def fast_stage(kb, slot, masked_tile=None):
        lo = 0 if masked_tile is None else masked_tile * tk
        for e in heads:
            s = _dot(k_ref[kb, :, e * C_PAD:(e + 1) * C_PAD], qx_scr[e * C_PAD:(e + 1) * C_PAD, lo:])
            if masked_tile is not None:
                s = jnp.where(visible(0)[:, :tq - lo], s, NEG)
            p = jnp.exp2(s)
            l_scr[e, :, lo:] += jnp.sum(p, axis=0, keepdims=True)
            p_scr[slot, e, :, lo:] = p.astype(BF16)
        for e in heads:
            acc_scr[e, :, lo:] += values(kb, e, p_scr[slot, e, :, lo:])

    load_q(True)
    reset()

    def fast_group(g, _):
        for j in range(MLA_UNROLL):
            fast_stage(MLA_UNROLL * g + j, j)
        return 0
    groups = plain // MLA_UNROLL
    lax.fori_loop(0, groups, fast_group, 0)

    def fast_single(kb, _):
        fast_stage(kb, 0)
        return 0
    lax.fori_loop(MLA_UNROLL * groups, plain, fast_single, 0)
    for d in range(per_q):
        fast_stage(plain + d, (1 + d) % MLA_UNROLL, masked_tile=d)
    denom = jnp.minimum(jnp.min(l_scr[0]), jnp.min(l_scr[1]))
    trusted = denom >= MLA_MIN_DENOM

    @pl.when(trusted)
    def _():
        write_out()

    @pl.when(jnp.logical_not(trusted))
    def _():
        load_q(False)
        reset()

        def step(kb, m):
            out = []
            for e in heads:
                s = jnp.where(visible(kb - plain), scores(kb, e), NEG)
                m_new = jnp.maximum(m[e], jnp.max(s, axis=0, keepdims=True))
                alpha = jnp.exp2(m[e] - m_new)
                p = jnp.exp2(s - m_new)
                l_scr[e] = alpha * l_scr[e] + jnp.sum(p, axis=0, keepdims=True)
                acc_scr[e] = alpha * acc_scr[e] + values(kb, e, p.astype(BF16))
                out.append(m_new)
            return tuple(out)
        lax.fori_loop(0, plain + per_q, step, tuple(jnp.full((1, tq), NEG, F32) for _ in heads))
        write_out()


def _mla_attention(qt, k, vt3, kn, tq):
    n = k.shape[0]
    pairs = H_C // 2
    n_tiles, _, tk = vt3.shape
    assert n_tiles * tk == n and kn.shape == (n_tiles, 1, H_C * C_PAD)
    assert tq % tk == 0 and n % tq == 0
    k3 = k.reshape(n_tiles, tk, H_C * C_PAD)
    return pl.pallas_call(
        _mla_kernel,
        grid=(pairs, n // tq),
        in_specs=[pl.BlockSpec((2 * C_PAD, tq), lambda p, i: (p, i)),
                  pl.BlockSpec((n_tiles, tk, 2 * C_PAD), lambda p, i: (0, 0, p)),
                  pl.BlockSpec((n_tiles, 2 * C_V, tk), lambda p, i: (0, p, 0)),
                  pl.BlockSpec((n_tiles, 1, 2 * C_PAD), lambda p, i: (0, 0, p))],
        out_specs=pl.BlockSpec((2 * C_V, tq), lambda p, i: (p, i)),
        out_shape=jax.ShapeDtypeStruct((H_C * C_V, n), BF16),
        scratch_shapes=[pltpu.VMEM((2 * C_PAD, tq), BF16), pltpu.VMEM((MLA_UNROLL, 2, tk, tq), BF16),
                        pltpu.VMEM((2, C_V, tq), F32), pltpu.VMEM((2, 1, tq), F32)],
        compiler_params=_params(2),
        name="mla_attn",
    )(qt, k3, vt3, kn)


def _sample_attn_kernel(qa_ref, ka_ref, va_ref, cak_ref, cav_ref, biasc_ref, biasn_ref,
                        qb_ref, kb_ref, vb_ref, cbk_ref, cbv_ref,
                        qc_ref, ckvn_ref, krbn_ref, ckv_ref, ckr_ref, wukt_ref, wuvp_ref,
                        ya_ref, yb_ref, yc_ref, s_scr, *, tile):
    ds = qa_ref.shape[1]
    past = cbk_ref.shape[2]

    ka, va, cak, cav = ka_ref[0], va_ref[0], cak_ref[0, 0].astype(BF16), cav_ref[0, 0].astype(BF16)
    parts = []
    for h, qm in enumerate(_head_masked(qa_ref[0], HEAD_DIM)):
        s_c = _dot_nt(qm, cak) + biasc_ref[h]
        s_n = _dot_nt(qm, ka) + biasn_ref[h]
        m = jnp.maximum(jnp.max(s_c, axis=1, keepdims=True), jnp.max(s_n, axis=1, keepdims=True))
        p_c = jnp.exp2(s_c - m)
        p_n = jnp.exp2(s_n - m)
        l = jnp.sum(p_c, axis=1, keepdims=True) + jnp.sum(p_n, axis=1, keepdims=True)
        parts.append((_dot(p_c.astype(BF16), cav) + _dot(p_n.astype(BF16), va)) / l)
    ya_ref[0] = _select_heads(parts, HEAD_DIM).astype(BF16)

    row = lax.broadcasted_iota(jnp.int32, (ds, ds), 0)
    col = lax.broadcasted_iota(jnp.int32, (ds, ds), 1)
    causal = col < row
    ones_new = _strict_upper_ones(ds)
    ones_ut = _strict_upper_ones(tile)
    kb, vb = kb_ref[0], vb_ref[0]
    n_tiles = past // tile
    parts = []
    for qm in _head_masked(qb_ref[0], HEAD_DIM):
        state = _sb_tile(qm, kb, vb, jnp.zeros((ds, 1), F32), jnp.zeros((ds, W_AB), F32),
                         ones_new, causal)

        def cache_tile(st, qm=qm):
            start = pl.multiple_of((n_tiles - 1 - st[0]) * tile, tile)
            return (st[0] + 1,) + _sb_tile(qm, cbk_ref[0, 0, pl.ds(start, tile), :].astype(BF16),
                                           cbv_ref[0, 0, pl.ds(start, tile), :].astype(BF16),
                                           st[1], st[2], ones_ut, None)
        state = lax.while_loop(lambda st: (st[0] < n_tiles) & _sb_alive(st[1]), cache_tile,
                               (jnp.int32(0),) + state)
        parts.append(state[2])
    yb_ref[0] = _select_heads(parts, HEAD_DIM).astype(BF16)

    kv_rank = ckv_ref.shape[3]
    qc = qc_ref[0]
    rope_lanes = lax.broadcasted_iota(jnp.int32, (ds, C_PAD), 1) >= C_NOPE
    q_rows = []
    for h in range(H_C):
        qh = qc[:, h * C_PAD:(h + 1) * C_PAD]
        q_lat = _dot(qh, wukt_ref[h]).astype(BF16)
        q_rows.append(jnp.concatenate([q_lat, jnp.where(rope_lanes, qh, jnp.zeros_like(qh))], axis=1))
    q_all = jnp.concatenate(q_rows, axis=0)

    place = (lax.broadcasted_iota(jnp.int32, (C_ROPE, C_PAD), 1)
             == lax.broadcasted_iota(jnp.int32, (C_ROPE, C_PAD), 0) + C_NOPE).astype(BF16)
    tile_c = SAMPLE_LATENT_TILE if past % SAMPLE_LATENT_TILE == 0 else tile
    cache_tiles = range(past // tile_c)
    latent = lambda j: ckv_ref[0, 0, j * tile_c:(j + 1) * tile_c, :].astype(BF16)

    lat_new = ckvn_ref[0].astype(BF16)
    s_new = _dot_nt(q_all, jnp.concatenate([lat_new, krbn_ref[0].astype(BF16)], axis=1))
    m = jnp.max(s_new, axis=1, keepdims=True)
    for j in cache_tiles:
        k_rope = _dot(ckr_ref[0, 0, j * tile_c:(j + 1) * tile_c, :].astype(BF16), place).astype(BF16)
        s = _dot_nt(q_all, jnp.concatenate([latent(j), k_rope], axis=1))
        s_scr[:, j * tile_c:(j + 1) * tile_c] = s
        m = jnp.maximum(m, jnp.max(s, axis=1, keepdims=True))
    p = jnp.exp2(s_new - m)
    denom = jnp.sum(p, axis=1, keepdims=True)
    o_lat = _dot(p.astype(BF16), lat_new)
    for j in cache_tiles:
        p = jnp.exp2(s_scr[:, j * tile_c:(j + 1) * tile_c] - m)
        denom = denom + jnp.sum(p, axis=1, keepdims=True)
        o_lat = o_lat + _dot(p.astype(BF16), latent(j))
    o_lat = (o_lat / denom).astype(BF16)
    out = jnp.zeros((ds, H_C * C_V), F32)
    for h in range(H_C):
        out = out + _dot(o_lat[h * ds:(h + 1) * ds], wuvp_ref[h])
    yc_ref[0] = out.astype(BF16)


def _sample_attention(new, caches, l, bias_c, bias_n, w_ukt, w_uvp, tile):
    qa, ka, va, qb, kb, vb, qc, ckvn, krbn = new
    cak, cav, cbk, cbv, ckv, ckr = caches
    nb, ds, _ = qa.shape
    past = ckv.shape[2]
    req = lambda a: pl.BlockSpec((1,) + a.shape[1:], lambda b: (b, 0, 0))
    cache = lambda a: pl.BlockSpec((1, 1) + a.shape[2:], lambda b: (l, b, 0, 0))
    ins = [qa, ka, va, cak, cav, bias_c, bias_n, qb, kb, vb, cbk, cbv, qc, ckvn, krbn, ckv, ckr,
           w_ukt, w_uvp]
    specs = [req(a) for a in ins]
    for idx in (3, 4, 10, 11, 15, 16):
        specs[idx] = cache(ins[idx])
    for idx in (5, 6, 17, 18):
        specs[idx] = _const_spec(ins[idx].shape)
    out_shape = [jax.ShapeDtypeStruct((nb, ds, W_AB), BF16), jax.ShapeDtypeStruct((nb, ds, W_AB), BF16),
                 jax.ShapeDtypeStruct((nb, ds, H_C * C_V), BF16)]
    return pl.pallas_call(
        functools.partial(_sample_attn_kernel, tile=tile),
        grid=(nb,),
        in_specs=specs,
        out_specs=[req(s) for s in out_shape],
        out_shape=out_shape,
        scratch_shapes=[pltpu.VMEM((H_C * ds, past), F32)],
        compiler_params=_params(1),
        name="sample_attn",
    )(*ins)


def _rope_tables(pos):
    half = C_ROPE // 2
    freq = ROPE_THETA ** (-jnp.arange(half, dtype=F32) / half)
    ang = pos.astype(F32)[:, None] * freq[None, :]
    cos, sin = jnp.cos(ang), jnp.sin(ang)
    n = pos.shape[0]
    z = lambda w: jnp.zeros((n, w), F32)
    tc = jnp.concatenate([jnp.ones((n, C_NOPE), F32), cos, cos, z(C_PAD - C_NOPE - C_ROPE)], axis=1)
    ts1 = jnp.concatenate([z(C_NOPE + half), sin, z(C_PAD - C_NOPE - C_ROPE)], axis=1)
    ts2 = jnp.concatenate([z(C_NOPE), -sin, z(C_PAD - C_NOPE - half)], axis=1)
    return tc, ts1, ts2


def _band_bias(rel_bias, q0, nq, k0, nk):
    period = 1 << int(np.ceil(np.log2(nq + nk)))
    d = np.arange(period)
    d = np.where(d >= nk, d - period, d)
    idx = np.clip((q0 - k0) - d, -REL_CLIP, REL_CLIP) + REL_CLIP
    g = rel_bias.astype(F32)[:, idx]
    toe = jnp.tile(g, (1, nq))[:, :nq * (period - 1)].reshape(-1, nq, period - 1)[:, :, :nk]
    q_pos, k_pos = q0 + np.arange(nq), k0 + np.arange(nk)
    qc, kc = q_pos[:, None] // CHUNK, k_pos[None, :] // CHUNK
    vis = (kc <= qc) & (kc >= qc - BAND_CHUNKS)
    return jnp.where(jnp.asarray(vis)[None], toe * LOG2E, NEG)


def _cast_kernel(w_ref, o_ref):
    o_ref[...] = w_ref[0].astype(BF16)


def _layer_bf16(w, l):
    _, rows, cols = w.shape
    rb = rows // 4 if rows % 64 == 0 else rows
    return pl.pallas_call(
        _cast_kernel,
        grid=(rows // rb,),
        in_specs=[pl.BlockSpec((1, rb, cols), lambda i: (l, i, 0))],
        out_specs=pl.BlockSpec((rb, cols), lambda i: (i, 0)),
        out_shape=jax.ShapeDtypeStruct((rows, cols), BF16),
        compiler_params=_params(1),
        name="cast_bf16",
    )(w)


def _layer_weights(l, w_in, w_uq, w_uk, w_uv):
    d = w_in.shape[1]
    q_rank, kv_rank = w_uq.shape[1], w_uk.shape[1]
    o = 6 * W_AB + q_rank + kv_rank
    zc = lambda w: jnp.zeros((d, w), w_in.dtype)
    w_main = jnp.concatenate([w_in[l, :, :o], zc(C_NOPE), w_in[l, :, o:o + C_ROPE],
                              zc(C_PAD - C_NOPE - C_ROPE)], axis=1).astype(BF16)
    w_gates = w_in[l, :, o + C_ROPE:].astype(BF16)
    pad_heads = lambda w: jnp.pad(w, ((0, 0), (0, 0), (0, C_PAD - w.shape[2]))).reshape(w.shape[0], -1)
    w_uq_p = pad_heads(w_uq[l]).astype(BF16)
    w_ukv = jnp.concatenate([pad_heads(w_uk[l]), w_uv[l].reshape(kv_rank, -1)], axis=1).astype(BF16)
    w_ukt = jnp.pad(jnp.transpose(w_uk[l], (1, 2, 0)), ((0, 0), (0, C_PAD - C_NOPE), (0, 0))).astype(BF16)
    w_uvp = jnp.einsum('rhd,hg->hrgd', w_uv[l], jnp.eye(H_C, dtype=w_uv.dtype)).reshape(
        H_C, kv_rank, H_C * C_V).astype(BF16)
    return w_main, w_gates, w_uq_p, w_ukv, w_ukt, w_uvp


def kernel(x_prompt, x_sample, cache_a_k, cache_a_v, cache_b_k, cache_b_v, cache_c_kv, cache_c_kr, ffn1_norm_pre, ffn1_norm_post, ffn1_w_gate, ffn1_w_up, ffn1_w_down, mix_norm_pre, mix_norm_post, w_in, cq_norm, ckv_norm, w_uq, w_uk, w_uv, rel_bias_a, w_br_a, w_br_b, w_br_c, w_out, ffn2_norm_pre, ffn2_norm_post, ffn2_w_gate, ffn2_w_up, ffn2_w_down):
    batch, seq, d = x_prompt.shape
    nb, ds, _ = x_sample.shape
    depth = w_in.shape[0]
    past = cache_b_k.shape[2]
    win_cache = cache_a_k.shape[2]
    assert batch == 1, "prompt group is a single sequence"
    tile = 256
    tm = 512 if seq % 512 == 0 else tile
    assert seq % tile == 0 and past % tile == 0 and tile % CHUNK == 0 and 2 * tile >= WIN_A
    assert past % CHUNK + ds <= CHUNK
    ns = nb * ds

    xp = x_prompt.reshape(seq, d)
    xs = x_sample.reshape(ns, d)
    tabs_p = _rope_tables(jnp.arange(seq))
    tabs_s = _rope_tables(jnp.tile(past + jnp.arange(ds), nb))

    row = lambda g, l: g[l][None, :]
    base = 2 * tile

    merged_heads = lambda c: c.reshape(c.shape[:3] + (-1,))
    caches = (merged_heads(cache_a_k), merged_heads(cache_a_v), merged_heads(cache_b_k),
              merged_heads(cache_b_v), cache_c_kv, cache_c_kr)
    states_p, states_s = [], []
    for l in range(depth):
        w_main, w_gates, w_uq_p, w_ukv, w_ukt, w_uvp = _layer_weights(l, w_in, w_uq, w_uk, w_uv)
        ffn1 = (row(ffn1_norm_pre, l), row(ffn1_norm_post, l), _layer_bf16(ffn1_w_gate, l),
                _layer_bf16(ffn1_w_up, l), _layer_bf16(ffn1_w_down, l))
        ffn2 = (row(ffn2_norm_pre, l), row(ffn2_norm_post, l), _layer_bf16(ffn2_w_gate, l),
                _layer_bf16(ffn2_w_up, l), _layer_bf16(ffn2_w_down, l))
        mix_in = (row(mix_norm_pre, l), w_main, row(cq_norm, l), row(ckv_norm, l), w_uq_p, w_ukv)
        mrg = (row(mix_norm_pre, l), row(mix_norm_post, l), w_gates, _layer_bf16(w_br_a, l),
               _layer_bf16(w_br_b, l), _layer_bf16(w_br_c, l), _layer_bf16(w_out, l))
        bias_p = _band_bias(rel_bias_a[l], base, tile, base - 2 * tile, 3 * tile)
        bias_c = _band_bias(rel_bias_a[l], past, ds, past - win_cache, win_cache)
        bias_n = _band_bias(rel_bias_a[l], past, ds, past, ds)

        xp = _ffn(xp, *ffn1, tm)
        (qa, kab, vab, qb, kbb, vbb, ka, va, kb, vb, ckv, krb, qc, kc, vc, kn) = _mixer_in(
            xp, *mix_in, tabs_p, tm, q_transposed=True)
        ya, yb = _band_sb_attention(qa, kab, vab, bias_p, qb, kbb, vbb, tile)
        yc = _mla_attention(qc, kc, vc, kn, 2 * tm if seq % (2 * tm) == 0 else tm)
        xp = _merge(xp, ya, yb, yc, *mrg, tm, yc_transposed=True)
        xp = _ffn(xp, *ffn2, tm)
        win = min(WIN_A, seq)
        heads = lambda t, h: t.reshape(1, t.shape[0], h, t.shape[1] // h)
        states_p.append((heads(ka[-win:], H_A), heads(va[-win:], H_A), heads(kb, H_B), heads(vb, H_B),
                         ckv[None], krb[None, :, C_NOPE:C_NOPE + C_ROPE]))

        xs = _ffn(xs, *ffn1, ns)
        (qa, kab, vab, qb, kbb, vbb, ka, va, kb, vb, ckv, krb, qc, kc, vc, _) = _mixer_in(
            xs, *mix_in, tabs_s, ns)
        per_req = lambda t: t.reshape(nb, ds, t.shape[1])
        new = tuple(per_req(t) for t in (qa, kab, vab, qb, kbb, vbb, qc, ckv, krb))
        ya, yb, yc = _sample_attention(new, caches, l, bias_c, bias_n, w_ukt, w_uvp, tile)
        xs = _merge(xs, ya.reshape(ns, -1), yb.reshape(ns, -1), yc.reshape(ns, -1), *mrg, ns)
        xs = _ffn(xs, *ffn2, ns)
        heads_s = lambda t, h: t.reshape(nb, ds, h, t.shape[1] // h)
        states_s.append((heads_s(ka, H_A), heads_s(va, H_A), heads_s(kb, H_B), heads_s(vb, H_B),
                         ckv.reshape(nb, ds, -1), krb[:, C_NOPE:C_NOPE + C_ROPE].reshape(nb, ds, C_ROPE)))

    stack = lambda states, i: jnp.stack([s[i] for s in states], axis=0)
    return ((xp.reshape(batch, seq, d), xs.reshape(nb, ds, d))
            + tuple(stack(states_p, i) for i in range(6))
            + tuple(stack(states_s, i) for i in range(6)))
```

```python
import functools

import numpy as np
import jax
import jax.numpy as jnp
from jax import lax
from jax.experimental import pallas as pl
from jax.experimental.pallas import tpu as pltpu

F32 = jnp.float32
BF16 = jnp.bfloat16

CHUNK = 64
BAND_CHUNKS = 8
WIN_A = BAND_CHUNKS * CHUNK
HEAD_DIM = 64
H_A = 4
H_B = 4
H_C = 8
C_NOPE = 64
C_ROPE = 32
C_V = 64
REL_CLIP = 128
ROPE_THETA = 10000.0
EPS = 1e-6
N_BRANCH = 3
W_AB = H_A * HEAD_DIM
C_PAD = 128
ATTN_SCALE = HEAD_DIM ** -0.5
MLA_SCALE = (C_NOPE + C_ROPE) ** -0.5
LOG2E = float(np.log2(np.e))
C_ONE = C_NOPE + C_ROPE
NORM_MARGIN = 1.02
MLA_MIN_DENOM = 2.0 ** -60
MLA_UNROLL = 4
SAMPLE_LATENT_TILE = 1024
NEG = -1e30
BAND_SB_SUB = 2
SB_ALWAYS = 2
SB_DEAD = -150.0

LANES = 128
MXU_WIDTH = 256
VMEM_LIMIT = 52 * 1024 * 1024

NT_DIMS = (((1,), (1,)), ((), ()))
TN_DIMS = (((0,), (0,)), ((), ()))


def _rms(x, g):
    return x * lax.rsqrt(jnp.mean(x * x, axis=-1, keepdims=True) + EPS) * g


def _dot(a, b):
    return jnp.dot(a, b, preferred_element_type=F32)


def _dot_nt(a, b):
    return lax.dot_general(a, b, NT_DIMS, preferred_element_type=F32)


def _dot_tn(a, b):
    return lax.dot_general(a, b, TN_DIMS, preferred_element_type=F32)


def _const_spec(shape):
    zeros = (0,) * len(shape)
    return pl.BlockSpec(shape, lambda *_: zeros, pipeline_mode=pl.Buffered(1))


def _row_spec(tm, width):
    return pl.BlockSpec((tm, width), lambda i: (i, 0))


def _params(n_grid):
    return pltpu.CompilerParams(dimension_semantics=("arbitrary",) * n_grid,
                                vmem_limit_bytes=VMEM_LIMIT)


def _ffn_kernel(x_ref, gpre_ref, gpost_ref, wg_ref, wu_ref, wd_ref, o_ref, *, f_cuts):
    x = x_ref[...]
    h = _rms(x, gpre_ref[...]).astype(BF16)
    y = jnp.zeros(x.shape, F32)
    for lo, hi in zip(f_cuts[:-1], f_cuts[1:]):
        sl = slice(lo, hi)
        g = _dot(h, wg_ref[:, sl])
        u = _dot(h, wu_ref[:, sl])
        a = (g * jax.nn.sigmoid(g) * u).astype(BF16)
        y = y + _dot(a, wd_ref[sl, :])
    o_ref[...] = x + 0.5 * _rms(y, gpost_ref[...])


def _ffn(x, g_pre, g_post, w_gate, w_up, w_down, tm):
    n, d = x.shape
    d_ff = w_gate.shape[1]
    cut = (d_ff // 2) // MXU_WIDTH * MXU_WIDTH
    f_cuts = (0, cut, d_ff) if 0 < cut < d_ff else (0, d_ff)
    return pl.pallas_call(
        functools.partial(_ffn_kernel, f_cuts=f_cuts),
        grid=(n // tm,),
        in_specs=[_row_spec(tm, d), _const_spec((1, d)), _const_spec((1, d)),
                  _const_spec((d, d_ff)), _const_spec((d, d_ff)), _const_spec((d_ff, d))],
        out_specs=_row_spec(tm, d),
        out_shape=jax.ShapeDtypeStruct((n, d), F32),
        compiler_params=_params(1),
        name="ffn",
    )(x, g_pre, g_post, w_gate, w_up, w_down)


def _rope(blk, tc, ts1, ts2):
    return blk * tc + pltpu.roll(blk, 16, 1) * ts1 + pltpu.roll(blk, C_PAD - 16, 1) * ts2


def _mixer_in_kernel(x_ref, gpre_ref, win_ref, gcq_ref, gckv_ref, wuq_ref, wukv_ref,
                     tc_ref, ts1_ref, ts2_ref,
                     qa_ref, kab_ref, vab_ref, qb_ref, kbb_ref, vbb_ref,
                     ka_ref, va_ref, kb_ref, vb_ref, ckv_ref, krb_ref,
                     qc_ref, kc_ref, vc_ref, kn_ref, *, q_transposed):
    w = W_AB
    q_rank = gcq_ref.shape[1]
    kv_rank = gckv_ref.shape[1]
    u = _rms(x_ref[...], gpre_ref[...]).astype(BF16)
    o = 6 * w
    lat = _dot(u, win_ref[:, o:])
    cq = lat[:, :q_rank]
    ckv = lat[:, q_rank:q_rank + kv_rank]
    krb = lat[:, q_rank + kv_rank:q_rank + kv_rank + C_PAD]
    proj = _dot(u, win_ref[:, :o])
    qa, ka, va = proj[:, 0:w], proj[:, w:2 * w], proj[:, 2 * w:3 * w]
    qb, kb, vb = proj[:, 3 * w:4 * w], proj[:, 4 * w:5 * w], proj[:, 5 * w:6 * w]
    qa_ref[...] = (qa * (ATTN_SCALE * LOG2E)).astype(BF16)
    qb_ref[...] = (qb * (ATTN_SCALE * LOG2E)).astype(BF16)
    ka_ref[...] = ka
    va_ref[...] = va
    kb_ref[...] = kb
    vb_ref[...] = vb
    kab_ref[...] = ka.astype(BF16)
    vab_ref[...] = va.astype(BF16)
    kbb_ref[...] = kb.astype(BF16)
    vbb_ref[...] = vb.astype(BF16)
    tc, ts1, ts2 = tc_ref[...], ts1_ref[...], ts2_ref[...]
    cqn = _rms(cq, gcq_ref[...]).astype(BF16)
    qall = _dot(cqn, wuq_ref[...])
    ckvn = _rms(ckv, gckv_ref[...])
    ckv_ref[...] = ckvn
    kv = _dot(ckvn.astype(BF16), wukv_ref[...])
    krot = _rope(krb, tc, ts1, ts2)
    krb_ref[...] = krot
    one_lane = (lax.broadcasted_iota(jnp.int32, (1, C_PAD), 1) == C_ONE).astype(F32)
    for h in range(H_C):
        sl = slice(h * C_PAD, (h + 1) * C_PAD)
        qh = _rope(qall[:, sl], tc, ts1, ts2) * (MLA_SCALE * LOG2E)
        if q_transposed:
            qt = qh.T
            norm = jnp.sqrt(jnp.sum(qt * qt, axis=0, keepdims=True)) * NORM_MARGIN
            row = lax.broadcasted_iota(jnp.int32, qt.shape, 0)
            qc_ref[sl, :] = jnp.where(row == C_ONE, norm, qt).astype(BF16)
        else:
            qc_ref[:, sl] = qh.astype(BF16)
        kh = kv[:, sl] + krot
        kc_ref[:, sl] = (kh + one_lane).astype(BF16)
        k_sq = _dot((kh * kh).astype(BF16), jnp.ones((C_PAD, C_PAD), BF16))
        kn_ref[0, :, sl] = jnp.sqrt(jnp.max(k_sq, axis=0, keepdims=True)) * NORM_MARGIN
    if not q_transposed:
        vc_ref[...] = kv[:, H_C * C_PAD:].astype(BF16)
    else:
        for p in range(H_C // 2):
            o = H_C * C_PAD + p * 2 * C_V
            vc_ref[0, p * 2 * C_V:(p + 1) * 2 * C_V, :] = kv[:, o:o + 2 * C_V].T.astype(BF16)


def _mixer_in(x, g_pre, w_main, g_cq, g_ckv, w_uq, w_ukv, tabs, tm, q_transposed=False):
    n, d = x.shape
    q_rank, kv_rank = g_cq.shape[1], g_ckv.shape[1]
    wc = H_C * C_PAD
    wv = H_C * C_V
    bf = lambda width: jax.ShapeDtypeStruct((n, width), BF16)
    f32 = lambda width: jax.ShapeDtypeStruct((n, width), F32)
    q_shape = jax.ShapeDtypeStruct((wc, n), BF16) if q_transposed else bf(wc)
    q_spec = pl.BlockSpec((wc, tm), lambda i: (0, i)) if q_transposed else _row_spec(tm, wc)
    v_shape = jax.ShapeDtypeStruct((n // tm, wv, tm), BF16) if q_transposed else bf(wv)
    v_spec = (pl.BlockSpec((1, wv, tm), lambda i: (i, 0, 0)) if q_transposed
              else _row_spec(tm, wv))
    out_shape = ([bf(W_AB)] * 6 + [f32(W_AB)] * 4 + [f32(kv_rank), f32(C_PAD)]
                 + [q_shape, bf(wc), v_shape, jax.ShapeDtypeStruct((n // tm, 1, wc), F32)])
    out_specs = ([_row_spec(tm, W_AB)] * 10 + [_row_spec(tm, kv_rank), _row_spec(tm, C_PAD)]
                 + [q_spec, _row_spec(tm, wc), v_spec, pl.BlockSpec((1, 1, wc), lambda i: (i, 0, 0))])
    return pl.pallas_call(
        functools.partial(_mixer_in_kernel, q_transposed=q_transposed),
        grid=(n // tm,),
        in_specs=[_row_spec(tm, d), _const_spec((1, d)), _const_spec(w_main.shape),
                  _const_spec((1, q_rank)), _const_spec((1, kv_rank)),
                  _const_spec(w_uq.shape), _const_spec(w_ukv.shape),
                  _row_spec(tm, C_PAD), _row_spec(tm, C_PAD), _row_spec(tm, C_PAD)],
        out_specs=out_specs,
        out_shape=out_shape,
        compiler_params=_params(1),
        name="mixer_in",
    )(x, g_pre, w_main, g_cq, g_ckv, w_uq, w_ukv, *tabs)


def _merge_kernel(x_ref, ya_ref, yb_ref, yc_ref, gpre_ref, gpost_ref, wgates_ref,
                  wa_ref, wb_ref, wc_ref, wout_ref, o_ref, *, yc_transposed):
    x = x_ref[...]
    d = x.shape[1]
    u = _rms(x, gpre_ref[...]).astype(BF16)
    m = jnp.zeros(x.shape, F32)
    for i, (y_ref, w_ref) in enumerate(((ya_ref, wa_ref), (yb_ref, wb_ref), (yc_ref, wc_ref))):
        gate = jax.nn.sigmoid(_dot(u, wgates_ref[:, i * d:(i + 1) * d]))
        branch_dot = _dot_tn if (yc_transposed and y_ref is yc_ref) else _dot
        m = m + gate * branch_dot(y_ref[...], w_ref[...])
    mixed = _dot(m.astype(BF16), wout_ref[...])
    o_ref[...] = x + _rms(mixed, gpost_ref[...])


def _merge(x, ya, yb, yc, g_pre, g_post, w_gates, w_a, w_b, w_c, w_out, tm, yc_transposed=False):
    n, d = x.shape
    yc_spec = (pl.BlockSpec((yc.shape[0], tm), lambda i: (0, i)) if yc_transposed
               else _row_spec(tm, yc.shape[1]))
    return pl.pallas_call(
        functools.partial(_merge_kernel, yc_transposed=yc_transposed),
        grid=(n // tm,),
        in_specs=[_row_spec(tm, d), _row_spec(tm, ya.shape[1]), _row_spec(tm, yb.shape[1]),
                  yc_spec, _const_spec((1, d)), _const_spec((1, d)),
                  _const_spec(w_gates.shape), _const_spec(w_a.shape), _const_spec(w_b.shape),
                  _const_spec(w_c.shape), _const_spec(w_out.shape)],
        out_specs=_row_spec(tm, d),
        out_shape=jax.ShapeDtypeStruct((n, d), F32),
        compiler_params=_params(1),
        name="merge",
    )(x, ya, yb, yc, g_pre, g_post, w_gates, w_a, w_b, w_c, w_out)


def _head_masked(q, width):
    lane_head = lax.broadcasted_iota(jnp.int32, q.shape, 1) // width
    zero = jnp.zeros_like(q)
    return [jnp.where(lane_head == h, q, zero) for h in range(q.shape[1] // width)]


def _strict_upper_ones(n):
    j = lax.broadcasted_iota(jnp.int32, (n, n), 0)
    s = lax.broadcasted_iota(jnp.int32, (n, n), 1)
    return (j > s).astype(BF16)


def _sb_terms(qm, kt, ones_ut, causal):
    z = _dot_nt(qm, kt)
    t = jnp.log2(1.0 + jnp.exp2(-jnp.abs(z)))
    log_keep = -jnp.maximum(z, 0.0) - t
    log_beta = jnp.minimum(z, 0.0) - t
    if causal is not None:
        log_keep = jnp.where(causal, log_keep, 0.0)
    hi = log_keep.astype(BF16)
    lo = (log_keep - hi.astype(F32)).astype(BF16)
    tail = _dot(hi, ones_ut) + _dot(lo, ones_ut)
    return log_beta + tail, jnp.sum(log_keep, axis=1, keepdims=True)


def _sb_apply(terms, vt, carry, acc, causal):
    logw, total = terms
    w = jnp.exp2(logw + carry)
    if causal is not None:
        w = jnp.where(causal, w, 0.0)
    return carry + total, acc + _dot(w.astype(BF16), vt)


def _sb_tile(qm, kt, vt, carry, acc, ones_ut, causal):
    return _sb_apply(_sb_terms(qm, kt, ones_ut, causal), vt, carry, acc, causal)


def _sb_alive(carry):
    return jnp.max(carry) > SB_DEAD


def _select_heads(parts, width):
    lane_head = lax.broadcasted_iota(jnp.int32, parts[0].shape, 1) // width
    out = parts[0]
    for h in range(1, len(parts)):
        out = jnp.where(lane_head == h, parts[h], out)
    return out


def _band_sb_kernel(qa_ref, ka0_ref, ka1_ref, va0_ref, va1_ref, bias_ref, q_ref, k_ref, v_ref,
                    oa_ref, o_ref, *, tile):
    i = pl.program_id(0)
    sub = BAND_SB_SUB
    k_band = jnp.concatenate([ka0_ref[...], ka1_ref[...]], axis=0)
    v_band = jnp.concatenate([va0_ref[...], va1_ref[...]], axis=0)
    col = lax.broadcasted_iota(jnp.int32, (tile, 3 * tile), 1)
    for t in range(sub):
        kwin = k_band[(sub - 2 + t) * tile:(sub + 1 + t) * tile]
        vwin = v_band[(sub - 2 + t) * tile:(sub + 1 + t) * tile]
        in_window = col >= (2 - (sub * i + t)) * tile
        band = []
        for h, qm in enumerate(_head_masked(qa_ref[t * tile:(t + 1) * tile, :], HEAD_DIM)):
            s = jnp.where(in_window, _dot_nt(qm, kwin) + bias_ref[h], NEG)
            m = jnp.max(s, axis=1, keepdims=True)
            p = jnp.exp2(s - m)
            l = jnp.sum(p, axis=1, keepdims=True)
            band.append(_dot(p.astype(BF16), vwin) / l)
        oa_ref[t * tile:(t + 1) * tile, :] = _select_heads(band, HEAD_DIM).astype(BF16)

    row = lax.broadcasted_iota(jnp.int32, (tile, tile), 0)
    col = lax.broadcasted_iota(jnp.int32, (tile, tile), 1)
    causal = col < row
    ones_ut = _strict_upper_ones(tile)
    width = q_ref.shape[1]
    rows = lambda ref, kb: ref[pl.ds(pl.multiple_of(kb * tile, tile), tile), :]

    work = []
    for t in range(sub):
        it = sub * i + t
        qms = _head_masked(q_ref[t * tile:(t + 1) * tile, :], HEAD_DIM)
        tiles = [jnp.maximum(it - d, 0) for d in range(SB_ALWAYS)]
        terms = [[_sb_terms(qm, rows(k_ref, kb), ones_ut, causal if d == 0 else None) for qm in qms]
                 for d, kb in enumerate(tiles)]
        values = [rows(v_ref, kb) for kb in tiles]
        states = []
        for h in range(len(qms)):
            carry, acc = jnp.zeros((tile, 1), F32), jnp.zeros((tile, width), F32)
            for d in range(SB_ALWAYS):
                carry_in = carry if d == 0 else jnp.where(it >= d, carry, NEG)
                carry, acc = _sb_apply(terms[d][h], values[d], carry_in, acc, causal if d == 0 else None)
            states.append((carry, acc))
        work.append((it, qms, states))
    for t, (it, qms, states) in enumerate(work):
        first = it - SB_ALWAYS
        parts = []
        for qm, (carry, acc) in zip(qms, states):
            state = lax.while_loop(
                lambda st, first=first: (st[0] <= first) & _sb_alive(st[1]),
                lambda st, qm=qm, first=first: (st[0] + 1,) + _sb_tile(
                    qm, rows(k_ref, first - st[0]), rows(v_ref, first - st[0]), st[1], st[2], ones_ut, None),
                (jnp.int32(0), carry, acc))
            parts.append(state[2])
        o_ref[t * tile:(t + 1) * tile, :] = _select_heads(parts, HEAD_DIM).astype(BF16)


def _band_sb_attention(qa, ka, va, bias, q, k, v, tile):
    n, w = q.shape
    rows = BAND_SB_SUB * tile
    assert BAND_SB_SUB >= 2 and n % rows == 0
    blk = lambda back: pl.BlockSpec((rows, w), lambda i: (jnp.maximum(i - back, 0), 0))
    out = jax.ShapeDtypeStruct((n, w), BF16)
    return pl.pallas_call(
        functools.partial(_band_sb_kernel, tile=tile),
        grid=(n // rows,),
        in_specs=[blk(0), blk(1), blk(0), blk(1), blk(0), _const_spec(bias.shape),
                  blk(0), _const_spec((n, w)), _const_spec((n, w))],
        out_specs=[blk(0), blk(0)],
        out_shape=[out, out],
        compiler_params=_params(1),
        name="band_sb_attn",
    )(qa, ka, ka, va, va, bias, q, k, v)


def _mla_kernel(qt_ref, k_ref, vt_ref, kn_ref, ot_ref, qx_scr, p_scr, acc_scr, l_scr):
    i = pl.program_id(1)
    heads = range(2)
    tk, tq = p_scr.shape[2:]
    per_q = tq // tk
    plain = per_q * i
    key = lax.broadcasted_iota(jnp.int32, (tk, tq), 0)
    qry = lax.broadcasted_iota(jnp.int32, (tk, tq), 1)
    row = lax.broadcasted_iota(jnp.int32, (C_PAD - C_ONE, tq), 0)

    def visible(d):
        return ((d * tk + key) // CHUNK) <= (qry // CHUNK)

    def load_q(shifted):
        k_bound = jnp.max(kn_ref[...], axis=0)
        for e in heads:
            lo = e * C_PAD
            qx_scr[lo:lo + C_ONE, :] = qt_ref[lo:lo + C_ONE, :]
            if shifted:
                bound = qt_ref[lo + C_ONE:lo + C_ONE + 1, :].astype(F32) * (k_bound[:, lo:lo + 1] * NORM_MARGIN)
                tail = jnp.where(row == 0, -bound, 0.0)
            else:
                tail = jnp.zeros(row.shape, F32)
            qx_scr[lo + C_ONE:lo + C_PAD, :] = tail.astype(BF16)

    def scores(kb, e):
        return _dot(k_ref[kb, :, e * C_PAD:(e + 1) * C_PAD], qx_scr[e * C_PAD:(e + 1) * C_PAD, :])

    def values(kb, e, p):
        return _dot(vt_ref[kb, e * C_V:(e + 1) * C_V, :], p)

    def write_out():
        ot_ref[...] = jnp.concatenate([acc_scr[e] / l_scr[e] for e in heads], axis=0).astype(BF16)

    def reset():
        acc_scr[...] = jnp.zeros(acc_scr.shape, F32)
        l_scr[...] = jnp.zeros(l_scr.shape, F32)

    def fast_stage(kb, slot, masked_tile=None):
        lo = 0 if masked_tile is None else masked_tile * tk
        for e in heads:
            s = _dot(k_ref[kb, :, e * C_PAD:(e + 1) * C_PAD], qx_scr[e * C_PAD:(e + 1) * C_PAD, lo:])
            if masked_tile is not None:
                s = jnp.where(visible(0)[:, :tq - lo], s, NEG)
            p = jnp.exp2(s)
            l_scr[e, :, lo:] += jnp.sum(p, axis=0, keepdims=True)
            p_scr[slot, e, :, lo:] = p.astype(BF16)
        for e in heads:
            acc_scr[e, :, lo:] += values(kb, e, p_scr[slot, e, :, lo:])

    load_q(True)
    reset()

    def fast_group(g, _):
        for j in range(MLA_UNROLL):
            fast_stage(MLA_UNROLL * g + j, j)
        return 0
    groups = plain // MLA_UNROLL
    lax.fori_loop(0, groups, fast_group, 0)

    def fast_single(kb, _):
        fast_stage(kb, 0)
        return 0
    lax.fori_loop(MLA_UNROLL * groups, plain, fast_single, 0)
    for d in range(per_q):
        fast_stage(plain + d, (1 + d) % MLA_UNROLL, masked_tile=d)
    denom = jnp.minimum(jnp.min(l_scr[0]), jnp.min(l_scr[1]))
    trusted = denom >= MLA_MIN_DENOM

    @pl.when(trusted)
    def _():
        write_out()

    @pl.when(jnp.logical_not(trusted))
    def _():
        load_q(False)
        reset()

        def step(kb, m):
            out = []
            for e in heads:
                s = jnp.where(visible(kb - plain), scores(kb, e), NEG)
                m_new = jnp.maximum(m[e], jnp.max(s, axis=0, keepdims=True))
                alpha = jnp.exp2(m[e] - m_new)
                p = jnp.exp2(s - m_new)
                l_scr[e] = alpha * l_scr[e] + jnp.sum(p, axis=0, keepdims=True)
                acc_scr[e] = alpha * acc_scr[e] + values(kb, e, p.astype(BF16))
                out.append(m_new)
            return tuple(out)
        lax.fori_loop(0, plain + per_q, step, tuple(jnp.full((1, tq), NEG, F32) for _ in heads))
        write_out()


def _mla_attention(qt, k, vt3, kn, tq):
    n = k.shape[0]
    pairs = H_C // 2
    n_tiles, _, tk = vt3.shape
    assert n_tiles * tk == n and kn.shape == (n_tiles, 1, H_C * C_PAD)
    assert tq % tk == 0 and n % tq == 0
    k3 = k.reshape(n_tiles, tk, H_C * C_PAD)
    return pl.pallas_call(
        _mla_kernel,
        grid=(pairs, n // tq),
        in_specs=[pl.BlockSpec((2 * C_PAD, tq), lambda p, i: (p, i)),
                  pl.BlockSpec((n_tiles, tk, 2 * C_PAD), lambda p, i: (0, 0, p)),
                  pl.BlockSpec((n_tiles, 2 * C_V, tk), lambda p, i: (0, p, 0)),
                  pl.BlockSpec((n_tiles, 1, 2 * C_PAD), lambda p, i: (0, 0, p))],
        out_specs=pl.BlockSpec((2 * C_V, tq), lambda p, i: (p, i)),
        out_shape=jax.ShapeDtypeStruct((H_C * C_V, n), BF16),
        scratch_shapes=[pltpu.VMEM((2 * C_PAD, tq), BF16), pltpu.VMEM((MLA_UNROLL, 2, tk, tq), BF16),
                        pltpu.VMEM((2, C_V, tq), F32), pltpu.VMEM((2, 1, tq), F32)],
        compiler_params=_params(2),
        name="mla_attn",
    )(qt, k3, vt3, kn)


def _sample_attn_kernel(qa_ref, ka_ref, va_ref, cak_ref, cav_ref, biasc_ref, biasn_ref,
                        qb_ref, kb_ref, vb_ref, cbk_ref, cbv_ref,
                        qc_ref, ckvn_ref, krbn_ref, ckv_ref, ckr_ref, wukt_ref, wuvp_ref,
                        ya_ref, yb_ref, yc_ref, s_scr, *, tile):
    ds = qa_ref.shape[1]
    past = cbk_ref.shape[2]

    ka, va, cak, cav = ka_ref[0], va_ref[0], cak_ref[0, 0].astype(BF16), cav_ref[0, 0].astype(BF16)
    parts = []
    for h, qm in enumerate(_head_masked(qa_ref[0], HEAD_DIM)):
        s_c = _dot_nt(qm, cak) + biasc_ref[h]
        s_n = _dot_nt(qm, ka) + biasn_ref[h]
        m = jnp.maximum(jnp.max(s_c, axis=1, keepdims=True), jnp.max(s_n, axis=1, keepdims=True))
        p_c = jnp.exp2(s_c - m)
        p_n = jnp.exp2(s_n - m)
        l = jnp.sum(p_c, axis=1, keepdims=True) + jnp.sum(p_n, axis=1, keepdims=True)
        parts.append((_dot(p_c.astype(BF16), cav) + _dot(p_n.astype(BF16), va)) / l)
    ya_ref[0] = _select_heads(parts, HEAD_DIM).astype(BF16)

    row = lax.broadcasted_iota(jnp.int32, (ds, ds), 0)
    col = lax.broadcasted_iota(jnp.int32, (ds, ds), 1)
    causal = col < row
    ones_new = _strict_upper_ones(ds)
    ones_ut = _strict_upper_ones(tile)
    kb, vb = kb_ref[0], vb_ref[0]
    n_tiles = past // tile

    def cache_rows(ref, j):
        start = pl.multiple_of((n_tiles - 1 - j) * tile, tile)
        return ref[0, 0, pl.ds(start, tile), :].astype(BF16)

    b_heads = _head_masked(qb_ref[0], HEAD_DIM)
    b_states = []
    for qm in b_heads:
        state = _sb_tile(qm, kb, vb, jnp.zeros((ds, 1), F32), jnp.zeros((ds, W_AB), F32),
                         ones_new, causal)
        b_states.append(_sb_tile(qm, cache_rows(cbk_ref, 0), cache_rows(cbv_ref, 0), state[0], state[1],
                                 ones_ut, None))

    kv_rank = ckv_ref.shape[3]
    qc = qc_ref[0]
    rope_lanes = lax.broadcasted_iota(jnp.int32, (ds, C_PAD), 1) >= C_NOPE
    q_rows = []
    for h in range(H_C):
        qh = qc[:, h * C_PAD:(h + 1) * C_PAD]
        q_lat = _dot(qh, wukt_ref[h]).astype(BF16)
        q_rows.append(jnp.concatenate([q_lat, jnp.where(rope_lanes, qh, jnp.zeros_like(qh))], axis=1))
    q_all = jnp.concatenate(q_rows, axis=0)

    place = (lax.broadcasted_iota(jnp.int32, (C_ROPE, C_PAD), 1)
             == lax.broadcasted_iota(jnp.int32, (C_ROPE, C_PAD), 0) + C_NOPE).astype(BF16)
    tile_c = SAMPLE_LATENT_TILE if past % SAMPLE_LATENT_TILE == 0 else tile
    cache_tiles = range(past // tile_c)
    latent = lambda j: ckv_ref[0, 0, j * tile_c:(j + 1) * tile_c, :].astype(BF16)

    lat_new = ckvn_ref[0].astype(BF16)
    s_new = _dot_nt(q_all, jnp.concatenate([lat_new, krbn_ref[0].astype(BF16)], axis=1))
    m = jnp.max(s_new, axis=1, keepdims=True)
    for j in cache_tiles:
        k_rope = _dot(ckr_ref[0, 0, j * tile_c:(j + 1) * tile_c, :].astype(BF16), place).astype(BF16)
        s = _dot_nt(q_all, jnp.concatenate([latent(j), k_rope], axis=1))
        s_scr[:, j * tile_c:(j + 1) * tile_c] = s
        m = jnp.maximum(m, jnp.max(s, axis=1, keepdims=True))
    p = jnp.exp2(s_new - m)
    denom = jnp.sum(p, axis=1, keepdims=True)
    o_lat = _dot(p.astype(BF16), lat_new)
    for j in cache_tiles:
        p = jnp.exp2(s_scr[:, j * tile_c:(j + 1) * tile_c] - m)
        denom = denom + jnp.sum(p, axis=1, keepdims=True)
        o_lat = o_lat + _dot(p.astype(BF16), latent(j))
    o_lat = (o_lat / denom).astype(BF16)
    out = jnp.zeros((ds, H_C * C_V), F32)
    for h in range(H_C):
        out = out + _dot(o_lat[h * ds:(h + 1) * ds], wuvp_ref[h])
    yc_ref[0] = out.astype(BF16)

    parts = []
    for qm, state in zip(b_heads, b_states):
        state = lax.while_loop(
            lambda st: (st[0] < n_tiles) & _sb_alive(st[1]),
            lambda st, qm=qm: (st[0] + 1,) + _sb_tile(qm, cache_rows(cbk_ref, st[0]), cache_rows(cbv_ref, st[0]),
                                                      st[1], st[2], ones_ut, None),
            (jnp.int32(1),) + state)
        parts.append(state[2])
    yb_ref[0] = _select_heads(parts, HEAD_DIM).astype(BF16)


def _sample_attention(new, caches, l, bias_c, bias_n, w_ukt, w_uvp, tile):
    qa, ka, va, qb, kb, vb, qc, ckvn, krbn = new
    cak, cav, cbk, cbv, ckv, ckr = caches
    nb, ds, _ = qa.shape
    past = ckv.shape[2]
    req = lambda a: pl.BlockSpec((1,) + a.shape[1:], lambda b: (b, 0, 0))
    cache = lambda a: pl.BlockSpec((1, 1) + a.shape[2:], lambda b: (l, b, 0, 0))
    ins = [qa, ka, va, cak, cav, bias_c, bias_n, qb, kb, vb, cbk, cbv, qc, ckvn, krbn, ckv, ckr,
           w_ukt, w_uvp]
    specs = [req(a) for a in ins]
    for idx in (3, 4, 10, 11, 15, 16):
        specs[idx] = cache(ins[idx])
    for idx in (5, 6, 17, 18):
        specs[idx] = _const_spec(ins[idx].shape)
    out_shape = [jax.ShapeDtypeStruct((nb, ds, W_AB), BF16), jax.ShapeDtypeStruct((nb, ds, W_AB), BF16),
                 jax.ShapeDtypeStruct((nb, ds, H_C * C_V), BF16)]
    return pl.pallas_call(
        functools.partial(_sample_attn_kernel, tile=tile),
        grid=(nb,),
        in_specs=specs,
        out_specs=[req(s) for s in out_shape],
        out_shape=out_shape,
        scratch_shapes=[pltpu.VMEM((H_C * ds, past), F32)],
        compiler_params=_params(1),
        name="sample_attn",
    )(*ins)


def _rope_tables(pos):
    half = C_ROPE // 2
    freq = ROPE_THETA ** (-jnp.arange(half, dtype=F32) / half)
    ang = pos.astype(F32)[:, None] * freq[None, :]
    cos, sin = jnp.cos(ang), jnp.sin(ang)
    n = pos.shape[0]
    z = lambda w: jnp.zeros((n, w), F32)
    tc = jnp.concatenate([jnp.ones((n, C_NOPE), F32), cos, cos, z(C_PAD - C_NOPE - C_ROPE)], axis=1)
    ts1 = jnp.concatenate([z(C_NOPE + half), sin, z(C_PAD - C_NOPE - C_ROPE)], axis=1)
    ts2 = jnp.concatenate([z(C_NOPE), -sin, z(C_PAD - C_NOPE - half)], axis=1)
    return tc, ts1, ts2


def _band_bias(rel_bias, q0, nq, k0, nk):
    period = 1 << int(np.ceil(np.log2(nq + nk)))
    d = np.arange(period)
    d = np.where(d >= nk, d - period, d)
    idx = np.clip((q0 - k0) - d, -REL_CLIP, REL_CLIP) + REL_CLIP
    g = rel_bias.astype(F32)[:, idx]
    toe = jnp.tile(g, (1, nq))[:, :nq * (period - 1)].reshape(-1, nq, period - 1)[:, :, :nk]
    q_pos, k_pos = q0 + np.arange(nq), k0 + np.arange(nk)
    qc, kc = q_pos[:, None] // CHUNK, k_pos[None, :] // CHUNK
    vis = (kc <= qc) & (kc >= qc - BAND_CHUNKS)
    return jnp.where(jnp.asarray(vis)[None], toe * LOG2E, NEG)


def _cast_kernel(w_ref, o_ref):
    o_ref[...] = w_ref[0].astype(BF16)


def _layer_bf16(w, l):
    _, rows, cols = w.shape
    rb = rows // 4 if rows % 64 == 0 else rows
    return pl.pallas_call(
        _cast_kernel,
        grid=(rows // rb,),
        in_specs=[pl.BlockSpec((1, rb, cols), lambda i: (l, i, 0))],
        out_specs=pl.BlockSpec((rb, cols), lambda i: (i, 0)),
        out_shape=jax.ShapeDtypeStruct((rows, cols), BF16),
        compiler_params=_params(1),
        name="cast_bf16",
    )(w)


def _layer_weights(l, w_in, w_uq, w_uk, w_uv):
    d = w_in.shape[1]
    q_rank, kv_rank = w_uq.shape[1], w_uk.shape[1]
    o = 6 * W_AB + q_rank + kv_rank
    zc = lambda w: jnp.zeros((d, w), w_in.dtype)
    w_main = jnp.concatenate([w_in[l, :, :o], zc(C_NOPE), w_in[l, :, o:o + C_ROPE],
                              zc(C_PAD - C_NOPE - C_ROPE)], axis=1).astype(BF16)
    w_gates = w_in[l, :, o + C_ROPE:].astype(BF16)
    pad_heads = lambda w: jnp.pad(w, ((0, 0), (0, 0), (0, C_PAD - w.shape[2]))).reshape(w.shape[0], -1)
    w_uq_p = pad_heads(w_uq[l]).astype(BF16)
    w_ukv = jnp.concatenate([pad_heads(w_uk[l]), w_uv[l].reshape(kv_rank, -1)], axis=1).astype(BF16)
    w_ukt = jnp.pad(jnp.transpose(w_uk[l], (1, 2, 0)), ((0, 0), (0, C_PAD - C_NOPE), (0, 0))).astype(BF16)
    w_uvp = jnp.einsum('rhd,hg->hrgd', w_uv[l], jnp.eye(H_C, dtype=w_uv.dtype)).reshape(
        H_C, kv_rank, H_C * C_V).astype(BF16)
    return w_main, w_gates, w_uq_p, w_ukv, w_ukt, w_uvp


def kernel(x_prompt, x_sample, cache_a_k, cache_a_v, cache_b_k, cache_b_v, cache_c_kv, cache_c_kr, ffn1_norm_pre, ffn1_norm_post, ffn1_w_gate, ffn1_w_up, ffn1_w_down, mix_norm_pre, mix_norm_post, w_in, cq_norm, ckv_norm, w_uq, w_uk, w_uv, rel_bias_a, w_br_a, w_br_b, w_br_c, w_out, ffn2_norm_pre, ffn2_norm_post, ffn2_w_gate, ffn2_w_up, ffn2_w_down):
    batch, seq, d = x_prompt.shape
    nb, ds, _ = x_sample.shape
    depth = w_in.shape[0]
    past = cache_b_k.shape[2]
    win_cache = cache_a_k.shape[2]
    assert batch == 1, "prompt group is a single sequence"
    tile = 256
    tm = 512 if seq % 512 == 0 else tile
    assert seq % tile == 0 and past % tile == 0 and tile % CHUNK == 0 and 2 * tile >= WIN_A
    assert past % CHUNK + ds <= CHUNK
    ns = nb * ds

    xp = x_prompt.reshape(seq, d)
    xs = x_sample.reshape(ns, d)
    tabs_p = _rope_tables(jnp.arange(seq))
    tabs_s = _rope_tables(jnp.tile(past + jnp.arange(ds), nb))

    row = lambda g, l: g[l][None, :]
    base = 2 * tile

    merged_heads = lambda c: c.reshape(c.shape[:3] + (-1,))
    caches = (merged_heads(cache_a_k), merged_heads(cache_a_v), merged_heads(cache_b_k),
              merged_heads(cache_b_v), cache_c_kv, cache_c_kr)
    states_p, states_s = [], []
    for l in range(depth):
        w_main, w_gates, w_uq_p, w_ukv, w_ukt, w_uvp = _layer_weights(l, w_in, w_uq, w_uk, w_uv)
        ffn1 = (row(ffn1_norm_pre, l), row(ffn1_norm_post, l), _layer_bf16(ffn1_w_gate, l),
                _layer_bf16(ffn1_w_up, l), _layer_bf16(ffn1_w_down, l))
        ffn2 = (row(ffn2_norm_pre, l), row(ffn2_norm_post, l), _layer_bf16(ffn2_w_gate, l),
                _layer_bf16(ffn2_w_up, l), _layer_bf16(ffn2_w_down, l))
        mix_in = (row(mix_norm_pre, l), w_main, row(cq_norm, l), row(ckv_norm, l), w_uq_p, w_ukv)
        mrg = (row(mix_norm_pre, l), row(mix_norm_post, l), w_gates, _layer_bf16(w_br_a, l),
               _layer_bf16(w_br_b, l), _layer_bf16(w_br_c, l), _layer_bf16(w_out, l))
        bias_p = _band_bias(rel_bias_a[l], base, tile, base - 2 * tile, 3 * tile)
        bias_c = _band_bias(rel_bias_a[l], past, ds, past - win_cache, win_cache)
        bias_n = _band_bias(rel_bias_a[l], past, ds, past, ds)

        xp = _ffn(xp, *ffn1, tm)
        (qa, kab, vab, qb, kbb, vbb, ka, va, kb, vb, ckv, krb, qc, kc, vc, kn) = _mixer_in(
            xp, *mix_in, tabs_p, tm, q_transposed=True)
        ya, yb = _band_sb_attention(qa, kab, vab, bias_p, qb, kbb, vbb, tile)
        yc = _mla_attention(qc, kc, vc, kn, 2 * tm if seq % (2 * tm) == 0 else tm)
        xp = _merge(xp, ya, yb, yc, *mrg, tm, yc_transposed=True)
        xp = _ffn(xp, *ffn2, tm)
        win = min(WIN_A, seq)
        heads = lambda t, h: t.reshape(1, t.shape[0], h, t.shape[1] // h)
        states_p.append((heads(ka[-win:], H_A), heads(va[-win:], H_A), heads(kb, H_B), heads(vb, H_B),
                         ckv[None], krb[None, :, C_NOPE:C_NOPE + C_ROPE]))

        xs = _ffn(xs, *ffn1, ns)
        (qa, kab, vab, qb, kbb, vbb, ka, va, kb, vb, ckv, krb, qc, kc, vc, _) = _mixer_in(
            xs, *mix_in, tabs_s, ns)
        per_req = lambda t: t.reshape(nb, ds, t.shape[1])
        new = tuple(per_req(t) for t in (qa, kab, vab, qb, kbb, vbb, qc, ckv, krb))
        ya, yb, yc = _sample_attention(new, caches, l, bias_c, bias_n, w_ukt, w_uvp, tile)
        xs = _merge(xs, ya.reshape(ns, -1), yb.reshape(ns, -1), yc.reshape(ns, -1), *mrg, ns)
        xs = _ffn(xs, *ffn2, ns)
        heads_s = lambda t, h: t.reshape(nb, ds, h, t.shape[1] // h)
        states_s.append((heads_s(ka, H_A), heads_s(va, H_A), heads_s(kb, H_B), heads_s(vb, H_B),
                         ckv.reshape(nb, ds, -1), krb[:, C_NOPE:C_NOPE + C_ROPE].reshape(nb, ds, C_ROPE)))

    stack = lambda states, i: jnp.stack([s[i] for s in states], axis=0)
    return ((xp.reshape(batch, seq, d), xs.reshape(nb, ds, d))
            + tuple(stack(states_p, i) for i in range(6))
            + tuple(stack(states_s, i) for i in range(6)))
```

```python
import functools

import numpy as np
import jax
import jax.numpy as jnp
from jax import lax
from jax.experimental import pallas as pl
from jax.experimental.pallas import tpu as pltpu

F32 = jnp.float32
BF16 = jnp.bfloat16

CHUNK = 64
BAND_CHUNKS = 8
WIN_A = BAND_CHUNKS * CHUNK
HEAD_DIM = 64
H_A = 4
H_B = 4
H_C = 8
C_NOPE = 64
C_ROPE = 32
C_V = 64
REL_CLIP = 128
ROPE_THETA = 10000.0
EPS = 1e-6
N_BRANCH = 3
W_AB = H_A * HEAD_DIM
C_PAD = 128
ATTN_SCALE = HEAD_DIM ** -0.5
MLA_SCALE = (C_NOPE + C_ROPE) ** -0.5
LOG2E = float(np.log2(np.e))
C_ONE = C_NOPE + C_ROPE
NORM_MARGIN = 1.02
MLA_MIN_DENOM = 2.0 ** -60
MLA_UNROLL = 4
SAMPLE_LATENT_TILE = 1024
NEG = -1e30
BAND_SB_SUB = 4
SB_ALWAYS = 2
SB_DEAD = -150.0

LANES = 128
MXU_WIDTH = 256
VMEM_LIMIT = 52 * 1024 * 1024

NT_DIMS = (((1,), (1,)), ((), ()))
TN_DIMS = (((0,), (0,)), ((), ()))


def _rms(x, g):
    return x * lax.rsqrt(jnp.mean(x * x, axis=-1, keepdims=True) + EPS) * g


def _dot(a, b):
    return jnp.dot(a, b, preferred_element_type=F32)


def _dot_nt(a, b):
    return lax.dot_general(a, b, NT_DIMS, preferred_element_type=F32)


def _dot_tn(a, b):
    return lax.dot_general(a, b, TN_DIMS, preferred_element_type=F32)


def _const_spec(shape):
    zeros = (0,) * len(shape)
    return pl.BlockSpec(shape, lambda *_: zeros, pipeline_mode=pl.Buffered(1))


def _row_spec(tm, width):
    return pl.BlockSpec((tm, width), lambda i: (i, 0))


def _params(n_grid):
    return pltpu.CompilerParams(dimension_semantics=("arbitrary",) * n_grid,
                                vmem_limit_bytes=VMEM_LIMIT)


def _ffn_kernel(x_ref, gpre_ref, gpost_ref, wg_ref, wu_ref, wd_ref, o_ref, *, f_cuts):
    x = x_ref[...]
    h = _rms(x, gpre_ref[...]).astype(BF16)
    y = jnp.zeros(x.shape, F32)
    for lo, hi in zip(f_cuts[:-1], f_cuts[1:]):
        sl = slice(lo, hi)
        g = _dot(h, wg_ref[:, sl])
        u = _dot(h, wu_ref[:, sl])
        a = (g * jax.nn.sigmoid(g) * u).astype(BF16)
        y = y + _dot(a, wd_ref[sl, :])
    o_ref[...] = x + 0.5 * _rms(y, gpost_ref[...])


def _ffn(x, g_pre, g_post, w_gate, w_up, w_down, tm):
    n, d = x.shape
    d_ff = w_gate.shape[1]
    cut = (d_ff // 2) // MXU_WIDTH * MXU_WIDTH
    f_cuts = (0, cut, d_ff) if 0 < cut < d_ff else (0, d_ff)
    return pl.pallas_call(
        functools.partial(_ffn_kernel, f_cuts=f_cuts),
        grid=(n // tm,),
        in_specs=[_row_spec(tm, d), _const_spec((1, d)), _const_spec((1, d)),
                  _const_spec((d, d_ff)), _const_spec((d, d_ff)), _const_spec((d_ff, d))],
        out_specs=_row_spec(tm, d),
        out_shape=jax.ShapeDtypeStruct((n, d), F32),
        compiler_params=_params(1),
        name="ffn",
    )(x, g_pre, g_post, w_gate, w_up, w_down)


def _rope(blk, tc, ts1, ts2):
    return blk * tc + pltpu.roll(blk, 16, 1) * ts1 + pltpu.roll(blk, C_PAD - 16, 1) * ts2


def _mixer_in_kernel(x_ref, gpre_ref, win_ref, gcq_ref, gckv_ref, wuq_ref, wukv_ref,
                     tc_ref, ts1_ref, ts2_ref,
                     qa_ref, kab_ref, vab_ref, qb_ref, kbb_ref, vbb_ref,
                     ka_ref, va_ref, kb_ref, vb_ref, ckv_ref, krb_ref,
                     qc_ref, kc_ref, vc_ref, kn_ref, *, q_transposed):
    w = W_AB
    q_rank = gcq_ref.shape[1]
    kv_rank = gckv_ref.shape[1]
    u = _rms(x_ref[...], gpre_ref[...]).astype(BF16)
    o = 6 * w
    lat = _dot(u, win_ref[:, o:])
    cq = lat[:, :q_rank]
    ckv = lat[:, q_rank:q_rank + kv_rank]
    krb = lat[:, q_rank + kv_rank:q_rank + kv_rank + C_PAD]
    proj = _dot(u, win_ref[:, :o])
    qa, ka, va = proj[:, 0:w], proj[:, w:2 * w], proj[:, 2 * w:3 * w]
    qb, kb, vb = proj[:, 3 * w:4 * w], proj[:, 4 * w:5 * w], proj[:, 5 * w:6 * w]
    qa_ref[...] = (qa * (ATTN_SCALE * LOG2E)).astype(BF16)
    qb_ref[...] = (qb * (ATTN_SCALE * LOG2E)).astype(BF16)
    ka_ref[...] = ka
    va_ref[...] = va
    kb_ref[...] = kb
    vb_ref[...] = vb
    kab_ref[...] = ka.astype(BF16)
    vab_ref[...] = va.astype(BF16)
    kbb_ref[...] = kb.astype(BF16)
    vbb_ref[...] = vb.astype(BF16)
    tc, ts1, ts2 = tc_ref[...], ts1_ref[...], ts2_ref[...]
    cqn = _rms(cq, gcq_ref[...]).astype(BF16)
    qall = _dot(cqn, wuq_ref[...])
    ckvn = _rms(ckv, gckv_ref[...])
    ckv_ref[...] = ckvn
    kv = _dot(ckvn.astype(BF16), wukv_ref[...])
    krot = _rope(krb, tc, ts1, ts2)
    krb_ref[...] = krot
    one_lane = (lax.broadcasted_iota(jnp.int32, (1, C_PAD), 1) == C_ONE).astype(F32)
    for h in range(H_C):
        sl = slice(h * C_PAD, (h + 1) * C_PAD)
        qh = _rope(qall[:, sl], tc, ts1, ts2) * (MLA_SCALE * LOG2E)
        if q_transposed:
            qt = qh.T
            norm = jnp.sqrt(jnp.sum(qt * qt, axis=0, keepdims=True)) * NORM_MARGIN
            row = lax.broadcasted_iota(jnp.int32, qt.shape, 0)
            qc_ref[sl, :] = jnp.where(row == C_ONE, norm, qt).astype(BF16)
        else:
            qc_ref[:, sl] = qh.astype(BF16)
        kh = kv[:, sl] + krot
        kc_ref[:, sl] = (kh + one_lane).astype(BF16)
        k_sq = _dot((kh * kh).astype(BF16), jnp.ones((C_PAD, C_PAD), BF16))
        kn_ref[0, :, sl] = jnp.sqrt(jnp.max(k_sq, axis=0, keepdims=True)) * NORM_MARGIN
    if not q_transposed:
        vc_ref[...] = kv[:, H_C * C_PAD:].astype(BF16)
    else:
        for p in range(H_C // 2):
            o = H_C * C_PAD + p * 2 * C_V
            vc_ref[0, p * 2 * C_V:(p + 1) * 2 * C_V, :] = kv[:, o:o + 2 * C_V].T.astype(BF16)


def _mixer_in(x, g_pre, w_main, g_cq, g_ckv, w_uq, w_ukv, tabs, tm, q_transposed=False):
    n, d = x.shape
    q_rank, kv_rank = g_cq.shape[1], g_ckv.shape[1]
    wc = H_C * C_PAD
    wv = H_C * C_V
    bf = lambda width: jax.ShapeDtypeStruct((n, width), BF16)
    f32 = lambda width: jax.ShapeDtypeStruct((n, width), F32)
    q_shape = jax.ShapeDtypeStruct((wc, n), BF16) if q_transposed else bf(wc)
    q_spec = pl.BlockSpec((wc, tm), lambda i: (0, i)) if q_transposed else _row_spec(tm, wc)
    v_shape = jax.ShapeDtypeStruct((n // tm, wv, tm), BF16) if q_transposed else bf(wv)
    v_spec = (pl.BlockSpec((1, wv, tm), lambda i: (i, 0, 0)) if q_transposed
              else _row_spec(tm, wv))
    out_shape = ([bf(W_AB)] * 6 + [f32(W_AB)] * 4 + [f32(kv_rank), f32(C_PAD)]
                 + [q_shape, bf(wc), v_shape, jax.ShapeDtypeStruct((n // tm, 1, wc), F32)])
    out_specs = ([_row_spec(tm, W_AB)] * 10 + [_row_spec(tm, kv_rank), _row_spec(tm, C_PAD)]
                 + [q_spec, _row_spec(tm, wc), v_spec, pl.BlockSpec((1, 1, wc), lambda i: (i, 0, 0))])
    return pl.pallas_call(
        functools.partial(_mixer_in_kernel, q_transposed=q_transposed),
        grid=(n // tm,),
        in_specs=[_row_spec(tm, d), _const_spec((1, d)), _const_spec(w_main.shape),
                  _const_spec((1, q_rank)), _const_spec((1, kv_rank)),
                  _const_spec(w_uq.shape), _const_spec(w_ukv.shape),
                  _row_spec(tm, C_PAD), _row_spec(tm, C_PAD), _row_spec(tm, C_PAD)],
        out_specs=out_specs,
        out_shape=out_shape,
        compiler_params=_params(1),
        name="mixer_in",
    )(x, g_pre, w_main, g_cq, g_ckv, w_uq, w_ukv, *tabs)


def _merge_kernel(x_ref, ya_ref, yb_ref, yc_ref, gpre_ref, gpost_ref, wgates_ref,
                  wa_ref, wb_ref, wc_ref, wout_ref, o_ref, *, yc_transposed):
    x = x_ref[...]
    d = x.shape[1]
    u = _rms(x, gpre_ref[...]).astype(BF16)
    m = jnp.zeros(x.shape, F32)
    for i, (y_ref, w_ref) in enumerate(((ya_ref, wa_ref), (yb_ref, wb_ref), (yc_ref, wc_ref))):
        gate = jax.nn.sigmoid(_dot(u, wgates_ref[:, i * d:(i + 1) * d]))
        branch_dot = _dot_tn if (yc_transposed and y_ref is yc_ref) else _dot
        m = m + gate * branch_dot(y_ref[...], w_ref[...])
    mixed = _dot(m.astype(BF16), wout_ref[...])
    o_ref[...] = x + _rms(mixed, gpost_ref[...])


def _merge(x, ya, yb, yc, g_pre, g_post, w_gates, w_a, w_b, w_c, w_out, tm, yc_transposed=False):
    n, d = x.shape
    yc_spec = (pl.BlockSpec((yc.shape[0], tm), lambda i: (0, i)) if yc_transposed
               else _row_spec(tm, yc.shape[1]))
    return pl.pallas_call(
        functools.partial(_merge_kernel, yc_transposed=yc_transposed),
        grid=(n // tm,),
        in_specs=[_row_spec(tm, d), _row_spec(tm, ya.shape[1]), _row_spec(tm, yb.shape[1]),
                  yc_spec, _const_spec((1, d)), _const_spec((1, d)),
                  _const_spec(w_gates.shape), _const_spec(w_a.shape), _const_spec(w_b.shape),
                  _const_spec(w_c.shape), _const_spec(w_out.shape)],
        out_specs=_row_spec(tm, d),
        out_shape=jax.ShapeDtypeStruct((n, d), F32),
        compiler_params=_params(1),
        name="merge",
    )(x, ya, yb, yc, g_pre, g_post, w_gates, w_a, w_b, w_c, w_out)


def _head_masked(q, width):
    lane_head = lax.broadcasted_iota(jnp.int32, q.shape, 1) // width
    zero = jnp.zeros_like(q)
    return [jnp.where(lane_head == h, q, zero) for h in range(q.shape[1] // width)]


def _strict_upper_ones(n):
    j = lax.broadcasted_iota(jnp.int32, (n, n), 0)
    s = lax.broadcasted_iota(jnp.int32, (n, n), 1)
    return (j > s).astype(BF16)


def _sb_terms(qm, kt, ones_ut, causal):
    z = _dot_nt(qm, kt)
    t = jnp.log2(1.0 + jnp.exp2(-jnp.abs(z)))
    log_keep = -jnp.maximum(z, 0.0) - t
    log_beta = jnp.minimum(z, 0.0) - t
    if causal is not None:
        log_keep = jnp.where(causal, log_keep, 0.0)
    hi = log_keep.astype(BF16)
    lo = (log_keep - hi.astype(F32)).astype(BF16)
    tail = _dot(hi, ones_ut) + _dot(lo, ones_ut)
    return log_beta + tail, jnp.sum(log_keep, axis=1, keepdims=True)


def _sb_apply(terms, vt, carry, acc, causal):
    logw, total = terms
    w = jnp.exp2(logw + carry)
    if causal is not None:
        w = jnp.where(causal, w, 0.0)
    return carry + total, acc + _dot(w.astype(BF16), vt)


def _sb_tile(qm, kt, vt, carry, acc, ones_ut, causal):
    return _sb_apply(_sb_terms(qm, kt, ones_ut, causal), vt, carry, acc, causal)


def _sb_alive(carry):
    return jnp.max(carry) > SB_DEAD


def _select_heads(parts, width):
    lane_head = lax.broadcasted_iota(jnp.int32, parts[0].shape, 1) // width
    out = parts[0]
    for h in range(1, len(parts)):
        out = jnp.where(lane_head == h, parts[h], out)
    return out


def _band_sb_kernel(qa_ref, ka0_ref, ka1_ref, va0_ref, va1_ref, bias_ref, q_ref, k_ref, v_ref,
                    oa_ref, o_ref, *, tile):
    i = pl.program_id(0)
    sub = BAND_SB_SUB
    k_band = jnp.concatenate([ka0_ref[...], ka1_ref[...]], axis=0)
    v_band = jnp.concatenate([va0_ref[...], va1_ref[...]], axis=0)
    col = lax.broadcasted_iota(jnp.int32, (tile, 3 * tile), 1)
    for t in range(sub):
        kwin = k_band[(sub - 2 + t) * tile:(sub + 1 + t) * tile]
        vwin = v_band[(sub - 2 + t) * tile:(sub + 1 + t) * tile]
        in_window = col >= (2 - (sub * i + t)) * tile
        band = []
        for h, qm in enumerate(_head_masked(qa_ref[t * tile:(t + 1) * tile, :], HEAD_DIM)):
            s = jnp.where(in_window, _dot_nt(qm, kwin) + bias_ref[h], NEG)
            m = jnp.max(s, axis=1, keepdims=True)
            p = jnp.exp2(s - m)
            l = jnp.sum(p, axis=1, keepdims=True)
            band.append(_dot(p.astype(BF16), vwin) / l)
        oa_ref[t * tile:(t + 1) * tile, :] = _select_heads(band, HEAD_DIM).astype(BF16)

    row = lax.broadcasted_iota(jnp.int32, (tile, tile), 0)
    col = lax.broadcasted_iota(jnp.int32, (tile, tile), 1)
    causal = col < row
    ones_ut = _strict_upper_ones(tile)
    width = q_ref.shape[1]
    rows = lambda ref, kb: ref[pl.ds(pl.multiple_of(kb * tile, tile), tile), :]

    work = []
    for t in range(sub):
        it = sub * i + t
        qms = _head_masked(q_ref[t * tile:(t + 1) * tile, :], HEAD_DIM)
        tiles = [jnp.maximum(it - d, 0) for d in range(SB_ALWAYS)]
        terms = [[_sb_terms(qm, rows(k_ref, kb), ones_ut, causal if d == 0 else None) for qm in qms]
                 for d, kb in enumerate(tiles)]
        values = [rows(v_ref, kb) for kb in tiles]
        states = []
        for h in range(len(qms)):
            carry, acc = jnp.zeros((tile, 1), F32), jnp.zeros((tile, width), F32)
            for d in range(SB_ALWAYS):
                carry_in = carry if d == 0 else jnp.where(it >= d, carry, NEG)
                carry, acc = _sb_apply(terms[d][h], values[d], carry_in, acc, causal if d == 0 else None)
            states.append((carry, acc))
        work.append((it, qms, states))
    for t, (it, qms, states) in enumerate(work):
        first = it - SB_ALWAYS
        parts = []
        for qm, (carry, acc) in zip(qms, states):
            state = lax.while_loop(
                lambda st, first=first: (st[0] <= first) & _sb_alive(st[1]),
                lambda st, qm=qm, first=first: (st[0] + 1,) + _sb_tile(
                    qm, rows(k_ref, first - st[0]), rows(v_ref, first - st[0]), st[1], st[2], ones_ut, None),
                (jnp.int32(0), carry, acc))
            parts.append(state[2])
        o_ref[t * tile:(t + 1) * tile, :] = _select_heads(parts, HEAD_DIM).astype(BF16)


def _band_sb_attention(qa, ka, va, bias, q, k, v, tile):
    n, w = q.shape
    rows = BAND_SB_SUB * tile
    assert BAND_SB_SUB >= 2 and n % rows == 0
    blk = lambda back: pl.BlockSpec((rows, w), lambda i: (jnp.maximum(i - back, 0), 0))
    out = jax.ShapeDtypeStruct((n, w), BF16)
    return pl.pallas_call(
        functools.partial(_band_sb_kernel, tile=tile),
        grid=(n // rows,),
        in_specs=[blk(0), blk(1), blk(0), blk(1), blk(0), _const_spec(bias.shape),
                  blk(0), _const_spec((n, w)), _const_spec((n, w))],
        out_specs=[blk(0), blk(0)],
        out_shape=[out, out],
        compiler_params=_params(1),
        name="band_sb_attn",
    )(qa, ka, ka, va, va, bias, q, k, v)


def _mla_kernel(qt_ref, k_ref, vt_ref, kn_ref, ot_ref, qx_scr, p_scr, acc_scr, l_scr):
    i = pl.program_id(1)
    heads = range(2)
    tk, tq = p_scr.shape[2:]
    per_q = tq // tk
    plain = per_q * i
    key = lax.broadcasted_iota(jnp.int32, (tk, tq), 0)
    qry = lax.broadcasted_iota(jnp.int32, (tk, tq), 1)
    row = lax.broadcasted_iota(jnp.int32, (C_PAD - C_ONE, tq), 0)

    def visible(d):
        return ((d * tk + key) // CHUNK) <= (qry // CHUNK)

    def load_q(shifted):
        k_bound = jnp.max(kn_ref[...], axis=0)
        for e in heads:
            lo = e * C_PAD
            qx_scr[lo:lo + C_ONE, :] = qt_ref[lo:lo + C_ONE, :]
            if shifted:
                bound = qt_ref[lo + C_ONE:lo + C_ONE + 1, :].astype(F32) * (k_bound[:, lo:lo + 1] * NORM_MARGIN)
                tail = jnp.where(row == 0, -bound, 0.0)
            else:
                tail = jnp.zeros(row.shape, F32)
            qx_scr[lo + C_ONE:lo + C_PAD, :] = tail.astype(BF16)

    def scores(kb, e):
        return _dot(k_ref[kb, :, e * C_PAD:(e + 1) * C_PAD], qx_scr[e * C_PAD:(e + 1) * C_PAD, :])

    def values(kb, e, p):
        return _dot(vt_ref[kb, e * C_V:(e + 1) * C_V, :], p)

    def write_out():
        ot_ref[...] = jnp.concatenate([acc_scr[e] / l_scr[e] for e in heads], axis=0).astype(BF16)

    def reset():
        acc_scr[...] = jnp.zeros(acc_scr.shape, F32)
        l_scr[...] = jnp.zeros(l_scr.shape, F32)

    def fast_stage(kb, slot, masked_tile=None):
        lo = 0 if masked_tile is None else masked_tile * tk
        for e in heads:
            s = _dot(k_ref[kb, :, e * C_PAD:(e + 1) * C_PAD], qx_scr[e * C_PAD:(e + 1) * C_PAD, lo:])
            if masked_tile is not None:
                s = jnp.where(visible(0)[:, :tq - lo], s, NEG)
            p = jnp.exp2(s)
            l_scr[e, :, lo:] += jnp.sum(p, axis=0, keepdims=True)
            p_scr[slot, e, :, lo:] = p.astype(BF16)
        for e in heads:
            acc_scr[e, :, lo:] += values(kb, e, p_scr[slot, e, :, lo:])

    load_q(True)
    reset()

    def fast_group(g, _):
        for j in range(MLA_UNROLL):
            fast_stage(MLA_UNROLL * g + j, j)
        return 0
    groups = plain // MLA_UNROLL
    lax.fori_loop(0, groups, fast_group, 0)

    def fast_single(kb, _):
        fast_stage(kb, 0)
        return 0
    lax.fori_loop(MLA_UNROLL * groups, plain, fast_single, 0)
    for d in range(per_q):
        fast_stage(plain + d, (1 + d) % MLA_UNROLL, masked_tile=d)
    denom = jnp.minimum(jnp.min(l_scr[0]), jnp.min(l_scr[1]))
    trusted = denom >= MLA_MIN_DENOM

    @pl.when(trusted)
    def _():
        write_out()

    @pl.when(jnp.logical_not(trusted))
    def _():
        load_q(False)
        reset()

        def step(kb, m):
            out = []
            for e in heads:
                s = jnp.where(visible(kb - plain), scores(kb, e), NEG)
                m_new = jnp.maximum(m[e], jnp.max(s, axis=0, keepdims=True))
                alpha = jnp.exp2(m[e] - m_new)
                p = jnp.exp2(s - m_new)
                l_scr[e] = alpha * l_scr[e] + jnp.sum(p, axis=0, keepdims=True)
                acc_scr[e] = alpha * acc_scr[e] + values(kb, e, p.astype(BF16))
                out.append(m_new)
            return tuple(out)
        lax.fori_loop(0, plain + per_q, step, tuple(jnp.full((1, tq), NEG, F32) for _ in heads))
        write_out()


def _mla_attention(qt, k, vt3, kn, tq):
    n = k.shape[0]
    pairs = H_C // 2
    n_tiles, _, tk = vt3.shape
    assert n_tiles * tk == n and kn.shape == (n_tiles, 1, H_C * C_PAD)
    assert tq % tk == 0 and n % tq == 0
    k3 = k.reshape(n_tiles, tk, H_C * C_PAD)
    return pl.pallas_call(
        _mla_kernel,
        grid=(pairs, n // tq),
        in_specs=[pl.BlockSpec((2 * C_PAD, tq), lambda p, i: (p, i)),
                  pl.BlockSpec((n_tiles, tk, 2 * C_PAD), lambda p, i: (0, 0, p)),
                  pl.BlockSpec((n_tiles, 2 * C_V, tk), lambda p, i: (0, p, 0)),
                  pl.BlockSpec((n_tiles, 1, 2 * C_PAD), lambda p, i: (0, 0, p))],
        out_specs=pl.BlockSpec((2 * C_V, tq), lambda p, i: (p, i)),
        out_shape=jax.ShapeDtypeStruct((H_C * C_V, n), BF16),
        scratch_shapes=[pltpu.VMEM((2 * C_PAD, tq), BF16), pltpu.VMEM((MLA_UNROLL, 2, tk, tq), BF16),
                        pltpu.VMEM((2, C_V, tq), F32), pltpu.VMEM((2, 1, tq), F32)],
        compiler_params=_params(2),
        name="mla_attn",
    )(qt, k3, vt3, kn)


def _sample_attn_kernel(qa_ref, ka_ref, va_ref, cak_ref, cav_ref, biasc_ref, biasn_ref,
                        qb_ref, kb_ref, vb_ref, cbk_ref, cbv_ref,
                        qc_ref, ckvn_ref, krbn_ref, ckv_ref, ckr_ref, wukt_ref, wuvp_ref,
                        ya_ref, yb_ref, yc_ref, s_scr, *, tile):
    ds = qa_ref.shape[1]
    past = cbk_ref.shape[2]

    ka, va, cak, cav = ka_ref[0], va_ref[0], cak_ref[0, 0].astype(BF16), cav_ref[0, 0].astype(BF16)
    parts = []
    for h, qm in enumerate(_head_masked(qa_ref[0], HEAD_DIM)):
        s_c = _dot_nt(qm, cak) + biasc_ref[h]
        s_n = _dot_nt(qm, ka) + biasn_ref[h]
        m = jnp.maximum(jnp.max(s_c, axis=1, keepdims=True), jnp.max(s_n, axis=1, keepdims=True))
        p_c = jnp.exp2(s_c - m)
        p_n = jnp.exp2(s_n - m)
        l = jnp.sum(p_c, axis=1, keepdims=True) + jnp.sum(p_n, axis=1, keepdims=True)
        parts.append((_dot(p_c.astype(BF16), cav) + _dot(p_n.astype(BF16), va)) / l)
    ya_ref[0] = _select_heads(parts, HEAD_DIM).astype(BF16)

    row = lax.broadcasted_iota(jnp.int32, (ds, ds), 0)
    col = lax.broadcasted_iota(jnp.int32, (ds, ds), 1)
    causal = col < row
    ones_new = _strict_upper_ones(ds)
    ones_ut = _strict_upper_ones(tile)
    kb, vb = kb_ref[0], vb_ref[0]
    n_tiles = past // tile

    def cache_rows(ref, j):
        start = pl.multiple_of((n_tiles - 1 - j) * tile, tile)
        return ref[0, 0, pl.ds(start, tile), :].astype(BF16)

    b_heads = _head_masked(qb_ref[0], HEAD_DIM)
    b_states = []
    for qm in b_heads:
        state = _sb_tile(qm, kb, vb, jnp.zeros((ds, 1), F32), jnp.zeros((ds, W_AB), F32),
                         ones_new, causal)
        b_states.append(_sb_tile(qm, cache_rows(cbk_ref, 0), cache_rows(cbv_ref, 0), state[0], state[1],
                                 ones_ut, None))

    kv_rank = ckv_ref.shape[3]
    qc = qc_ref[0]
    rope_lanes = lax.broadcasted_iota(jnp.int32, (ds, C_PAD), 1) >= C_NOPE
    q_rows = []
    for h in range(H_C):
        qh = qc[:, h * C_PAD:(h + 1) * C_PAD]
        q_lat = _dot(qh, wukt_ref[h]).astype(BF16)
        q_rows.append(jnp.concatenate([q_lat, jnp.where(rope_lanes, qh, jnp.zeros_like(qh))], axis=1))
    q_all = jnp.concatenate(q_rows, axis=0)

    place = (lax.broadcasted_iota(jnp.int32, (C_ROPE, C_PAD), 1)
             == lax.broadcasted_iota(jnp.int32, (C_ROPE, C_PAD), 0) + C_NOPE).astype(BF16)
    tile_c = SAMPLE_LATENT_TILE if past % SAMPLE_LATENT_TILE == 0 else tile
    cache_tiles = range(past // tile_c)
    latent = lambda j: ckv_ref[0, 0, j * tile_c:(j + 1) * tile_c, :].astype(BF16)

    lat_new = ckvn_ref[0].astype(BF16)
    s_new = _dot_nt(q_all, jnp.concatenate([lat_new, krbn_ref[0].astype(BF16)], axis=1))
    m = jnp.max(s_new, axis=1, keepdims=True)
    for j in cache_tiles:
        k_rope = _dot(ckr_ref[0, 0, j * tile_c:(j + 1) * tile_c, :].astype(BF16), place).astype(BF16)
        s = _dot_nt(q_all, jnp.concatenate([latent(j), k_rope], axis=1))
        s_scr[:, j * tile_c:(j + 1) * tile_c] = s
        m = jnp.maximum(m, jnp.max(s, axis=1, keepdims=True))
    p = jnp.exp2(s_new - m)
    denom = jnp.sum(p, axis=1, keepdims=True)
    o_lat = _dot(p.astype(BF16), lat_new)
    for j in cache_tiles:
        p = jnp.exp2(s_scr[:, j * tile_c:(j + 1) * tile_c] - m)
        denom = denom + jnp.sum(p, axis=1, keepdims=True)
        o_lat = o_lat + _dot(p.astype(BF16), latent(j))
    o_lat = (o_lat / denom).astype(BF16)
    out = jnp.zeros((ds, H_C * C_V), F32)
    for h in range(H_C):
        out = out + _dot(o_lat[h * ds:(h + 1) * ds], wuvp_ref[h])
    yc_ref[0] = out.astype(BF16)

    parts = []
    for qm, state in zip(b_heads, b_states):
        state = lax.while_loop(
            lambda st: (st[0] < n_tiles) & _sb_alive(st[1]),
            lambda st, qm=qm: (st[0] + 1,) + _sb_tile(qm, cache_rows(cbk_ref, st[0]), cache_rows(cbv_ref, st[0]),
                                                      st[1], st[2], ones_ut, None),
            (jnp.int32(1),) + state)
        parts.append(state[2])
    yb_ref[0] = _select_heads(parts, HEAD_DIM).astype(BF16)


def _sample_attention(new, caches, l, bias_c, bias_n, w_ukt, w_uvp, tile):
    qa, ka, va, qb, kb, vb, qc, ckvn, krbn = new
    cak, cav, cbk, cbv, ckv, ckr = caches
    nb, ds, _ = qa.shape
    past = ckv.shape[2]
    req = lambda a: pl.BlockSpec((1,) + a.shape[1:], lambda b: (b, 0, 0))
    cache = lambda a: pl.BlockSpec((1, 1) + a.shape[2:], lambda b: (l, b, 0, 0))
    ins = [qa, ka, va, cak, cav, bias_c, bias_n, qb, kb, vb, cbk, cbv, qc, ckvn, krbn, ckv, ckr,
           w_ukt, w_uvp]
    specs = [req(a) for a in ins]
    for idx in (3, 4, 10, 11, 15, 16):
        specs[idx] = cache(ins[idx])
    for idx in (5, 6, 17, 18):
        specs[idx] = _const_spec(ins[idx].shape)
    out_shape = [jax.ShapeDtypeStruct((nb, ds, W_AB), BF16), jax.ShapeDtypeStruct((nb, ds, W_AB), BF16),
                 jax.ShapeDtypeStruct((nb, ds, H_C * C_V), BF16)]
    return pl.pallas_call(
        functools.partial(_sample_attn_kernel, tile=tile),
        grid=(nb,),
        in_specs=specs,
        out_specs=[req(s) for s in out_shape],
        out_shape=out_shape,
        scratch_shapes=[pltpu.VMEM((H_C * ds, past), F32)],
        compiler_params=_params(1),
        name="sample_attn",
    )(*ins)


def _rope_tables(pos):
    half = C_ROPE // 2
    freq = ROPE_THETA ** (-jnp.arange(half, dtype=F32) / half)
    ang = pos.astype(F32)[:, None] * freq[None, :]
    cos, sin = jnp.cos(ang), jnp.sin(ang)
    n = pos.shape[0]
    z = lambda w: jnp.zeros((n, w), F32)
    tc = jnp.concatenate([jnp.ones((n, C_NOPE), F32), cos, cos, z(C_PAD - C_NOPE - C_ROPE)], axis=1)
    ts1 = jnp.concatenate([z(C_NOPE + half), sin, z(C_PAD - C_NOPE - C_ROPE)], axis=1)
    ts2 = jnp.concatenate([z(C_NOPE), -sin, z(C_PAD - C_NOPE - half)], axis=1)
    return tc, ts1, ts2


def _band_bias(rel_bias, q0, nq, k0, nk):
    period = 1 << int(np.ceil(np.log2(nq + nk)))
    d = np.arange(period)
    d = np.where(d >= nk, d - period, d)
    idx = np.clip((q0 - k0) - d, -REL_CLIP, REL_CLIP) + REL_CLIP
    g = rel_bias.astype(F32)[:, idx]
    toe = jnp.tile(g, (1, nq))[:, :nq * (period - 1)].reshape(-1, nq, period - 1)[:, :, :nk]
    q_pos, k_pos = q0 + np.arange(nq), k0 + np.arange(nk)
    qc, kc = q_pos[:, None] // CHUNK, k_pos[None, :] // CHUNK
    vis = (kc <= qc) & (kc >= qc - BAND_CHUNKS)
    return jnp.where(jnp.asarray(vis)[None], toe * LOG2E, NEG)


def _cast_kernel(w_ref, o_ref):
    o_ref[...] = w_ref[0].astype(BF16)


def _layer_bf16(w, l):
    _, rows, cols = w.shape
    rb = rows // 4 if rows % 64 == 0 else rows
    return pl.pallas_call(
        _cast_kernel,
        grid=(rows // rb,),
        in_specs=[pl.BlockSpec((1, rb, cols), lambda i: (l, i, 0))],
        out_specs=pl.BlockSpec((rb, cols), lambda i: (i, 0)),
        out_shape=jax.ShapeDtypeStruct((rows, cols), BF16),
        compiler_params=_params(1),
        name="cast_bf16",
    )(w)


def _layer_weights(l, w_in, w_uq, w_uk, w_uv):
    d = w_in.shape[1]
    q_rank, kv_rank = w_uq.shape[1], w_uk.shape[1]
    o = 6 * W_AB + q_rank + kv_rank
    zc = lambda w: jnp.zeros((d, w), w_in.dtype)
    w_main = jnp.concatenate([w_in[l, :, :o], zc(C_NOPE), w_in[l, :, o:o + C_ROPE],
                              zc(C_PAD - C_NOPE - C_ROPE)], axis=1).astype(BF16)
    w_gates = w_in[l, :, o + C_ROPE:].astype(BF16)
    pad_heads = lambda w: jnp.pad(w, ((0, 0), (0, 0), (0, C_PAD - w.shape[2]))).reshape(w.shape[0], -1)
    w_uq_p = pad_heads(w_uq[l]).astype(BF16)
    w_ukv = jnp.concatenate([pad_heads(w_uk[l]), w_uv[l].reshape(kv_rank, -1)], axis=1).astype(BF16)
    w_ukt = jnp.pad(jnp.transpose(w_uk[l], (1, 2, 0)), ((0, 0), (0, C_PAD - C_NOPE), (0, 0))).astype(BF16)
    w_uvp = jnp.einsum('rhd,hg->hrgd', w_uv[l], jnp.eye(H_C, dtype=w_uv.dtype)).reshape(
        H_C, kv_rank, H_C * C_V).astype(BF16)
    return w_main, w_gates, w_uq_p, w_ukv, w_ukt, w_uvp


def kernel(x_prompt, x_sample, cache_a_k, cache_a_v, cache_b_k, cache_b_v, cache_c_kv, cache_c_kr, ffn1_norm_pre, ffn1_norm_post, ffn1_w_gate, ffn1_w_up, ffn1_w_down, mix_norm_pre, mix_norm_post, w_in, cq_norm, ckv_norm, w_uq, w_uk, w_uv, rel_bias_a, w_br_a, w_br_b, w_br_c, w_out, ffn2_norm_pre, ffn2_norm_post, ffn2_w_gate, ffn2_w_up, ffn2_w_down):
    batch, seq, d = x_prompt.shape
    nb, ds, _ = x_sample.shape
    depth = w_in.shape[0]
    past = cache_b_k.shape[2]
    win_cache = cache_a_k.shape[2]
    assert batch == 1, "prompt group is a single sequence"
    tile = 256
    tm = 512 if seq % 512 == 0 else tile
    assert seq % tile == 0 and past % tile == 0 and tile % CHUNK == 0 and 2 * tile >= WIN_A
    assert past % CHUNK + ds <= CHUNK
    ns = nb * ds

    xp = x_prompt.reshape(seq, d)
    xs = x_sample.reshape(ns, d)
    tabs_p = _rope_tables(jnp.arange(seq))
    tabs_s = _rope_tables(jnp.tile(past + jnp.arange(ds), nb))

    row = lambda g, l: g[l][None, :]
    base = 2 * tile

    merged_heads = lambda c: c.reshape(c.shape[:3] + (-1,))
    caches = (merged_heads(cache_a_k), merged_heads(cache_a_v), merged_heads(cache_b_k),
              merged_heads(cache_b_v), cache_c_kv, cache_c_kr)
    states_p, states_s = [], []
    for l in range(depth):
        w_main, w_gates, w_uq_p, w_ukv, w_ukt, w_uvp = _layer_weights(l, w_in, w_uq, w_uk, w_uv)
        ffn1 = (row(ffn1_norm_pre, l), row(ffn1_norm_post, l), _layer_bf16(ffn1_w_gate, l),
                _layer_bf16(ffn1_w_up, l), _layer_bf16(ffn1_w_down, l))
        ffn2 = (row(ffn2_norm_pre, l), row(ffn2_norm_post, l), _layer_bf16(ffn2_w_gate, l),
                _layer_bf16(ffn2_w_up, l), _layer_bf16(ffn2_w_down, l))
        mix_in = (row(mix_norm_pre, l), w_main, row(cq_norm, l), row(ckv_norm, l), w_uq_p, w_ukv)
        mrg = (row(mix_norm_pre, l), row(mix_norm_post, l), w_gates, _layer_bf16(w_br_a, l),
               _layer_bf16(w_br_b, l), _layer_bf16(w_br_c, l), _layer_bf16(w_out, l))
        bias_p = _band_bias(rel_bias_a[l], base, tile, base - 2 * tile, 3 * tile)
        bias_c = _band_bias(rel_bias_a[l], past, ds, past - win_cache, win_cache)
        bias_n = _band_bias(rel_bias_a[l], past, ds, past, ds)

        xp = _ffn(xp, *ffn1, tm)
        (qa, kab, vab, qb, kbb, vbb, ka, va, kb, vb, ckv, krb, qc, kc, vc, kn) = _mixer_in(
            xp, *mix_in, tabs_p, tm, q_transposed=True)
        ya, yb = _band_sb_attention(qa, kab, vab, bias_p, qb, kbb, vbb, tile)
        yc = _mla_attention(qc, kc, vc, kn, 2 * tm if seq % (2 * tm) == 0 else tm)
        xp = _merge(xp, ya, yb, yc, *mrg, tm, yc_transposed=True)
        xp = _ffn(xp, *ffn2, tm)
        win = min(WIN_A, seq)
        heads = lambda t, h: t.reshape(1, t.shape[0], h, t.shape[1] // h)
        states_p.append((heads(ka[-win:], H_A), heads(va[-win:], H_A), heads(kb, H_B), heads(vb, H_B),
                         ckv[None], krb[None, :, C_NOPE:C_NOPE + C_ROPE]))

        xs = _ffn(xs, *ffn1, ns)
        (qa, kab, vab, qb, kbb, vbb, ka, va, kb, vb, ckv, krb, qc, kc, vc, _) = _mixer_in(
            xs, *mix_in, tabs_s, ns)
        per_req = lambda t: t.reshape(nb, ds, t.shape[1])
        new = tuple(per_req(t) for t in (qa, kab, vab, qb, kbb, vbb, qc, ckv, krb))
        ya, yb, yc = _sample_attention(new, caches, l, bias_c, bias_n, w_ukt, w_uvp, tile)
        xs = _merge(xs, ya.reshape(ns, -1), yb.reshape(ns, -1), yc.reshape(ns, -1), *mrg, ns)
        xs = _ffn(xs, *ffn2, ns)
        heads_s = lambda t, h: t.reshape(nb, ds, h, t.shape[1] // h)
        states_s.append((heads_s(ka, H_A), heads_s(va, H_A), heads_s(kb, H_B), heads_s(vb, H_B),
                         ckv.reshape(nb, ds, -1), krb[:, C_NOPE:C_NOPE + C_ROPE].reshape(nb, ds, C_ROPE)))

    stack = lambda states, i: jnp.stack([s[i] for s in states], axis=0)
    return ((xp.reshape(batch, seq, d), xs.reshape(nb, ds, d))
            + tuple(stack(states_p, i) for i in range(6))
            + tuple(stack(states_s, i) for i in range(6)))
```

```python
import functools

import numpy as np
import jax
import jax.numpy as jnp
from jax import lax
from jax.experimental import pallas as pl
from jax.experimental.pallas import tpu as pltpu

F32 = jnp.float32
BF16 = jnp.bfloat16

CHUNK = 64
BAND_CHUNKS = 8
WIN_A = BAND_CHUNKS * CHUNK
HEAD_DIM = 64
H_A = 4
H_B = 4
H_C = 8
C_NOPE = 64
C_ROPE = 32
C_V = 64
REL_CLIP = 128
ROPE_THETA = 10000.0
EPS = 1e-6
N_BRANCH = 3
W_AB = H_A * HEAD_DIM
C_PAD = 128
ATTN_SCALE = HEAD_DIM ** -0.5
MLA_SCALE = (C_NOPE + C_ROPE) ** -0.5
LOG2E = float(np.log2(np.e))
C_ONE = C_NOPE + C_ROPE
NORM_MARGIN = 1.02
MLA_MIN_DENOM = 2.0 ** -60
MLA_UNROLL = 4
SAMPLE_LATENT_TILE = 1024
NEG = -1e30
BAND_SB_SUB = 4
SB_ALWAYS = 2
SB_DEAD = -150.0

LANES = 128
MXU_WIDTH = 256
VMEM_LIMIT = 52 * 1024 * 1024

NT_DIMS = (((1,), (1,)), ((), ()))
TN_DIMS = (((0,), (0,)), ((), ()))


def _rms(x, g):
    return x * lax.rsqrt(jnp.mean(x * x, axis=-1, keepdims=True) + EPS) * g


def _dot(a, b):
    return jnp.dot(a, b, preferred_element_type=F32)


def _dot_nt(a, b):
    return lax.dot_general(a, b, NT_DIMS, preferred_element_type=F32)


def _dot_tn(a, b):
    return lax.dot_general(a, b, TN_DIMS, preferred_element_type=F32)


def _const_spec(shape):
    zeros = (0,) * len(shape)
    return pl.BlockSpec(shape, lambda *_: zeros, pipeline_mode=pl.Buffered(1))


def _row_spec(tm, width):
    return pl.BlockSpec((tm, width), lambda i: (i, 0))


def _params(n_grid):
    return pltpu.CompilerParams(dimension_semantics=("arbitrary",) * n_grid,
                                vmem_limit_bytes=VMEM_LIMIT)


def _ffn_kernel(x_ref, gpre_ref, gpost_ref, wg_ref, wu_ref, wd_ref, o_ref, *, f_cuts):
    x = x_ref[...]
    h = _rms(x, gpre_ref[...]).astype(BF16)
    y = jnp.zeros(x.shape, F32)
    for lo, hi in zip(f_cuts[:-1], f_cuts[1:]):
        sl = slice(lo, hi)
        g = _dot(h, wg_ref[:, sl])
        u = _dot(h, wu_ref[:, sl])
        a = (g * jax.nn.sigmoid(g) * u).astype(BF16)
        y = y + _dot(a, wd_ref[sl, :])
    o_ref[...] = x + 0.5 * _rms(y, gpost_ref[...])


def _ffn(x, g_pre, g_post, w_gate, w_up, w_down, tm):
    n, d = x.shape
    d_ff = w_gate.shape[1]
    cut = (d_ff // 2) // MXU_WIDTH * MXU_WIDTH
    f_cuts = (0, cut, d_ff) if 0 < cut < d_ff else (0, d_ff)
    return pl.pallas_call(
        functools.partial(_ffn_kernel, f_cuts=f_cuts),
        grid=(n // tm,),
        in_specs=[_row_spec(tm, d), _const_spec((1, d)), _const_spec((1, d)),
                  _const_spec((d, d_ff)), _const_spec((d, d_ff)), _const_spec((d_ff, d))],
        out_specs=_row_spec(tm, d),
        out_shape=jax.ShapeDtypeStruct((n, d), F32),
        compiler_params=_params(1),
        name="ffn",
    )(x, g_pre, g_post, w_gate, w_up, w_down)


def _rope(blk, tc, ts1, ts2):
    return blk * tc + pltpu.roll(blk, 16, 1) * ts1 + pltpu.roll(blk, C_PAD - 16, 1) * ts2


def _mixer_in_kernel(x_ref, gpre_ref, win_ref, gcq_ref, gckv_ref, wuq_ref, wukv_ref,
                     tc_ref, ts1_ref, ts2_ref,
                     qa_ref, kab_ref, vab_ref, qb_ref, kbb_ref, vbb_ref,
                     ka_ref, va_ref, kb_ref, vb_ref, ckv_ref, krb_ref,
                     qc_ref, kc_ref, vc_ref, kn_ref, *, q_transposed):
    w = W_AB
    q_rank = gcq_ref.shape[1]
    kv_rank = gckv_ref.shape[1]
    u = _rms(x_ref[...], gpre_ref[...]).astype(BF16)
    o = 6 * w
    lat = _dot(u, win_ref[:, o:])
    cq = lat[:, :q_rank]
    ckv = lat[:, q_rank:q_rank + kv_rank]
    krb = lat[:, q_rank + kv_rank:q_rank + kv_rank + C_PAD]
    proj = _dot(u, win_ref[:, :o])
    qa, ka, va = proj[:, 0:w], proj[:, w:2 * w], proj[:, 2 * w:3 * w]
    qb, kb, vb = proj[:, 3 * w:4 * w], proj[:, 4 * w:5 * w], proj[:, 5 * w:6 * w]
    qa_ref[...] = (qa * (ATTN_SCALE * LOG2E)).astype(BF16)
    qb_ref[...] = (qb * (ATTN_SCALE * LOG2E)).astype(BF16)
    ka_ref[...] = ka
    va_ref[...] = va
    kb_ref[...] = kb
    vb_ref[...] = vb
    kab_ref[...] = ka.astype(BF16)
    vab_ref[...] = va.astype(BF16)
    kbb_ref[...] = kb.astype(BF16)
    vbb_ref[...] = vb.astype(BF16)
    tc, ts1, ts2 = tc_ref[...], ts1_ref[...], ts2_ref[...]
    cqn = _rms(cq, gcq_ref[...]).astype(BF16)
    qall = _dot(cqn, wuq_ref[...])
    ckvn = _rms(ckv, gckv_ref[...])
    ckv_ref[...] = ckvn
    kv = _dot(ckvn.astype(BF16), wukv_ref[...])
    krot = _rope(krb, tc, ts1, ts2)
    krb_ref[...] = krot
    one_lane = (lax.broadcasted_iota(jnp.int32, (1, C_PAD), 1) == C_ONE).astype(F32)
    for h in range(H_C):
        sl = slice(h * C_PAD, (h + 1) * C_PAD)
        qh = _rope(qall[:, sl], tc, ts1, ts2) * (MLA_SCALE * LOG2E)
        if q_transposed:
            qt = qh.T
            norm = jnp.sqrt(jnp.sum(qt * qt, axis=0, keepdims=True)) * NORM_MARGIN
            row = lax.broadcasted_iota(jnp.int32, qt.shape, 0)
            qc_ref[sl, :] = jnp.where(row == C_ONE, norm, qt).astype(BF16)
        else:
            qc_ref[:, sl] = qh.astype(BF16)
        kh = kv[:, sl] + krot
        kc_ref[:, sl] = (kh + one_lane).astype(BF16)
        k_sq = _dot((kh * kh).astype(BF16), jnp.ones((C_PAD, C_PAD), BF16))
        kn_ref[0, :, sl] = jnp.sqrt(jnp.max(k_sq, axis=0, keepdims=True)) * NORM_MARGIN
    if not q_transposed:
        vc_ref[...] = kv[:, H_C * C_PAD:].astype(BF16)
    else:
        for p in range(H_C // 2):
            o = H_C * C_PAD + p * 2 * C_V
            vc_ref[0, p * 2 * C_V:(p + 1) * 2 * C_V, :] = kv[:, o:o + 2 * C_V].T.astype(BF16)


def _mixer_in(x, g_pre, w_main, g_cq, g_ckv, w_uq, w_ukv, tabs, tm, q_transposed=False):
    n, d = x.shape
    q_rank, kv_rank = g_cq.shape[1], g_ckv.shape[1]
    wc = H_C * C_PAD
    wv = H_C * C_V
    bf = lambda width: jax.ShapeDtypeStruct((n, width), BF16)
    f32 = lambda width: jax.ShapeDtypeStruct((n, width), F32)
    q_shape = jax.ShapeDtypeStruct((wc, n), BF16) if q_transposed else bf(wc)
    q_spec = pl.BlockSpec((wc, tm), lambda i: (0, i)) if q_transposed else _row_spec(tm, wc)
    v_shape = jax.ShapeDtypeStruct((n // tm, wv, tm), BF16) if q_transposed else bf(wv)
    v_spec = (pl.BlockSpec((1, wv, tm), lambda i: (i, 0, 0)) if q_transposed
              else _row_spec(tm, wv))
    out_shape = ([bf(W_AB)] * 6 + [f32(W_AB)] * 4 + [f32(kv_rank), f32(C_PAD)]
                 + [q_shape, bf(wc), v_shape, jax.ShapeDtypeStruct((n // tm, 1, wc), F32)])
    out_specs = ([_row_spec(tm, W_AB)] * 10 + [_row_spec(tm, kv_rank), _row_spec(tm, C_PAD)]
                 + [q_spec, _row_spec(tm, wc), v_spec, pl.BlockSpec((1, 1, wc), lambda i: (i, 0, 0))])
    return pl.pallas_call(
        functools.partial(_mixer_in_kernel, q_transposed=q_transposed),
        grid=(n // tm,),
        in_specs=[_row_spec(tm, d), _const_spec((1, d)), _const_spec(w_main.shape),
                  _const_spec((1, q_rank)), _const_spec((1, kv_rank)),
                  _const_spec(w_uq.shape), _const_spec(w_ukv.shape),
                  _row_spec(tm, C_PAD), _row_spec(tm, C_PAD), _row_spec(tm, C_PAD)],
        out_specs=out_specs,
        out_shape=out_shape,
        compiler_params=_params(1),
        name="mixer_in",
    )(x, g_pre, w_main, g_cq, g_ckv, w_uq, w_ukv, *tabs)


def _merge_kernel(x_ref, ya_ref, yb_ref, yc_ref, gpre_ref, gpost_ref, wgates_ref,
                  wa_ref, wb_ref, wc_ref, wout_ref, o_ref, *, yc_transposed):
    x = x_ref[...]
    d = x.shape[1]
    u = _rms(x, gpre_ref[...]).astype(BF16)
    m = jnp.zeros(x.shape, F32)
    for i, (y_ref, w_ref) in enumerate(((ya_ref, wa_ref), (yb_ref, wb_ref), (yc_ref, wc_ref))):
        gate = jax.nn.sigmoid(_dot(u, wgates_ref[:, i * d:(i + 1) * d]))
        branch_dot = _dot_tn if (yc_transposed and y_ref is yc_ref) else _dot
        m = m + gate * branch_dot(y_ref[...], w_ref[...])
    mixed = _dot(m.astype(BF16), wout_ref[...])
    o_ref[...] = x + _rms(mixed, gpost_ref[...])


def _merge(x, ya, yb, yc, g_pre, g_post, w_gates, w_a, w_b, w_c, w_out, tm, yc_transposed=False):
    n, d = x.shape
    yc_spec = (pl.BlockSpec((yc.shape[0], tm), lambda i: (0, i)) if yc_transposed
               else _row_spec(tm, yc.shape[1]))
    return pl.pallas_call(
        functools.partial(_merge_kernel, yc_transposed=yc_transposed),
        grid=(n // tm,),
        in_specs=[_row_spec(tm, d), _row_spec(tm, ya.shape[1]), _row_spec(tm, yb.shape[1]),
                  yc_spec, _const_spec((1, d)), _const_spec((1, d)),
                  _const_spec(w_gates.shape), _const_spec(w_a.shape), _const_spec(w_b.shape),
                  _const_spec(w_c.shape), _const_spec(w_out.shape)],
        out_specs=_row_spec(tm, d),
        out_shape=jax.ShapeDtypeStruct((n, d), F32),
        compiler_params=_params(1),
        name="merge",
    )(x, ya, yb, yc, g_pre, g_post, w_gates, w_a, w_b, w_c, w_out)


def _head_masked(q, width):
    lane_head = lax.broadcasted_iota(jnp.int32, q.shape, 1) // width
    zero = jnp.zeros_like(q)
    return [jnp.where(lane_head == h, q, zero) for h in range(q.shape[1] // width)]


def _strict_upper_ones(n):
    j = lax.broadcasted_iota(jnp.int32, (n, n), 0)
    s = lax.broadcasted_iota(jnp.int32, (n, n), 1)
    return (j > s).astype(BF16)


def _sb_terms(qm, kt, ones_ut, causal):
    z = _dot_nt(qm, kt)
    t = jnp.log2(1.0 + jnp.exp2(-jnp.abs(z)))
    log_keep = -jnp.maximum(z, 0.0) - t
    log_beta = jnp.minimum(z, 0.0) - t
    if causal is not None:
        log_keep = jnp.where(causal, log_keep, 0.0)
    hi = log_keep.astype(BF16)
    lo = (log_keep - hi.astype(F32)).astype(BF16)
    tail = _dot(hi, ones_ut) + _dot(lo, ones_ut)
    return log_beta + tail, jnp.sum(log_keep, axis=1, keepdims=True)


def _sb_apply(terms, vt, carry, acc, causal):
    logw, total = terms
    w = jnp.exp2(logw + carry)
    if causal is not None:
        w = jnp.where(causal, w, 0.0)
    return carry + total, acc + _dot(w.astype(BF16), vt)


def _sb_tile(qm, kt, vt, carry, acc, ones_ut, causal):
    return _sb_apply(_sb_terms(qm, kt, ones_ut, causal), vt, carry, acc, causal)


def _sb_alive(carry):
    return jnp.max(carry) > SB_DEAD


def _select_heads(parts, width):
    lane_head = lax.broadcasted_iota(jnp.int32, parts[0].shape, 1) // width
    out = parts[0]
    for h in range(1, len(parts)):
        out = jnp.where(lane_head == h, parts[h], out)
    return out


def _band_sb_kernel(qa_ref, ka0_ref, ka1_ref, va0_ref, va1_ref, bias_ref, q_ref, k_ref, v_ref,
                    oa_ref, o_ref, *, tile):
    i = pl.program_id(0)
    sub = BAND_SB_SUB
    k_band = jnp.concatenate([ka0_ref[...], ka1_ref[...]], axis=0)
    v_band = jnp.concatenate([va0_ref[...], va1_ref[...]], axis=0)
    col = lax.broadcasted_iota(jnp.int32, (tile, 3 * tile), 1)
    for t in range(sub):
        kwin = k_band[(sub - 2 + t) * tile:(sub + 1 + t) * tile]
        vwin = v_band[(sub - 2 + t) * tile:(sub + 1 + t) * tile]
        in_window = col >= (2 - (sub * i + t)) * tile
        band = []
        for h, qm in enumerate(_head_masked(qa_ref[t * tile:(t + 1) * tile, :], HEAD_DIM)):
            s = jnp.where(in_window, _dot_nt(qm, kwin) + bias_ref[h], NEG)
            m = jnp.max(s, axis=1, keepdims=True)
            p = jnp.exp2(s - m)
            l = jnp.sum(p, axis=1, keepdims=True)
            band.append(_dot(p.astype(BF16), vwin) / l)
        oa_ref[t * tile:(t + 1) * tile, :] = _select_heads(band, HEAD_DIM).astype(BF16)

    row = lax.broadcasted_iota(jnp.int32, (tile, tile), 0)
    col = lax.broadcasted_iota(jnp.int32, (tile, tile), 1)
    causal = col < row
    ones_ut = _strict_upper_ones(tile)
    width = q_ref.shape[1]
    rows = lambda ref, kb: ref[pl.ds(pl.multiple_of(kb * tile, tile), tile), :]

    work = []
    for t in range(sub):
        it = sub * i + t
        qms = _head_masked(q_ref[t * tile:(t + 1) * tile, :], HEAD_DIM)
        tiles = [jnp.maximum(it - d, 0) for d in range(SB_ALWAYS)]
        terms = [[_sb_terms(qm, rows(k_ref, kb), ones_ut, causal if d == 0 else None) for qm in qms]
                 for d, kb in enumerate(tiles)]
        values = [rows(v_ref, kb) for kb in tiles]
        states = []
        for h in range(len(qms)):
            carry, acc = jnp.zeros((tile, 1), F32), jnp.zeros((tile, width), F32)
            for d in range(SB_ALWAYS):
                carry_in = carry if d == 0 else jnp.where(it >= d, carry, NEG)
                carry, acc = _sb_apply(terms[d][h], values[d], carry_in, acc, causal if d == 0 else None)
            states.append((carry, acc))
        work.append((it, qms, states))
    for t, (it, qms, states) in enumerate(work):
        first = it - SB_ALWAYS
        parts = []
        for qm, (carry, acc) in zip(qms, states):
            state = lax.while_loop(
                lambda st, first=first: (st[0] <= first) & _sb_alive(st[1]),
                lambda st, qm=qm, first=first: (st[0] + 1,) + _sb_tile(
                    qm, rows(k_ref, first - st[0]), rows(v_ref, first - st[0]), st[1], st[2], ones_ut, None),
                (jnp.int32(0), carry, acc))
            parts.append(state[2])
        o_ref[t * tile:(t + 1) * tile, :] = _select_heads(parts, HEAD_DIM).astype(BF16)


def _band_sb_attention(qa, ka, va, bias, q, k, v, tile):
    n, w = q.shape
    rows = BAND_SB_SUB * tile
    assert BAND_SB_SUB >= 2 and n % rows == 0
    blk = lambda back: pl.BlockSpec((rows, w), lambda i: (jnp.maximum(i - back, 0), 0))
    out = jax.ShapeDtypeStruct((n, w), BF16)
    return pl.pallas_call(
        functools.partial(_band_sb_kernel, tile=tile),
        grid=(n // rows,),
        in_specs=[blk(0), blk(1), blk(0), blk(1), blk(0), _const_spec(bias.shape),
                  blk(0), _const_spec((n, w)), _const_spec((n, w))],
        out_specs=[blk(0), blk(0)],
        out_shape=[out, out],
        compiler_params=_params(1),
        name="band_sb_attn",
    )(qa, ka, ka, va, va, bias, q, k, v)


def _mla_kernel(qt_ref, k_ref, vt_ref, kn_ref, ot_ref, qx_scr, p_scr, acc_scr, l_scr):
    i = pl.program_id(1)
    heads = range(2)
    tk, tq = p_scr.shape[2:]
    per_q = tq // tk
    plain = per_q * i
    key = lax.broadcasted_iota(jnp.int32, (tk, tq), 0)
    qry = lax.broadcasted_iota(jnp.int32, (tk, tq), 1)
    row = lax.broadcasted_iota(jnp.int32, (C_PAD - C_ONE, tq), 0)

    def visible(d):
        return ((d * tk + key) // CHUNK) <= (qry // CHUNK)

    def load_q(shifted):
        k_bound = jnp.max(kn_ref[...], axis=0)
        for e in heads:
            lo = e * C_PAD
            qx_scr[lo:lo + C_ONE, :] = qt_ref[lo:lo + C_ONE, :]
            if shifted:
                bound = qt_ref[lo + C_ONE:lo + C_ONE + 1, :].astype(F32) * (k_bound[:, lo:lo + 1] * NORM_MARGIN)
                tail = jnp.where(row == 0, -bound, 0.0)
            else:
                tail = jnp.zeros(row.shape, F32)
            qx_scr[lo + C_ONE:lo + C_PAD, :] = tail.astype(BF16)

    def scores(kb, e):
        return _dot(k_ref[kb, :, e * C_PAD:(e + 1) * C_PAD], qx_scr[e * C_PAD:(e + 1) * C_PAD, :])

    def values(kb, e, p):
        return _dot(vt_ref[kb, e * C_V:(e + 1) * C_V, :], p)

    def write_out():
        ot_ref[...] = jnp.concatenate([acc_scr[e] / l_scr[e] for e in heads], axis=0).astype(BF16)

    def reset():
        acc_scr[...] = jnp.zeros(acc_scr.shape, F32)
        l_scr[...] = jnp.zeros(l_scr.shape, F32)

    def fast_stage(kb, slot, masked_tile=None):
        lo = 0 if masked_tile is None else masked_tile * tk
        for e in heads:
            s = _dot(k_ref[kb, :, e * C_PAD:(e + 1) * C_PAD], qx_scr[e * C_PAD:(e + 1) * C_PAD, lo:])
            if masked_tile is not None:
                s = jnp.where(visible(0)[:, :tq - lo], s, NEG)
            p = jnp.exp2(s)
            l_scr[e, :, lo:] += jnp.sum(p, axis=0, keepdims=True)
            p_scr[slot, e, :, lo:] = p.astype(BF16)
        for e in heads:
            acc_scr[e, :, lo:] += values(kb, e, p_scr[slot, e, :, lo:])

    load_q(True)
    reset()

    def fast_group(g, _):
        for j in range(MLA_UNROLL):
            fast_stage(MLA_UNROLL * g + j, j)
        return 0
    groups = plain // MLA_UNROLL
    lax.fori_loop(0, groups, fast_group, 0)

    def fast_single(kb, _):
        fast_stage(kb, 0)
        return 0
    lax.fori_loop(MLA_UNROLL * groups, plain, fast_single, 0)
    for d in range(per_q):
        fast_stage(plain + d, (1 + d) % MLA_UNROLL, masked_tile=d)
    denom = jnp.minimum(jnp.min(l_scr[0]), jnp.min(l_scr[1]))
    trusted = denom >= MLA_MIN_DENOM

    @pl.when(trusted)
    def _():
        write_out()

    @pl.when(jnp.logical_not(trusted))
    def _():
        load_q(False)
        reset()

        def step(kb, m):
            out = []
            for e in heads:
                s = jnp.where(visible(kb - plain), scores(kb, e), NEG)
                m_new = jnp.maximum(m[e], jnp.max(s, axis=0, keepdims=True))
                alpha = jnp.exp2(m[e] - m_new)
                p = jnp.exp2(s - m_new)
                l_scr[e] = alpha * l_scr[e] + jnp.sum(p, axis=0, keepdims=True)
                acc_scr[e] = alpha * acc_scr[e] + values(kb, e, p.astype(BF16))
                out.append(m_new)
            return tuple(out)
        lax.fori_loop(0, plain + per_q, step, tuple(jnp.full((1, tq), NEG, F32) for _ in heads))
        write_out()


def _mla_attention(qt, k, vt3, kn, tq):
    n = k.shape[0]
    pairs = H_C // 2
    n_tiles, _, tk = vt3.shape
    assert n_tiles * tk == n and kn.shape == (n_tiles, 1, H_C * C_PAD)
    assert tq % tk == 0 and n % tq == 0
    k3 = k.reshape(n_tiles, tk, H_C * C_PAD)
    return pl.pallas_call(
        _mla_kernel,
        grid=(pairs, n // tq),
        in_specs=[pl.BlockSpec((2 * C_PAD, tq), lambda p, i: (p, i)),
                  pl.BlockSpec((n_tiles, tk, 2 * C_PAD), lambda p, i: (0, 0, p)),
                  pl.BlockSpec((n_tiles, 2 * C_V, tk), lambda p, i: (0, p, 0)),
                  pl.BlockSpec((n_tiles, 1, 2 * C_PAD), lambda p, i: (0, 0, p))],
        out_specs=pl.BlockSpec((2 * C_V, tq), lambda p, i: (p, i)),
        out_shape=jax.ShapeDtypeStruct((H_C * C_V, n), BF16),
        scratch_shapes=[pltpu.VMEM((2 * C_PAD, tq), BF16), pltpu.VMEM((MLA_UNROLL, 2, tk, tq), BF16),
                        pltpu.VMEM((2, C_V, tq), F32), pltpu.VMEM((2, 1, tq), F32)],
        compiler_params=_params(2),
        name="mla_attn",
    )(qt, k3, vt3, kn)


def _sample_attn_kernel(qa_ref, ka_ref, va_ref, cak_ref, cav_ref, biasc_ref, biasn_ref,
                        qb_ref, kb_ref, vb_ref, cbk_ref, cbv_ref,
                        qc_ref, ckvn_ref, krbn_ref, ckv_ref, ckr_ref, wukt_ref, wuvp_ref,
                        ya_ref, yb_ref, yc_ref, s_scr, *, tile):
    ds = qa_ref.shape[1]
    past = cbk_ref.shape[2]

    ka, va, cak, cav = ka_ref[0], va_ref[0], cak_ref[0, 0].astype(BF16), cav_ref[0, 0].astype(BF16)
    parts = []
    for h, qm in enumerate(_head_masked(qa_ref[0], HEAD_DIM)):
        s_c = _dot_nt(qm, cak) + biasc_ref[h]
        s_n = _dot_nt(qm, ka) + biasn_ref[h]
        m = jnp.maximum(jnp.max(s_c, axis=1, keepdims=True), jnp.max(s_n, axis=1, keepdims=True))
        p_c = jnp.exp2(s_c - m)
        p_n = jnp.exp2(s_n - m)
        l = jnp.sum(p_c, axis=1, keepdims=True) + jnp.sum(p_n, axis=1, keepdims=True)
        parts.append((_dot(p_c.astype(BF16), cav) + _dot(p_n.astype(BF16), va)) / l)
    ya_ref[0] = _select_heads(parts, HEAD_DIM).astype(BF16)

    row = lax.broadcasted_iota(jnp.int32, (ds, ds), 0)
    col = lax.broadcasted_iota(jnp.int32, (ds, ds), 1)
    causal = col < row
    ones_new = _strict_upper_ones(ds)
    ones_ut = _strict_upper_ones(tile)
    kb, vb = kb_ref[0], vb_ref[0]
    n_tiles = past // tile

    def cache_rows(ref, j):
        start = pl.multiple_of((n_tiles - 1 - j) * tile, tile)
        return ref[0, 0, pl.ds(start, tile), :].astype(BF16)

    b_heads = _head_masked(qb_ref[0], HEAD_DIM)
    b_states = []
    for qm in b_heads:
        state = _sb_tile(qm, kb, vb, jnp.zeros((ds, 1), F32), jnp.zeros((ds, W_AB), F32),
                         ones_new, causal)
        b_states.append(_sb_tile(qm, cache_rows(cbk_ref, 0), cache_rows(cbv_ref, 0), state[0], state[1],
                                 ones_ut, None))

    kv_rank = ckv_ref.shape[3]
    qc = qc_ref[0]
    rope_lanes = lax.broadcasted_iota(jnp.int32, (ds, C_PAD), 1) >= C_NOPE
    q_rows = []
    for h in range(H_C):
        qh = qc[:, h * C_PAD:(h + 1) * C_PAD]
        q_lat = _dot(qh, wukt_ref[h]).astype(BF16)
        q_rows.append(jnp.concatenate([q_lat, jnp.where(rope_lanes, qh, jnp.zeros_like(qh))], axis=1))
    q_all = jnp.concatenate(q_rows, axis=0)

    place = (lax.broadcasted_iota(jnp.int32, (C_ROPE, C_PAD), 1)
             == lax.broadcasted_iota(jnp.int32, (C_ROPE, C_PAD), 0) + C_NOPE).astype(BF16)
    tile_c = SAMPLE_LATENT_TILE if past % SAMPLE_LATENT_TILE == 0 else tile
    cache_tiles = range(past // tile_c)
    latent = lambda j: ckv_ref[0, 0, j * tile_c:(j + 1) * tile_c, :].astype(BF16)

    lat_new = ckvn_ref[0].astype(BF16)
    s_new = _dot_nt(q_all, jnp.concatenate([lat_new, krbn_ref[0].astype(BF16)], axis=1))
    m = jnp.max(s_new, axis=1, keepdims=True)
    for j in cache_tiles:
        k_rope = _dot(ckr_ref[0, 0, j * tile_c:(j + 1) * tile_c, :].astype(BF16), place).astype(BF16)
        s = _dot_nt(q_all, jnp.concatenate([latent(j), k_rope], axis=1))
        s_scr[:, j * tile_c:(j + 1) * tile_c] = s
        m = jnp.maximum(m, jnp.max(s, axis=1, keepdims=True))
    p = jnp.exp2(s_new - m)
    denom = jnp.sum(p, axis=1, keepdims=True)
    o_lat = _dot(p.astype(BF16), lat_new)
    for j in cache_tiles:
        p = jnp.exp2(s_scr[:, j * tile_c:(j + 1) * tile_c] - m)
        denom = denom + jnp.sum(p, axis=1, keepdims=True)
        o_lat = o_lat + _dot(p.astype(BF16), latent(j))
    o_lat = (o_lat / denom).astype(BF16)
    out = jnp.zeros((ds, H_C * C_V), F32)
    for h in range(H_C):
        out = out + _dot(o_lat[h * ds:(h + 1) * ds], wuvp_ref[h])
    yc_ref[0] = out.astype(BF16)

    parts = []
    for qm, state in zip(b_heads, b_states):
        state = lax.while_loop(
            lambda st: (st[0] < n_tiles) & _sb_alive(st[1]),
            lambda st, qm=qm: (st[0] + 1,) + _sb_tile(qm, cache_rows(cbk_ref, st[0]), cache_rows(cbv_ref, st[0]),
                                                      st[1], st[2], ones_ut, None),
            (jnp.int32(1),) + state)
        parts.append(state[2])
    yb_ref[0] = _select_heads(parts, HEAD_DIM).astype(BF16)


def _sample_attention(new, caches, l, bias_c, bias_n, w_ukt, w_uvp, tile):
    qa, ka, va, qb, kb, vb, qc, ckvn, krbn = new
    cak, cav, cbk, cbv, ckv, ckr = caches
    nb, ds, _ = qa.shape
    past = ckv.shape[2]
    req = lambda a: pl.BlockSpec((1,) + a.shape[1:], lambda b: (b, 0, 0))
    cache = lambda a: pl.BlockSpec((1, 1) + a.shape[2:], lambda b: (l, b, 0, 0))
    ins = [qa, ka, va, cak, cav, bias_c, bias_n, qb, kb, vb, cbk, cbv, qc, ckvn, krbn, ckv, ckr,
           w_ukt, w_uvp]
    specs = [req(a) for a in ins]
    for idx in (3, 4, 10, 11, 15, 16):
        specs[idx] = cache(ins[idx])
    for idx in (5, 6, 17, 18):
        specs[idx] = _const_spec(ins[idx].shape)
    out_shape = [jax.ShapeDtypeStruct((nb, ds, W_AB), BF16), jax.ShapeDtypeStruct((nb, ds, W_AB), BF16),
                 jax.ShapeDtypeStruct((nb, ds, H_C * C_V), BF16)]
    return pl.pallas_call(
        functools.partial(_sample_attn_kernel, tile=tile),
        grid=(nb,),
        in_specs=specs,
        out_specs=[req(s) for s in out_shape],
        out_shape=out_shape,
        scratch_shapes=[pltpu.VMEM((H_C * ds, past), F32)],
        compiler_params=_params(1),
        name="sample_attn",
    )(*ins)


def _rope_tables(pos):
    half = C_ROPE // 2
    freq = ROPE_THETA ** (-jnp.arange(half, dtype=F32) / half)
    ang = pos.astype(F32)[:, None] * freq[None, :]
    cos, sin = jnp.cos(ang), jnp.sin(ang)
    n = pos.shape[0]
    z = lambda w: jnp.zeros((n, w), F32)
    tc = jnp.concatenate([jnp.ones((n, C_NOPE), F32), cos, cos, z(C_PAD - C_NOPE - C_ROPE)], axis=1)
    ts1 = jnp.concatenate([z(C_NOPE + half), sin, z(C_PAD - C_NOPE - C_ROPE)], axis=1)
    ts2 = jnp.concatenate([z(C_NOPE), -sin, z(C_PAD - C_NOPE - half)], axis=1)
    return tc, ts1, ts2


def _band_bias(rel_bias, q0, nq, k0, nk):
    period = 1 << int(np.ceil(np.log2(nq + nk)))
    d = np.arange(period)
    d = np.where(d >= nk, d - period, d)
    idx = np.clip((q0 - k0) - d, -REL_CLIP, REL_CLIP) + REL_CLIP
    g = rel_bias.astype(F32)[:, idx]
    toe = jnp.tile(g, (1, nq))[:, :nq * (period - 1)].reshape(-1, nq, period - 1)[:, :, :nk]
    q_pos, k_pos = q0 + np.arange(nq), k0 + np.arange(nk)
    qc, kc = q_pos[:, None] // CHUNK, k_pos[None, :] // CHUNK
    vis = (kc <= qc) & (kc >= qc - BAND_CHUNKS)
    return jnp.where(jnp.asarray(vis)[None], toe * LOG2E, NEG)


def _cast_kernel(w_ref, o_ref):
    o_ref[...] = w_ref[0].astype(BF16)


def _layer_bf16(w, l):
    _, rows, cols = w.shape
    rb = rows // 2 if rows % 32 == 0 else rows
    return pl.pallas_call(
        _cast_kernel,
        grid=(rows // rb,),
        in_specs=[pl.BlockSpec((1, rb, cols), lambda i: (l, i, 0))],
        out_specs=pl.BlockSpec((rb, cols), lambda i: (i, 0)),
        out_shape=jax.ShapeDtypeStruct((rows, cols), BF16),
        compiler_params=_params(1),
        name="cast_bf16",
    )(w)


def _layer_weights(l, w_in, w_uq, w_uk, w_uv):
    d = w_in.shape[1]
    q_rank, kv_rank = w_uq.shape[1], w_uk.shape[1]
    o = 6 * W_AB + q_rank + kv_rank
    zc = lambda w: jnp.zeros((d, w), w_in.dtype)
    w_main = jnp.concatenate([w_in[l, :, :o], zc(C_NOPE), w_in[l, :, o:o + C_ROPE],
                              zc(C_PAD - C_NOPE - C_ROPE)], axis=1).astype(BF16)
    w_gates = w_in[l, :, o + C_ROPE:].astype(BF16)
    pad_heads = lambda w: jnp.pad(w, ((0, 0), (0, 0), (0, C_PAD - w.shape[2]))).reshape(w.shape[0], -1)
    w_uq_p = pad_heads(w_uq[l]).astype(BF16)
    w_ukv = jnp.concatenate([pad_heads(w_uk[l]), w_uv[l].reshape(kv_rank, -1)], axis=1).astype(BF16)
    w_ukt = jnp.pad(jnp.transpose(w_uk[l], (1, 2, 0)), ((0, 0), (0, C_PAD - C_NOPE), (0, 0))).astype(BF16)
    w_uvp = jnp.einsum('rhd,hg->hrgd', w_uv[l], jnp.eye(H_C, dtype=w_uv.dtype)).reshape(
        H_C, kv_rank, H_C * C_V).astype(BF16)
    return w_main, w_gates, w_uq_p, w_ukv, w_ukt, w_uvp


def kernel(x_prompt, x_sample, cache_a_k, cache_a_v, cache_b_k, cache_b_v, cache_c_kv, cache_c_kr, ffn1_norm_pre, ffn1_norm_post, ffn1_w_gate, ffn1_w_up, ffn1_w_down, mix_norm_pre, mix_norm_post, w_in, cq_norm, ckv_norm, w_uq, w_uk, w_uv, rel_bias_a, w_br_a, w_br_b, w_br_c, w_out, ffn2_norm_pre, ffn2_norm_post, ffn2_w_gate, ffn2_w_up, ffn2_w_down):
    batch, seq, d = x_prompt.shape
    nb, ds, _ = x_sample.shape
    depth = w_in.shape[0]
    past = cache_b_k.shape[2]
    win_cache = cache_a_k.shape[2]
    assert batch == 1, "prompt group is a single sequence"
    tile = 256
    tm = 512 if seq % 512 == 0 else tile
    assert seq % tile == 0 and past % tile == 0 and tile % CHUNK == 0 and 2 * tile >= WIN_A
    assert past % CHUNK + ds <= CHUNK
    ns = nb * ds

    xp = x_prompt.reshape(seq, d)
    xs = x_sample.reshape(ns, d)
    tabs_p = _rope_tables(jnp.arange(seq))
    tabs_s = _rope_tables(jnp.tile(past + jnp.arange(ds), nb))

    row = lambda g, l: g[l][None, :]
    base = 2 * tile

    merged_heads = lambda c: c.reshape(c.shape[:3] + (-1,))
    caches = (merged_heads(cache_a_k), merged_heads(cache_a_v), merged_heads(cache_b_k),
              merged_heads(cache_b_v), cache_c_kv, cache_c_kr)
    states_p, states_s = [], []
    for l in range(depth):
        w_main, w_gates, w_uq_p, w_ukv, w_ukt, w_uvp = _layer_weights(l, w_in, w_uq, w_uk, w_uv)
        ffn1 = (row(ffn1_norm_pre, l), row(ffn1_norm_post, l), _layer_bf16(ffn1_w_gate, l),
                _layer_bf16(ffn1_w_up, l), _layer_bf16(ffn1_w_down, l))
        ffn2 = (row(ffn2_norm_pre, l), row(ffn2_norm_post, l), _layer_bf16(ffn2_w_gate, l),
                _layer_bf16(ffn2_w_up, l), _layer_bf16(ffn2_w_down, l))
        mix_in = (row(mix_norm_pre, l), w_main, row(cq_norm, l), row(ckv_norm, l), w_uq_p, w_ukv)
        mrg = (row(mix_norm_pre, l), row(mix_norm_post, l), w_gates, _layer_bf16(w_br_a, l),
               _layer_bf16(w_br_b, l), _layer_bf16(w_br_c, l), _layer_bf16(w_out, l))
        bias_p = _band_bias(rel_bias_a[l], base, tile, base - 2 * tile, 3 * tile)
        bias_c = _band_bias(rel_bias_a[l], past, ds, past - win_cache, win_cache)
        bias_n = _band_bias(rel_bias_a[l], past, ds, past, ds)

        xp = _ffn(xp, *ffn1, tm)
        (qa, kab, vab, qb, kbb, vbb, ka, va, kb, vb, ckv, krb, qc, kc, vc, kn) = _mixer_in(
            xp, *mix_in, tabs_p, tm, q_transposed=True)
        ya, yb = _band_sb_attention(qa, kab, vab, bias_p, qb, kbb, vbb, tile)
        yc = _mla_attention(qc, kc, vc, kn, 2 * tm if seq % (2 * tm) == 0 else tm)
        xp = _merge(xp, ya, yb, yc, *mrg, tm, yc_transposed=True)
        xp = _ffn(xp, *ffn2, tm)
        win = min(WIN_A, seq)
        heads = lambda t, h: t.reshape(1, t.shape[0], h, t.shape[1] // h)
        states_p.append((heads(ka[-win:], H_A), heads(va[-win:], H_A), heads(kb, H_B), heads(vb, H_B),
                         ckv[None], krb[None, :, C_NOPE:C_NOPE + C_ROPE]))

        xs = _ffn(xs, *ffn1, ns)
        (qa, kab, vab, qb, kbb, vbb, ka, va, kb, vb, ckv, krb, qc, kc, vc, _) = _mixer_in(
            xs, *mix_in, tabs_s, ns)
        per_req = lambda t: t.reshape(nb, ds, t.shape[1])
        new = tuple(per_req(t) for t in (qa, kab, vab, qb, kbb, vbb, qc, ckv, krb))
        ya, yb, yc = _sample_attention(new, caches, l, bias_c, bias_n, w_ukt, w_uvp, tile)
        xs = _merge(xs, ya.reshape(ns, -1), yb.reshape(ns, -1), yc.reshape(ns, -1), *mrg, ns)
        xs = _ffn(xs, *ffn2, ns)
        heads_s = lambda t, h: t.reshape(nb, ds, h, t.shape[1] // h)
        states_s.append((heads_s(ka, H_A), heads_s(va, H_A), heads_s(kb, H_B), heads_s(vb, H_B),
                         ckv.reshape(nb, ds, -1), krb[:, C_NOPE:C_NOPE + C_ROPE].reshape(nb, ds, C_ROPE)))

    stack = lambda states, i: jnp.stack([s[i] for s in states], axis=0)
    return ((xp.reshape(batch, seq, d), xs.reshape(nb, ds, d))
            + tuple(stack(states_p, i) for i in range(6))
            + tuple(stack(states_s, i) for i in range(6)))
```

```python
import functools

import numpy as np
import jax
import jax.numpy as jnp
from jax import lax
from jax.experimental import pallas as pl
from jax.experimental.pallas import tpu as pltpu

F32 = jnp.float32
BF16 = jnp.bfloat16

CHUNK = 64
BAND_CHUNKS = 8
WIN_A = BAND_CHUNKS * CHUNK
HEAD_DIM = 64
H_A = 4
H_B = 4
H_C = 8
C_NOPE = 64
C_ROPE = 32
C_V = 64
REL_CLIP = 128
ROPE_THETA = 10000.0
EPS = 1e-6
N_BRANCH = 3
W_AB = H_A * HEAD_DIM
C_PAD = 128
ATTN_SCALE = HEAD_DIM ** -0.5
MLA_SCALE = (C_NOPE + C_ROPE) ** -0.5
LOG2E = float(np.log2(np.e))
C_ONE = C_NOPE + C_ROPE
NORM_MARGIN = 1.02
MLA_MIN_DENOM = 2.0 ** -60
MLA_UNROLL = 4
SAMPLE_LATENT_TILE = 1024
NEG = -1e30
BAND_SB_SUB = 4
SB_ALWAYS = 2
SB_DEAD = -150.0

LANES = 128
MXU_WIDTH = 256
VMEM_LIMIT = 52 * 1024 * 1024

NT_DIMS = (((1,), (1,)), ((), ()))
TN_DIMS = (((0,), (0,)), ((), ()))


def _rms(x, g):
    return x * lax.rsqrt(jnp.mean(x * x, axis=-1, keepdims=True) + EPS) * g


def _dot(a, b):
    return jnp.dot(a, b, preferred_element_type=F32)


def _dot_nt(a, b):
    return lax.dot_general(a, b, NT_DIMS, preferred_element_type=F32)


def _dot_tn(a, b):
    return lax.dot_general(a, b, TN_DIMS, preferred_element_type=F32)


def _const_spec(shape):
    zeros = (0,) * len(shape)
    return pl.BlockSpec(shape, lambda *_: zeros, pipeline_mode=pl.Buffered(1))


def _row_spec(tm, width):
    return pl.BlockSpec((tm, width), lambda i: (i, 0))


def _params(n_grid):
    return pltpu.CompilerParams(dimension_semantics=("arbitrary",) * n_grid,
                                vmem_limit_bytes=VMEM_LIMIT)


def _ffn_kernel(x_ref, gpre_ref, gpost_ref, wg_ref, wu_ref, wd_ref, o_ref, *, f_cuts):
    x = x_ref[...]
    h = _rms(x, gpre_ref[...]).astype(BF16)
    y = jnp.zeros(x.shape, F32)
    for lo, hi in zip(f_cuts[:-1], f_cuts[1:]):
        sl = slice(lo, hi)
        g = _dot(h, wg_ref[:, sl])
        u = _dot(h, wu_ref[:, sl])
        a = (g * jax.nn.sigmoid(g) * u).astype(BF16)
        y = y + _dot(a, wd_ref[sl, :])
    o_ref[...] = x + 0.5 * _rms(y, gpost_ref[...])


def _ffn(x, g_pre, g_post, w_gate, w_up, w_down, tm):
    n, d = x.shape
    d_ff = w_gate.shape[1]
    cut = (d_ff // 2) // MXU_WIDTH * MXU_WIDTH
    f_cuts = (0, cut, d_ff) if 0 < cut < d_ff else (0, d_ff)
    return pl.pallas_call(
        functools.partial(_ffn_kernel, f_cuts=f_cuts),
        grid=(n // tm,),
        in_specs=[_row_spec(tm, d), _const_spec((1, d)), _const_spec((1, d)),
                  _const_spec((d, d_ff)), _const_spec((d, d_ff)), _const_spec((d_ff, d))],
        out_specs=_row_spec(tm, d),
        out_shape=jax.ShapeDtypeStruct((n, d), F32),
        compiler_params=_params(1),
        name="ffn",
    )(x, g_pre, g_post, w_gate, w_up, w_down)


def _rope(blk, tc, ts1, ts2):
    return blk * tc + pltpu.roll(blk, 16, 1) * ts1 + pltpu.roll(blk, C_PAD - 16, 1) * ts2


def _mixer_in_kernel(x_ref, gpre_ref, win_ref, gcq_ref, gckv_ref, wuq_ref, wukv_ref,
                     tc_ref, ts1_ref, ts2_ref,
                     qa_ref, kab_ref, vab_ref, qb_ref, kbb_ref, vbb_ref,
                     ka_ref, va_ref, kb_ref, vb_ref, ckv_ref, krb_ref,
                     qc_ref, kc_ref, vc_ref, kn_ref, *, q_transposed):
    w = W_AB
    q_rank = gcq_ref.shape[1]
    kv_rank = gckv_ref.shape[1]
    u = _rms(x_ref[...], gpre_ref[...]).astype(BF16)
    o = 6 * w
    lat = _dot(u, win_ref[:, o:])
    cq = lat[:, :q_rank]
    ckv = lat[:, q_rank:q_rank + kv_rank]
    krb = lat[:, q_rank + kv_rank:q_rank + kv_rank + C_PAD]
    proj = _dot(u, win_ref[:, :o])
    qa, ka, va = proj[:, 0:w], proj[:, w:2 * w], proj[:, 2 * w:3 * w]
    qb, kb, vb = proj[:, 3 * w:4 * w], proj[:, 4 * w:5 * w], proj[:, 5 * w:6 * w]
    qa_ref[...] = (qa * (ATTN_SCALE * LOG2E)).astype(BF16)
    qb_ref[...] = (qb * (ATTN_SCALE * LOG2E)).astype(BF16)
    ka_ref[...] = ka
    va_ref[...] = va
    kb_ref[...] = kb
    vb_ref[...] = vb
    kab_ref[...] = ka.astype(BF16)
    vab_ref[...] = va.astype(BF16)
    kbb_ref[...] = kb.astype(BF16)
    vbb_ref[...] = vb.astype(BF16)
    tc, ts1, ts2 = tc_ref[...], ts1_ref[...], ts2_ref[...]
    cqn = _rms(cq, gcq_ref[...]).astype(BF16)
    qall = _dot(cqn, wuq_ref[...])
    ckvn = _rms(ckv, gckv_ref[...])
    ckv_ref[...] = ckvn
    kv = _dot(ckvn.astype(BF16), wukv_ref[...])
    krot = _rope(krb, tc, ts1, ts2)
    krb_ref[...] = krot
    one_lane = (lax.broadcasted_iota(jnp.int32, (1, C_PAD), 1) == C_ONE).astype(F32)
    for h in range(H_C):
        sl = slice(h * C_PAD, (h + 1) * C_PAD)
        qh = _rope(qall[:, sl], tc, ts1, ts2) * (MLA_SCALE * LOG2E)
        if q_transposed:
            qt = qh.T
            norm = jnp.sqrt(jnp.sum(qt * qt, axis=0, keepdims=True)) * NORM_MARGIN
            row = lax.broadcasted_iota(jnp.int32, qt.shape, 0)
            qc_ref[sl, :] = jnp.where(row == C_ONE, norm, qt).astype(BF16)
        else:
            qc_ref[:, sl] = qh.astype(BF16)
        kh = kv[:, sl] + krot
        kc_ref[:, sl] = (kh + one_lane).astype(BF16)
        k_sq = _dot((kh * kh).astype(BF16), jnp.ones((C_PAD, C_PAD), BF16))
        kn_ref[0, :, sl] = jnp.sqrt(jnp.max(k_sq, axis=0, keepdims=True)) * NORM_MARGIN
    if not q_transposed:
        vc_ref[...] = kv[:, H_C * C_PAD:].astype(BF16)
    else:
        for p in range(H_C // 2):
            o = H_C * C_PAD + p * 2 * C_V
            vc_ref[0, p * 2 * C_V:(p + 1) * 2 * C_V, :] = kv[:, o:o + 2 * C_V].T.astype(BF16)


def _mixer_in(x, g_pre, w_main, g_cq, g_ckv, w_uq, w_ukv, tabs, tm, q_transposed=False):
    n, d = x.shape
    q_rank, kv_rank = g_cq.shape[1], g_ckv.shape[1]
    wc = H_C * C_PAD
    wv = H_C * C_V
    bf = lambda width: jax.ShapeDtypeStruct((n, width), BF16)
    f32 = lambda width: jax.ShapeDtypeStruct((n, width), F32)
    q_shape = jax.ShapeDtypeStruct((wc, n), BF16) if q_transposed else bf(wc)
    q_spec = pl.BlockSpec((wc, tm), lambda i: (0, i)) if q_transposed else _row_spec(tm, wc)
    v_shape = jax.ShapeDtypeStruct((n // tm, wv, tm), BF16) if q_transposed else bf(wv)
    v_spec = (pl.BlockSpec((1, wv, tm), lambda i: (i, 0, 0)) if q_transposed
              else _row_spec(tm, wv))
    out_shape = ([bf(W_AB)] * 6 + [f32(W_AB)] * 4 + [f32(kv_rank), f32(C_PAD)]
                 + [q_shape, bf(wc), v_shape, jax.ShapeDtypeStruct((n // tm, 1, wc), F32)])
    out_specs = ([_row_spec(tm, W_AB)] * 10 + [_row_spec(tm, kv_rank), _row_spec(tm, C_PAD)]
                 + [q_spec, _row_spec(tm, wc), v_spec, pl.BlockSpec((1, 1, wc), lambda i: (i, 0, 0))])
    return pl.pallas_call(
        functools.partial(_mixer_in_kernel, q_transposed=q_transposed),
        grid=(n // tm,),
        in_specs=[_row_spec(tm, d), _const_spec((1, d)), _const_spec(w_main.shape),
                  _const_spec((1, q_rank)), _const_spec((1, kv_rank)),
                  _const_spec(w_uq.shape), _const_spec(w_ukv.shape),
                  _row_spec(tm, C_PAD), _row_spec(tm, C_PAD), _row_spec(tm, C_PAD)],
        out_specs=out_specs,
        out_shape=out_shape,
        compiler_params=_params(1),
        name="mixer_in",
    )(x, g_pre, w_main, g_cq, g_ckv, w_uq, w_ukv, *tabs)


def _merge_kernel(x_ref, ya_ref, yb_ref, yc_ref, gpre_ref, gpost_ref, wgates_ref,
                  wa32_ref, wb32_ref, wc32_ref, wout32_ref, o_ref, wa_ref, wb_ref, wc_ref, wout_ref,
                  *, yc_transposed):
    @pl.when(pl.program_id(0) == 0)
    def _():
        for src, dst in ((wa32_ref, wa_ref), (wb32_ref, wb_ref), (wc32_ref, wc_ref), (wout32_ref, wout_ref)):
            dst[...] = src[0].astype(BF16)

    x = x_ref[...]
    d = x.shape[1]
    u = _rms(x, gpre_ref[...]).astype(BF16)
    m = jnp.zeros(x.shape, F32)
    for i, (y_ref, w_ref) in enumerate(((ya_ref, wa_ref), (yb_ref, wb_ref), (yc_ref, wc_ref))):
        gate = jax.nn.sigmoid(_dot(u, wgates_ref[:, i * d:(i + 1) * d]))
        branch_dot = _dot_tn if (yc_transposed and y_ref is yc_ref) else _dot
        m = m + gate * branch_dot(y_ref[...], w_ref[...])
    mixed = _dot(m.astype(BF16), wout_ref[...])
    o_ref[...] = x + _rms(mixed, gpost_ref[...])


def _merge(x, ya, yb, yc, g_pre, g_post, w_gates, l, w_a, w_b, w_c, w_out, tm, yc_transposed=False):
    n, d = x.shape
    yc_spec = (pl.BlockSpec((yc.shape[0], tm), lambda i: (0, i)) if yc_transposed
               else _row_spec(tm, yc.shape[1]))
    layer = lambda w: pl.BlockSpec((1,) + w.shape[1:], lambda i: (l, 0, 0), pipeline_mode=pl.Buffered(1))
    return pl.pallas_call(
        functools.partial(_merge_kernel, yc_transposed=yc_transposed),
        grid=(n // tm,),
        in_specs=[_row_spec(tm, d), _row_spec(tm, ya.shape[1]), _row_spec(tm, yb.shape[1]),
                  yc_spec, _const_spec((1, d)), _const_spec((1, d)),
                  _const_spec(w_gates.shape), layer(w_a), layer(w_b), layer(w_c), layer(w_out)],
        out_specs=_row_spec(tm, d),
        out_shape=jax.ShapeDtypeStruct((n, d), F32),
        scratch_shapes=[pltpu.VMEM(w.shape[1:], BF16) for w in (w_a, w_b, w_c, w_out)],
        compiler_params=_params(1),
        name="merge",
    )(x, ya, yb, yc, g_pre, g_post, w_gates, w_a, w_b, w_c, w_out)


def _head_masked(q, width):
    lane_head = lax.broadcasted_iota(jnp.int32, q.shape, 1) // width
    zero = jnp.zeros_like(q)
    return [jnp.where(lane_head == h, q, zero) for h in range(q.shape[1] // width)]


def _strict_upper_ones(n):
    j = lax.broadcasted_iota(jnp.int32, (n, n), 0)
    s = lax.broadcasted_iota(jnp.int32, (n, n), 1)
    return (j > s).astype(BF16)


def _sb_terms(qm, kt, ones_ut, causal):
    z = _dot_nt(qm, kt)
    t = jnp.log2(1.0 + jnp.exp2(-jnp.abs(z)))
    log_keep = -jnp.maximum(z, 0.0) - t
    log_beta = jnp.minimum(z, 0.0) - t
    if causal is not None:
        log_keep = jnp.where(causal, log_keep, 0.0)
    hi = log_keep.astype(BF16)
    lo = (log_keep - hi.astype(F32)).astype(BF16)
    tail = _dot(hi, ones_ut) + _dot(lo, ones_ut)
    return log_beta + tail, jnp.sum(log_keep, axis=1, keepdims=True)


def _sb_apply(terms, vt, carry, acc, causal):
    logw, total = terms
    w = jnp.exp2(logw + carry)
    if causal is not None:
        w = jnp.where(causal, w, 0.0)
    return carry + total, acc + _dot(w.astype(BF16), vt)


def _sb_tile(qm, kt, vt, carry, acc, ones_ut, causal):
    return _sb_apply(_sb_terms(qm, kt, ones_ut, causal), vt, carry, acc, causal)


def _sb_alive(carry):
    return jnp.max(carry) > SB_DEAD


def _select_heads(parts, width):
    lane_head = lax.broadcasted_iota(jnp.int32, parts[0].shape, 1) // width
    out = parts[0]
    for h in range(1, len(parts)):
        out = jnp.where(lane_head == h, parts[h], out)
    return out


def _band_sb_kernel(qa_ref, ka0_ref, ka1_ref, va0_ref, va1_ref, bias_ref, q_ref, k_ref, v_ref,
                    oa_ref, o_ref, *, tile):
    i = pl.program_id(0)
    sub = BAND_SB_SUB
    k_band = jnp.concatenate([ka0_ref[...], ka1_ref[...]], axis=0)
    v_band = jnp.concatenate([va0_ref[...], va1_ref[...]], axis=0)
    col = lax.broadcasted_iota(jnp.int32, (tile, 3 * tile), 1)
    for t in range(sub):
        kwin = k_band[(sub - 2 + t) * tile:(sub + 1 + t) * tile]
        vwin = v_band[(sub - 2 + t) * tile:(sub + 1 + t) * tile]
        in_window = col >= (2 - (sub * i + t)) * tile
        band = []
        for h, qm in enumerate(_head_masked(qa_ref[t * tile:(t + 1) * tile, :], HEAD_DIM)):
            s = jnp.where(in_window, _dot_nt(qm, kwin) + bias_ref[h], NEG)
            m = jnp.max(s, axis=1, keepdims=True)
            p = jnp.exp2(s - m)
            l = jnp.sum(p, axis=1, keepdims=True)
            band.append(_dot(p.astype(BF16), vwin) / l)
        oa_ref[t * tile:(t + 1) * tile, :] = _select_heads(band, HEAD_DIM).astype(BF16)

    row = lax.broadcasted_iota(jnp.int32, (tile, tile), 0)
    col = lax.broadcasted_iota(jnp.int32, (tile, tile), 1)
    causal = col < row
    ones_ut = _strict_upper_ones(tile)
    width = q_ref.shape[1]
    rows = lambda ref, kb: ref[pl.ds(pl.multiple_of(kb * tile, tile), tile), :]

    work = []
    for t in range(sub):
        it = sub * i + t
        qms = _head_masked(q_ref[t * tile:(t + 1) * tile, :], HEAD_DIM)
        tiles = [jnp.maximum(it - d, 0) for d in range(SB_ALWAYS)]
        terms = [[_sb_terms(qm, rows(k_ref, kb), ones_ut, causal if d == 0 else None) for qm in qms]
                 for d, kb in enumerate(tiles)]
        values = [rows(v_ref, kb) for kb in tiles]
        states = []
        for h in range(len(qms)):
            carry, acc = jnp.zeros((tile, 1), F32), jnp.zeros((tile, width), F32)
            for d in range(SB_ALWAYS):
                carry_in = carry if d == 0 else jnp.where(it >= d, carry, NEG)
                carry, acc = _sb_apply(terms[d][h], values[d], carry_in, acc, causal if d == 0 else None)
            states.append((carry, acc))
        work.append((it, qms, states))
    for t, (it, qms, states) in enumerate(work):
        first = it - SB_ALWAYS
        parts = []
        for qm, (carry, acc) in zip(qms, states):
            state = lax.while_loop(
                lambda st, first=first: (st[0] <= first) & _sb_alive(st[1]),
                lambda st, qm=qm, first=first: (st[0] + 1,) + _sb_tile(
                    qm, rows(k_ref, first - st[0]), rows(v_ref, first - st[0]), st[1], st[2], ones_ut, None),
                (jnp.int32(0), carry, acc))
            parts.append(state[2])
        o_ref[t * tile:(t + 1) * tile, :] = _select_heads(parts, HEAD_DIM).astype(BF16)


def _band_sb_attention(qa, ka, va, bias, q, k, v, tile):
    n, w = q.shape
    rows = BAND_SB_SUB * tile
    assert BAND_SB_SUB >= 2 and n % rows == 0
    blk = lambda back: pl.BlockSpec((rows, w), lambda i: (jnp.maximum(i - back, 0), 0))
    out = jax.ShapeDtypeStruct((n, w), BF16)
    return pl.pallas_call(
        functools.partial(_band_sb_kernel, tile=tile),
        grid=(n // rows,),
        in_specs=[blk(0), blk(1), blk(0), blk(1), blk(0), _const_spec(bias.shape),
                  blk(0), _const_spec((n, w)), _const_spec((n, w))],
        out_specs=[blk(0), blk(0)],
        out_shape=[out, out],
        compiler_params=_params(1),
        name="band_sb_attn",
    )(qa, ka, ka, va, va, bias, q, k, v)


def _mla_kernel(qt_ref, k_ref, vt_ref, kn_ref, ot_ref, qx_scr, p_scr, acc_scr, l_scr):
    i = pl.program_id(1)
    heads = range(2)
    tk, tq = p_scr.shape[2:]
    per_q = tq // tk
    plain = per_q * i
    key = lax.broadcasted_iota(jnp.int32, (tk, tq), 0)
    qry = lax.broadcasted_iota(jnp.int32, (tk, tq), 1)
    row = lax.broadcasted_iota(jnp.int32, (C_PAD - C_ONE, tq), 0)

    def visible(d):
        return ((d * tk + key) // CHUNK) <= (qry // CHUNK)

    def load_q(shifted):
        k_bound = jnp.max(kn_ref[...], axis=0)
        for e in heads:
            lo = e * C_PAD
            qx_scr[lo:lo + C_ONE, :] = qt_ref[lo:lo + C_ONE, :]
            if shifted:
                bound = qt_ref[lo + C_ONE:lo + C_ONE + 1, :].astype(F32) * (k_bound[:, lo:lo + 1] * NORM_MARGIN)
                tail = jnp.where(row == 0, -bound, 0.0)
            else:
                tail = jnp.zeros(row.shape, F32)
            qx_scr[lo + C_ONE:lo + C_PAD, :] = tail.astype(BF16)

    def scores(kb, e):
        return _dot(k_ref[kb, :, e * C_PAD:(e + 1) * C_PAD], qx_scr[e * C_PAD:(e + 1) * C_PAD, :])

    def values(kb, e, p):
        return _dot(vt_ref[kb, e * C_V:(e + 1) * C_V, :], p)

    def write_out():
        ot_ref[...] = jnp.concatenate([acc_scr[e] / l_scr[e] for e in heads], axis=0).astype(BF16)

    def reset():
        acc_scr[...] = jnp.zeros(acc_scr.shape, F32)
        l_scr[...] = jnp.zeros(l_scr.shape, F32)

    def fast_stage(kb, slot, masked_tile=None):
        lo = 0 if masked_tile is None else masked_tile * tk
        for e in heads:
            s = _dot(k_ref[kb, :, e * C_PAD:(e + 1) * C_PAD], qx_scr[e * C_PAD:(e + 1) * C_PAD, lo:])
            if masked_tile is not None:
                s = jnp.where(visible(0)[:, :tq - lo], s, NEG)
            p = jnp.exp2(s)
            l_scr[e, :, lo:] += jnp.sum(p, axis=0, keepdims=True)
            p_scr[slot, e, :, lo:] = p.astype(BF16)
        for e in heads:
            acc_scr[e, :, lo:] += values(kb, e, p_scr[slot, e, :, lo:])

    load_q(True)
    reset()

    def fast_group(g, _):
        for j in range(MLA_UNROLL):
            fast_stage(MLA_UNROLL * g + j, j)
        return 0
    groups = plain // MLA_UNROLL
    lax.fori_loop(0, groups, fast_group, 0)

    def fast_single(kb, _):
        fast_stage(kb, 0)
        return 0
    lax.fori_loop(MLA_UNROLL * groups, plain, fast_single, 0)
    for d in range(per_q):
        fast_stage(plain + d, (1 + d) % MLA_UNROLL, masked_tile=d)
    denom = jnp.minimum(jnp.min(l_scr[0]), jnp.min(l_scr[1]))
    trusted = denom >= MLA_MIN_DENOM

    @pl.when(trusted)
    def _():
        write_out()

    @pl.when(jnp.logical_not(trusted))
    def _():
        load_q(False)
        reset()

        def step(kb, m):
            out = []
            for e in heads:
                s = jnp.where(visible(kb - plain), scores(kb, e), NEG)
                m_new = jnp.maximum(m[e], jnp.max(s, axis=0, keepdims=True))
                alpha = jnp.exp2(m[e] - m_new)
                p = jnp.exp2(s - m_new)
                l_scr[e] = alpha * l_scr[e] + jnp.sum(p, axis=0, keepdims=True)
                acc_scr[e] = alpha * acc_scr[e] + values(kb, e, p.astype(BF16))
                out.append(m_new)
            return tuple(out)
        lax.fori_loop(0, plain + per_q, step, tuple(jnp.full((1, tq), NEG, F32) for _ in heads))
        write_out()


def _mla_attention(qt, k, vt3, kn, tq):
    n = k.shape[0]
    pairs = H_C // 2
    n_tiles, _, tk = vt3.shape
    assert n_tiles * tk == n and kn.shape == (n_tiles, 1, H_C * C_PAD)
    assert tq % tk == 0 and n % tq == 0
    k3 = k.reshape(n_tiles, tk, H_C * C_PAD)
    return pl.pallas_call(
        _mla_kernel,
        grid=(pairs, n // tq),
        in_specs=[pl.BlockSpec((2 * C_PAD, tq), lambda p, i: (p, i)),
                  pl.BlockSpec((n_tiles, tk, 2 * C_PAD), lambda p, i: (0, 0, p)),
                  pl.BlockSpec((n_tiles, 2 * C_V, tk), lambda p, i: (0, p, 0)),
                  pl.BlockSpec((n_tiles, 1, 2 * C_PAD), lambda p, i: (0, 0, p))],
        out_specs=pl.BlockSpec((2 * C_V, tq), lambda p, i: (p, i)),
        out_shape=jax.ShapeDtypeStruct((H_C * C_V, n), BF16),
        scratch_shapes=[pltpu.VMEM((2 * C_PAD, tq), BF16), pltpu.VMEM((MLA_UNROLL, 2, tk, tq), BF16),
                        pltpu.VMEM((2, C_V, tq), F32), pltpu.VMEM((2, 1, tq), F32)],
        compiler_params=_params(2),
        name="mla_attn",
    )(qt, k3, vt3, kn)


def _sample_attn_kernel(qa_ref, ka_ref, va_ref, cak_ref, cav_ref, biasc_ref, biasn_ref,
                        qb_ref, kb_ref, vb_ref, cbk_ref, cbv_ref,
                        qc_ref, ckvn_ref, krbn_ref, ckv_ref, ckr_ref, wukt_ref, wuvp_ref,
                        ya_ref, yb_ref, yc_ref, s_scr, *, tile):
    ds = qa_ref.shape[1]
    past = cbk_ref.shape[2]

    ka, va, cak, cav = ka_ref[0], va_ref[0], cak_ref[0, 0].astype(BF16), cav_ref[0, 0].astype(BF16)
    parts = []
    for h, qm in enumerate(_head_masked(qa_ref[0], HEAD_DIM)):
        s_c = _dot_nt(qm, cak) + biasc_ref[h]
        s_n = _dot_nt(qm, ka) + biasn_ref[h]
        m = jnp.maximum(jnp.max(s_c, axis=1, keepdims=True), jnp.max(s_n, axis=1, keepdims=True))
        p_c = jnp.exp2(s_c - m)
        p_n = jnp.exp2(s_n - m)
        l = jnp.sum(p_c, axis=1, keepdims=True) + jnp.sum(p_n, axis=1, keepdims=True)
        parts.append((_dot(p_c.astype(BF16), cav) + _dot(p_n.astype(BF16), va)) / l)
    ya_ref[0] = _select_heads(parts, HEAD_DIM).astype(BF16)

    row = lax.broadcasted_iota(jnp.int32, (ds, ds), 0)
    col = lax.broadcasted_iota(jnp.int32, (ds, ds), 1)
    causal = col < row
    ones_new = _strict_upper_ones(ds)
    ones_ut = _strict_upper_ones(tile)
    kb, vb = kb_ref[0], vb_ref[0]
    n_tiles = past // tile

    def cache_rows(ref, j):
        start = pl.multiple_of((n_tiles - 1 - j) * tile, tile)
        return ref[0, 0, pl.ds(start, tile), :].astype(BF16)

    b_heads = _head_masked(qb_ref[0], HEAD_DIM)
    b_states = []
    for qm in b_heads:
        state = _sb_tile(qm, kb, vb, jnp.zeros((ds, 1), F32), jnp.zeros((ds, W_AB), F32),
                         ones_new, causal)
        b_states.append(_sb_tile(qm, cache_rows(cbk_ref, 0), cache_rows(cbv_ref, 0), state[0], state[1],
                                 ones_ut, None))

    kv_rank = ckv_ref.shape[3]
    qc = qc_ref[0]
    rope_lanes = lax.broadcasted_iota(jnp.int32, (ds, C_PAD), 1) >= C_NOPE
    q_rows = []
    for h in range(H_C):
        qh = qc[:, h * C_PAD:(h + 1) * C_PAD]
        q_lat = _dot(qh, wukt_ref[h]).astype(BF16)
        q_rows.append(jnp.concatenate([q_lat, jnp.where(rope_lanes, qh, jnp.zeros_like(qh))], axis=1))
    q_all = jnp.concatenate(q_rows, axis=0)

    place = (lax.broadcasted_iota(jnp.int32, (C_ROPE, C_PAD), 1)
             == lax.broadcasted_iota(jnp.int32, (C_ROPE, C_PAD), 0) + C_NOPE).astype(BF16)
    tile_c = SAMPLE_LATENT_TILE if past % SAMPLE_LATENT_TILE == 0 else tile
    cache_tiles = range(past // tile_c)
    latent = lambda j: ckv_ref[0, 0, j * tile_c:(j + 1) * tile_c, :].astype(BF16)

    lat_new = ckvn_ref[0].astype(BF16)
    s_new = _dot_nt(q_all, jnp.concatenate([lat_new, krbn_ref[0].astype(BF16)], axis=1))
    m = jnp.max(s_new, axis=1, keepdims=True)
    for j in cache_tiles:
        k_rope = _dot(ckr_ref[0, 0, j * tile_c:(j + 1) * tile_c, :].astype(BF16), place).astype(BF16)
        s = _dot_nt(q_all, jnp.concatenate([latent(j), k_rope], axis=1))
        s_scr[:, j * tile_c:(j + 1) * tile_c] = s
        m = jnp.maximum(m, jnp.max(s, axis=1, keepdims=True))
    p = jnp.exp2(s_new - m)
    denom = jnp.sum(p, axis=1, keepdims=True)
    o_lat = _dot(p.astype(BF16), lat_new)
    for j in cache_tiles:
        p = jnp.exp2(s_scr[:, j * tile_c:(j + 1) * tile_c] - m)
        denom = denom + jnp.sum(p, axis=1, keepdims=True)
        o_lat = o_lat + _dot(p.astype(BF16), latent(j))
    o_lat = (o_lat / denom).astype(BF16)
    out = jnp.zeros((ds, H_C * C_V), F32)
    for h in range(H_C):
        out = out + _dot(o_lat[h * ds:(h + 1) * ds], wuvp_ref[h])
    yc_ref[0] = out.astype(BF16)

    parts = []
    for qm, state in zip(b_heads, b_states):
        state = lax.while_loop(
            lambda st: (st[0] < n_tiles) & _sb_alive(st[1]),
            lambda st, qm=qm: (st[0] + 1,) + _sb_tile(qm, cache_rows(cbk_ref, st[0]), cache_rows(cbv_ref, st[0]),
                                                      st[1], st[2], ones_ut, None),
            (jnp.int32(1),) + state)
        parts.append(state[2])
    yb_ref[0] = _select_heads(parts, HEAD_DIM).astype(BF16)


def _sample_attention(new, caches, l, bias_c, bias_n, w_ukt, w_uvp, tile):
    qa, ka, va, qb, kb, vb, qc, ckvn, krbn = new
    cak, cav, cbk, cbv, ckv, ckr = caches
    nb, ds, _ = qa.shape
    past = ckv.shape[2]
    req = lambda a: pl.BlockSpec((1,) + a.shape[1:], lambda b: (b, 0, 0))
    cache = lambda a: pl.BlockSpec((1, 1) + a.shape[2:], lambda b: (l, b, 0, 0))
    ins = [qa, ka, va, cak, cav, bias_c, bias_n, qb, kb, vb, cbk, cbv, qc, ckvn, krbn, ckv, ckr,
           w_ukt, w_uvp]
    specs = [req(a) for a in ins]
    for idx in (3, 4, 10, 11, 15, 16):
        specs[idx] = cache(ins[idx])
    for idx in (5, 6, 17, 18):
        specs[idx] = _const_spec(ins[idx].shape)
    out_shape = [jax.ShapeDtypeStruct((nb, ds, W_AB), BF16), jax.ShapeDtypeStruct((nb, ds, W_AB), BF16),
                 jax.ShapeDtypeStruct((nb, ds, H_C * C_V), BF16)]
    return pl.pallas_call(
        functools.partial(_sample_attn_kernel, tile=tile),
        grid=(nb,),
        in_specs=specs,
        out_specs=[req(s) for s in out_shape],
        out_shape=out_shape,
        scratch_shapes=[pltpu.VMEM((H_C * ds, past), F32)],
        compiler_params=_params(1),
        name="sample_attn",
    )(*ins)


def _rope_tables(pos):
    half = C_ROPE // 2
    freq = ROPE_THETA ** (-jnp.arange(half, dtype=F32) / half)
    ang = pos.astype(F32)[:, None] * freq[None, :]
    cos, sin = jnp.cos(ang), jnp.sin(ang)
    n = pos.shape[0]
    z = lambda w: jnp.zeros((n, w), F32)
    tc = jnp.concatenate([jnp.ones((n, C_NOPE), F32), cos, cos, z(C_PAD - C_NOPE - C_ROPE)], axis=1)
    ts1 = jnp.concatenate([z(C_NOPE + half), sin, z(C_PAD - C_NOPE - C_ROPE)], axis=1)
    ts2 = jnp.concatenate([z(C_NOPE), -sin, z(C_PAD - C_NOPE - half)], axis=1)
    return tc, ts1, ts2


def _band_bias(rel_bias, q0, nq, k0, nk):
    period = 1 << int(np.ceil(np.log2(nq + nk)))
    d = np.arange(period)
    d = np.where(d >= nk, d - period, d)
    idx = np.clip((q0 - k0) - d, -REL_CLIP, REL_CLIP) + REL_CLIP
    g = rel_bias.astype(F32)[:, idx]
    toe = jnp.tile(g, (1, nq))[:, :nq * (period - 1)].reshape(-1, nq, period - 1)[:, :, :nk]
    q_pos, k_pos = q0 + np.arange(nq), k0 + np.arange(nk)
    qc, kc = q_pos[:, None] // CHUNK, k_pos[None, :] // CHUNK
    vis = (kc <= qc) & (kc >= qc - BAND_CHUNKS)
    return jnp.where(jnp.asarray(vis)[None], toe * LOG2E, NEG)


def _cast_kernel(w_ref, o_ref):
    o_ref[...] = w_ref[0].astype(BF16)


def _layer_bf16(w, l):
    _, rows, cols = w.shape
    rb = rows // 2 if rows % 32 == 0 else rows
    return pl.pallas_call(
        _cast_kernel,
        grid=(rows // rb,),
        in_specs=[pl.BlockSpec((1, rb, cols), lambda i: (l, i, 0))],
        out_specs=pl.BlockSpec((rb, cols), lambda i: (i, 0)),
        out_shape=jax.ShapeDtypeStruct((rows, cols), BF16),
        compiler_params=_params(1),
        name="cast_bf16",
    )(w)


def _layer_weights(l, w_in, w_uq, w_uk, w_uv):
    d = w_in.shape[1]
    q_rank, kv_rank = w_uq.shape[1], w_uk.shape[1]
    o = 6 * W_AB + q_rank + kv_rank
    zc = lambda w: jnp.zeros((d, w), w_in.dtype)
    w_main = jnp.concatenate([w_in[l, :, :o], zc(C_NOPE), w_in[l, :, o:o + C_ROPE],
                              zc(C_PAD - C_NOPE - C_ROPE)], axis=1).astype(BF16)
    w_gates = w_in[l, :, o + C_ROPE:].astype(BF16)
    pad_heads = lambda w: jnp.pad(w, ((0, 0), (0, 0), (0, C_PAD - w.shape[2]))).reshape(w.shape[0], -1)
    w_uq_p = pad_heads(w_uq[l]).astype(BF16)
    w_ukv = jnp.concatenate([pad_heads(w_uk[l]), w_uv[l].reshape(kv_rank, -1)], axis=1).astype(BF16)
    w_ukt = jnp.pad(jnp.transpose(w_uk[l], (1, 2, 0)), ((0, 0), (0, C_PAD - C_NOPE), (0, 0))).astype(BF16)
    w_uvp = jnp.einsum('rhd,hg->hrgd', w_uv[l], jnp.eye(H_C, dtype=w_uv.dtype)).reshape(
        H_C, kv_rank, H_C * C_V).astype(BF16)
    return w_main, w_gates, w_uq_p, w_ukv, w_ukt, w_uvp


def kernel(x_prompt, x_sample, cache_a_k, cache_a_v, cache_b_k, cache_b_v, cache_c_kv, cache_c_kr, ffn1_norm_pre, ffn1_norm_post, ffn1_w_gate, ffn1_w_up, ffn1_w_down, mix_norm_pre, mix_norm_post, w_in, cq_norm, ckv_norm, w_uq, w_uk, w_uv, rel_bias_a, w_br_a, w_br_b, w_br_c, w_out, ffn2_norm_pre, ffn2_norm_post, ffn2_w_gate, ffn2_w_up, ffn2_w_down):
    batch, seq, d = x_prompt.shape
    nb, ds, _ = x_sample.shape
    depth = w_in.shape[0]
    past = cache_b_k.shape[2]
    win_cache = cache_a_k.shape[2]
    assert batch == 1, "prompt group is a single sequence"
    tile = 256
    tm = 512 if seq % 512 == 0 else tile
    assert seq % tile == 0 and past % tile == 0 and tile % CHUNK == 0 and 2 * tile >= WIN_A
    assert past % CHUNK + ds <= CHUNK
    ns = nb * ds

    xp = x_prompt.reshape(seq, d)
    xs = x_sample.reshape(ns, d)
    tabs_p = _rope_tables(jnp.arange(seq))
    tabs_s = _rope_tables(jnp.tile(past + jnp.arange(ds), nb))

    row = lambda g, l: g[l][None, :]
    base = 2 * tile

    merged_heads = lambda c: c.reshape(c.shape[:3] + (-1,))
    caches = (merged_heads(cache_a_k), merged_heads(cache_a_v), merged_heads(cache_b_k),
              merged_heads(cache_b_v), cache_c_kv, cache_c_kr)
    states_p, states_s = [], []
    for l in range(depth):
        w_main, w_gates, w_uq_p, w_ukv, w_ukt, w_uvp = _layer_weights(l, w_in, w_uq, w_uk, w_uv)
        ffn1 = (row(ffn1_norm_pre, l), row(ffn1_norm_post, l), _layer_bf16(ffn1_w_gate, l),
                _layer_bf16(ffn1_w_up, l), _layer_bf16(ffn1_w_down, l))
        ffn2 = (row(ffn2_norm_pre, l), row(ffn2_norm_post, l), _layer_bf16(ffn2_w_gate, l),
                _layer_bf16(ffn2_w_up, l), _layer_bf16(ffn2_w_down, l))
        mix_in = (row(mix_norm_pre, l), w_main, row(cq_norm, l), row(ckv_norm, l), w_uq_p, w_ukv)
        mrg = (row(mix_norm_pre, l), row(mix_norm_post, l), w_gates, l, w_br_a, w_br_b, w_br_c, w_out)
        bias_p = _band_bias(rel_bias_a[l], base, tile, base - 2 * tile, 3 * tile)
        bias_c = _band_bias(rel_bias_a[l], past, ds, past - win_cache, win_cache)
        bias_n = _band_bias(rel_bias_a[l], past, ds, past, ds)

        xp = _ffn(xp, *ffn1, tm)
        (qa, kab, vab, qb, kbb, vbb, ka, va, kb, vb, ckv, krb, qc, kc, vc, kn) = _mixer_in(
            xp, *mix_in, tabs_p, tm, q_transposed=True)
        ya, yb = _band_sb_attention(qa, kab, vab, bias_p, qb, kbb, vbb, tile)
        yc = _mla_attention(qc, kc, vc, kn, 2 * tm if seq % (2 * tm) == 0 else tm)
        xp = _merge(xp, ya, yb, yc, *mrg, tm, yc_transposed=True)
        xp = _ffn(xp, *ffn2, tm)
        win = min(WIN_A, seq)
        heads = lambda t, h: t.reshape(1, t.shape[0], h, t.shape[1] // h)
        states_p.append((heads(ka[-win:], H_A), heads(va[-win:], H_A), heads(kb, H_B), heads(vb, H_B),
                         ckv[None], krb[None, :, C_NOPE:C_NOPE + C_ROPE]))

        xs = _ffn(xs, *ffn1, ns)
        (qa, kab, vab, qb, kbb, vbb, ka, va, kb, vb, ckv, krb, qc, kc, vc, _) = _mixer_in(
            xs, *mix_in, tabs_s, ns)
        per_req = lambda t: t.reshape(nb, ds, t.shape[1])
        new = tuple(per_req(t) for t in (qa, kab, vab, qb, kbb, vbb, qc, ckv, krb))
        ya, yb, yc = _sample_attention(new, caches, l, bias_c, bias_n, w_ukt, w_uvp, tile)
        xs = _merge(xs, ya.reshape(ns, -1), yb.reshape(ns, -1), yc.reshape(ns, -1), *mrg, ns)
        xs = _ffn(xs, *ffn2, ns)
        heads_s = lambda t, h: t.reshape(nb, ds, h, t.shape[1] // h)
        states_s.append((heads_s(ka, H_A), heads_s(va, H_A), heads_s(kb, H_B), heads_s(vb, H_B),
                         ckv.reshape(nb, ds, -1), krb[:, C_NOPE:C_NOPE + C_ROPE].reshape(nb, ds, C_ROPE)))

    stack = lambda states, i: jnp.stack([s[i] for s in states], axis=0)
    return ((xp.reshape(batch, seq, d), xs.reshape(nb, ds, d))
            + tuple(stack(states_p, i) for i in range(6))
            + tuple(stack(states_s, i) for i in range(6)))
```

```python
import functools

import numpy as np
import jax
import jax.numpy as jnp
from jax import lax
from jax.experimental import pallas as pl
from jax.experimental.pallas import tpu as pltpu

F32 = jnp.float32
BF16 = jnp.bfloat16

CHUNK = 64
BAND_CHUNKS = 8
WIN_A = BAND_CHUNKS * CHUNK
HEAD_DIM = 64
H_A = 4
H_B = 4
H_C = 8
C_NOPE = 64
C_ROPE = 32
C_V = 64
REL_CLIP = 128
ROPE_THETA = 10000.0
EPS = 1e-6
N_BRANCH = 3
W_AB = H_A * HEAD_DIM
C_PAD = 128
ATTN_SCALE = HEAD_DIM ** -0.5
MLA_SCALE = (C_NOPE + C_ROPE) ** -0.5
LOG2E = float(np.log2(np.e))
C_ONE = C_NOPE + C_ROPE
NORM_MARGIN = 1.02
MLA_MIN_DENOM = 2.0 ** -60
MLA_UNROLL = 4
SAMPLE_LATENT_TILE = 1024
NEG = -1e30
BAND_SB_SUB = 4
SB_ALWAYS = 2
SB_DEAD = -150.0

LANES = 128
MXU_WIDTH = 256
VMEM_LIMIT = 52 * 1024 * 1024

NT_DIMS = (((1,), (1,)), ((), ()))
TN_DIMS = (((0,), (0,)), ((), ()))


def _rms(x, g):
    return x * lax.rsqrt(jnp.mean(x * x, axis=-1, keepdims=True) + EPS) * g


def _dot(a, b):
    return jnp.dot(a, b, preferred_element_type=F32)


def _dot_nt(a, b):
    return lax.dot_general(a, b, NT_DIMS, preferred_element_type=F32)


def _dot_tn(a, b):
    return lax.dot_general(a, b, TN_DIMS, preferred_element_type=F32)


def _const_spec(shape):
    zeros = (0,) * len(shape)
    return pl.BlockSpec(shape, lambda *_: zeros, pipeline_mode=pl.Buffered(1))


def _row_spec(tm, width):
    return pl.BlockSpec((tm, width), lambda i: (i, 0))


def _params(n_grid):
    return pltpu.CompilerParams(dimension_semantics=("arbitrary",) * n_grid,
                                vmem_limit_bytes=VMEM_LIMIT)


def _ffn_kernel(x_ref, gpre_ref, gpost_ref, wg_ref, wu_ref, wd_ref, o_ref, *, f_cuts):
    x = x_ref[...]
    h = _rms(x, gpre_ref[...]).astype(BF16)
    y = jnp.zeros(x.shape, F32)
    for lo, hi in zip(f_cuts[:-1], f_cuts[1:]):
        sl = slice(lo, hi)
        g = _dot(h, wg_ref[:, sl])
        u = _dot(h, wu_ref[:, sl])
        a = (g * jax.nn.sigmoid(g) * u).astype(BF16)
        y = y + _dot(a, wd_ref[sl, :])
    o_ref[...] = x + 0.5 * _rms(y, gpost_ref[...])


def _ffn(x, g_pre, g_post, w_gate, w_up, w_down, tm):
    n, d = x.shape
    d_ff = w_gate.shape[1]
    cut = (d_ff // 2) // MXU_WIDTH * MXU_WIDTH
    f_cuts = (0, cut, d_ff) if 0 < cut < d_ff else (0, d_ff)
    return pl.pallas_call(
        functools.partial(_ffn_kernel, f_cuts=f_cuts),
        grid=(n // tm,),
        in_specs=[_row_spec(tm, d), _const_spec((1, d)), _const_spec((1, d)),
                  _const_spec((d, d_ff)), _const_spec((d, d_ff)), _const_spec((d_ff, d))],
        out_specs=_row_spec(tm, d),
        out_shape=jax.ShapeDtypeStruct((n, d), F32),
        compiler_params=_params(1),
        name="ffn",
    )(x, g_pre, g_post, w_gate, w_up, w_down)


def _rope(blk, tc, ts1, ts2):
    return blk * tc + pltpu.roll(blk, 16, 1) * ts1 + pltpu.roll(blk, C_PAD - 16, 1) * ts2


def _mixer_in_kernel(x_ref, gpre_ref, win_ref, gcq_ref, gckv_ref, wuq_ref, wukv_ref,
                     tc_ref, ts1_ref, ts2_ref,
                     qa_ref, kab_ref, vab_ref, qb_ref, kbb_ref, vbb_ref,
                     ka_ref, va_ref, kb_ref, vb_ref, ckv_ref, krb_ref,
                     qc_ref, kc_ref, vc_ref, kn_ref, *, q_transposed):
    w = W_AB
    q_rank = gcq_ref.shape[1]
    kv_rank = gckv_ref.shape[1]
    u = _rms(x_ref[...], gpre_ref[...]).astype(BF16)
    o = 6 * w
    lat = _dot(u, win_ref[0, :, o:])
    cq = lat[:, :q_rank]
    ckv = lat[:, q_rank:q_rank + kv_rank]
    kr_first = lat[:, q_rank + kv_rank:q_rank + kv_rank + C_PAD]
    lane = lax.broadcasted_iota(jnp.int32, kr_first.shape, 1)
    krb = jnp.where((lane >= C_NOPE) & (lane < C_NOPE + C_ROPE), pltpu.roll(kr_first, C_NOPE, 1), 0.0)
    proj = _dot(u, win_ref[0, :, :o])
    qa, ka, va = proj[:, 0:w], proj[:, w:2 * w], proj[:, 2 * w:3 * w]
    qb, kb, vb = proj[:, 3 * w:4 * w], proj[:, 4 * w:5 * w], proj[:, 5 * w:6 * w]
    qa_ref[...] = (qa * (ATTN_SCALE * LOG2E)).astype(BF16)
    qb_ref[...] = (qb * (ATTN_SCALE * LOG2E)).astype(BF16)
    ka_ref[...] = ka
    va_ref[...] = va
    kb_ref[...] = kb
    vb_ref[...] = vb
    kab_ref[...] = ka.astype(BF16)
    vab_ref[...] = va.astype(BF16)
    kbb_ref[...] = kb.astype(BF16)
    vbb_ref[...] = vb.astype(BF16)
    tc, ts1, ts2 = tc_ref[...], ts1_ref[...], ts2_ref[...]
    cqn = _rms(cq, gcq_ref[...]).astype(BF16)
    qall = _dot(cqn, wuq_ref[...])
    ckvn = _rms(ckv, gckv_ref[...])
    ckv_ref[...] = ckvn
    kv = _dot(ckvn.astype(BF16), wukv_ref[...])
    krot = _rope(krb, tc, ts1, ts2)
    krb_ref[...] = krot
    one_lane = (lax.broadcasted_iota(jnp.int32, (1, C_PAD), 1) == C_ONE).astype(F32)
    for h in range(H_C):
        sl = slice(h * C_PAD, (h + 1) * C_PAD)
        qh = _rope(qall[:, sl], tc, ts1, ts2) * (MLA_SCALE * LOG2E)
        if q_transposed:
            qt = qh.T
            norm = jnp.sqrt(jnp.sum(qt * qt, axis=0, keepdims=True)) * NORM_MARGIN
            row = lax.broadcasted_iota(jnp.int32, qt.shape, 0)
            qc_ref[sl, :] = jnp.where(row == C_ONE, norm, qt).astype(BF16)
        else:
            qc_ref[:, sl] = qh.astype(BF16)
        kh = kv[:, sl] + krot
        kc_ref[:, sl] = (kh + one_lane).astype(BF16)
        k_sq = _dot((kh * kh).astype(BF16), jnp.ones((C_PAD, C_PAD), BF16))
        kn_ref[0, :, sl] = jnp.sqrt(jnp.max(k_sq, axis=0, keepdims=True)) * NORM_MARGIN
    if not q_transposed:
        vc_ref[...] = kv[:, H_C * C_PAD:].astype(BF16)
    else:
        for p in range(H_C // 2):
            o = H_C * C_PAD + p * 2 * C_V
            vc_ref[0, p * 2 * C_V:(p + 1) * 2 * C_V, :] = kv[:, o:o + 2 * C_V].T.astype(BF16)


def _mixer_in(x, g_pre, w_in, l, g_cq, g_ckv, w_uq, w_ukv, tabs, tm, q_transposed=False):
    n, d = x.shape
    q_rank, kv_rank = g_cq.shape[1], g_ckv.shape[1]
    wc = H_C * C_PAD
    wv = H_C * C_V
    bf = lambda width: jax.ShapeDtypeStruct((n, width), BF16)
    f32 = lambda width: jax.ShapeDtypeStruct((n, width), F32)
    q_shape = jax.ShapeDtypeStruct((wc, n), BF16) if q_transposed else bf(wc)
    q_spec = pl.BlockSpec((wc, tm), lambda i: (0, i)) if q_transposed else _row_spec(tm, wc)
    v_shape = jax.ShapeDtypeStruct((n // tm, wv, tm), BF16) if q_transposed else bf(wv)
    v_spec = (pl.BlockSpec((1, wv, tm), lambda i: (i, 0, 0)) if q_transposed
              else _row_spec(tm, wv))
    out_shape = ([bf(W_AB)] * 6 + [f32(W_AB)] * 4 + [f32(kv_rank), f32(C_PAD)]
                 + [q_shape, bf(wc), v_shape, jax.ShapeDtypeStruct((n // tm, 1, wc), F32)])
    out_specs = ([_row_spec(tm, W_AB)] * 10 + [_row_spec(tm, kv_rank), _row_spec(tm, C_PAD)]
                 + [q_spec, _row_spec(tm, wc), v_spec, pl.BlockSpec((1, 1, wc), lambda i: (i, 0, 0))])
    return pl.pallas_call(
        functools.partial(_mixer_in_kernel, q_transposed=q_transposed),
        grid=(n // tm,),
        in_specs=[_row_spec(tm, d), _const_spec((1, d)),
                  pl.BlockSpec((1, d, 6 * W_AB + q_rank + kv_rank + C_PAD), lambda i: (l, 0, 0),
                               pipeline_mode=pl.Buffered(1)),
                  _const_spec((1, q_rank)), _const_spec((1, kv_rank)),
                  _const_spec(w_uq.shape), _const_spec(w_ukv.shape),
                  _row_spec(tm, C_PAD), _row_spec(tm, C_PAD), _row_spec(tm, C_PAD)],
        out_specs=out_specs,
        out_shape=out_shape,
        compiler_params=_params(1),
        name="mixer_in",
    )(x, g_pre, w_in, g_cq, g_ckv, w_uq, w_ukv, *tabs)


def _merge_kernel(x_ref, ya_ref, yb_ref, yc_ref, gpre_ref, gpost_ref, wgates_ref,
                  wa32_ref, wb32_ref, wc32_ref, wout32_ref, o_ref, wa_ref, wb_ref, wc_ref, wout_ref,
                  *, yc_transposed):
    @pl.when(pl.program_id(0) == 0)
    def _():
        for src, dst in ((wa32_ref, wa_ref), (wb32_ref, wb_ref), (wc32_ref, wc_ref), (wout32_ref, wout_ref)):
            dst[...] = src[0].astype(BF16)

    x = x_ref[...]
    d = x.shape[1]
    u = _rms(x, gpre_ref[...]).astype(BF16)
    m = jnp.zeros(x.shape, F32)
    for i, (y_ref, w_ref) in enumerate(((ya_ref, wa_ref), (yb_ref, wb_ref), (yc_ref, wc_ref))):
        gate = jax.nn.sigmoid(_dot(u, wgates_ref[:, i * d:(i + 1) * d]))
        branch_dot = _dot_tn if (yc_transposed and y_ref is yc_ref) else _dot
        m = m + gate * branch_dot(y_ref[...], w_ref[...])
    mixed = _dot(m.astype(BF16), wout_ref[...])
    o_ref[...] = x + _rms(mixed, gpost_ref[...])


def _merge(x, ya, yb, yc, g_pre, g_post, w_gates, l, w_a, w_b, w_c, w_out, tm, yc_transposed=False):
    n, d = x.shape
    yc_spec = (pl.BlockSpec((yc.shape[0], tm), lambda i: (0, i)) if yc_transposed
               else _row_spec(tm, yc.shape[1]))
    layer = lambda w: pl.BlockSpec((1,) + w.shape[1:], lambda i: (l, 0, 0), pipeline_mode=pl.Buffered(1))
    return pl.pallas_call(
        functools.partial(_merge_kernel, yc_transposed=yc_transposed),
        grid=(n // tm,),
        in_specs=[_row_spec(tm, d), _row_spec(tm, ya.shape[1]), _row_spec(tm, yb.shape[1]),
                  yc_spec, _const_spec((1, d)), _const_spec((1, d)),
                  _const_spec(w_gates.shape), layer(w_a), layer(w_b), layer(w_c), layer(w_out)],
        out_specs=_row_spec(tm, d),
        out_shape=jax.ShapeDtypeStruct((n, d), F32),
        scratch_shapes=[pltpu.VMEM(w.shape[1:], BF16) for w in (w_a, w_b, w_c, w_out)],
        compiler_params=_params(1),
        name="merge",
    )(x, ya, yb, yc, g_pre, g_post, w_gates, w_a, w_b, w_c, w_out)


def _head_masked(q, width):
    lane_head = lax.broadcasted_iota(jnp.int32, q.shape, 1) // width
    zero = jnp.zeros_like(q)
    return [jnp.where(lane_head == h, q, zero) for h in range(q.shape[1] // width)]


def _strict_upper_ones(n):
    j = lax.broadcasted_iota(jnp.int32, (n, n), 0)
    s = lax.broadcasted_iota(jnp.int32, (n, n), 1)
    return (j > s).astype(BF16)


def _sb_terms(qm, kt, ones_ut, causal):
    z = _dot_nt(qm, kt)
    t = jnp.log2(1.0 + jnp.exp2(-jnp.abs(z)))
    log_keep = -jnp.maximum(z, 0.0) - t
    log_beta = jnp.minimum(z, 0.0) - t
    if causal is not None:
        log_keep = jnp.where(causal, log_keep, 0.0)
    hi = log_keep.astype(BF16)
    lo = (log_keep - hi.astype(F32)).astype(BF16)
    tail = _dot(hi, ones_ut) + _dot(lo, ones_ut)
    return log_beta + tail, jnp.sum(log_keep, axis=1, keepdims=True)


def _sb_apply(terms, vt, carry, acc, causal):
    logw, total = terms
    w = jnp.exp2(logw + carry)
    if causal is not None:
        w = jnp.where(causal, w, 0.0)
    return carry + total, acc + _dot(w.astype(BF16), vt)


def _sb_tile(qm, kt, vt, carry, acc, ones_ut, causal):
    return _sb_apply(_sb_terms(qm, kt, ones_ut, causal), vt, carry, acc, causal)


def _sb_alive(carry):
    return jnp.max(carry) > SB_DEAD


def _select_heads(parts, width):
    lane_head = lax.broadcasted_iota(jnp.int32, parts[0].shape, 1) // width
    out = parts[0]
    for h in range(1, len(parts)):
        out = jnp.where(lane_head == h, parts[h], out)
    return out


def _band_sb_kernel(qa_ref, ka0_ref, ka1_ref, va0_ref, va1_ref, bias_ref, q_ref, k_ref, v_ref,
                    oa_ref, o_ref, *, tile):
    i = pl.program_id(0)
    sub = BAND_SB_SUB
    k_band = jnp.concatenate([ka0_ref[...], ka1_ref[...]], axis=0)
    v_band = jnp.concatenate([va0_ref[...], va1_ref[...]], axis=0)
    col = lax.broadcasted_iota(jnp.int32, (tile, 3 * tile), 1)
    for t in range(sub):
        kwin = k_band[(sub - 2 + t) * tile:(sub + 1 + t) * tile]
        vwin = v_band[(sub - 2 + t) * tile:(sub + 1 + t) * tile]
        in_window = col >= (2 - (sub * i + t)) * tile
        band = []
        for h, qm in enumerate(_head_masked(qa_ref[t * tile:(t + 1) * tile, :], HEAD_DIM)):
            s = jnp.where(in_window, _dot_nt(qm, kwin) + bias_ref[h], NEG)
            m = jnp.max(s, axis=1, keepdims=True)
            p = jnp.exp2(s - m)
            l = jnp.sum(p, axis=1, keepdims=True)
            band.append(_dot(p.astype(BF16), vwin) / l)
        oa_ref[t * tile:(t + 1) * tile, :] = _select_heads(band, HEAD_DIM).astype(BF16)

    row = lax.broadcasted_iota(jnp.int32, (tile, tile), 0)
    col = lax.broadcasted_iota(jnp.int32, (tile, tile), 1)
    causal = col < row
    ones_ut = _strict_upper_ones(tile)
    width = q_ref.shape[1]
    rows = lambda ref, kb: ref[pl.ds(pl.multiple_of(kb * tile, tile), tile), :]

    work = []
    for t in range(sub):
        it = sub * i + t
        qms = _head_masked(q_ref[t * tile:(t + 1) * tile, :], HEAD_DIM)
        tiles = [jnp.maximum(it - d, 0) for d in range(SB_ALWAYS)]
        terms = [[_sb_terms(qm, rows(k_ref, kb), ones_ut, causal if d == 0 else None) for qm in qms]
                 for d, kb in enumerate(tiles)]
        values = [rows(v_ref, kb) for kb in tiles]
        states = []
        for h in range(len(qms)):
            carry, acc = jnp.zeros((tile, 1), F32), jnp.zeros((tile, width), F32)
            for d in range(SB_ALWAYS):
                carry_in = carry if d == 0 else jnp.where(it >= d, carry, NEG)
                carry, acc = _sb_apply(terms[d][h], values[d], carry_in, acc, causal if d == 0 else None)
            states.append((carry, acc))
        work.append((it, qms, states))
    for t, (it, qms, states) in enumerate(work):
        first = it - SB_ALWAYS
        parts = []
        for qm, (carry, acc) in zip(qms, states):
            state = lax.while_loop(
                lambda st, first=first: (st[0] <= first) & _sb_alive(st[1]),
                lambda st, qm=qm, first=first: (st[0] + 1,) + _sb_tile(
                    qm, rows(k_ref, first - st[0]), rows(v_ref, first - st[0]), st[1], st[2], ones_ut, None),
                (jnp.int32(0), carry, acc))
            parts.append(state[2])
        o_ref[t * tile:(t + 1) * tile, :] = _select_heads(parts, HEAD_DIM).astype(BF16)


def _band_sb_attention(qa, ka, va, bias, q, k, v, tile):
    n, w = q.shape
    rows = BAND_SB_SUB * tile
    assert BAND_SB_SUB >= 2 and n % rows == 0
    blk = lambda back: pl.BlockSpec((rows, w), lambda i: (jnp.maximum(i - back, 0), 0))
    out = jax.ShapeDtypeStruct((n, w), BF16)
    return pl.pallas_call(
        functools.partial(_band_sb_kernel, tile=tile),
        grid=(n // rows,),
        in_specs=[blk(0), blk(1), blk(0), blk(1), blk(0), _const_spec(bias.shape),
                  blk(0), _const_spec((n, w)), _const_spec((n, w))],
        out_specs=[blk(0), blk(0)],
        out_shape=[out, out],
        compiler_params=_params(1),
        name="band_sb_attn",
    )(qa, ka, ka, va, va, bias, q, k, v)


def _mla_kernel(qt_ref, k_ref, vt_ref, kn_ref, ot_ref, qx_scr, p_scr, acc_scr, l_scr):
    i = pl.program_id(1)
    heads = range(2)
    tk, tq = p_scr.shape[2:]
    per_q = tq // tk
    plain = per_q * i
    key = lax.broadcasted_iota(jnp.int32, (tk, tq), 0)
    qry = lax.broadcasted_iota(jnp.int32, (tk, tq), 1)
    row = lax.broadcasted_iota(jnp.int32, (C_PAD - C_ONE, tq), 0)

    def visible(d):
        return ((d * tk + key) // CHUNK) <= (qry // CHUNK)

    def load_q(shifted):
        k_bound = jnp.max(kn_ref[...], axis=0)
        for e in heads:
            lo = e * C_PAD
            qx_scr[lo:lo + C_ONE, :] = qt_ref[lo:lo + C_ONE, :]
            if shifted:
                bound = qt_ref[lo + C_ONE:lo + C_ONE + 1, :].astype(F32) * (k_bound[:, lo:lo + 1] * NORM_MARGIN)
                tail = jnp.where(row == 0, -bound, 0.0)
            else:
                tail = jnp.zeros(row.shape, F32)
            qx_scr[lo + C_ONE:lo + C_PAD, :] = tail.astype(BF16)

    def scores(kb, e):
        return _dot(k_ref[kb, :, e * C_PAD:(e + 1) * C_PAD], qx_scr[e * C_PAD:(e + 1) * C_PAD, :])

    def values(kb, e, p):
        return _dot(vt_ref[kb, e * C_V:(e + 1) * C_V, :], p)

    def write_out():
        ot_ref[...] = jnp.concatenate([acc_scr[e] / l_scr[e] for e in heads], axis=0).astype(BF16)

    def reset():
        acc_scr[...] = jnp.zeros(acc_scr.shape, F32)
        l_scr[...] = jnp.zeros(l_scr.shape, F32)

    def fast_stage(kb, slot, masked_tile=None):
        lo = 0 if masked_tile is None else masked_tile * tk
        for e in heads:
            s = _dot(k_ref[kb, :, e * C_PAD:(e + 1) * C_PAD], qx_scr[e * C_PAD:(e + 1) * C_PAD, lo:])
            if masked_tile is not None:
                s = jnp.where(visible(0)[:, :tq - lo], s, NEG)
            p = jnp.exp2(s)
            l_scr[e, :, lo:] += jnp.sum(p, axis=0, keepdims=True)
            p_scr[slot, e, :, lo:] = p.astype(BF16)
        for e in heads:
            acc_scr[e, :, lo:] += values(kb, e, p_scr[slot, e, :, lo:])

    load_q(True)
    reset()

    def fast_group(g, _):
        for j in range(MLA_UNROLL):
            fast_stage(MLA_UNROLL * g + j, j)
        return 0
    groups = plain // MLA_UNROLL
    lax.fori_loop(0, groups, fast_group, 0)

    def fast_single(kb, _):
        fast_stage(kb, 0)
        return 0
    lax.fori_loop(MLA_UNROLL * groups, plain, fast_single, 0)
    for d in range(per_q):
        fast_stage(plain + d, (1 + d) % MLA_UNROLL, masked_tile=d)
    denom = jnp.minimum(jnp.min(l_scr[0]), jnp.min(l_scr[1]))
    trusted = denom >= MLA_MIN_DENOM

    @pl.when(trusted)
    def _():
        write_out()

    @pl.when(jnp.logical_not(trusted))
    def _():
        load_q(False)
        reset()

        def step(kb, m):
            out = []
            for e in heads:
                s = jnp.where(visible(kb - plain), scores(kb, e), NEG)
                m_new = jnp.maximum(m[e], jnp.max(s, axis=0, keepdims=True))
                alpha = jnp.exp2(m[e] - m_new)
                p = jnp.exp2(s - m_new)
                l_scr[e] = alpha * l_scr[e] + jnp.sum(p, axis=0, keepdims=True)
                acc_scr[e] = alpha * acc_scr[e] + values(kb, e, p.astype(BF16))
                out.append(m_new)
            return tuple(out)
        lax.fori_loop(0, plain + per_q, step, tuple(jnp.full((1, tq), NEG, F32) for _ in heads))
        write_out()


def _mla_attention(qt, k, vt3, kn, tq):
    n = k.shape[0]
    pairs = H_C // 2
    n_tiles, _, tk = vt3.shape
    assert n_tiles * tk == n and kn.shape == (n_tiles, 1, H_C * C_PAD)
    assert tq % tk == 0 and n % tq == 0
    k3 = k.reshape(n_tiles, tk, H_C * C_PAD)
    return pl.pallas_call(
        _mla_kernel,
        grid=(pairs, n // tq),
        in_specs=[pl.BlockSpec((2 * C_PAD, tq), lambda p, i: (p, i)),
                  pl.BlockSpec((n_tiles, tk, 2 * C_PAD), lambda p, i: (0, 0, p)),
                  pl.BlockSpec((n_tiles, 2 * C_V, tk), lambda p, i: (0, p, 0)),
                  pl.BlockSpec((n_tiles, 1, 2 * C_PAD), lambda p, i: (0, 0, p))],
        out_specs=pl.BlockSpec((2 * C_V, tq), lambda p, i: (p, i)),
        out_shape=jax.ShapeDtypeStruct((H_C * C_V, n), BF16),
        scratch_shapes=[pltpu.VMEM((2 * C_PAD, tq), BF16), pltpu.VMEM((MLA_UNROLL, 2, tk, tq), BF16),
                        pltpu.VMEM((2, C_V, tq), F32), pltpu.VMEM((2, 1, tq), F32)],
        compiler_params=_params(2),
        name="mla_attn",
    )(qt, k3, vt3, kn)


def _sample_attn_kernel(qa_ref, ka_ref, va_ref, cak_ref, cav_ref, biasc_ref, biasn_ref,
                        qb_ref, kb_ref, vb_ref, cbk_ref, cbv_ref,
                        qc_ref, ckvn_ref, krbn_ref, ckv_ref, ckr_ref, wukt_ref, wuvp_ref,
                        ya_ref, yb_ref, yc_ref, s_scr, *, tile):
    ds = qa_ref.shape[1]
    past = cbk_ref.shape[2]

    ka, va, cak, cav = ka_ref[0], va_ref[0], cak_ref[0, 0].astype(BF16), cav_ref[0, 0].astype(BF16)
    parts = []
    for h, qm in enumerate(_head_masked(qa_ref[0], HEAD_DIM)):
        s_c = _dot_nt(qm, cak) + biasc_ref[h]
        s_n = _dot_nt(qm, ka) + biasn_ref[h]
        m = jnp.maximum(jnp.max(s_c, axis=1, keepdims=True), jnp.max(s_n, axis=1, keepdims=True))
        p_c = jnp.exp2(s_c - m)
        p_n = jnp.exp2(s_n - m)
        l = jnp.sum(p_c, axis=1, keepdims=True) + jnp.sum(p_n, axis=1, keepdims=True)
        parts.append((_dot(p_c.astype(BF16), cav) + _dot(p_n.astype(BF16), va)) / l)
    ya_ref[0] = _select_heads(parts, HEAD_DIM).astype(BF16)

    row = lax.broadcasted_iota(jnp.int32, (ds, ds), 0)
    col = lax.broadcasted_iota(jnp.int32, (ds, ds), 1)
    causal = col < row
    ones_new = _strict_upper_ones(ds)
    ones_ut = _strict_upper_ones(tile)
    kb, vb = kb_ref[0], vb_ref[0]
    n_tiles = past // tile

    def cache_rows(ref, j):
        start = pl.multiple_of((n_tiles - 1 - j) * tile, tile)
        return ref[0, 0, pl.ds(start, tile), :].astype(BF16)

    b_heads = _head_masked(qb_ref[0], HEAD_DIM)
    b_states = []
    for qm in b_heads:
        state = _sb_tile(qm, kb, vb, jnp.zeros((ds, 1), F32), jnp.zeros((ds, W_AB), F32),
                         ones_new, causal)
        b_states.append(_sb_tile(qm, cache_rows(cbk_ref, 0), cache_rows(cbv_ref, 0), state[0], state[1],
                                 ones_ut, None))

    kv_rank = ckv_ref.shape[3]
    qc = qc_ref[0]
    rope_lanes = lax.broadcasted_iota(jnp.int32, (ds, C_PAD), 1) >= C_NOPE
    q_rows = []
    for h in range(H_C):
        qh = qc[:, h * C_PAD:(h + 1) * C_PAD]
        q_lat = _dot(qh, wukt_ref[h]).astype(BF16)
        q_rows.append(jnp.concatenate([q_lat, jnp.where(rope_lanes, qh, jnp.zeros_like(qh))], axis=1))
    q_all = jnp.concatenate(q_rows, axis=0)

    place = (lax.broadcasted_iota(jnp.int32, (C_ROPE, C_PAD), 1)
             == lax.broadcasted_iota(jnp.int32, (C_ROPE, C_PAD), 0) + C_NOPE).astype(BF16)
    tile_c = SAMPLE_LATENT_TILE if past % SAMPLE_LATENT_TILE == 0 else tile
    cache_tiles = range(past // tile_c)
    latent = lambda j: ckv_ref[0, 0, j * tile_c:(j + 1) * tile_c, :].astype(BF16)

    lat_new = ckvn_ref[0].astype(BF16)
    s_new = _dot_nt(q_all, jnp.concatenate([lat_new, krbn_ref[0].astype(BF16)], axis=1))
    m = jnp.max(s_new, axis=1, keepdims=True)
    for j in cache_tiles:
        k_rope = _dot(ckr_ref[0, 0, j * tile_c:(j + 1) * tile_c, :].astype(BF16), place).astype(BF16)
        s = _dot_nt(q_all, jnp.concatenate([latent(j), k_rope], axis=1))
        s_scr[:, j * tile_c:(j + 1) * tile_c] = s
        m = jnp.maximum(m, jnp.max(s, axis=1, keepdims=True))
    p = jnp.exp2(s_new - m)
    denom = jnp.sum(p, axis=1, keepdims=True)
    o_lat = _dot(p.astype(BF16), lat_new)
    for j in cache_tiles:
        p = jnp.exp2(s_scr[:, j * tile_c:(j + 1) * tile_c] - m)
        denom = denom + jnp.sum(p, axis=1, keepdims=True)
        o_lat = o_lat + _dot(p.astype(BF16), latent(j))
    o_lat = (o_lat / denom).astype(BF16)
    out = jnp.zeros((ds, H_C * C_V), F32)
    for h in range(H_C):
        out = out + _dot(o_lat[h * ds:(h + 1) * ds], wuvp_ref[h])
    yc_ref[0] = out.astype(BF16)

    parts = []
    for qm, state in zip(b_heads, b_states):
        state = lax.while_loop(
            lambda st: (st[0] < n_tiles) & _sb_alive(st[1]),
            lambda st, qm=qm: (st[0] + 1,) + _sb_tile(qm, cache_rows(cbk_ref, st[0]), cache_rows(cbv_ref, st[0]),
                                                      st[1], st[2], ones_ut, None),
            (jnp.int32(1),) + state)
        parts.append(state[2])
    yb_ref[0] = _select_heads(parts, HEAD_DIM).astype(BF16)


def _sample_attention(new, caches, l, bias_c, bias_n, w_ukt, w_uvp, tile):
    qa, ka, va, qb, kb, vb, qc, ckvn, krbn = new
    cak, cav, cbk, cbv, ckv, ckr = caches
    nb, ds, _ = qa.shape
    past = ckv.shape[2]
    req = lambda a: pl.BlockSpec((1,) + a.shape[1:], lambda b: (b, 0, 0))
    cache = lambda a: pl.BlockSpec((1, 1) + a.shape[2:], lambda b: (l, b, 0, 0))
    ins = [qa, ka, va, cak, cav, bias_c, bias_n, qb, kb, vb, cbk, cbv, qc, ckvn, krbn, ckv, ckr,
           w_ukt, w_uvp]
    specs = [req(a) for a in ins]
    for idx in (3, 4, 10, 11, 15, 16):
        specs[idx] = cache(ins[idx])
    for idx in (5, 6, 17, 18):
        specs[idx] = _const_spec(ins[idx].shape)
    out_shape = [jax.ShapeDtypeStruct((nb, ds, W_AB), BF16), jax.ShapeDtypeStruct((nb, ds, W_AB), BF16),
                 jax.ShapeDtypeStruct((nb, ds, H_C * C_V), BF16)]
    return pl.pallas_call(
        functools.partial(_sample_attn_kernel, tile=tile),
        grid=(nb,),
        in_specs=specs,
        out_specs=[req(s) for s in out_shape],
        out_shape=out_shape,
        scratch_shapes=[pltpu.VMEM((H_C * ds, past), F32)],
        compiler_params=_params(1),
        name="sample_attn",
    )(*ins)


def _rope_tables(pos):
    half = C_ROPE // 2
    freq = ROPE_THETA ** (-jnp.arange(half, dtype=F32) / half)
    ang = pos.astype(F32)[:, None] * freq[None, :]
    cos, sin = jnp.cos(ang), jnp.sin(ang)
    n = pos.shape[0]
    z = lambda w: jnp.zeros((n, w), F32)
    tc = jnp.concatenate([jnp.ones((n, C_NOPE), F32), cos, cos, z(C_PAD - C_NOPE - C_ROPE)], axis=1)
    ts1 = jnp.concatenate([z(C_NOPE + half), sin, z(C_PAD - C_NOPE - C_ROPE)], axis=1)
    ts2 = jnp.concatenate([z(C_NOPE), -sin, z(C_PAD - C_NOPE - half)], axis=1)
    return tc, ts1, ts2


def _band_bias(rel_bias, q0, nq, k0, nk):
    period = 1 << int(np.ceil(np.log2(nq + nk)))
    d = np.arange(period)
    d = np.where(d >= nk, d - period, d)
    idx = np.clip((q0 - k0) - d, -REL_CLIP, REL_CLIP) + REL_CLIP
    g = rel_bias.astype(F32)[:, idx]
    toe = jnp.tile(g, (1, nq))[:, :nq * (period - 1)].reshape(-1, nq, period - 1)[:, :, :nk]
    q_pos, k_pos = q0 + np.arange(nq), k0 + np.arange(nk)
    qc, kc = q_pos[:, None] // CHUNK, k_pos[None, :] // CHUNK
    vis = (kc <= qc) & (kc >= qc - BAND_CHUNKS)
    return jnp.where(jnp.asarray(vis)[None], toe * LOG2E, NEG)


def _cast_kernel(w_ref, o_ref):
    o_ref[...] = w_ref[0].astype(BF16)


def _layer_bf16(w, l):
    _, rows, cols = w.shape
    rb = rows // 2 if rows % 32 == 0 else rows
    return pl.pallas_call(
        _cast_kernel,
        grid=(rows // rb,),
        in_specs=[pl.BlockSpec((1, rb, cols), lambda i: (l, i, 0))],
        out_specs=pl.BlockSpec((rb, cols), lambda i: (i, 0)),
        out_shape=jax.ShapeDtypeStruct((rows, cols), BF16),
        compiler_params=_params(1),
        name="cast_bf16",
    )(w)


def _layer_weights(l, w_in, w_uq, w_uk, w_uv):
    d = w_in.shape[1]
    q_rank, kv_rank = w_uq.shape[1], w_uk.shape[1]
    o = 6 * W_AB + q_rank + kv_rank
    w_gates = w_in[l, :, o + C_ROPE:].astype(BF16)
    pad_heads = lambda w: jnp.pad(w, ((0, 0), (0, 0), (0, C_PAD - w.shape[2]))).reshape(w.shape[0], -1)
    w_uq_p = pad_heads(w_uq[l]).astype(BF16)
    w_ukv = jnp.concatenate([pad_heads(w_uk[l]), w_uv[l].reshape(kv_rank, -1)], axis=1).astype(BF16)
    w_ukt = jnp.pad(jnp.transpose(w_uk[l], (1, 2, 0)), ((0, 0), (0, C_PAD - C_NOPE), (0, 0))).astype(BF16)
    w_uvp = jnp.einsum('rhd,hg->hrgd', w_uv[l], jnp.eye(H_C, dtype=w_uv.dtype)).reshape(
        H_C, kv_rank, H_C * C_V).astype(BF16)
    return w_gates, w_uq_p, w_ukv, w_ukt, w_uvp


def kernel(x_prompt, x_sample, cache_a_k, cache_a_v, cache_b_k, cache_b_v, cache_c_kv, cache_c_kr, ffn1_norm_pre, ffn1_norm_post, ffn1_w_gate, ffn1_w_up, ffn1_w_down, mix_norm_pre, mix_norm_post, w_in, cq_norm, ckv_norm, w_uq, w_uk, w_uv, rel_bias_a, w_br_a, w_br_b, w_br_c, w_out, ffn2_norm_pre, ffn2_norm_post, ffn2_w_gate, ffn2_w_up, ffn2_w_down):
    batch, seq, d = x_prompt.shape
    nb, ds, _ = x_sample.shape
    depth = w_in.shape[0]
    past = cache_b_k.shape[2]
    win_cache = cache_a_k.shape[2]
    assert batch == 1, "prompt group is a single sequence"
    tile = 256
    tm = 512 if seq % 512 == 0 else tile
    assert seq % tile == 0 and past % tile == 0 and tile % CHUNK == 0 and 2 * tile >= WIN_A
    assert past % CHUNK + ds <= CHUNK
    ns = nb * ds

    xp = x_prompt.reshape(seq, d)
    xs = x_sample.reshape(ns, d)
    tabs_p = _rope_tables(jnp.arange(seq))
    tabs_s = _rope_tables(jnp.tile(past + jnp.arange(ds), nb))

    row = lambda g, l: g[l][None, :]
    base = 2 * tile

    merged_heads = lambda c: c.reshape(c.shape[:3] + (-1,))
    caches = (merged_heads(cache_a_k), merged_heads(cache_a_v), merged_heads(cache_b_k),
              merged_heads(cache_b_v), cache_c_kv, cache_c_kr)
    w_in_bf16 = w_in.astype(BF16)
    states_p, states_s = [], []
    for l in range(depth):
        w_gates, w_uq_p, w_ukv, w_ukt, w_uvp = _layer_weights(l, w_in, w_uq, w_uk, w_uv)
        ffn1 = (row(ffn1_norm_pre, l), row(ffn1_norm_post, l), _layer_bf16(ffn1_w_gate, l),
                _layer_bf16(ffn1_w_up, l), _layer_bf16(ffn1_w_down, l))
        ffn2 = (row(ffn2_norm_pre, l), row(ffn2_norm_post, l), _layer_bf16(ffn2_w_gate, l),
                _layer_bf16(ffn2_w_up, l), _layer_bf16(ffn2_w_down, l))
        mix_in = (row(mix_norm_pre, l), w_in_bf16, l, row(cq_norm, l), row(ckv_norm, l), w_uq_p, w_ukv)
        mrg = (row(mix_norm_pre, l), row(mix_norm_post, l), w_gates, l, w_br_a, w_br_b, w_br_c, w_out)
        bias_p = _band_bias(rel_bias_a[l], base, tile, base - 2 * tile, 3 * tile)
        bias_c = _band_bias(rel_bias_a[l], past, ds, past - win_cache, win_cache)
        bias_n = _band_bias(rel_bias_a[l], past, ds, past, ds)

        xp = _ffn(xp, *ffn1, tm)
        (qa, kab, vab, qb, kbb, vbb, ka, va, kb, vb, ckv, krb, qc, kc, vc, kn) = _mixer_in(
            xp, *mix_in, tabs_p, tm, q_transposed=True)
        ya, yb = _band_sb_attention(qa, kab, vab, bias_p, qb, kbb, vbb, tile)
        yc = _mla_attention(qc, kc, vc, kn, 2 * tm if seq % (2 * tm) == 0 else tm)
        xp = _merge(xp, ya, yb, yc, *mrg, tm, yc_transposed=True)
        xp = _ffn(xp, *ffn2, tm)
        win = min(WIN_A, seq)
        heads = lambda t, h: t.reshape(1, t.shape[0], h, t.shape[1] // h)
        states_p.append((heads(ka[-win:], H_A), heads(va[-win:], H_A), heads(kb, H_B), heads(vb, H_B),
                         ckv[None], krb[None, :, C_NOPE:C_NOPE + C_ROPE]))

        xs = _ffn(xs, *ffn1, ns)
        (qa, kab, vab, qb, kbb, vbb, ka, va, kb, vb, ckv, krb, qc, kc, vc, _) = _mixer_in(
            xs, *mix_in, tabs_s, ns)
        per_req = lambda t: t.reshape(nb, ds, t.shape[1])
        new = tuple(per_req(t) for t in (qa, kab, vab, qb, kbb, vbb, qc, ckv, krb))
        ya, yb, yc = _sample_attention(new, caches, l, bias_c, bias_n, w_ukt, w_uvp, tile)
        xs = _merge(xs, ya.reshape(ns, -1), yb.reshape(ns, -1), yc.reshape(ns, -1), *mrg, ns)
        xs = _ffn(xs, *ffn2, ns)
        heads_s = lambda t, h: t.reshape(nb, ds, h, t.shape[1] // h)
        states_s.append((heads_s(ka, H_A), heads_s(va, H_A), heads_s(kb, H_B), heads_s(vb, H_B),
                         ckv.reshape(nb, ds, -1), krb[:, C_NOPE:C_NOPE + C_ROPE].reshape(nb, ds, C_ROPE)))

    stack = lambda states, i: jnp.stack([s[i] for s in states], axis=0)
    return ((xp.reshape(batch, seq, d), xs.reshape(nb, ds, d))
            + tuple(stack(states_p, i) for i in range(6))
            + tuple(stack(states_s, i) for i in range(6)))
```

```python
import functools

import numpy as np
import jax
import jax.numpy as jnp
from jax import lax
from jax.experimental import pallas as pl
from jax.experimental.pallas import tpu as pltpu

F32 = jnp.float32
BF16 = jnp.bfloat16

CHUNK = 64
BAND_CHUNKS = 8
WIN_A = BAND_CHUNKS * CHUNK
HEAD_DIM = 64
H_A = 4
H_B = 4
H_C = 8
C_NOPE = 64
C_ROPE = 32
C_V = 64
REL_CLIP = 128
ROPE_THETA = 10000.0
EPS = 1e-6
N_BRANCH = 3
W_AB = H_A * HEAD_DIM
C_PAD = 128
ATTN_SCALE = HEAD_DIM ** -0.5
MLA_SCALE = (C_NOPE + C_ROPE) ** -0.5
LOG2E = float(np.log2(np.e))
C_ONE = C_NOPE + C_ROPE
NORM_MARGIN = 1.02
MLA_MIN_DENOM = 2.0 ** -60
MLA_UNROLL = 4
SAMPLE_LATENT_TILE = 4096
NEG = -1e30
BAND_SB_SUB = 4
SB_ALWAYS = 2
SB_DEAD = -150.0

LANES = 128
MXU_WIDTH = 256
VMEM_LIMIT = 52 * 1024 * 1024

NT_DIMS = (((1,), (1,)), ((), ()))
TN_DIMS = (((0,), (0,)), ((), ()))


def _rms(x, g):
    return x * lax.rsqrt(jnp.mean(x * x, axis=-1, keepdims=True) + EPS) * g


def _dot(a, b):
    return jnp.dot(a, b, preferred_element_type=F32)


def _dot_nt(a, b):
    return lax.dot_general(a, b, NT_DIMS, preferred_element_type=F32)


def _dot_tn(a, b):
    return lax.dot_general(a, b, TN_DIMS, preferred_element_type=F32)


def _const_spec(shape):
    zeros = (0,) * len(shape)
    return pl.BlockSpec(shape, lambda *_: zeros, pipeline_mode=pl.Buffered(1))


def _row_spec(tm, width):
    return pl.BlockSpec((tm, width), lambda i: (i, 0))


def _params(n_grid):
    return pltpu.CompilerParams(dimension_semantics=("arbitrary",) * n_grid,
                                vmem_limit_bytes=VMEM_LIMIT)


def _ffn_kernel(x_ref, gpre_ref, gpost_ref, wg_ref, wu_ref, wd_ref, o_ref, *, f_cuts):
    x = x_ref[...]
    h = _rms(x, gpre_ref[...]).astype(BF16)
    y = jnp.zeros(x.shape, F32)
    for lo, hi in zip(f_cuts[:-1], f_cuts[1:]):
        sl = slice(lo, hi)
        g = _dot(h, wg_ref[:, sl])
        u = _dot(h, wu_ref[:, sl])
        a = (g * jax.nn.sigmoid(g) * u).astype(BF16)
        y = y + _dot(a, wd_ref[sl, :])
    o_ref[...] = x + 0.5 * _rms(y, gpost_ref[...])


def _ffn(x, g_pre, g_post, w_gate, w_up, w_down, tm):
    n, d = x.shape
    d_ff = w_gate.shape[1]
    cut = (d_ff // 2) // MXU_WIDTH * MXU_WIDTH
    f_cuts = (0, cut, d_ff) if 0 < cut < d_ff else (0, d_ff)
    return pl.pallas_call(
        functools.partial(_ffn_kernel, f_cuts=f_cuts),
        grid=(n // tm,),
        in_specs=[_row_spec(tm, d), _const_spec((1, d)), _const_spec((1, d)),
                  _const_spec((d, d_ff)), _const_spec((d, d_ff)), _const_spec((d_ff, d))],
        out_specs=_row_spec(tm, d),
        out_shape=jax.ShapeDtypeStruct((n, d), F32),
        compiler_params=_params(1),
        name="ffn",
    )(x, g_pre, g_post, w_gate, w_up, w_down)


def _rope(blk, tc, ts1, ts2):
    return blk * tc + pltpu.roll(blk, 16, 1) * ts1 + pltpu.roll(blk, C_PAD - 16, 1) * ts2


def _mixer_in_kernel(x_ref, gpre_ref, win_ref, gcq_ref, gckv_ref, wuq_ref, wukv_ref,
                     tc_ref, ts1_ref, ts2_ref,
                     qa_ref, kab_ref, vab_ref, qb_ref, kbb_ref, vbb_ref,
                     ka_ref, va_ref, kb_ref, vb_ref, ckv_ref, krb_ref,
                     qc_ref, kc_ref, vc_ref, kn_ref, *, q_transposed):
    w = W_AB
    q_rank = gcq_ref.shape[1]
    kv_rank = gckv_ref.shape[1]
    u = _rms(x_ref[...], gpre_ref[...]).astype(BF16)
    o = 6 * w
    lat = _dot(u, win_ref[0, :, o:])
    cq = lat[:, :q_rank]
    ckv = lat[:, q_rank:q_rank + kv_rank]
    kr_first = lat[:, q_rank + kv_rank:q_rank + kv_rank + C_PAD]
    lane = lax.broadcasted_iota(jnp.int32, kr_first.shape, 1)
    krb = jnp.where((lane >= C_NOPE) & (lane < C_NOPE + C_ROPE), pltpu.roll(kr_first, C_NOPE, 1), 0.0)
    proj = _dot(u, win_ref[0, :, :o])
    qa, ka, va = proj[:, 0:w], proj[:, w:2 * w], proj[:, 2 * w:3 * w]
    qb, kb, vb = proj[:, 3 * w:4 * w], proj[:, 4 * w:5 * w], proj[:, 5 * w:6 * w]
    qa_ref[...] = (qa * (ATTN_SCALE * LOG2E)).astype(BF16)
    qb_ref[...] = (qb * (ATTN_SCALE * LOG2E)).astype(BF16)
    ka_ref[...] = ka
    va_ref[...] = va
    kb_ref[...] = kb
    vb_ref[...] = vb
    kab_ref[...] = ka.astype(BF16)
    vab_ref[...] = va.astype(BF16)
    kbb_ref[...] = kb.astype(BF16)
    vbb_ref[...] = vb.astype(BF16)
    tc, ts1, ts2 = tc_ref[...], ts1_ref[...], ts2_ref[...]
    cqn = _rms(cq, gcq_ref[...]).astype(BF16)
    qall = _dot(cqn, wuq_ref[...])
    ckvn = _rms(ckv, gckv_ref[...])
    ckv_ref[...] = ckvn
    kv = _dot(ckvn.astype(BF16), wukv_ref[...])
    krot = _rope(krb, tc, ts1, ts2)
    krb_ref[...] = krot
    one_lane = (lax.broadcasted_iota(jnp.int32, (1, C_PAD), 1) == C_ONE).astype(F32)
    for h in range(H_C):
        sl = slice(h * C_PAD, (h + 1) * C_PAD)
        qh = _rope(qall[:, sl], tc, ts1, ts2) * (MLA_SCALE * LOG2E)
        if q_transposed:
            qt = qh.T
            norm = jnp.sqrt(jnp.sum(qt * qt, axis=0, keepdims=True)) * NORM_MARGIN
            row = lax.broadcasted_iota(jnp.int32, qt.shape, 0)
            qc_ref[sl, :] = jnp.where(row == C_ONE, norm, qt).astype(BF16)
        else:
            qc_ref[:, sl] = qh.astype(BF16)
        kh = kv[:, sl] + krot
        kc_ref[:, sl] = (kh + one_lane).astype(BF16)
        k_sq = _dot((kh * kh).astype(BF16), jnp.ones((C_PAD, C_PAD), BF16))
        kn_ref[0, :, sl] = jnp.sqrt(jnp.max(k_sq, axis=0, keepdims=True)) * NORM_MARGIN
    if not q_transposed:
        vc_ref[...] = kv[:, H_C * C_PAD:].astype(BF16)
    else:
        for p in range(H_C // 2):
            o = H_C * C_PAD + p * 2 * C_V
            vc_ref[0, p * 2 * C_V:(p + 1) * 2 * C_V, :] = kv[:, o:o + 2 * C_V].T.astype(BF16)


def _mixer_in(x, g_pre, w_in, l, g_cq, g_ckv, w_uq, w_ukv, tabs, tm, q_transposed=False):
    n, d = x.shape
    q_rank, kv_rank = g_cq.shape[1], g_ckv.shape[1]
    wc = H_C * C_PAD
    wv = H_C * C_V
    bf = lambda width: jax.ShapeDtypeStruct((n, width), BF16)
    f32 = lambda width: jax.ShapeDtypeStruct((n, width), F32)
    q_shape = jax.ShapeDtypeStruct((wc, n), BF16) if q_transposed else bf(wc)
    q_spec = pl.BlockSpec((wc, tm), lambda i: (0, i)) if q_transposed else _row_spec(tm, wc)
    v_shape = jax.ShapeDtypeStruct((n // tm, wv, tm), BF16) if q_transposed else bf(wv)
    v_spec = (pl.BlockSpec((1, wv, tm), lambda i: (i, 0, 0)) if q_transposed
              else _row_spec(tm, wv))
    out_shape = ([bf(W_AB)] * 6 + [f32(W_AB)] * 4 + [f32(kv_rank), f32(C_PAD)]
                 + [q_shape, bf(wc), v_shape, jax.ShapeDtypeStruct((n // tm, 1, wc), F32)])
    out_specs = ([_row_spec(tm, W_AB)] * 10 + [_row_spec(tm, kv_rank), _row_spec(tm, C_PAD)]
                 + [q_spec, _row_spec(tm, wc), v_spec, pl.BlockSpec((1, 1, wc), lambda i: (i, 0, 0))])
    return pl.pallas_call(
        functools.partial(_mixer_in_kernel, q_transposed=q_transposed),
        grid=(n // tm,),
        in_specs=[_row_spec(tm, d), _const_spec((1, d)),
                  pl.BlockSpec((1, d, 6 * W_AB + q_rank + kv_rank + C_PAD), lambda i: (l, 0, 0),
                               pipeline_mode=pl.Buffered(1)),
                  _const_spec((1, q_rank)), _const_spec((1, kv_rank)),
                  _const_spec(w_uq.shape), _const_spec(w_ukv.shape),
                  _row_spec(tm, C_PAD), _row_spec(tm, C_PAD), _row_spec(tm, C_PAD)],
        out_specs=out_specs,
        out_shape=out_shape,
        compiler_params=_params(1),
        name="mixer_in",
    )(x, g_pre, w_in, g_cq, g_ckv, w_uq, w_ukv, *tabs)


def _merge_kernel(x_ref, ya_ref, yb_ref, yc_ref, gpre_ref, gpost_ref, wgates_ref,
                  wa32_ref, wb32_ref, wc32_ref, wout32_ref, o_ref, wa_ref, wb_ref, wc_ref, wout_ref,
                  *, yc_transposed):
    @pl.when(pl.program_id(0) == 0)
    def _():
        for src, dst in ((wa32_ref, wa_ref), (wb32_ref, wb_ref), (wc32_ref, wc_ref), (wout32_ref, wout_ref)):
            dst[...] = src[0].astype(BF16)

    x = x_ref[...]
    d = x.shape[1]
    u = _rms(x, gpre_ref[...]).astype(BF16)
    m = jnp.zeros(x.shape, F32)
    for i, (y_ref, w_ref) in enumerate(((ya_ref, wa_ref), (yb_ref, wb_ref), (yc_ref, wc_ref))):
        gate = jax.nn.sigmoid(_dot(u, wgates_ref[:, i * d:(i + 1) * d]))
        branch_dot = _dot_tn if (yc_transposed and y_ref is yc_ref) else _dot
        m = m + gate * branch_dot(y_ref[...], w_ref[...])
    mixed = _dot(m.astype(BF16), wout_ref[...])
    o_ref[...] = x + _rms(mixed, gpost_ref[...])


def _merge(x, ya, yb, yc, g_pre, g_post, w_gates, l, w_a, w_b, w_c, w_out, tm, yc_transposed=False):
    n, d = x.shape
    yc_spec = (pl.BlockSpec((yc.shape[0], tm), lambda i: (0, i)) if yc_transposed
               else _row_spec(tm, yc.shape[1]))
    layer = lambda w: pl.BlockSpec((1,) + w.shape[1:], lambda i: (l, 0, 0), pipeline_mode=pl.Buffered(1))
    return pl.pallas_call(
        functools.partial(_merge_kernel, yc_transposed=yc_transposed),
        grid=(n // tm,),
        in_specs=[_row_spec(tm, d), _row_spec(tm, ya.shape[1]), _row_spec(tm, yb.shape[1]),
                  yc_spec, _const_spec((1, d)), _const_spec((1, d)),
                  _const_spec(w_gates.shape), layer(w_a), layer(w_b), layer(w_c), layer(w_out)],
        out_specs=_row_spec(tm, d),
        out_shape=jax.ShapeDtypeStruct((n, d), F32),
        scratch_shapes=[pltpu.VMEM(w.shape[1:], BF16) for w in (w_a, w_b, w_c, w_out)],
        compiler_params=_params(1),
        name="merge",
    )(x, ya, yb, yc, g_pre, g_post, w_gates, w_a, w_b, w_c, w_out)


def _head_masked(q, width):
    lane_head = lax.broadcasted_iota(jnp.int32, q.shape, 1) // width
    zero = jnp.zeros_like(q)
    return [jnp.where(lane_head == h, q, zero) for h in range(q.shape[1] // width)]


def _strict_upper_ones(n):
    j = lax.broadcasted_iota(jnp.int32, (n, n), 0)
    s = lax.broadcasted_iota(jnp.int32, (n, n), 1)
    return (j > s).astype(BF16)


def _sb_terms(qm, kt, ones_ut, causal):
    z = _dot_nt(qm, kt)
    t = jnp.log2(1.0 + jnp.exp2(-jnp.abs(z)))
    log_keep = -jnp.maximum(z, 0.0) - t
    log_beta = jnp.minimum(z, 0.0) - t
    if causal is not None:
        log_keep = jnp.where(causal, log_keep, 0.0)
    hi = log_keep.astype(BF16)
    lo = (log_keep - hi.astype(F32)).astype(BF16)
    tail = _dot(hi, ones_ut) + _dot(lo, ones_ut)
    return log_beta + tail, jnp.sum(log_keep, axis=1, keepdims=True)


def _sb_apply(terms, vt, carry, acc, causal):
    logw, total = terms
    w = jnp.exp2(logw + carry)
    if causal is not None:
        w = jnp.where(causal, w, 0.0)
    return carry + total, acc + _dot(w.astype(BF16), vt)


def _sb_tile(qm, kt, vt, carry, acc, ones_ut, causal):
    return _sb_apply(_sb_terms(qm, kt, ones_ut, causal), vt, carry, acc, causal)


def _sb_alive(carry):
    return jnp.max(carry) > SB_DEAD


def _select_heads(parts, width):
    lane_head = lax.broadcasted_iota(jnp.int32, parts[0].shape, 1) // width
    out = parts[0]
    for h in range(1, len(parts)):
        out = jnp.where(lane_head == h, parts[h], out)
    return out


def _band_sb_kernel(qa_ref, ka0_ref, ka1_ref, va0_ref, va1_ref, bias_ref, q_ref, k_ref, v_ref,
                    oa_ref, o_ref, *, tile):
    i = pl.program_id(0)
    sub = BAND_SB_SUB
    k_band = jnp.concatenate([ka0_ref[...], ka1_ref[...]], axis=0)
    v_band = jnp.concatenate([va0_ref[...], va1_ref[...]], axis=0)
    col = lax.broadcasted_iota(jnp.int32, (tile, 3 * tile), 1)
    for t in range(sub):
        kwin = k_band[(sub - 2 + t) * tile:(sub + 1 + t) * tile]
        vwin = v_band[(sub - 2 + t) * tile:(sub + 1 + t) * tile]
        in_window = col >= (2 - (sub * i + t)) * tile
        band = []
        for h, qm in enumerate(_head_masked(qa_ref[t * tile:(t + 1) * tile, :], HEAD_DIM)):
            s = jnp.where(in_window, _dot_nt(qm, kwin) + bias_ref[h], NEG)
            m = jnp.max(s, axis=1, keepdims=True)
            p = jnp.exp2(s - m)
            l = jnp.sum(p, axis=1, keepdims=True)
            band.append(_dot(p.astype(BF16), vwin) / l)
        oa_ref[t * tile:(t + 1) * tile, :] = _select_heads(band, HEAD_DIM).astype(BF16)

    row = lax.broadcasted_iota(jnp.int32, (tile, tile), 0)
    col = lax.broadcasted_iota(jnp.int32, (tile, tile), 1)
    causal = col < row
    ones_ut = _strict_upper_ones(tile)
    width = q_ref.shape[1]
    rows = lambda ref, kb: ref[pl.ds(pl.multiple_of(kb * tile, tile), tile), :]

    work = []
    for t in range(sub):
        it = sub * i + t
        qms = _head_masked(q_ref[t * tile:(t + 1) * tile, :], HEAD_DIM)
        tiles = [jnp.maximum(it - d, 0) for d in range(SB_ALWAYS)]
        terms = [[_sb_terms(qm, rows(k_ref, kb), ones_ut, causal if d == 0 else None) for qm in qms]
                 for d, kb in enumerate(tiles)]
        values = [rows(v_ref, kb) for kb in tiles]
        states = []
        for h in range(len(qms)):
            carry, acc = jnp.zeros((tile, 1), F32), jnp.zeros((tile, width), F32)
            for d in range(SB_ALWAYS):
                carry_in = carry if d == 0 else jnp.where(it >= d, carry, NEG)
                carry, acc = _sb_apply(terms[d][h], values[d], carry_in, acc, causal if d == 0 else None)
            states.append((carry, acc))
        work.append((it, qms, states))
    for t, (it, qms, states) in enumerate(work):
        first = it - SB_ALWAYS
        parts = []
        for qm, (carry, acc) in zip(qms, states):
            state = lax.while_loop(
                lambda st, first=first: (st[0] <= first) & _sb_alive(st[1]),
                lambda st, qm=qm, first=first: (st[0] + 1,) + _sb_tile(
                    qm, rows(k_ref, first - st[0]), rows(v_ref, first - st[0]), st[1], st[2], ones_ut, None),
                (jnp.int32(0), carry, acc))
            parts.append(state[2])
        o_ref[t * tile:(t + 1) * tile, :] = _select_heads(parts, HEAD_DIM).astype(BF16)


def _band_sb_attention(qa, ka, va, bias, q, k, v, tile):
    n, w = q.shape
    rows = BAND_SB_SUB * tile
    assert BAND_SB_SUB >= 2 and n % rows == 0
    blk = lambda back: pl.BlockSpec((rows, w), lambda i: (jnp.maximum(i - back, 0), 0))
    out = jax.ShapeDtypeStruct((n, w), BF16)
    return pl.pallas_call(
        functools.partial(_band_sb_kernel, tile=tile),
        grid=(n // rows,),
        in_specs=[blk(0), blk(1), blk(0), blk(1), blk(0), _const_spec(bias.shape),
                  blk(0), _const_spec((n, w)), _const_spec((n, w))],
        out_specs=[blk(0), blk(0)],
        out_shape=[out, out],
        compiler_params=_params(1),
        name="band_sb_attn",
    )(qa, ka, ka, va, va, bias, q, k, v)


def _mla_kernel(qt_ref, k_ref, vt_ref, kn_ref, ot_ref, qx_scr, p_scr, acc_scr, l_scr):
    i = pl.program_id(1)
    heads = range(2)
    tk, tq = p_scr.shape[2:]
    per_q = tq // tk
    plain = per_q * i
    key = lax.broadcasted_iota(jnp.int32, (tk, tq), 0)
    qry = lax.broadcasted_iota(jnp.int32, (tk, tq), 1)
    row = lax.broadcasted_iota(jnp.int32, (C_PAD - C_ONE, tq), 0)

    def visible(d):
        return ((d * tk + key) // CHUNK) <= (qry // CHUNK)

    def load_q(shifted):
        k_bound = jnp.max(kn_ref[...], axis=0)
        for e in heads:
            lo = e * C_PAD
            qx_scr[lo:lo + C_ONE, :] = qt_ref[lo:lo + C_ONE, :]
            if shifted:
                bound = qt_ref[lo + C_ONE:lo + C_ONE + 1, :].astype(F32) * (k_bound[:, lo:lo + 1] * NORM_MARGIN)
                tail = jnp.where(row == 0, -bound, 0.0)
            else:
                tail = jnp.zeros(row.shape, F32)
            qx_scr[lo + C_ONE:lo + C_PAD, :] = tail.astype(BF16)

    def scores(kb, e):
        return _dot(k_ref[kb, :, e * C_PAD:(e + 1) * C_PAD], qx_scr[e * C_PAD:(e + 1) * C_PAD, :])

    def values(kb, e, p):
        return _dot(vt_ref[kb, e * C_V:(e + 1) * C_V, :], p)

    def write_out():
        ot_ref[...] = jnp.concatenate([acc_scr[e] / l_scr[e] for e in heads], axis=0).astype(BF16)

    def reset():
        acc_scr[...] = jnp.zeros(acc_scr.shape, F32)
        l_scr[...] = jnp.zeros(l_scr.shape, F32)

    def fast_stage(kb, slot, masked_tile=None):
        lo = 0 if masked_tile is None else masked_tile * tk
        for e in heads:
            s = _dot(k_ref[kb, :, e * C_PAD:(e + 1) * C_PAD], qx_scr[e * C_PAD:(e + 1) * C_PAD, lo:])
            if masked_tile is not None:
                s = jnp.where(visible(0)[:, :tq - lo], s, NEG)
            p = jnp.exp2(s)
            l_scr[e, :, lo:] += jnp.sum(p, axis=0, keepdims=True)
            p_scr[slot, e, :, lo:] = p.astype(BF16)
        for e in heads:
            acc_scr[e, :, lo:] += values(kb, e, p_scr[slot, e, :, lo:])

    load_q(True)
    reset()

    def fast_group(g, _):
        for j in range(MLA_UNROLL):
            fast_stage(MLA_UNROLL * g + j, j)
        return 0
    groups = plain // MLA_UNROLL
    lax.fori_loop(0, groups, fast_group, 0)

    def fast_single(kb, _):
        fast_stage(kb, 0)
        return 0
    lax.fori_loop(MLA_UNROLL * groups, plain, fast_single, 0)
    for d in range(per_q):
        fast_stage(plain + d, (1 + d) % MLA_UNROLL, masked_tile=d)
    denom = jnp.minimum(jnp.min(l_scr[0]), jnp.min(l_scr[1]))
    trusted = denom >= MLA_MIN_DENOM

    @pl.when(trusted)
    def _():
        write_out()

    @pl.when(jnp.logical_not(trusted))
    def _():
        load_q(False)
        reset()

        def step(kb, m):
            out = []
            for e in heads:
                s = jnp.where(visible(kb - plain), scores(kb, e), NEG)
                m_new = jnp.maximum(m[e], jnp.max(s, axis=0, keepdims=True))
                alpha = jnp.exp2(m[e] - m_new)
                p = jnp.exp2(s - m_new)
                l_scr[e] = alpha * l_scr[e] + jnp.sum(p, axis=0, keepdims=True)
                acc_scr[e] = alpha * acc_scr[e] + values(kb, e, p.astype(BF16))
                out.append(m_new)
            return tuple(out)
        lax.fori_loop(0, plain + per_q, step, tuple(jnp.full((1, tq), NEG, F32) for _ in heads))
        write_out()


def _mla_attention(qt, k, vt3, kn, tq):
    n = k.shape[0]
    pairs = H_C // 2
    n_tiles, _, tk = vt3.shape
    assert n_tiles * tk == n and kn.shape == (n_tiles, 1, H_C * C_PAD)
    assert tq % tk == 0 and n % tq == 0
    k3 = k.reshape(n_tiles, tk, H_C * C_PAD)
    return pl.pallas_call(
        _mla_kernel,
        grid=(pairs, n // tq),
        in_specs=[pl.BlockSpec((2 * C_PAD, tq), lambda p, i: (p, i)),
                  pl.BlockSpec((n_tiles, tk, 2 * C_PAD), lambda p, i: (0, 0, p)),
                  pl.BlockSpec((n_tiles, 2 * C_V, tk), lambda p, i: (0, p, 0)),
                  pl.BlockSpec((n_tiles, 1, 2 * C_PAD), lambda p, i: (0, 0, p))],
        out_specs=pl.BlockSpec((2 * C_V, tq), lambda p, i: (p, i)),
        out_shape=jax.ShapeDtypeStruct((H_C * C_V, n), BF16),
        scratch_shapes=[pltpu.VMEM((2 * C_PAD, tq), BF16), pltpu.VMEM((MLA_UNROLL, 2, tk, tq), BF16),
                        pltpu.VMEM((2, C_V, tq), F32), pltpu.VMEM((2, 1, tq), F32)],
        compiler_params=_params(2),
        name="mla_attn",
    )(qt, k3, vt3, kn)


def _sample_attn_kernel(qa_ref, ka_ref, va_ref, cak_ref, cav_ref, biasc_ref, biasn_ref,
                        qb_ref, kb_ref, vb_ref, cbk_ref, cbv_ref,
                        qc_ref, ckvn_ref, krbn_ref, ckv_ref, ckr_ref, wukt_ref, wuvp_ref,
                        ya_ref, yb_ref, yc_ref, s_scr, *, tile):
    ds = qa_ref.shape[1]
    past = cbk_ref.shape[2]

    ka, va, cak, cav = ka_ref[0], va_ref[0], cak_ref[0, 0].astype(BF16), cav_ref[0, 0].astype(BF16)
    parts = []
    for h, qm in enumerate(_head_masked(qa_ref[0], HEAD_DIM)):
        s_c = _dot_nt(qm, cak) + biasc_ref[h]
        s_n = _dot_nt(qm, ka) + biasn_ref[h]
        m = jnp.maximum(jnp.max(s_c, axis=1, keepdims=True), jnp.max(s_n, axis=1, keepdims=True))
        p_c = jnp.exp2(s_c - m)
        p_n = jnp.exp2(s_n - m)
        l = jnp.sum(p_c, axis=1, keepdims=True) + jnp.sum(p_n, axis=1, keepdims=True)
        parts.append((_dot(p_c.astype(BF16), cav) + _dot(p_n.astype(BF16), va)) / l)
    ya_ref[0] = _select_heads(parts, HEAD_DIM).astype(BF16)

    row = lax.broadcasted_iota(jnp.int32, (ds, ds), 0)
    col = lax.broadcasted_iota(jnp.int32, (ds, ds), 1)
    causal = col < row
    ones_new = _strict_upper_ones(ds)
    ones_ut = _strict_upper_ones(tile)
    kb, vb = kb_ref[0], vb_ref[0]
    n_tiles = past // tile

    def cache_rows(ref, j):
        start = pl.multiple_of((n_tiles - 1 - j) * tile, tile)
        return ref[0, 0, pl.ds(start, tile), :].astype(BF16)

    b_heads = _head_masked(qb_ref[0], HEAD_DIM)
    b_states = []
    for qm in b_heads:
        state = _sb_tile(qm, kb, vb, jnp.zeros((ds, 1), F32), jnp.zeros((ds, W_AB), F32),
                         ones_new, causal)
        b_states.append(_sb_tile(qm, cache_rows(cbk_ref, 0), cache_rows(cbv_ref, 0), state[0], state[1],
                                 ones_ut, None))

    kv_rank = ckv_ref.shape[3]
    qc = qc_ref[0]
    rope_lanes = lax.broadcasted_iota(jnp.int32, (ds, C_PAD), 1) >= C_NOPE
    q_rows = []
    for h in range(H_C):
        qh = qc[:, h * C_PAD:(h + 1) * C_PAD]
        q_lat = _dot(qh, wukt_ref[h]).astype(BF16)
        q_rows.append(jnp.concatenate([q_lat, jnp.where(rope_lanes, qh, jnp.zeros_like(qh))], axis=1))
    q_all = jnp.concatenate(q_rows, axis=0)

    place = (lax.broadcasted_iota(jnp.int32, (C_ROPE, C_PAD), 1)
             == lax.broadcasted_iota(jnp.int32, (C_ROPE, C_PAD), 0) + C_NOPE).astype(BF16)
    tile_c = SAMPLE_LATENT_TILE if past % SAMPLE_LATENT_TILE == 0 else tile
    cache_tiles = range(past // tile_c)
    latent = lambda j: ckv_ref[0, 0, j * tile_c:(j + 1) * tile_c, :].astype(BF16)

    lat_new = ckvn_ref[0].astype(BF16)
    s_new = _dot_nt(q_all, jnp.concatenate([lat_new, krbn_ref[0].astype(BF16)], axis=1))
    m = jnp.max(s_new, axis=1, keepdims=True)
    for j in cache_tiles:
        k_rope = _dot(ckr_ref[0, 0, j * tile_c:(j + 1) * tile_c, :].astype(BF16), place).astype(BF16)
        s = _dot_nt(q_all, jnp.concatenate([latent(j), k_rope], axis=1))
        s_scr[:, j * tile_c:(j + 1) * tile_c] = s
        m = jnp.maximum(m, jnp.max(s, axis=1, keepdims=True))
    p = jnp.exp2(s_new - m)
    denom = jnp.sum(p, axis=1, keepdims=True)
    o_lat = _dot(p.astype(BF16), lat_new)
    for j in cache_tiles:
        p = jnp.exp2(s_scr[:, j * tile_c:(j + 1) * tile_c] - m)
        denom = denom + jnp.sum(p, axis=1, keepdims=True)
        o_lat = o_lat + _dot(p.astype(BF16), latent(j))
    o_lat = (o_lat / denom).astype(BF16)
    out = jnp.zeros((ds, H_C * C_V), F32)
    for h in range(H_C):
        out = out + _dot(o_lat[h * ds:(h + 1) * ds], wuvp_ref[h])
    yc_ref[0] = out.astype(BF16)

    parts = []
    for qm, state in zip(b_heads, b_states):
        state = lax.while_loop(
            lambda st: (st[0] < n_tiles) & _sb_alive(st[1]),
            lambda st, qm=qm: (st[0] + 1,) + _sb_tile(qm, cache_rows(cbk_ref, st[0]), cache_rows(cbv_ref, st[0]),
                                                      st[1], st[2], ones_ut, None),
            (jnp.int32(1),) + state)
        parts.append(state[2])
    yb_ref[0] = _select_heads(parts, HEAD_DIM).astype(BF16)


def _sample_attention(new, caches, l, bias_c, bias_n, w_ukt, w_uvp, tile):
    qa, ka, va, qb, kb, vb, qc, ckvn, krbn = new
    cak, cav, cbk, cbv, ckv, ckr = caches
    nb, ds, _ = qa.shape
    past = ckv.shape[2]
    req = lambda a: pl.BlockSpec((1,) + a.shape[1:], lambda b: (b, 0, 0))
    cache = lambda a: pl.BlockSpec((1, 1) + a.shape[2:], lambda b: (l, b, 0, 0))
    ins = [qa, ka, va, cak, cav, bias_c, bias_n, qb, kb, vb, cbk, cbv, qc, ckvn, krbn, ckv, ckr,
           w_ukt, w_uvp]
    specs = [req(a) for a in ins]
    for idx in (3, 4, 10, 11, 15, 16):
        specs[idx] = cache(ins[idx])
    for idx in (5, 6, 17, 18):
        specs[idx] = _const_spec(ins[idx].shape)
    out_shape = [jax.ShapeDtypeStruct((nb, ds, W_AB), BF16), jax.ShapeDtypeStruct((nb, ds, W_AB), BF16),
                 jax.ShapeDtypeStruct((nb, ds, H_C * C_V), BF16)]
    return pl.pallas_call(
        functools.partial(_sample_attn_kernel, tile=tile),
        grid=(nb,),
        in_specs=specs,
        out_specs=[req(s) for s in out_shape],
        out_shape=out_shape,
        scratch_shapes=[pltpu.VMEM((H_C * ds, past), F32)],
        compiler_params=_params(1),
        name="sample_attn",
    )(*ins)


def _rope_tables(pos):
    half = C_ROPE // 2
    freq = ROPE_THETA ** (-jnp.arange(half, dtype=F32) / half)
    ang = pos.astype(F32)[:, None] * freq[None, :]
    cos, sin = jnp.cos(ang), jnp.sin(ang)
    n = pos.shape[0]
    z = lambda w: jnp.zeros((n, w), F32)
    tc = jnp.concatenate([jnp.ones((n, C_NOPE), F32), cos, cos, z(C_PAD - C_NOPE - C_ROPE)], axis=1)
    ts1 = jnp.concatenate([z(C_NOPE + half), sin, z(C_PAD - C_NOPE - C_ROPE)], axis=1)
    ts2 = jnp.concatenate([z(C_NOPE), -sin, z(C_PAD - C_NOPE - half)], axis=1)
    return tc, ts1, ts2


def _band_bias(rel_bias, q0, nq, k0, nk):
    period = 1 << int(np.ceil(np.log2(nq + nk)))
    d = np.arange(period)
    d = np.where(d >= nk, d - period, d)
    idx = np.clip((q0 - k0) - d, -REL_CLIP, REL_CLIP) + REL_CLIP
    g = rel_bias.astype(F32)[:, idx]
    toe = jnp.tile(g, (1, nq))[:, :nq * (period - 1)].reshape(-1, nq, period - 1)[:, :, :nk]
    q_pos, k_pos = q0 + np.arange(nq), k0 + np.arange(nk)
    qc, kc = q_pos[:, None] // CHUNK, k_pos[None, :] // CHUNK
    vis = (kc <= qc) & (kc >= qc - BAND_CHUNKS)
    return jnp.where(jnp.asarray(vis)[None], toe * LOG2E, NEG)


def _cast_kernel(w_ref, o_ref):
    o_ref[...] = w_ref[0].astype(BF16)


def _layer_bf16(w, l):
    _, rows, cols = w.shape
    rb = rows // 2 if rows % 32 == 0 else rows
    return pl.pallas_call(
        _cast_kernel,
        grid=(rows // rb,),
        in_specs=[pl.BlockSpec((1, rb, cols), lambda i: (l, i, 0))],
        out_specs=pl.BlockSpec((rb, cols), lambda i: (i, 0)),
        out_shape=jax.ShapeDtypeStruct((rows, cols), BF16),
        compiler_params=_params(1),
        name="cast_bf16",
    )(w)


def _layer_weights(l, w_in, w_uq, w_uk, w_uv):
    q_rank, kv_rank = w_uq.shape[1], w_uk.shape[1]
    o = 6 * W_AB + q_rank + kv_rank
    w_gates = w_in[l, :, o + C_ROPE:].astype(BF16)
    pad_heads = lambda w: jnp.pad(w, ((0, 0), (0, 0), (0, C_PAD - w.shape[2]))).reshape(w.shape[0], -1)
    w_uq_p = pad_heads(w_uq[l]).astype(BF16)
    w_ukv = jnp.concatenate([pad_heads(w_uk[l]), w_uv[l].reshape(kv_rank, -1)], axis=1).astype(BF16)
    w_ukt = jnp.pad(jnp.transpose(w_uk[l], (1, 2, 0)), ((0, 0), (0, C_PAD - C_NOPE), (0, 0))).astype(BF16)
    w_uvp = jnp.einsum('rhd,hg->hrgd', w_uv[l], jnp.eye(H_C, dtype=w_uv.dtype)).reshape(
        H_C, kv_rank, H_C * C_V).astype(BF16)
    return w_gates, w_uq_p, w_ukv, w_ukt, w_uvp


def kernel(x_prompt, x_sample, cache_a_k, cache_a_v, cache_b_k, cache_b_v, cache_c_kv, cache_c_kr, ffn1_norm_pre, ffn1_norm_post, ffn1_w_gate, ffn1_w_up, ffn1_w_down, mix_norm_pre, mix_norm_post, w_in, cq_norm, ckv_norm, w_uq, w_uk, w_uv, rel_bias_a, w_br_a, w_br_b, w_br_c, w_out, ffn2_norm_pre, ffn2_norm_post, ffn2_w_gate, ffn2_w_up, ffn2_w_down):
    batch, seq, d = x_prompt.shape
    nb, ds, _ = x_sample.shape
    depth = w_in.shape[0]
    past = cache_b_k.shape[2]
    win_cache = cache_a_k.shape[2]
    assert batch == 1, "prompt group is a single sequence"
    tile = 256
    tm = 512 if seq % 512 == 0 else tile
    assert seq % tile == 0 and past % tile == 0 and tile % CHUNK == 0 and 2 * tile >= WIN_A
    assert past % CHUNK + ds <= CHUNK
    ns = nb * ds

    xp = x_prompt.reshape(seq, d)
    xs = x_sample.reshape(ns, d)
    tabs_p = _rope_tables(jnp.arange(seq))
    tabs_s = _rope_tables(jnp.tile(past + jnp.arange(ds), nb))

    row = lambda g, l: g[l][None, :]
    base = 2 * tile

    merged_heads = lambda c: c.reshape(c.shape[:3] + (-1,))
    caches = (merged_heads(cache_a_k), merged_heads(cache_a_v), merged_heads(cache_b_k),
              merged_heads(cache_b_v), cache_c_kv, cache_c_kr)
    w_in_bf16 = w_in.astype(BF16)
    states_p, states_s = [], []
    for l in range(depth):
        w_gates, w_uq_p, w_ukv, w_ukt, w_uvp = _layer_weights(l, w_in, w_uq, w_uk, w_uv)
        ffn1 = (row(ffn1_norm_pre, l), row(ffn1_norm_post, l), _layer_bf16(ffn1_w_gate, l),
                _layer_bf16(ffn1_w_up, l), _layer_bf16(ffn1_w_down, l))
        ffn2 = (row(ffn2_norm_pre, l), row(ffn2_norm_post, l), _layer_bf16(ffn2_w_gate, l),
                _layer_bf16(ffn2_w_up, l), _layer_bf16(ffn2_w_down, l))
        mix_in = (row(mix_norm_pre, l), w_in_bf16, l, row(cq_norm, l), row(ckv_norm, l), w_uq_p, w_ukv)
        mrg = (row(mix_norm_pre, l), row(mix_norm_post, l), w_gates, l, w_br_a, w_br_b, w_br_c, w_out)
        bias_p = _band_bias(rel_bias_a[l], base, tile, base - 2 * tile, 3 * tile)
        bias_c = _band_bias(rel_bias_a[l], past, ds, past - win_cache, win_cache)
        bias_n = _band_bias(rel_bias_a[l], past, ds, past, ds)

        xp = _ffn(xp, *ffn1, tm)
        (qa, kab, vab, qb, kbb, vbb, ka, va, kb, vb, ckv, krb, qc, kc, vc, kn) = _mixer_in(
            xp, *mix_in, tabs_p, tm, q_transposed=True)
        ya, yb = _band_sb_attention(qa, kab, vab, bias_p, qb, kbb, vbb, tile)
        yc = _mla_attention(qc, kc, vc, kn, 2 * tm if seq % (2 * tm) == 0 else tm)
        xp = _merge(xp, ya, yb, yc, *mrg, tm, yc_transposed=True)
        xp = _ffn(xp, *ffn2, tm)
        win = min(WIN_A, seq)
        heads = lambda t, h: t.reshape(1, t.shape[0], h, t.shape[1] // h)
        states_p.append((heads(ka[-win:], H_A), heads(va[-win:], H_A), heads(kb, H_B), heads(vb, H_B),
                         ckv[None], krb[None, :, C_NOPE:C_NOPE + C_ROPE]))

        xs = _ffn(xs, *ffn1, ns)
        (qa, kab, vab, qb, kbb, vbb, ka, va, kb, vb, ckv, krb, qc, kc, vc, _) = _mixer_in(
            xs, *mix_in, tabs_s, ns)
        per_req = lambda t: t.reshape(nb, ds, t.shape[1])
        new = tuple(per_req(t) for t in (qa, kab, vab, qb, kbb, vbb, qc, ckv, krb))
        ya, yb, yc = _sample_attention(new, caches, l, bias_c, bias_n, w_ukt, w_uvp, tile)
        xs = _merge(xs, ya.reshape(ns, -1), yb.reshape(ns, -1), yc.reshape(ns, -1), *mrg, ns)
        xs = _ffn(xs, *ffn2, ns)
        heads_s = lambda t, h: t.reshape(nb, ds, h, t.shape[1] // h)
        states_s.append((heads_s(ka, H_A), heads_s(va, H_A), heads_s(kb, H_B), heads_s(vb, H_B),
                         ckv.reshape(nb, ds, -1), krb[:, C_NOPE:C_NOPE + C_ROPE].reshape(nb, ds, C_ROPE)))

    stack = lambda states, i: jnp.stack([s[i] for s in states], axis=0)
    return ((xp.reshape(batch, seq, d), xs.reshape(nb, ds, d))
            + tuple(stack(states_p, i) for i in range(6))
            + tuple(stack(states_s, i) for i in range(6)))
```
